```python
import math
import jax, jax.numpy as jnp
from jax import lax
import numpy as np

D_MODEL = 2048
BATCH = 8
SEQ = 2048
DEPTH = 2

HEAD_DIM = 128
N_HEADS = D_MODEL // HEAD_DIM
MIX_W = N_HEADS * HEAD_DIM
A_HEADS = 3 * N_HEADS // 8
C_HEADS = N_HEADS // 4
B_HEADS = N_HEADS - A_HEADS - C_HEADS
A_W = A_HEADS * HEAD_DIM
B_W = B_HEADS * HEAD_DIM
C_W = C_HEADS * HEAD_DIM
DIL_PATTERNS = ((128, 1), (512, 4), (2048, 16))
BAND_BLK = 128
NSA_KV = 2
NSA_GROUP = B_HEADS // NSA_KV
KV_W = NSA_KV * HEAD_DIM
CMP_LEN = 32
CMP_STRIDE = 16
SLC_LEN = 64
SLC_TOPN = 8
NSA_WINDOW = 512
FORCE = 1e4
MOBA_BLK = 256
MOBA_TOPK = 3
GATHER_QBLK = 16
REL_BUCKETS = 32
REL_MAX_DIST = 1024
PEER_HEADS = 8
PEER_NKEYS = 128
PEER_EXPERTS = PEER_NKEYS * PEER_NKEYS
PEER_DQ = 256
PEER_TOPK = 16
PEER_TOKBLK = 128
PLE_DIM = 256
EPS = 1e-6
PROJ_SPLITS = (A_W, A_W, A_W, B_W, KV_W, KV_W, KV_W, KV_W, KV_W, KV_W, 3 * B_HEADS, C_W, C_W, C_W)
PROJ_COLS = sum(PROJ_SPLITS)
PROJ_BOUNDS = tuple(int(c) for c in np.cumsum(PROJ_SPLITS)[:-1])

kernel_name = 'hybrid_dilated_nsa_moba_peer'


def rmsnorm(x, g):
    x32 = x.astype(jnp.float32)
    y = x32 * lax.rsqrt(jnp.mean(x32 * x32, axis=-1, keepdims=True) + EPS)
    return (y * g.astype(jnp.float32)).astype(x.dtype)


def rel_bucket(dist):
    exact = REL_BUCKETS // 2
    d = jnp.maximum(dist, 0)
    logd = jnp.log(jnp.maximum(d, 1).astype(jnp.float32) / exact)
    large = exact + (logd / math.log(REL_MAX_DIST / exact) * (REL_BUCKETS - exact)).astype(jnp.int32)
    large = jnp.clip(large, exact, REL_BUCKETS - 1)
    return jnp.where(d < exact, d, large)


def masked_softmax(s, mask):
    s = jnp.where(mask, s.astype(jnp.float32), -jnp.inf)
    m = jnp.max(s, axis=-1, keepdims=True)
    m = jnp.where(jnp.isfinite(m), m, 0.0)
    e = jnp.exp(s - m)
    z = jnp.sum(e, axis=-1, keepdims=True)
    zs = jnp.where(z > 0, z, 1.0)
    p = e / zs
    lse = jnp.where(z > 0, m + jnp.log(zs), -jnp.inf)
    return p, lse[..., 0]


def merge_partials(outs, lses):
    w = jax.nn.softmax(jnp.stack(lses, axis=0), axis=0)
    o = jnp.sum(w[..., None] * jnp.stack(outs, axis=0).astype(jnp.float32), axis=0)
    return o.astype(outs[0].dtype)


def banded_attention(q, k, v, n_back, dist_scale, bias_tab):
    n, l, h, hd = q.shape
    blk = BAND_BLK
    nb = l // blk
    npv = -(-n_back // blk)
    w = (npv + 1) * blk
    pad = ((0, 0), (npv * blk, 0), (0, 0), (0, 0))
    kb = jnp.pad(k, pad).reshape(n, nb + npv, blk, h, hd)
    vb = jnp.pad(v, pad).reshape(n, nb + npv, blk, h, hd)
    kwin = jnp.concatenate([kb[:, j:j + nb] for j in range(npv + 1)], axis=2)
    vwin = jnp.concatenate([vb[:, j:j + nb] for j in range(npv + 1)], axis=2)
    qb = q.reshape(n, nb, blk, h, hd)
    s = jnp.einsum('nbqhd,nbkhd->nbhqk', qb, kwin).astype(jnp.float32) * (hd ** -0.5)
    dist = jnp.arange(blk)[:, None] - jnp.arange(w)[None, :] + npv * blk
    bias = bias_tab[rel_bucket(dist * dist_scale)].astype(jnp.float32)
    s = s + jnp.transpose(bias, (2, 0, 1))
    kpos = jnp.arange(nb)[:, None] * blk + jnp.arange(w)[None, :] - npv * blk
    mask = ((dist >= 0) & (dist <= n_back))[None, :, :] & (kpos >= 0)[:, None, :]
    p, lse = masked_softmax(s, mask[None, :, None])
    o = jnp.einsum('nbhqk,nbkhd->nbqhd', p.astype(v.dtype), vwin)
    return o.reshape(n, l, h, hd), jnp.transpose(lse, (0, 1, 3, 2)).reshape(n, l, h)


def dilated_attention(q, k, v, bias_tab):
    b, s, h, hd = q.shape
    outs, lses = [], []
    for window, dil in DIL_PATTERNS:
        l = s // dil
        lp = -(-l // BAND_BLK) * BAND_BLK

        def to_res(t):
            t = t.reshape(b, l, dil, h, hd).transpose(0, 2, 1, 3, 4).reshape(b * dil, l, h, hd)
            return jnp.pad(t, ((0, 0), (0, lp - l), (0, 0), (0, 0)))

        o, lse = banded_attention(to_res(q), to_res(k), to_res(v), window // dil, dil, bias_tab)
        outs.append(o[:, :l].reshape(b, dil, l, h, hd).transpose(0, 2, 1, 3, 4).reshape(b, s, h, hd))
        lses.append(lse[:, :l].reshape(b, dil, l, h).transpose(0, 2, 1, 3).reshape(b, s, h))
    return merge_partials(outs, lses)


def nsa_attention(q, kc, vc, ks, vs, kw, vw, gates, cmp_wk, cmp_wv, cmp_pos, bias_tab):
    b, s, h, hd = q.shape
    f32 = jnp.float32
    scale = hd ** -0.5
    pos = jnp.arange(s)
    qg = q.reshape(b, s, NSA_KV, NSA_GROUP, hd)
    ncmp = (s - CMP_LEN) // CMP_STRIDE + 1
    cidx = jnp.arange(ncmp)[:, None] * CMP_STRIDE + jnp.arange(CMP_LEN)[None, :]
    k_cmp = jnp.einsum('bnlgd,lde->bnge', kc[:, cidx] + cmp_pos[:, None, :], cmp_wk)
    v_cmp = jnp.einsum('bnlgd,lde->bnge', vc[:, cidx] + cmp_pos[:, None, :], cmp_wv)
    s_c = jnp.einsum('bqgrd,bngd->bgrqn', qg, k_cmp).astype(f32) * scale
    cend = jnp.arange(ncmp) * CMP_STRIDE + CMP_LEN - 1
    dist_c = pos[:, None] - cend[None, :]
    bias_c = bias_tab[rel_bucket(dist_c)].astype(f32).reshape(s, ncmp, NSA_KV, NSA_GROUP)
    s_c = s_c + bias_c.transpose(2, 3, 0, 1)
    p_c, _ = masked_softmax(s_c, dist_c >= 0)
    o_cmp = jnp.einsum('bgrqn,bngd->bqgrd', p_c.astype(vc.dtype), v_cmp).reshape(b, s, h, hd)
    nslc = s // SLC_LEN
    cstart = jnp.arange(ncmp) * CMP_STRIDE
    sstart = jnp.arange(nslc) * SLC_LEN
    cover = ((cstart[:, None] < sstart[None, :] + SLC_LEN)
             & (cstart[:, None] + CMP_LEN > sstart[None, :])).astype(f32)
    imp = jnp.einsum('bgrqn,nj->bgqj', p_c, cover)
    qblk = pos // SLC_LEN
    jb = jnp.arange(nslc)
    forced = (jb[None, :] == 0) | (jb[None, :] == qblk[:, None]) | (jb[None, :] == qblk[:, None] - 1)
    future = jb[None, :] > qblk[:, None]
    imp = jnp.where(forced, FORCE, jnp.where(future, -FORCE, imp))
    topn = min(SLC_TOPN, nslc)
    _, sel = lax.top_k(imp, topn)
    ksb = ks.transpose(0, 2, 1, 3).reshape(b, NSA_KV, nslc, SLC_LEN * hd)
    vsb = vs.transpose(0, 2, 1, 3).reshape(b, NSA_KV, nslc, SLC_LEN * hd)
    nq = s // GATHER_QBLK
    q_ch = qg.reshape(b, nq, GATHER_QBLK, NSA_KV, NSA_GROUP, hd).transpose(1, 0, 2, 3, 4, 5)
    sel_ch = sel.reshape(b, NSA_KV, nq, GATHER_QBLK, topn).transpose(2, 0, 1, 3, 4)
    tab = bias_tab.reshape(REL_BUCKETS, NSA_KV, NSA_GROUP)
    bi = jnp.arange(b)[:, None, None]
    gi = jnp.arange(NSA_KV)[None, :, None]
    kk_len = topn * SLC_LEN

    def sel_block(args):
        ci, qc, ic = args
        flat = ic.reshape(b, NSA_KV, GATHER_QBLK * topn)
        kk = ksb[bi, gi, flat].reshape(b, NSA_KV, GATHER_QBLK, kk_len, hd)
        vv = vsb[bi, gi, flat].reshape(b, NSA_KV, GATHER_QBLK, kk_len, hd)
        sc = jnp.einsum('bqgrd,bgqkd->bgrqk', qc, kk).astype(f32) * scale
        qpos = ci * GATHER_QBLK + jnp.arange(GATHER_QBLK)
        kpos = (ic[..., None] * SLC_LEN + jnp.arange(SLC_LEN)).reshape(b, NSA_KV, GATHER_QBLK, kk_len)
        dist = qpos[None, None, :, None] - kpos
        bias = tab[rel_bucket(dist), jnp.arange(NSA_KV)[None, :, None, None]].astype(f32)
        sc = sc + jnp.moveaxis(bias, -1, 2)
        p, _ = masked_softmax(sc, (dist >= 0)[:, :, None])
        return jnp.einsum('bgrqk,bgqkd->bqgrd', p.astype(vv.dtype), vv)

    o_sel = lax.map(sel_block, (jnp.arange(nq), q_ch, sel_ch))
    o_sel = o_sel.transpose(1, 0, 2, 3, 4, 5).reshape(b, s, h, hd)
    o_win, _ = banded_attention(q, jnp.repeat(kw, NSA_GROUP, axis=2), jnp.repeat(vw, NSA_GROUP, axis=2),
                                NSA_WINDOW - 1, 1, bias_tab)
    return gates[..., 0:1] * o_cmp + gates[..., 1:2] * o_sel + gates[..., 2:3] * o_win


def moba_attention(q, k, v, bias_tab):
    b, s, h, hd = q.shape
    f32 = jnp.float32
    scale = hd ** -0.5
    sp = -(-s // MOBA_BLK) * MOBA_BLK
    pad = ((0, 0), (0, sp - s), (0, 0), (0, 0))
    q, k, v = jnp.pad(q, pad), jnp.pad(k, pad), jnp.pad(v, pad)
    nblk = sp // MOBA_BLK
    qb = q.reshape(b, nblk, MOBA_BLK, h, hd)
    kb = k.reshape(b, nblk, MOBA_BLK, h, hd)
    vb = v.reshape(b, nblk, MOBA_BLK, h, hd)
    s_o = jnp.einsum('bnqhd,bnkhd->bnhqk', qb, kb).astype(f32) * scale
    dist_o = jnp.arange(MOBA_BLK)[:, None] - jnp.arange(MOBA_BLK)[None, :]
    s_o = s_o + jnp.transpose(bias_tab[rel_bucket(dist_o)].astype(f32), (2, 0, 1))
    p_o, lse_o = masked_softmax(s_o, dist_o >= 0)
    o_own = jnp.einsum('bnhqk,bnkhd->bnqhd', p_o.astype(v.dtype), vb).reshape(b, sp, h, hd)
    lse_own = jnp.transpose(lse_o, (0, 1, 3, 2)).reshape(b, sp, h)
    ntop = min(MOBA_TOPK, nblk - 1)
    if ntop == 0:
        return o_own[:, :s]
    kmean = jnp.mean(kb.astype(f32), axis=2)
    gate = jnp.einsum('bshd,bnhd->bhsn', q.astype(f32), kmean)
    qblk_all = jnp.arange(sp) // MOBA_BLK
    past = jnp.arange(nblk)[None, :] < qblk_all[:, None]
    gate = jnp.where(past, gate, -jnp.inf)
    _, sel = lax.top_k(gate, ntop)
    kbf = kb.transpose(0, 3, 1, 2, 4).reshape(b, h, nblk, MOBA_BLK * hd)
    vbf = vb.transpose(0, 3, 1, 2, 4).reshape(b, h, nblk, MOBA_BLK * hd)
    nq = sp // GATHER_QBLK
    q_ch = q.reshape(b, nq, GATHER_QBLK, h, hd).transpose(1, 0, 2, 3, 4)
    sel_ch = sel.reshape(b, h, nq, GATHER_QBLK, ntop).transpose(2, 0, 1, 3, 4)
    bi = jnp.arange(b)[:, None, None]
    hi = jnp.arange(h)[None, :, None]
    kk_len = ntop * MOBA_BLK

    def blk_fn(args):
        ci, qc, ic = args
        flat = ic.reshape(b, h, GATHER_QBLK * ntop)
        kk = kbf[bi, hi, flat].reshape(b, h, GATHER_QBLK, kk_len, hd)
        vv = vbf[bi, hi, flat].reshape(b, h, GATHER_QBLK, kk_len, hd)
        sc = jnp.einsum('bqhd,bhqkd->bhqk', qc, kk).astype(f32) * scale
        qpos = ci * GATHER_QBLK + jnp.arange(GATHER_QBLK)
        kpos = (ic[..., None] * MOBA_BLK + jnp.arange(MOBA_BLK)).reshape(b, h, GATHER_QBLK, kk_len)
        dist = qpos[None, None, :, None] - kpos
        sc = sc + bias_tab[rel_bucket(dist), jnp.arange(h)[None, :, None, None]].astype(f32)
        valid = ic < (qpos // MOBA_BLK)[None, None, :, None]
        p, lse = masked_softmax(sc, jnp.repeat(valid, MOBA_BLK, axis=-1))
        o = jnp.einsum('bhqk,bhqkd->bqhd', p.astype(vv.dtype), vv)
        return o, jnp.transpose(lse, (0, 2, 1))

    o_g, lse_g = lax.map(blk_fn, (jnp.arange(nq), q_ch, sel_ch))
    o_g = o_g.transpose(1, 0, 2, 3, 4).reshape(b, sp, h, hd)
    lse_g = lse_g.transpose(1, 0, 2, 3).reshape(b, sp, h)
    return merge_partials([o_own, o_g], [lse_own, lse_g])[:, :s]


def hybrid_mixer(a, w_in, cmp_wk, cmp_wv, cmp_pos, out_norm, w_out, rel_bias):
    b, s, _ = a.shape
    (qa, ka, va, qb, kc, vc, ks, vs, kw, vw, gt, qc, kc_c, vc_c) = jnp.split(a @ w_in, PROJ_BOUNDS, axis=-1)

    def hs(t):
        return t.reshape(b, s, -1, HEAD_DIM)

    tab_a = rel_bias[:, :A_HEADS]
    tab_b = rel_bias[:, A_HEADS:A_HEADS + B_HEADS]
    tab_c = rel_bias[:, A_HEADS + B_HEADS:]
    oa = dilated_attention(hs(qa), hs(ka), hs(va), tab_a)
    gates = jax.nn.sigmoid(gt.astype(jnp.float32)).astype(a.dtype).reshape(b, s, B_HEADS, 3)
    ob = nsa_attention(hs(qb), hs(kc), hs(vc), hs(ks), hs(vs), hs(kw), hs(vw), gates,
                       cmp_wk, cmp_wv, cmp_pos, tab_b)
    oc = moba_attention(hs(qc), hs(kc_c), hs(vc_c), tab_c)
    g_a, g_b, g_c = jnp.split(out_norm, [A_W, A_W + B_W])
    y = jnp.concatenate([rmsnorm(oa.reshape(b, s, A_W), g_a),
                         rmsnorm(ob.reshape(b, s, B_W), g_b),
                         rmsnorm(oc.reshape(b, s, C_W), g_c)], axis=-1)
    return y @ w_out


def peer_ffn(x, wq, subkeys, u, v):
    b, s, d = x.shape
    q = (x @ wq).reshape(b, s, PEER_HEADS, 2, PEER_DQ // 2)
    sc = jnp.einsum('bshcd,hckd->bshck', q, subkeys).astype(jnp.float32)
    top_s, top_i = lax.top_k(sc, PEER_TOPK)
    cand = top_s[..., 0, :, None] + top_s[..., 1, None, :]
    cand_i = top_i[..., 0, :, None] * PEER_NKEYS + top_i[..., 1, None, :]
    best_s, best_j = lax.top_k(cand.reshape(b, s, PEER_HEADS, PEER_TOPK * PEER_TOPK), PEER_TOPK)
    expert = jnp.take_along_axis(cand_i.reshape(b, s, PEER_HEADS, PEER_TOPK * PEER_TOPK), best_j, axis=-1)
    g = jax.nn.softmax(best_s, axis=-1)
    nt = (b * s) // PEER_TOKBLK
    hk = PEER_HEADS * PEER_TOPK
    xs = x.reshape(nt, PEER_TOKBLK, d)
    es = expert.reshape(nt, PEER_TOKBLK, hk)
    gs = g.reshape(nt, PEER_TOKBLK, hk).astype(x.dtype)

    def tok_fn(args):
        xc, ec, gc = args
        act = jax.nn.gelu(jnp.einsum('td,tkd->tk', xc, u[ec]), approximate=False)
        return jnp.einsum('tk,tkd->td', gc * act, v[ec])

    return lax.map(tok_fn, (xs, es, gs)).reshape(b, s, d)


def setup_inputs(seed: int = 0) -> dict:
    key = jax.random.key(seed)
    k = jax.random.split(key, 20)
    f32 = jnp.float32

    def nrm(kk, shape, scale):
        return jax.random.normal(kk, shape, f32) * scale

    return {
        'x': nrm(k[0], (BATCH, SEQ, D_MODEL), 1.0),
        'p': nrm(k[1], (DEPTH, BATCH, SEQ, PLE_DIM), 1.0),
        'ln_mix': 1.0 + nrm(k[2], (DEPTH, D_MODEL), 0.02),
        'w_in': nrm(k[3], (DEPTH, D_MODEL, PROJ_COLS), D_MODEL ** -0.5),
        'cmp_wk': nrm(k[4], (DEPTH, CMP_LEN, HEAD_DIM, HEAD_DIM), (CMP_LEN * HEAD_DIM) ** -0.5),
        'cmp_wv': nrm(k[5], (DEPTH, CMP_LEN, HEAD_DIM, HEAD_DIM), (CMP_LEN * HEAD_DIM) ** -0.5),
        'cmp_pos': nrm(k[6], (DEPTH, CMP_LEN, HEAD_DIM), 0.1),
        'out_norm': 1.0 + nrm(k[7], (DEPTH, MIX_W), 0.02),
        'w_out': nrm(k[8], (DEPTH, MIX_W, D_MODEL), MIX_W ** -0.5),
        'rel_bias': nrm(k[9], (REL_BUCKETS, N_HEADS), 0.5),
        'ln_ffn': 1.0 + nrm(k[10], (DEPTH, D_MODEL), 0.02),
        'peer_wq': nrm(k[11], (DEPTH, D_MODEL, PEER_HEADS * PEER_DQ), D_MODEL ** -0.5),
        'peer_keys': nrm(k[12], (DEPTH, PEER_HEADS, 2, PEER_NKEYS, PEER_DQ // 2), (PEER_DQ // 2) ** -0.5),
        'peer_u': nrm(k[13], (DEPTH, PEER_EXPERTS, D_MODEL), D_MODEL ** -0.5),
        'peer_v': nrm(k[14], (DEPTH, PEER_EXPERTS, D_MODEL), PEER_HEADS ** -0.5),
        'ln_ple': 1.0 + nrm(k[15], (DEPTH, D_MODEL), 0.02),
        'ple_gate': nrm(k[16], (DEPTH, D_MODEL, D_MODEL), D_MODEL ** -0.5),
        'ple_proj': nrm(k[17], (DEPTH, PLE_DIM, D_MODEL), PLE_DIM ** -0.5),
        'ln_final': 1.0 + nrm(k[18], (D_MODEL,), 0.02),
    }


def reference(x, p, ln_mix, w_in, cmp_wk, cmp_wv, cmp_pos, out_norm, w_out, rel_bias, ln_ffn,
              peer_wq, peer_keys, peer_u, peer_v, ln_ple, ple_gate, ple_proj, ln_final):
    h = x
    for i in range(DEPTH):
        h = h + hybrid_mixer(rmsnorm(h, ln_mix[i]), w_in[i], cmp_wk[i], cmp_wv[i], cmp_pos[i],
                             out_norm[i], w_out[i], rel_bias)
        h = h + peer_ffn(rmsnorm(h, ln_ffn[i]), peer_wq[i], peer_keys[i], peer_u[i], peer_v[i])
        gate = jax.nn.sigmoid((rmsnorm(h, ln_ple[i]) @ ple_gate[i]).astype(jnp.float32)).astype(h.dtype)
        h = h + gate * (p[i] @ ple_proj[i])
    return rmsnorm(h, ln_final)
```

```python
import functools
import math

import numpy as np
import jax
import jax.numpy as jnp
from jax import lax
from jax.experimental import pallas as pl
from jax.experimental.pallas import tpu as pltpu

F32 = jnp.float32
BF16 = jnp.bfloat16
I32 = jnp.int32

HEAD_DIM = 128
A_HEADS, B_HEADS, C_HEADS = 6, 6, 4
NSA_KV, NSA_GROUP = 2, 3
DIL_PATTERNS = ((128, 1), (512, 4), (2048, 16))
CMP_LEN, CMP_STRIDE = 32, 16
SLC_LEN, SLC_TOPN = 64, 8
NSA_WINDOW = 512
FORCE = 1e4
MOBA_BLK, MOBA_TOPK = 256, 3
REL_BUCKETS, REL_MAX_DIST = 32, 1024
PEER_HEADS, PEER_NKEYS, PEER_TOPK = 8, 128, 16
EPS = 1e-6

LANE = 128
BQ = 256
BK = 256
NEG = -1e30
HALF_NEG = -5e29
VMEM_LIMIT = 56 * 1024 * 1024

CB_QA, CB_KA, CB_VA = 0, 6, 12
CB_QB, CB_KC, CB_VC, CB_KS, CB_VS, CB_KW, CB_VW = 18, 24, 26, 28, 30, 32, 34
CB_QC, CB_KCC, CB_VCC = 36, 40, 44
MAIN_COLS = 48 * LANE


def _nt(a, b, precision=None):
    return lax.dot_general(a, b, (((1,), (1,)), ((), ())), preferred_element_type=F32,
                           precision=precision)


def _rel_bucket(dist):
    exact = REL_BUCKETS // 2
    d = jnp.maximum(dist, 0)
    logd = jnp.log(jnp.maximum(d, 1).astype(F32) / exact)
    large = exact + (logd / math.log(REL_MAX_DIST / exact) * (REL_BUCKETS - exact)).astype(I32)
    large = jnp.clip(large, exact, REL_BUCKETS - 1)
    return jnp.where(d < exact, d, large)


def _masked_softmax(s, axis):
    valid = s > HALF_NEG
    m = jnp.max(s, axis=axis, keepdims=True)
    e = jnp.where(valid, jnp.exp(s - m), 0.0)
    z = jnp.sum(e, axis=axis, keepdims=True)
    zs = jnp.where(z > 0, z, 1.0)
    return e / zs


def _params(*sem):
    return pltpu.CompilerParams(dimension_semantics=sem, vmem_limit_bytes=VMEM_LIMIT)


def _normmm_body(x_ref, g_ref, w_ref, o_ref, xn_ref):
    @pl.when(pl.program_id(1) == 0)
    def _():
        x = x_ref[...]
        ms = jnp.mean(x * x, axis=-1, keepdims=True)
        xn_ref[...] = (x * lax.rsqrt(ms + EPS) * g_ref[...]).astype(BF16)

    o_ref[...] = jnp.dot(xn_ref[...], w_ref[...], preferred_element_type=F32).astype(o_ref.dtype)


def _normmm(x, g, w, out_dtype, tm, tn, name):
    t, d = x.shape
    n = w.shape[1]
    return pl.pallas_call(
        _normmm_body,
        grid=(t // tm, n // tn),
        in_specs=[pl.BlockSpec((tm, d), lambda i, j: (i, 0)),
                  pl.BlockSpec((1, d), lambda i, j: (0, 0)),
                  pl.BlockSpec((d, tn), lambda i, j: (0, j))],
        out_specs=pl.BlockSpec((tm, tn), lambda i, j: (i, j)),
        out_shape=jax.ShapeDtypeStruct((t, n), out_dtype),
        scratch_shapes=[pltpu.VMEM((tm, d), BF16)],
        compiler_params=_params("parallel", "arbitrary"),
        name=name,
    )(x, g.reshape(1, d), w)


def _flash(q, k_ref, v_ref, lo, hi, score_fn, scale):
    def body(kj, carry):
        m, l, acc = carry
        off = pl.multiple_of(kj * BK, BK)
        k = k_ref[0, pl.ds(off, BK), :]
        v = v_ref[0, pl.ds(off, BK), :]
        s = score_fn(kj, _nt(q, k) * scale)
        m_new = jnp.maximum(m, jnp.max(s, axis=1, keepdims=True))
        alpha = jnp.exp(m - m_new)
        p = jnp.exp(s - m_new)
        l = alpha * l + jnp.sum(p, axis=1, keepdims=True)
        acc = alpha * acc + jnp.dot(p.astype(BF16), v, preferred_element_type=F32)
        return m_new, l, acc

    init = (jnp.full((BQ, 1), NEG, F32), jnp.zeros((BQ, 1), F32), jnp.zeros((BQ, HEAD_DIM), F32))
    _, l, acc = lax.fori_loop(lo, hi, body, init)
    return acc / l


def _dilated_body(q_ref, k_ref, v_ref, tz_ref, o_ref):
    qi = pl.program_id(2)
    scale = HEAD_DIM ** -0.5

    def score(kj, s):
        return s + tz_ref[0, qi - kj]

    o_ref[0] = _flash(q_ref[0], k_ref, v_ref, 0, qi + 1, score, scale)


def _dilated(proj, tzd):
    b, s, _ = proj.shape
    nd = s // BK
    return pl.pallas_call(
        _dilated_body,
        grid=(b, A_HEADS, s // BQ),
        in_specs=[pl.BlockSpec((1, BQ, LANE), lambda bi, h, qi: (bi, qi, CB_QA + h)),
                  pl.BlockSpec((1, s, LANE), lambda bi, h, qi: (bi, 0, CB_KA + h)),
                  pl.BlockSpec((1, s, LANE), lambda bi, h, qi: (bi, 0, CB_VA + h)),
                  pl.BlockSpec((1, nd, BQ, BK), lambda bi, h, qi: (h, 0, 0, 0))],
        out_specs=pl.BlockSpec((1, BQ, LANE), lambda bi, h, qi: (bi, qi, h)),
        out_shape=jax.ShapeDtypeStruct((b, s, A_HEADS * LANE), F32),
        compiler_params=_params("parallel", "parallel", "arbitrary"),
        name="dilated_attention",
    )(proj, proj, proj, tzd)


def _compress_body(kc_ref, vc_ref, wk_ref, wv_ref, pos_ref, ko_ref, vo_ref, xk_ref, xv_ref):
    s = kc_ref.shape[1]
    ncmp = ko_ref.shape[2]
    xk_ref[pl.ds(0, s), :] = kc_ref[0].astype(F32)
    xv_ref[pl.ds(0, s), :] = vc_ref[0].astype(F32)
    xk_ref[pl.ds(s, CMP_LEN), :] = jnp.zeros((CMP_LEN, HEAD_DIM), F32)
    xv_ref[pl.ds(s, CMP_LEN), :] = jnp.zeros((CMP_LEN, HEAD_DIM), F32)
    acck = jnp.zeros((ncmp, HEAD_DIM), F32)
    accv = jnp.zeros((ncmp, HEAD_DIM), F32)
    for l in range(CMP_LEN):
        pos = pos_ref[pl.ds(l, 1), :]
        rk = (xk_ref[pl.ds(l, ncmp, stride=CMP_STRIDE), :] + pos).astype(BF16)
        rv = (xv_ref[pl.ds(l, ncmp, stride=CMP_STRIDE), :] + pos).astype(BF16)
        acck = acck + jnp.dot(rk, wk_ref[l], preferred_element_type=F32)
        accv = accv + jnp.dot(rv, wv_ref[l], preferred_element_type=F32)
    ko_ref[0, 0] = acck.astype(BF16)
    vo_ref[0, 0] = accv.astype(BF16)


def _compress(proj, cmp_wk, cmp_wv, cmp_pos):
    b, s, _ = proj.shape
    ncmp = s // CMP_STRIDE
    out = jax.ShapeDtypeStruct((b, NSA_KV, ncmp, HEAD_DIM), BF16)
    return pl.pallas_call(
        _compress_body,
        grid=(b, NSA_KV),
        in_specs=[pl.BlockSpec((1, s, LANE), lambda bi, g: (bi, 0, CB_KC + g)),
                  pl.BlockSpec((1, s, LANE), lambda bi, g: (bi, 0, CB_VC + g)),
                  pl.BlockSpec((CMP_LEN, HEAD_DIM, HEAD_DIM), lambda bi, g: (0, 0, 0)),
                  pl.BlockSpec((CMP_LEN, HEAD_DIM, HEAD_DIM), lambda bi, g: (0, 0, 0)),
                  pl.BlockSpec((CMP_LEN, HEAD_DIM), lambda bi, g: (0, 0))],
        out_specs=[pl.BlockSpec((1, 1, ncmp, HEAD_DIM), lambda bi, g: (bi, g, 0, 0)),
                   pl.BlockSpec((1, 1, ncmp, HEAD_DIM), lambda bi, g: (bi, g, 0, 0))],
        out_shape=[out, out],
        scratch_shapes=[pltpu.VMEM((s + CMP_LEN, HEAD_DIM), F32),
                        pltpu.VMEM((s + CMP_LEN, HEAD_DIM), F32)],
        compiler_params=_params("parallel", "parallel"),
        name="nsa_compress",
    )(proj, proj, cmp_wk.astype(BF16), cmp_wv.astype(BF16), cmp_pos)


def _nsa_body(q_ref, kcmp_ref, vcmp_ref, ks_ref, vs_ref, kw_ref, vw_ref, gt_ref, bcq_ref, bct_ref,
              covt_ref, tzs_ref, tzw_ref, o_ref, mask_ref):
    qi = pl.program_id(2)
    scale = HEAD_DIM ** -0.5
    nslc = covt_ref.shape[0]
    nkt = mask_ref.shape[0]
    kcmp = kcmp_ref[0, 0]
    vcmp = vcmp_ref[0, 0]
    gates = jax.nn.sigmoid(gt_ref[0])

    psum = jnp.zeros((kcmp.shape[0], BQ), F32)
    for r in range(NSA_GROUP):
        q = q_ref[0, :, r * LANE:(r + 1) * LANE]
        p = _masked_softmax(_nt(q, kcmp) * scale + bcq_ref[r], axis=1)
        o_cmp = jnp.dot(p.astype(BF16), vcmp, preferred_element_type=F32)
        o_ref[0, :, r * LANE:(r + 1) * LANE] = gates[:, 3 * r:3 * r + 1] * o_cmp
        psum = psum + _masked_softmax(_nt(kcmp, q) * scale + bct_ref[r], axis=0)
    imp = jnp.dot(covt_ref[...], psum, preferred_element_type=F32,
                  precision=lax.Precision.HIGHEST)
    jb = lax.broadcasted_iota(I32, (nslc, BQ), 0)
    qpos = qi * BQ + lax.broadcasted_iota(I32, (nslc, BQ), 1)
    qblk = lax.shift_right_logical(qpos, int(math.log2(SLC_LEN)))
    forced = (jb == 0) | (jb == qblk) | (jb == qblk - 1)
    imp = jnp.where(forced, FORCE, jnp.where(jb > qblk, -FORCE, imp))
    beaten = jnp.zeros((nslc, BQ), F32)
    for jp in range(nslc):
        row = imp[jp:jp + 1, :]
        wins = (row > imp) | ((row == imp) & (jp < jb))
        beaten = beaten + wins.astype(F32)
    selt = (beaten < SLC_TOPN).astype(BF16)
    eye = (lax.broadcasted_iota(I32, (BQ, BQ), 0) == lax.broadcasted_iota(I32, (BQ, BQ), 1)).astype(BF16)
    selq = _nt(eye, selt).astype(BF16)
    for t in range(nkt):
        kblk = (t * BK + lax.broadcasted_iota(I32, (nslc, BK), 1)) // SLC_LEN
        expand = (kblk == lax.broadcasted_iota(I32, (nslc, BK), 0)).astype(BF16)
        mask_ref[t] = jnp.dot(selq, expand, preferred_element_type=F32)

    for r in range(NSA_GROUP):
        q = q_ref[0, :, r * LANE:(r + 1) * LANE]

        def score_sel(kj, s, r=r):
            return jnp.where(mask_ref[kj] > 0.5, s + tzs_ref[r, qi - kj], NEG)

        def score_win(kj, s, r=r):
            return s + tzw_ref[r, qi - kj]

        o_sel = _flash(q, ks_ref, vs_ref, 0, qi + 1, score_sel, scale)
        nwin = tzw_ref.shape[1]
        o_win = _flash(q, kw_ref, vw_ref, jnp.maximum(qi - (nwin - 1), 0), qi + 1, score_win, scale)
        o_ref[0, :, r * LANE:(r + 1) * LANE] += (gates[:, 3 * r + 1:3 * r + 2] * o_sel
                                                 + gates[:, 3 * r + 2:3 * r + 3] * o_win)


def _nsa(proj, gates, kcmp, vcmp, bcq, bct, covt, tzs, tzw):
    b, s, _ = proj.shape
    nd = s // BK
    ncmp = kcmp.shape[2]
    nslc = covt.shape[0]
    nwin = tzw.shape[1]
    gw = NSA_GROUP * LANE
    kv = lambda cb: pl.BlockSpec((1, s, LANE), lambda bi, g, qi: (bi, 0, cb + g))
    return pl.pallas_call(
        _nsa_body,
        grid=(b, NSA_KV, s // BQ),
        in_specs=[pl.BlockSpec((1, BQ, gw), lambda bi, g, qi: (bi, qi, CB_QB // NSA_GROUP + g)),
                  pl.BlockSpec((1, 1, ncmp, HEAD_DIM), lambda bi, g, qi: (bi, g, 0, 0)),
                  pl.BlockSpec((1, 1, ncmp, HEAD_DIM), lambda bi, g, qi: (bi, g, 0, 0)),
                  kv(CB_KS), kv(CB_VS), kv(CB_KW), kv(CB_VW),
                  pl.BlockSpec((1, BQ, LANE), lambda bi, g, qi: (bi, qi, g)),
                  pl.BlockSpec((NSA_GROUP, BQ, ncmp), lambda bi, g, qi: (g, qi, 0)),
                  pl.BlockSpec((NSA_GROUP, ncmp, BQ), lambda bi, g, qi: (g, 0, qi)),
                  pl.BlockSpec((nslc, ncmp), lambda bi, g, qi: (0, 0)),
                  pl.BlockSpec((NSA_GROUP, nd, BQ, BK), lambda bi, g, qi: (g, 0, 0, 0)),
                  pl.BlockSpec((NSA_GROUP, nwin, BQ, BK), lambda bi, g, qi: (g, 0, 0, 0))],
        out_specs=pl.BlockSpec((1, BQ, gw), lambda bi, g, qi: (bi, qi, g)),
        out_shape=jax.ShapeDtypeStruct((b, s, B_HEADS * LANE), F32),
        scratch_shapes=[pltpu.VMEM((nd, BQ, BK), F32)],
        compiler_params=_params("parallel", "parallel", "arbitrary"),
        name="nsa_attention",
    )(proj, kcmp, vcmp, proj, proj, proj, proj, gates, bcq, bct, covt, tzs, tzw)


def _moba_body(q_ref, k_ref, v_ref, avg_ref, tz_ref, o_ref, mask_ref):
    qi = pl.program_id(2)
    scale = HEAD_DIM ** -0.5
    nblk = avg_ref.shape[0]
    q = q_ref[0]
    kmean = jnp.dot(avg_ref[...], k_ref[0], preferred_element_type=F32)
    gate = _nt(kmean, q.astype(F32), precision=lax.Precision.HIGHEST)
    nb = lax.broadcasted_iota(I32, (nblk, BQ), 0)
    past = nb < qi
    beaten = jnp.zeros((nblk, BQ), F32)
    for np_ in range(nblk):
        row = gate[np_:np_ + 1, :]
        wins = (np_ < qi) & ((row > gate) | ((row == gate) & (np_ < nb)))
        beaten = beaten + wins.astype(F32)
    selt = (past & (beaten < MOBA_TOPK)).astype(BF16)
    eye = (lax.broadcasted_iota(I32, (BQ, BQ), 0) == lax.broadcasted_iota(I32, (BQ, BQ), 1)).astype(BF16)
    selq = _nt(eye, selt)
    for n in range(nblk):
        mask_ref[n] = jnp.broadcast_to(selq[:, n:n + 1], (BQ, BK))

    def score(kj, s):
        keep = (mask_ref[kj] > 0.5) | (kj == qi)
        return jnp.where(keep, s + tz_ref[0, qi - kj], NEG)

    o_ref[0] = _flash(q, k_ref, v_ref, 0, qi + 1, score, scale)


def _moba(proj, avg, tzc):
    b, s, _ = proj.shape
    nd = s // BK
    nblk = avg.shape[0]
    return pl.pallas_call(
        _moba_body,
        grid=(b, C_HEADS, s // BQ),
        in_specs=[pl.BlockSpec((1, BQ, LANE), lambda bi, h, qi: (bi, qi, CB_QC + h)),
                  pl.BlockSpec((1, s, LANE), lambda bi, h, qi: (bi, 0, CB_KCC + h)),
                  pl.BlockSpec((1, s, LANE), lambda bi, h, qi: (bi, 0, CB_VCC + h)),
                  pl.BlockSpec((nblk, s), lambda bi, h, qi: (0, 0)),
                  pl.BlockSpec((1, nd, BQ, BK), lambda bi, h, qi: (h, 0, 0, 0))],
        out_specs=pl.BlockSpec((1, BQ, LANE), lambda bi, h, qi: (bi, qi, h)),
        out_shape=jax.ShapeDtypeStruct((b, s, C_HEADS * LANE), F32),
        scratch_shapes=[pltpu.VMEM((nblk, BQ, BK), F32)],
        compiler_params=_params("parallel", "parallel", "arbitrary"),
        name="moba_attention",
    )(proj, proj, proj, avg, tzc)


def _outproj_body(oa_ref, ob_ref, oc_ref, g_ref, w_ref, h_ref, o_ref, y_ref):
    @pl.when(pl.program_id(1) == 0)
    def _():
        c0 = 0
        for ref in (oa_ref, ob_ref, oc_ref):
            x = ref[...]
            wd = x.shape[1]
            ms = jnp.mean(x * x, axis=-1, keepdims=True)
            y_ref[:, c0:c0 + wd] = (x * lax.rsqrt(ms + EPS) * g_ref[:, c0:c0 + wd]).astype(BF16)
            c0 += wd

    o_ref[...] = h_ref[...] + jnp.dot(y_ref[...], w_ref[...], preferred_element_type=F32)


def _outproj(oa, ob, oc, g, w, h, tm, tn):
    t, d = h.shape
    row = lambda a: pl.BlockSpec((tm, a.shape[1]), lambda i, j: (i, 0))
    return pl.pallas_call(
        _outproj_body,
        grid=(t // tm, d // tn),
        in_specs=[row(oa), row(ob), row(oc),
                  pl.BlockSpec((1, d), lambda i, j: (0, 0)),
                  pl.BlockSpec((d, tn), lambda i, j: (0, j)),
                  pl.BlockSpec((tm, tn), lambda i, j: (i, j))],
        out_specs=pl.BlockSpec((tm, tn), lambda i, j: (i, j)),
        out_shape=jax.ShapeDtypeStruct((t, d), F32),
        scratch_shapes=[pltpu.VMEM((tm, d), BF16)],
        compiler_params=_params("parallel", "arbitrary"),
        name="out_projection",
    )(oa, ob, oc, g.reshape(1, d), w, h)


def _topk_rows(x, k, payload=None):
    n, tm = x.shape
    rows = lax.broadcasted_iota(I32, x.shape, 0)
    slot = lax.broadcasted_iota(I32, (k, tm), 0)
    vals = jnp.zeros((k, tm), F32)
    idxs = jnp.zeros((k, tm), I32)
    for it in range(k):
        mx = jnp.max(x, axis=0, keepdims=True)
        idx = jnp.min(jnp.where(x == mx, rows, n), axis=0, keepdims=True)
        hit = rows == idx
        if payload is not None:
            idx = jnp.sum(jnp.where(hit, payload, 0), axis=0, keepdims=True)
        vals = jnp.where(slot == it, mx, vals)
        idxs = jnp.where(slot == it, idx, idxs)
        x = jnp.where(hit, -jnp.inf, x)
    return vals, idxs


def _peer_topk_body(q_ref, keys_ref, e_ref, g_ref):
    for h in range(PEER_HEADS):
        tops = []
        for c in range(2):
            hc = 2 * h + c
            sc = _nt(keys_ref[hc], q_ref[:, hc * LANE:(hc + 1) * LANE])
            tops.append(_topk_rows(sc, PEER_TOPK))
        (s0, i0), (s1, i1) = tops
        cand = jnp.concatenate([s0[a:a + 1] + s1 for a in range(PEER_TOPK)], axis=0)
        cand_i = jnp.concatenate([i0[a:a + 1] * PEER_NKEYS + i1 for a in range(PEER_TOPK)], axis=0)
        bs, be = _topk_rows(cand, PEER_TOPK, payload=cand_i)
        e = jnp.exp(bs - jnp.max(bs, axis=0, keepdims=True))
        g_ref[h * PEER_TOPK:(h + 1) * PEER_TOPK, :] = e / jnp.sum(e, axis=0, keepdims=True)
        e_ref[h * PEER_TOPK:(h + 1) * PEER_TOPK, :] = be


def _peer_topk(qp, keys, tm):
    t, d = qp.shape
    hk = PEER_HEADS * PEER_TOPK
    return pl.pallas_call(
        _peer_topk_body,
        grid=(t // tm,),
        in_specs=[pl.BlockSpec((tm, d), lambda i: (i, 0)),
                  pl.BlockSpec(keys.shape, lambda i: (0, 0, 0))],
        out_specs=[pl.BlockSpec((hk, tm), lambda i: (0, i)),
                   pl.BlockSpec((hk, tm), lambda i: (0, i))],
        out_shape=[jax.ShapeDtypeStruct((hk, t), I32), jax.ShapeDtypeStruct((hk, t), F32)],
        compiler_params=_params("parallel"),
        name="peer_topk",
    )(qp, keys)


PEER_NBUF = 4


def _gelu(x):
    return 0.5 * x * (1.0 + lax.erf(x * (2.0 ** -0.5)))


def _split_rows(row, nrows):
    hi = row.astype(BF16).astype(F32)
    lo = row - hi
    r = lax.broadcasted_iota(I32, (nrows, row.shape[1]), 0)
    stacked = jnp.where(r == 0, jnp.broadcast_to(hi, r.shape),
                        jnp.where(r == 1, jnp.broadcast_to(lo, r.shape), 0.0))
    return stacked.astype(BF16)


def _peer_expert_body(ids_ref, h_ref, ln_ref, g2_ref, uvw_ref, o_ref, xn_ref, buf_ref, sem_ref):
    tb = h_ref.shape[0]
    hk = ids_ref.shape[1]
    x = h_ref[...]
    ms = jnp.mean(x * x, axis=-1, keepdims=True)
    xn_ref[...] = x * lax.rsqrt(ms + EPS) * ln_ref[...]

    def row_copy(t, k, slot):
        e = ids_ref[t, k]
        return pltpu.make_async_copy(uvw_ref.at[pl.ds(e, 1), :], buf_ref.at[slot, pl.ds(k, 1), :],
                                     sem_ref.at[slot])

    def issue(t, slot):
        def one(k, c):
            row_copy(t, k, slot).start()
            return c
        lax.fori_loop(0, hk, one, 0, unroll=8)

    def drain(t, slot):
        def one(k, c):
            row_copy(t, k, slot).wait()
            return c
        lax.fori_loop(0, hk, one, 0, unroll=8)

    for t0 in range(PEER_NBUF - 1):
        issue(t0, t0)

    def token(t, c):
        slot = lax.rem(t, PEER_NBUF)

        @pl.when(t + PEER_NBUF - 1 < tb)
        def _():
            issue(t + PEER_NBUF - 1, lax.rem(t + PEER_NBUF - 1, PEER_NBUF))

        drain(t, slot)
        z = pltpu.bitcast(buf_ref[slot], BF16)
        act = _nt(_split_rows(xn_ref[pl.ds(t, 1), :], 16), z)
        act = jnp.broadcast_to(act[0:1, :] + act[1:2, :], (8, 2 * hk))
        coef = pltpu.roll(_gelu(act), 1, 1) * g2_ref[pl.ds(t, 1), :]
        y = jnp.dot(_split_rows(coef[0:1, :], 16), z, preferred_element_type=F32)
        o_ref[pl.ds(t, 1), :] = h_ref[pl.ds(t, 1), :] + y[0:1, :] + y[1:2, :]
        return c

    lax.fori_loop(0, tb, token, 0)


def _peer_experts(ids, h, ln, g2, uvw, tb):
    t, d = h.shape
    hk = ids.shape[1]
    return pl.pallas_call(
        _peer_expert_body,
        grid=(t // tb,),
        in_specs=[pl.BlockSpec((tb, hk), lambda i: (i, 0), memory_space=pltpu.SMEM),
                  pl.BlockSpec((tb, d), lambda i: (i, 0)),
                  pl.BlockSpec((1, d), lambda i: (0, 0)),
                  pl.BlockSpec((tb, 2 * hk), lambda i: (i, 0)),
                  pl.BlockSpec(memory_space=pl.ANY)],
        out_specs=pl.BlockSpec((tb, d), lambda i: (i, 0)),
        out_shape=jax.ShapeDtypeStruct((t, d), F32),
        scratch_shapes=[pltpu.VMEM((tb, d), F32),
                        pltpu.VMEM((PEER_NBUF, hk, d), jnp.uint32),
                        pltpu.SemaphoreType.DMA((PEER_NBUF,))],
        compiler_params=_params("arbitrary"),
        name="peer_experts",
    )(ids, h, ln.reshape(1, d), g2, uvw)


def _ple_body(x_ref, g_ref, wg_ref, p_ref, wp_ref, h_ref, o_ref, xn_ref):
    @pl.when(pl.program_id(1) == 0)
    def _():
        x = x_ref[...]
        ms = jnp.mean(x * x, axis=-1, keepdims=True)
        xn_ref[...] = (x * lax.rsqrt(ms + EPS) * g_ref[...]).astype(BF16)

    z = jnp.dot(xn_ref[...], wg_ref[...], preferred_element_type=F32)
    pp = jnp.dot(p_ref[...].astype(BF16), wp_ref[...], preferred_element_type=F32)
    o_ref[...] = h_ref[...] + jax.nn.sigmoid(z) * pp


def _ple(h, g, wg, p, wp, tm, tn):
    t, d = h.shape
    pd = p.shape[1]
    return pl.pallas_call(
        _ple_body,
        grid=(t // tm, d // tn),
        in_specs=[pl.BlockSpec((tm, d), lambda i, j: (i, 0)),
                  pl.BlockSpec((1, d), lambda i, j: (0, 0)),
                  pl.BlockSpec((d, tn), lambda i, j: (0, j)),
                  pl.BlockSpec((tm, pd), lambda i, j: (i, 0)),
                  pl.BlockSpec((pd, tn), lambda i, j: (0, j)),
                  pl.BlockSpec((tm, tn), lambda i, j: (i, j))],
        out_specs=pl.BlockSpec((tm, tn), lambda i, j: (i, j)),
        out_shape=jax.ShapeDtypeStruct((t, d), F32),
        scratch_shapes=[pltpu.VMEM((tm, d), BF16)],
        compiler_params=_params("parallel", "arbitrary"),
        name="ple_gate",
    )(h, g.reshape(1, d), wg, p, wp, h)


def _rmsnorm_body(x_ref, g_ref, o_ref):
    x = x_ref[...]
    ms = jnp.mean(x * x, axis=-1, keepdims=True)
    o_ref[...] = x * lax.rsqrt(ms + EPS) * g_ref[...]


def _rmsnorm(x, g, tm):
    t, d = x.shape
    return pl.pallas_call(
        _rmsnorm_body,
        grid=(t // tm,),
        in_specs=[pl.BlockSpec((tm, d), lambda i: (i, 0)), pl.BlockSpec((1, d), lambda i: (0, 0))],
        out_specs=pl.BlockSpec((tm, d), lambda i: (i, 0)),
        out_shape=jax.ShapeDtypeStruct((t, d), F32),
        compiler_params=_params("parallel"),
        name="final_rmsnorm",
    )(x, g.reshape(1, d))


def _bias_tables(rel_bias, s):
    nd = s // BK
    bd = rel_bias[_rel_bucket(jnp.arange(s))].astype(F32)
    dd = (jnp.arange(nd)[:, None, None] * BK + jnp.arange(BQ)[None, :, None]
          - jnp.arange(BK)[None, None, :])
    tz = jnp.transpose(bd[jnp.clip(dd, 0, s - 1)], (3, 0, 1, 2))
    causal = dd >= 0
    mult = sum(((dd % dil == 0) & (dd // dil <= window // dil)).astype(F32) for window, dil in DIL_PATTERNS)
    ok = causal & (mult > 0)
    tzd = jnp.where(ok, tz[:A_HEADS] + jnp.log(jnp.where(ok, mult, 1.0)), NEG)
    tzb = tz[A_HEADS:A_HEADS + B_HEADS]
    tzs = jnp.where(causal, tzb, NEG)
    nwin = -(-(NSA_WINDOW - 1) // BK) + 1
    tzw = jnp.where(causal & (dd <= NSA_WINDOW - 1), tzb, NEG)[:, :nwin]
    tzc = jnp.where(causal, tz[A_HEADS + B_HEADS:], NEG)
    ncmp = s // CMP_STRIDE
    dist_c = jnp.arange(s)[:, None] - (jnp.arange(ncmp) * CMP_STRIDE + CMP_LEN - 1)[None, :]
    bc = rel_bias[_rel_bucket(dist_c)].astype(F32)[:, :, A_HEADS:A_HEADS + B_HEADS]
    bc = jnp.where((dist_c >= 0)[:, :, None], bc, NEG)
    bcq = jnp.transpose(bc, (2, 0, 1))
    bct = jnp.transpose(bc, (2, 1, 0))
    nslc = s // SLC_LEN
    cstart = jnp.arange(ncmp) * CMP_STRIDE
    sstart = jnp.arange(nslc) * SLC_LEN
    covt = ((cstart[None, :] < sstart[:, None] + SLC_LEN)
            & (cstart[None, :] + CMP_LEN > sstart[:, None])).astype(F32)
    nblk = s // MOBA_BLK
    avg = ((jnp.arange(s)[None, :] // MOBA_BLK == jnp.arange(nblk)[:, None]).astype(F32)
           / MOBA_BLK).astype(BF16)
    return tzd, tzs, tzw, tzc, bcq, bct, covt, avg


def _pack_uv(u, v):
    ub = lax.bitcast_convert_type(u.astype(BF16), jnp.uint16).astype(jnp.uint32)
    vb = lax.bitcast_convert_type(v.astype(BF16), jnp.uint16).astype(jnp.uint32)
    return ub | (vb << 16)


def _reorder_w_in(w):
    g0 = 3 * A_HEADS * LANE + B_HEADS * LANE + 6 * NSA_KV * LANE
    ng = 3 * B_HEADS
    main = jnp.concatenate([w[:, :g0], w[:, g0 + ng:]], axis=1)
    per = ng // NSA_KV
    gate = jnp.concatenate(
        [jnp.pad(w[:, g0 + g * per:g0 + (g + 1) * per], ((0, 0), (0, LANE - per))) for g in range(NSA_KV)],
        axis=1)
    return main.astype(BF16), gate.astype(BF16)


def kernel(x, p, ln_mix, w_in, cmp_wk, cmp_wv, cmp_pos, out_norm, w_out, rel_bias, ln_ffn, peer_wq,
           peer_keys, peer_u, peer_v, ln_ple, ple_gate, ple_proj, ln_final):
    b, s, d = x.shape
    t = b * s
    depth = w_in.shape[0]
    assert s % BQ == 0 and BQ == BK == MOBA_BLK and d % LANE == 0
    tm = 512
    tzd, tzs, tzw, tzc, bcq, bct, covt, avg = _bias_tables(rel_bias, s)
    h = x.reshape(t, d)
    for i in range(depth):
        w_main, w_gate = _reorder_w_in(w_in[i])
        proj = _normmm(h, ln_mix[i], w_main, BF16, tm, 1024, "in_projection").reshape(b, s, MAIN_COLS)
        gates = _normmm(h, ln_mix[i], w_gate, F32, tm, w_gate.shape[1], "gate_projection")
        gates = gates.reshape(b, s, NSA_KV * LANE)
        oa = _dilated(proj, tzd)
        kcmp, vcmp = _compress(proj, cmp_wk[i], cmp_wv[i], cmp_pos[i])
        ob = _nsa(proj, gates, kcmp, vcmp, bcq, bct, covt, tzs, tzw)
        oc = _moba(proj, avg, tzc)
        h = _outproj(oa.reshape(t, -1), ob.reshape(t, -1), oc.reshape(t, -1), out_norm[i],
                     w_out[i].astype(BF16), h, tm, 1024)
        qp = _normmm(h, ln_ffn[i], peer_wq[i].astype(BF16), F32, tm, 1024, "peer_query")
        keys = peer_keys[i].reshape(PEER_HEADS * 2, PEER_NKEYS, -1)
        e_t, g_t = _peer_topk(qp, keys, 256)
        ids = e_t.T
        g2 = jnp.stack([jnp.zeros_like(g_t.T), g_t.T], axis=-1).reshape(t, -1)
        h = _peer_experts(ids, h, ln_ffn[i], g2, _pack_uv(peer_u[i], peer_v[i]), 128)
        h = _ple(h, ln_ple[i], ple_gate[i].astype(BF16), p[i].reshape(t, -1), ple_proj[i].astype(BF16),
                 tm, 1024)
    return _rmsnorm(h, ln_final, tm).reshape(b, s, d)
```

```python
import functools
import math

import numpy as np
import jax
import jax.numpy as jnp
from jax import lax
from jax.experimental import pallas as pl
from jax.experimental.pallas import tpu as pltpu

F32 = jnp.float32
BF16 = jnp.bfloat16
I32 = jnp.int32

HEAD_DIM = 128
A_HEADS, B_HEADS, C_HEADS = 6, 6, 4
NSA_KV, NSA_GROUP = 2, 3
DIL_PATTERNS = ((128, 1), (512, 4), (2048, 16))
CMP_LEN, CMP_STRIDE = 32, 16
SLC_LEN, SLC_TOPN = 64, 8
NSA_WINDOW = 512
FORCE = 1e4
MOBA_BLK, MOBA_TOPK = 256, 3
REL_BUCKETS, REL_MAX_DIST = 32, 1024
PEER_HEADS, PEER_NKEYS, PEER_TOPK = 8, 128, 16
EPS = 1e-6

LANE = 128
BQ = 256
BK = 256
NEG = -1e30
HALF_NEG = -5e29
VMEM_LIMIT = 56 * 1024 * 1024

CB_QA, CB_KA, CB_VA = 0, 6, 12
CB_QB, CB_KC, CB_VC, CB_KS, CB_VS, CB_KW, CB_VW = 18, 24, 26, 28, 30, 32, 34
CB_QC, CB_KCC, CB_VCC = 36, 40, 44
MAIN_COLS = 48 * LANE


def _nt(a, b, precision=None):
    return lax.dot_general(a, b, (((1,), (1,)), ((), ())), preferred_element_type=F32,
                           precision=precision)


def _rel_bucket(dist):
    exact = REL_BUCKETS // 2
    d = jnp.maximum(dist, 0)
    logd = jnp.log(jnp.maximum(d, 1).astype(F32) / exact)
    large = exact + (logd / math.log(REL_MAX_DIST / exact) * (REL_BUCKETS - exact)).astype(I32)
    large = jnp.clip(large, exact, REL_BUCKETS - 1)
    return jnp.where(d < exact, d, large)


def _masked_softmax(s, axis):
    valid = s > HALF_NEG
    m = jnp.max(s, axis=axis, keepdims=True)
    e = jnp.where(valid, jnp.exp(s - m), 0.0)
    z = jnp.sum(e, axis=axis, keepdims=True)
    zs = jnp.where(z > 0, z, 1.0)
    return e / zs


def _params(*sem):
    return pltpu.CompilerParams(dimension_semantics=sem, vmem_limit_bytes=VMEM_LIMIT)


def _normmm_body(x_ref, g_ref, w_ref, o_ref, xn_ref):
    @pl.when(pl.program_id(1) == 0)
    def _():
        x = x_ref[...]
        ms = jnp.mean(x * x, axis=-1, keepdims=True)
        xn_ref[...] = (x * lax.rsqrt(ms + EPS) * g_ref[...]).astype(BF16)

    o_ref[...] = jnp.dot(xn_ref[...], w_ref[...], preferred_element_type=F32).astype(o_ref.dtype)


def _normmm(x, g, w, out_dtype, tm, tn, name):
    t, d = x.shape
    n = w.shape[1]
    return pl.pallas_call(
        _normmm_body,
        grid=(t // tm, n // tn),
        in_specs=[pl.BlockSpec((tm, d), lambda i, j: (i, 0)),
                  pl.BlockSpec((1, d), lambda i, j: (0, 0)),
                  pl.BlockSpec((d, tn), lambda i, j: (0, j))],
        out_specs=pl.BlockSpec((tm, tn), lambda i, j: (i, j)),
        out_shape=jax.ShapeDtypeStruct((t, n), out_dtype),
        scratch_shapes=[pltpu.VMEM((tm, d), BF16)],
        compiler_params=_params("parallel", "arbitrary"),
        name=name,
    )(x, g.reshape(1, d), w)


def _flash(q, k_ref, v_ref, lo, hi, score_fn, scale):
    def body(kj, carry):
        m, l, acc = carry
        off = pl.multiple_of(kj * BK, BK)
        k = k_ref[0, pl.ds(off, BK), :]
        v = v_ref[0, pl.ds(off, BK), :]
        s = score_fn(kj, _nt(q, k) * scale)
        m_new = jnp.maximum(m, jnp.max(s, axis=1, keepdims=True))
        alpha = jnp.exp(m - m_new)
        p = jnp.exp(s - m_new)
        l = alpha * l + jnp.sum(p, axis=1, keepdims=True)
        acc = alpha * acc + jnp.dot(p.astype(BF16), v, preferred_element_type=F32)
        return m_new, l, acc

    init = (jnp.full((BQ, 1), NEG, F32), jnp.zeros((BQ, 1), F32), jnp.zeros((BQ, HEAD_DIM), F32))
    _, l, acc = lax.fori_loop(lo, hi, body, init)
    return acc / l


def _dilated_body(q_ref, k_ref, v_ref, tz_ref, o_ref):
    qi = pl.program_id(2)
    scale = HEAD_DIM ** -0.5

    def score(kj, s):
        return s + tz_ref[0, qi - kj]

    o_ref[0] = _flash(q_ref[0], k_ref, v_ref, 0, qi + 1, score, scale)


def _dilated(proj, tzd):
    b, s, _ = proj.shape
    nd = s // BK
    return pl.pallas_call(
        _dilated_body,
        grid=(b, A_HEADS, s // BQ),
        in_specs=[pl.BlockSpec((1, BQ, LANE), lambda bi, h, qi: (bi, qi, CB_QA + h)),
                  pl.BlockSpec((1, s, LANE), lambda bi, h, qi: (bi, 0, CB_KA + h)),
                  pl.BlockSpec((1, s, LANE), lambda bi, h, qi: (bi, 0, CB_VA + h)),
                  pl.BlockSpec((1, nd, BQ, BK), lambda bi, h, qi: (h, 0, 0, 0))],
        out_specs=pl.BlockSpec((1, BQ, LANE), lambda bi, h, qi: (bi, qi, h)),
        out_shape=jax.ShapeDtypeStruct((b, s, A_HEADS * LANE), F32),
        compiler_params=_params("parallel", "parallel", "arbitrary"),
        name="dilated_attention",
    )(proj, proj, proj, tzd)


def _compress_body(kc_ref, vc_ref, wk_ref, wv_ref, pos_ref, ko_ref, vo_ref, xk_ref, xv_ref):
    s = kc_ref.shape[1]
    ncmp = ko_ref.shape[2]
    xk_ref[pl.ds(0, s), :] = kc_ref[0].astype(F32)
    xv_ref[pl.ds(0, s), :] = vc_ref[0].astype(F32)
    xk_ref[pl.ds(s, CMP_LEN), :] = jnp.zeros((CMP_LEN, HEAD_DIM), F32)
    xv_ref[pl.ds(s, CMP_LEN), :] = jnp.zeros((CMP_LEN, HEAD_DIM), F32)
    acck = jnp.zeros((ncmp, HEAD_DIM), F32)
    accv = jnp.zeros((ncmp, HEAD_DIM), F32)
    for l in range(CMP_LEN):
        pos = pos_ref[pl.ds(l, 1), :]
        rk = (xk_ref[pl.ds(l, ncmp, stride=CMP_STRIDE), :] + pos).astype(BF16)
        rv = (xv_ref[pl.ds(l, ncmp, stride=CMP_STRIDE), :] + pos).astype(BF16)
        acck = acck + jnp.dot(rk, wk_ref[l], preferred_element_type=F32)
        accv = accv + jnp.dot(rv, wv_ref[l], preferred_element_type=F32)
    ko_ref[0, 0] = acck.astype(BF16)
    vo_ref[0, 0] = accv.astype(BF16)


def _compress(proj, cmp_wk, cmp_wv, cmp_pos):
    b, s, _ = proj.shape
    ncmp = s // CMP_STRIDE
    out = jax.ShapeDtypeStruct((b, NSA_KV, ncmp, HEAD_DIM), BF16)
    return pl.pallas_call(
        _compress_body,
        grid=(b, NSA_KV),
        in_specs=[pl.BlockSpec((1, s, LANE), lambda bi, g: (bi, 0, CB_KC + g)),
                  pl.BlockSpec((1, s, LANE), lambda bi, g: (bi, 0, CB_VC + g)),
                  pl.BlockSpec((CMP_LEN, HEAD_DIM, HEAD_DIM), lambda bi, g: (0, 0, 0)),
                  pl.BlockSpec((CMP_LEN, HEAD_DIM, HEAD_DIM), lambda bi, g: (0, 0, 0)),
                  pl.BlockSpec((CMP_LEN, HEAD_DIM), lambda bi, g: (0, 0))],
        out_specs=[pl.BlockSpec((1, 1, ncmp, HEAD_DIM), lambda bi, g: (bi, g, 0, 0)),
                   pl.BlockSpec((1, 1, ncmp, HEAD_DIM), lambda bi, g: (bi, g, 0, 0))],
        out_shape=[out, out],
        scratch_shapes=[pltpu.VMEM((s + CMP_LEN, HEAD_DIM), F32),
                        pltpu.VMEM((s + CMP_LEN, HEAD_DIM), F32)],
        compiler_params=_params("parallel", "parallel"),
        name="nsa_compress",
    )(proj, proj, cmp_wk.astype(BF16), cmp_wv.astype(BF16), cmp_pos)


def _nsa_body(q_ref, kcmp_ref, vcmp_ref, ks_ref, vs_ref, kw_ref, vw_ref, gt_ref, bcq_ref, bct_ref,
              covt_ref, tzs_ref, tzw_ref, o_ref, mask_ref):
    qi = pl.program_id(2)
    scale = HEAD_DIM ** -0.5
    nslc = covt_ref.shape[0]
    nkt = mask_ref.shape[0]
    kcmp = kcmp_ref[0, 0]
    vcmp = vcmp_ref[0, 0]
    gates = jax.nn.sigmoid(gt_ref[0])

    psum = jnp.zeros((kcmp.shape[0], BQ), F32)
    for r in range(NSA_GROUP):
        q = q_ref[0, :, r * LANE:(r + 1) * LANE]
        p = _masked_softmax(_nt(q, kcmp) * scale + bcq_ref[r], axis=1)
        o_cmp = jnp.dot(p.astype(BF16), vcmp, preferred_element_type=F32)
        o_ref[0, :, r * LANE:(r + 1) * LANE] = gates[:, 3 * r:3 * r + 1] * o_cmp
        psum = psum + _masked_softmax(_nt(kcmp, q) * scale + bct_ref[r], axis=0)
    imp = jnp.dot(covt_ref[...], psum, preferred_element_type=F32,
                  precision=lax.Precision.HIGHEST)
    jb = lax.broadcasted_iota(I32, (nslc, BQ), 0)
    qpos = qi * BQ + lax.broadcasted_iota(I32, (nslc, BQ), 1)
    qblk = lax.shift_right_logical(qpos, int(math.log2(SLC_LEN)))
    forced = (jb == 0) | (jb == qblk) | (jb == qblk - 1)
    imp = jnp.where(forced, FORCE, jnp.where(jb > qblk, -FORCE, imp))
    beaten = jnp.zeros((nslc, BQ), F32)
    for jp in range(nslc):
        row = imp[jp:jp + 1, :]
        wins = (row > imp) | ((row == imp) & (jp < jb))
        beaten = beaten + wins.astype(F32)
    selt = (beaten < SLC_TOPN).astype(BF16)
    eye = (lax.broadcasted_iota(I32, (BQ, BQ), 0) == lax.broadcasted_iota(I32, (BQ, BQ), 1)).astype(BF16)
    selq = _nt(eye, selt).astype(BF16)
    for t in range(nkt):
        kblk = (t * BK + lax.broadcasted_iota(I32, (nslc, BK), 1)) // SLC_LEN
        expand = (kblk == lax.broadcasted_iota(I32, (nslc, BK), 0)).astype(BF16)
        mask_ref[t] = jnp.dot(selq, expand, preferred_element_type=F32)

    for r in range(NSA_GROUP):
        q = q_ref[0, :, r * LANE:(r + 1) * LANE]

        def score_sel(kj, s, r=r):
            return jnp.where(mask_ref[kj] > 0.5, s + tzs_ref[r, qi - kj], NEG)

        def score_win(kj, s, r=r):
            return s + tzw_ref[r, qi - kj]

        o_sel = _flash(q, ks_ref, vs_ref, 0, qi + 1, score_sel, scale)
        nwin = tzw_ref.shape[1]
        o_win = _flash(q, kw_ref, vw_ref, jnp.maximum(qi - (nwin - 1), 0), qi + 1, score_win, scale)
        o_ref[0, :, r * LANE:(r + 1) * LANE] += (gates[:, 3 * r + 1:3 * r + 2] * o_sel
                                                 + gates[:, 3 * r + 2:3 * r + 3] * o_win)


def _nsa(proj, gates, kcmp, vcmp, bcq, bct, covt, tzs, tzw):
    b, s, _ = proj.shape
    nd = s // BK
    ncmp = kcmp.shape[2]
    nslc = covt.shape[0]
    nwin = tzw.shape[1]
    gw = NSA_GROUP * LANE
    kv = lambda cb: pl.BlockSpec((1, s, LANE), lambda bi, g, qi: (bi, 0, cb + g))
    return pl.pallas_call(
        _nsa_body,
        grid=(b, NSA_KV, s // BQ),
        in_specs=[pl.BlockSpec((1, BQ, gw), lambda bi, g, qi: (bi, qi, CB_QB // NSA_GROUP + g)),
                  pl.BlockSpec((1, 1, ncmp, HEAD_DIM), lambda bi, g, qi: (bi, g, 0, 0)),
                  pl.BlockSpec((1, 1, ncmp, HEAD_DIM), lambda bi, g, qi: (bi, g, 0, 0)),
                  kv(CB_KS), kv(CB_VS), kv(CB_KW), kv(CB_VW),
                  pl.BlockSpec((1, BQ, LANE), lambda bi, g, qi: (bi, qi, g)),
                  pl.BlockSpec((NSA_GROUP, BQ, ncmp), lambda bi, g, qi: (g, qi, 0)),
                  pl.BlockSpec((NSA_GROUP, ncmp, BQ), lambda bi, g, qi: (g, 0, qi)),
                  pl.BlockSpec((nslc, ncmp), lambda bi, g, qi: (0, 0)),
                  pl.BlockSpec((NSA_GROUP, nd, BQ, BK), lambda bi, g, qi: (g, 0, 0, 0)),
                  pl.BlockSpec((NSA_GROUP, nwin, BQ, BK), lambda bi, g, qi: (g, 0, 0, 0))],
        out_specs=pl.BlockSpec((1, BQ, gw), lambda bi, g, qi: (bi, qi, g)),
        out_shape=jax.ShapeDtypeStruct((b, s, B_HEADS * LANE), F32),
        scratch_shapes=[pltpu.VMEM((nd, BQ, BK), F32)],
        compiler_params=_params("parallel", "parallel", "arbitrary"),
        name="nsa_attention",
    )(proj, kcmp, vcmp, proj, proj, proj, proj, gates, bcq, bct, covt, tzs, tzw)


def _moba_body(q_ref, k_ref, v_ref, avg_ref, tz_ref, o_ref, mask_ref):
    qi = pl.program_id(2)
    scale = HEAD_DIM ** -0.5
    nblk = avg_ref.shape[0]
    q = q_ref[0]
    kmean = jnp.dot(avg_ref[...], k_ref[0], preferred_element_type=F32)
    gate = _nt(kmean, q.astype(F32), precision=lax.Precision.HIGHEST)
    nb = lax.broadcasted_iota(I32, (nblk, BQ), 0)
    past = nb < qi
    beaten = jnp.zeros((nblk, BQ), F32)
    for np_ in range(nblk):
        row = gate[np_:np_ + 1, :]
        wins = (np_ < qi) & ((row > gate) | ((row == gate) & (np_ < nb)))
        beaten = beaten + wins.astype(F32)
    selt = (past & (beaten < MOBA_TOPK)).astype(BF16)
    eye = (lax.broadcasted_iota(I32, (BQ, BQ), 0) == lax.broadcasted_iota(I32, (BQ, BQ), 1)).astype(BF16)
    selq = _nt(eye, selt)
    for n in range(nblk):
        mask_ref[n] = jnp.broadcast_to(selq[:, n:n + 1], (BQ, BK))

    def score(kj, s):
        keep = (mask_ref[kj] > 0.5) | (kj == qi)
        return jnp.where(keep, s + tz_ref[0, qi - kj], NEG)

    o_ref[0] = _flash(q, k_ref, v_ref, 0, qi + 1, score, scale)


def _moba(proj, avg, tzc):
    b, s, _ = proj.shape
    nd = s // BK
    nblk = avg.shape[0]
    return pl.pallas_call(
        _moba_body,
        grid=(b, C_HEADS, s // BQ),
        in_specs=[pl.BlockSpec((1, BQ, LANE), lambda bi, h, qi: (bi, qi, CB_QC + h)),
                  pl.BlockSpec((1, s, LANE), lambda bi, h, qi: (bi, 0, CB_KCC + h)),
                  pl.BlockSpec((1, s, LANE), lambda bi, h, qi: (bi, 0, CB_VCC + h)),
                  pl.BlockSpec((nblk, s), lambda bi, h, qi: (0, 0)),
                  pl.BlockSpec((1, nd, BQ, BK), lambda bi, h, qi: (h, 0, 0, 0))],
        out_specs=pl.BlockSpec((1, BQ, LANE), lambda bi, h, qi: (bi, qi, h)),
        out_shape=jax.ShapeDtypeStruct((b, s, C_HEADS * LANE), F32),
        scratch_shapes=[pltpu.VMEM((nblk, BQ, BK), F32)],
        compiler_params=_params("parallel", "parallel", "arbitrary"),
        name="moba_attention",
    )(proj, proj, proj, avg, tzc)


def _outproj_body(oa_ref, ob_ref, oc_ref, g_ref, w_ref, h_ref, o_ref, y_ref):
    @pl.when(pl.program_id(1) == 0)
    def _():
        c0 = 0
        for ref in (oa_ref, ob_ref, oc_ref):
            x = ref[...]
            wd = x.shape[1]
            ms = jnp.mean(x * x, axis=-1, keepdims=True)
            y_ref[:, c0:c0 + wd] = (x * lax.rsqrt(ms + EPS) * g_ref[:, c0:c0 + wd]).astype(BF16)
            c0 += wd

    o_ref[...] = h_ref[...] + jnp.dot(y_ref[...], w_ref[...], preferred_element_type=F32)


def _outproj(oa, ob, oc, g, w, h, tm, tn):
    t, d = h.shape
    row = lambda a: pl.BlockSpec((tm, a.shape[1]), lambda i, j: (i, 0))
    return pl.pallas_call(
        _outproj_body,
        grid=(t // tm, d // tn),
        in_specs=[row(oa), row(ob), row(oc),
                  pl.BlockSpec((1, d), lambda i, j: (0, 0)),
                  pl.BlockSpec((d, tn), lambda i, j: (0, j)),
                  pl.BlockSpec((tm, tn), lambda i, j: (i, j))],
        out_specs=pl.BlockSpec((tm, tn), lambda i, j: (i, j)),
        out_shape=jax.ShapeDtypeStruct((t, d), F32),
        scratch_shapes=[pltpu.VMEM((tm, d), BF16)],
        compiler_params=_params("parallel", "arbitrary"),
        name="out_projection",
    )(oa, ob, oc, g.reshape(1, d), w, h)


def _topk_rows(x, k, payload=None):
    n, tm = x.shape
    rows = lax.broadcasted_iota(I32, x.shape, 0)
    slot = lax.broadcasted_iota(I32, (k, tm), 0)
    vals = jnp.zeros((k, tm), F32)
    idxs = jnp.zeros((k, tm), I32)
    for it in range(k):
        mx = jnp.max(x, axis=0, keepdims=True)
        idx = jnp.min(jnp.where(x == mx, rows, n), axis=0, keepdims=True)
        hit = rows == idx
        if payload is not None:
            idx = jnp.sum(jnp.where(hit, payload, 0), axis=0, keepdims=True)
        vals = jnp.where(slot == it, mx, vals)
        idxs = jnp.where(slot == it, idx, idxs)
        x = jnp.where(hit, -jnp.inf, x)
    return vals, idxs


def _peer_topk_body(q_ref, keys_ref, e_ref, g_ref):
    for h in range(PEER_HEADS):
        tops = []
        for c in range(2):
            hc = 2 * h + c
            sc = _nt(keys_ref[hc], q_ref[:, hc * LANE:(hc + 1) * LANE])
            tops.append(_topk_rows(sc, PEER_TOPK))
        (s0, i0), (s1, i1) = tops
        cand = jnp.concatenate([s0[a:a + 1] + s1 for a in range(PEER_TOPK)], axis=0)
        cand_i = jnp.concatenate([i0[a:a + 1] * PEER_NKEYS + i1 for a in range(PEER_TOPK)], axis=0)
        bs, be = _topk_rows(cand, PEER_TOPK, payload=cand_i)
        e = jnp.exp(bs - jnp.max(bs, axis=0, keepdims=True))
        g_ref[h * PEER_TOPK:(h + 1) * PEER_TOPK, :] = e / jnp.sum(e, axis=0, keepdims=True)
        e_ref[h * PEER_TOPK:(h + 1) * PEER_TOPK, :] = be


def _peer_topk(qp, keys, tm):
    t, d = qp.shape
    hk = PEER_HEADS * PEER_TOPK
    return pl.pallas_call(
        _peer_topk_body,
        grid=(t // tm,),
        in_specs=[pl.BlockSpec((tm, d), lambda i: (i, 0)),
                  pl.BlockSpec(keys.shape, lambda i: (0, 0, 0))],
        out_specs=[pl.BlockSpec((hk, tm), lambda i: (0, i)),
                   pl.BlockSpec((hk, tm), lambda i: (0, i))],
        out_shape=[jax.ShapeDtypeStruct((hk, t), I32), jax.ShapeDtypeStruct((hk, t), F32)],
        compiler_params=_params("parallel"),
        name="peer_topk",
    )(qp, keys)


PEER_NBUF = 4


def _gelu(x):
    return 0.5 * x * (1.0 + lax.erf(x * (2.0 ** -0.5)))


def _hi_lo(x):
    hi = x.astype(BF16)
    lo = (x - hi.astype(F32)).astype(BF16)
    return jnp.concatenate([hi, lo], axis=1)


def _peer_expert_body(ids0_ref, idsn_ref, h_ref, ln_ref, gt_ref, uvw_ref, o_ref, xn_ref, ghl_ref, buf_ref,
                      sem_ref):
    tb, d = h_ref.shape
    hk = idsn_ref.shape[1]
    nrg = hk // 8
    nlt = d // LANE
    ahead = PEER_NBUF - 1
    step = pl.program_id(0)
    x = h_ref[...]
    ms = jnp.mean(x * x, axis=-1, keepdims=True)
    xn_ref[...] = x * lax.rsqrt(ms + EPS) * ln_ref[...]
    ghl_ref[...] = _hi_lo(gt_ref[...])
    ones = jnp.ones((2 * LANE, LANE), BF16)

    def issue(ids_ref, row, slot):
        for k in range(hk):
            e = ids_ref[row, k]
            pltpu.make_async_copy(uvw_ref.at[e], buf_ref.at[slot, :, k, :],
                                  sem_ref.at[slot]).start(priority=k % 2)

    def wait_all(slot):
        pltpu.make_async_copy(buf_ref.at[slot], buf_ref.at[slot], sem_ref.at[slot]).wait()

    @pl.when(step == 0)
    def _():
        for t0 in range(ahead):
            issue(ids0_ref, t0, t0)

    def compute(t, slot):
        xrow = xn_ref[pl.ds(t, 1), :]
        xb = [jnp.broadcast_to(xrow[:, j * LANE:(j + 1) * LANE], (8, LANE)) for j in range(nlt)]
        parts = []
        for r in range(nrg):
            a = None
            for j in range(nlt):
                w = buf_ref[slot, j, r * 8:(r + 1) * 8, :]
                pr = lax.bitcast_convert_type(lax.shift_left(w, jnp.uint32(16)), F32) * xb[j]
                a = pr if a is None else a + pr
            parts.append(a)
        acc = jnp.concatenate(parts, axis=0)
        act = jnp.dot(_hi_lo(acc), ones, preferred_element_type=F32)
        pick = ((lax.broadcasted_iota(I32, (2 * tb, LANE), 0) & (tb - 1)) == t).astype(BF16)
        gate = jnp.dot(ghl_ref[...], pick, preferred_element_type=F32)
        coef = _gelu(act) * gate
        ys = [None] * nlt
        for r in range(nrg):
            cr = coef[r * 8:(r + 1) * 8, :]
            for j in range(nlt):
                w = buf_ref[slot, j, r * 8:(r + 1) * 8, :]
                pv = lax.bitcast_convert_type(w & jnp.uint32(0xFFFF0000), F32) * cr
                ys[j] = pv if ys[j] is None else ys[j] + pv
        y = jnp.sum(jnp.concatenate(ys, axis=1), axis=0, keepdims=True)
        o_ref[pl.ds(t, 1), :] = h_ref[pl.ds(t, 1), :] + y

    def ring(i, c):
        for slot in range(PEER_NBUF):
            t = i * PEER_NBUF + slot
            issue(idsn_ref, t, (slot + ahead) % PEER_NBUF)
            wait_all(slot)
            compute(t, slot)
        return c

    lax.fori_loop(0, tb // PEER_NBUF, ring, 0)

    @pl.when(step == pl.num_programs(0) - 1)
    def _():
        for t0 in range(ahead):
            wait_all((tb + t0) % PEER_NBUF)


def _peer_experts(ids, h, ln, g_t, uvw, tb):
    t, d = h.shape
    hk = ids.shape[1]
    ahead = PEER_NBUF - 1
    assert tb == LANE and hk % 8 == 0 and tb % PEER_NBUF == 0 and d % LANE == 0
    ids_next = jnp.concatenate([ids[ahead:], ids[:ahead]], axis=0)
    slabs = uvw.reshape(uvw.shape[0], d // LANE, LANE)
    return pl.pallas_call(
        _peer_expert_body,
        grid=(t // tb,),
        in_specs=[pl.BlockSpec((8, hk), lambda i: (0, 0), memory_space=pltpu.SMEM),
                  pl.BlockSpec((tb, hk), lambda i: (i, 0), memory_space=pltpu.SMEM),
                  pl.BlockSpec((tb, d), lambda i: (i, 0)),
                  pl.BlockSpec((1, d), lambda i: (0, 0)),
                  pl.BlockSpec((hk, tb), lambda i: (0, i)),
                  pl.BlockSpec(memory_space=pl.ANY)],
        out_specs=pl.BlockSpec((tb, d), lambda i: (i, 0)),
        out_shape=jax.ShapeDtypeStruct((t, d), F32),
        scratch_shapes=[pltpu.VMEM((tb, d), F32),
                        pltpu.VMEM((hk, 2 * tb), BF16),
                        pltpu.VMEM((PEER_NBUF, d // LANE, hk, LANE), jnp.uint32),
                        pltpu.SemaphoreType.DMA((PEER_NBUF,))],
        compiler_params=_params("arbitrary"),
        name="peer_experts",
    )(ids[:8], ids_next, h, ln.reshape(1, d), g_t, slabs)


def _ple_body(x_ref, g_ref, wg_ref, p_ref, wp_ref, h_ref, o_ref, xn_ref):
    @pl.when(pl.program_id(1) == 0)
    def _():
        x = x_ref[...]
        ms = jnp.mean(x * x, axis=-1, keepdims=True)
        xn_ref[...] = (x * lax.rsqrt(ms + EPS) * g_ref[...]).astype(BF16)

    z = jnp.dot(xn_ref[...], wg_ref[...], preferred_element_type=F32)
    pp = jnp.dot(p_ref[...].astype(BF16), wp_ref[...], preferred_element_type=F32)
    o_ref[...] = h_ref[...] + jax.nn.sigmoid(z) * pp


def _ple(h, g, wg, p, wp, tm, tn):
    t, d = h.shape
    pd = p.shape[1]
    return pl.pallas_call(
        _ple_body,
        grid=(t // tm, d // tn),
        in_specs=[pl.BlockSpec((tm, d), lambda i, j: (i, 0)),
                  pl.BlockSpec((1, d), lambda i, j: (0, 0)),
                  pl.BlockSpec((d, tn), lambda i, j: (0, j)),
                  pl.BlockSpec((tm, pd), lambda i, j: (i, 0)),
                  pl.BlockSpec((pd, tn), lambda i, j: (0, j)),
                  pl.BlockSpec((tm, tn), lambda i, j: (i, j))],
        out_specs=pl.BlockSpec((tm, tn), lambda i, j: (i, j)),
        out_shape=jax.ShapeDtypeStruct((t, d), F32),
        scratch_shapes=[pltpu.VMEM((tm, d), BF16)],
        compiler_params=_params("parallel", "arbitrary"),
        name="ple_gate",
    )(h, g.reshape(1, d), wg, p, wp, h)


def _rmsnorm_body(x_ref, g_ref, o_ref):
    x = x_ref[...]
    ms = jnp.mean(x * x, axis=-1, keepdims=True)
    o_ref[...] = x * lax.rsqrt(ms + EPS) * g_ref[...]


def _rmsnorm(x, g, tm):
    t, d = x.shape
    return pl.pallas_call(
        _rmsnorm_body,
        grid=(t // tm,),
        in_specs=[pl.BlockSpec((tm, d), lambda i: (i, 0)), pl.BlockSpec((1, d), lambda i: (0, 0))],
        out_specs=pl.BlockSpec((tm, d), lambda i: (i, 0)),
        out_shape=jax.ShapeDtypeStruct((t, d), F32),
        compiler_params=_params("parallel"),
        name="final_rmsnorm",
    )(x, g.reshape(1, d))


def _bias_tables(rel_bias, s):
    nd = s // BK
    bd = rel_bias[_rel_bucket(jnp.arange(s))].astype(F32)
    dd = (jnp.arange(nd)[:, None, None] * BK + jnp.arange(BQ)[None, :, None]
          - jnp.arange(BK)[None, None, :])
    tz = jnp.transpose(bd[jnp.clip(dd, 0, s - 1)], (3, 0, 1, 2))
    causal = dd >= 0
    mult = sum(((dd % dil == 0) & (dd // dil <= window // dil)).astype(F32) for window, dil in DIL_PATTERNS)
    ok = causal & (mult > 0)
    tzd = jnp.where(ok, tz[:A_HEADS] + jnp.log(jnp.where(ok, mult, 1.0)), NEG)
    tzb = tz[A_HEADS:A_HEADS + B_HEADS]
    tzs = jnp.where(causal, tzb, NEG)
    nwin = -(-(NSA_WINDOW - 1) // BK) + 1
    tzw = jnp.where(causal & (dd <= NSA_WINDOW - 1), tzb, NEG)[:, :nwin]
    tzc = jnp.where(causal, tz[A_HEADS + B_HEADS:], NEG)
    ncmp = s // CMP_STRIDE
    dist_c = jnp.arange(s)[:, None] - (jnp.arange(ncmp) * CMP_STRIDE + CMP_LEN - 1)[None, :]
    bc = rel_bias[_rel_bucket(dist_c)].astype(F32)[:, :, A_HEADS:A_HEADS + B_HEADS]
    bc = jnp.where((dist_c >= 0)[:, :, None], bc, NEG)
    bcq = jnp.transpose(bc, (2, 0, 1))
    bct = jnp.transpose(bc, (2, 1, 0))
    nslc = s // SLC_LEN
    cstart = jnp.arange(ncmp) * CMP_STRIDE
    sstart = jnp.arange(nslc) * SLC_LEN
    covt = ((cstart[None, :] < sstart[:, None] + SLC_LEN)
            & (cstart[None, :] + CMP_LEN > sstart[:, None])).astype(F32)
    nblk = s // MOBA_BLK
    avg = ((jnp.arange(s)[None, :] // MOBA_BLK == jnp.arange(nblk)[:, None]).astype(F32)
           / MOBA_BLK).astype(BF16)
    return tzd, tzs, tzw, tzc, bcq, bct, covt, avg


def _pack_uv(u, v):
    ub = lax.bitcast_convert_type(u.astype(BF16), jnp.uint16).astype(jnp.uint32)
    vb = lax.bitcast_convert_type(v.astype(BF16), jnp.uint16).astype(jnp.uint32)
    return ub | (vb << 16)


def _reorder_w_in(w):
    g0 = 3 * A_HEADS * LANE + B_HEADS * LANE + 6 * NSA_KV * LANE
    ng = 3 * B_HEADS
    main = jnp.concatenate([w[:, :g0], w[:, g0 + ng:]], axis=1)
    per = ng // NSA_KV
    gate = jnp.concatenate(
        [jnp.pad(w[:, g0 + g * per:g0 + (g + 1) * per], ((0, 0), (0, LANE - per))) for g in range(NSA_KV)],
        axis=1)
    return main.astype(BF16), gate.astype(BF16)


def kernel(x, p, ln_mix, w_in, cmp_wk, cmp_wv, cmp_pos, out_norm, w_out, rel_bias, ln_ffn, peer_wq,
           peer_keys, peer_u, peer_v, ln_ple, ple_gate, ple_proj, ln_final):
    b, s, d = x.shape
    t = b * s
    depth = w_in.shape[0]
    assert s % BQ == 0 and BQ == BK == MOBA_BLK and d % LANE == 0
    tm = 512
    tzd, tzs, tzw, tzc, bcq, bct, covt, avg = _bias_tables(rel_bias, s)
    h = x.reshape(t, d)
    for i in range(depth):
        w_main, w_gate = _reorder_w_in(w_in[i])
        proj = _normmm(h, ln_mix[i], w_main, BF16, tm, 1024, "in_projection").reshape(b, s, MAIN_COLS)
        gates = _normmm(h, ln_mix[i], w_gate, F32, tm, w_gate.shape[1], "gate_projection")
        gates = gates.reshape(b, s, NSA_KV * LANE)
        oa = _dilated(proj, tzd)
        kcmp, vcmp = _compress(proj, cmp_wk[i], cmp_wv[i], cmp_pos[i])
        ob = _nsa(proj, gates, kcmp, vcmp, bcq, bct, covt, tzs, tzw)
        oc = _moba(proj, avg, tzc)
        h = _outproj(oa.reshape(t, -1), ob.reshape(t, -1), oc.reshape(t, -1), out_norm[i],
                     w_out[i].astype(BF16), h, tm, 1024)
        qp = _normmm(h, ln_ffn[i], peer_wq[i].astype(BF16), F32, tm, 1024, "peer_query")
        keys = peer_keys[i].reshape(PEER_HEADS * 2, PEER_NKEYS, -1)
        e_t, g_t = _peer_topk(qp, keys, 256)
        h = _peer_experts(e_t.T, h, ln_ffn[i], g_t, _pack_uv(peer_u[i], peer_v[i]), LANE)
        h = _ple(h, ln_ple[i], ple_gate[i].astype(BF16), p[i].reshape(t, -1), ple_proj[i].astype(BF16),
                 tm, 1024)
    return _rmsnorm(h, ln_final, tm).reshape(b, s, d)
```

```python
import functools
import math

import numpy as np
import jax
import jax.numpy as jnp
from jax import lax
from jax.experimental import pallas as pl
from jax.experimental.pallas import tpu as pltpu

F32 = jnp.float32
BF16 = jnp.bfloat16
I32 = jnp.int32

HEAD_DIM = 128
A_HEADS, B_HEADS, C_HEADS = 6, 6, 4
NSA_KV, NSA_GROUP = 2, 3
DIL_PATTERNS = ((128, 1), (512, 4), (2048, 16))
CMP_LEN, CMP_STRIDE = 32, 16
SLC_LEN, SLC_TOPN = 64, 8
NSA_WINDOW = 512
FORCE = 1e4
MOBA_BLK, MOBA_TOPK = 256, 3
REL_BUCKETS, REL_MAX_DIST = 32, 1024
PEER_HEADS, PEER_NKEYS, PEER_TOPK = 8, 128, 16
EPS = 1e-6

LANE = 128
BQ = 256
BK = 256
NEG = -1e30
HALF_NEG = -5e29
VMEM_LIMIT = 56 * 1024 * 1024

CB_QA, CB_KA, CB_VA = 0, 6, 12
CB_QB, CB_KC, CB_VC, CB_KS, CB_VS, CB_KW, CB_VW = 18, 24, 26, 28, 30, 32, 34
CB_QC, CB_KCC, CB_VCC = 36, 40, 44
MAIN_COLS = 48 * LANE


def _nt(a, b, precision=None):
    return lax.dot_general(a, b, (((1,), (1,)), ((), ())), preferred_element_type=F32,
                           precision=precision)


def _rel_bucket(dist):
    exact = REL_BUCKETS // 2
    d = jnp.maximum(dist, 0)
    logd = jnp.log(jnp.maximum(d, 1).astype(F32) / exact)
    large = exact + (logd / math.log(REL_MAX_DIST / exact) * (REL_BUCKETS - exact)).astype(I32)
    large = jnp.clip(large, exact, REL_BUCKETS - 1)
    return jnp.where(d < exact, d, large)


def _masked_softmax(s, axis):
    valid = s > HALF_NEG
    m = jnp.max(s, axis=axis, keepdims=True)
    e = jnp.where(valid, jnp.exp(s - m), 0.0)
    z = jnp.sum(e, axis=axis, keepdims=True)
    zs = jnp.where(z > 0, z, 1.0)
    return e / zs


def _params(*sem):
    return pltpu.CompilerParams(dimension_semantics=sem, vmem_limit_bytes=VMEM_LIMIT)


def _normmm_body(x_ref, g_ref, w_ref, o_ref, xn_ref):
    @pl.when(pl.program_id(1) == 0)
    def _():
        x = x_ref[...]
        ms = jnp.mean(x * x, axis=-1, keepdims=True)
        xn_ref[...] = (x * lax.rsqrt(ms + EPS) * g_ref[...]).astype(BF16)

    o_ref[...] = jnp.dot(xn_ref[...], w_ref[...], preferred_element_type=F32).astype(o_ref.dtype)


def _normmm(x, g, w, out_dtype, tm, tn, name):
    t, d = x.shape
    n = w.shape[1]
    return pl.pallas_call(
        _normmm_body,
        grid=(t // tm, n // tn),
        in_specs=[pl.BlockSpec((tm, d), lambda i, j: (i, 0)),
                  pl.BlockSpec((1, d), lambda i, j: (0, 0)),
                  pl.BlockSpec((d, tn), lambda i, j: (0, j))],
        out_specs=pl.BlockSpec((tm, tn), lambda i, j: (i, j)),
        out_shape=jax.ShapeDtypeStruct((t, n), out_dtype),
        scratch_shapes=[pltpu.VMEM((tm, d), BF16)],
        compiler_params=_params("parallel", "arbitrary"),
        name=name,
    )(x, g.reshape(1, d), w)


def _flash(q, k_ref, v_ref, lo, hi, score_fn, scale):
    def body(kj, carry):
        m, l, acc = carry
        off = pl.multiple_of(kj * BK, BK)
        k = k_ref[0, pl.ds(off, BK), :]
        v = v_ref[0, pl.ds(off, BK), :]
        s = score_fn(kj, _nt(q, k) * scale)
        m_new = jnp.maximum(m, jnp.max(s, axis=1, keepdims=True))
        alpha = jnp.exp(m - m_new)
        p = jnp.exp(s - m_new)
        l = alpha * l + jnp.sum(p, axis=1, keepdims=True)
        acc = alpha * acc + jnp.dot(p.astype(BF16), v, preferred_element_type=F32)
        return m_new, l, acc

    init = (jnp.full((BQ, 1), NEG, F32), jnp.zeros((BQ, 1), F32), jnp.zeros((BQ, HEAD_DIM), F32))
    _, l, acc = lax.fori_loop(lo, hi, body, init)
    return acc / l


def _dilated_body(q_ref, k_ref, v_ref, tz_ref, o_ref):
    qi = pl.program_id(2)
    scale = HEAD_DIM ** -0.5

    def score(kj, s):
        return s + tz_ref[0, qi - kj]

    o_ref[0] = _flash(q_ref[0], k_ref, v_ref, 0, qi + 1, score, scale)


def _dilated(proj, tzd):
    b, s, _ = proj.shape
    nd = s // BK
    return pl.pallas_call(
        _dilated_body,
        grid=(b, A_HEADS, s // BQ),
        in_specs=[pl.BlockSpec((1, BQ, LANE), lambda bi, h, qi: (bi, qi, CB_QA + h)),
                  pl.BlockSpec((1, s, LANE), lambda bi, h, qi: (bi, 0, CB_KA + h)),
                  pl.BlockSpec((1, s, LANE), lambda bi, h, qi: (bi, 0, CB_VA + h)),
                  pl.BlockSpec((1, nd, BQ, BK), lambda bi, h, qi: (h, 0, 0, 0))],
        out_specs=pl.BlockSpec((1, BQ, LANE), lambda bi, h, qi: (bi, qi, h)),
        out_shape=jax.ShapeDtypeStruct((b, s, A_HEADS * LANE), F32),
        compiler_params=_params("parallel", "parallel", "arbitrary"),
        name="dilated_attention",
    )(proj, proj, proj, tzd)


def _compress_body(kc_ref, vc_ref, wk_ref, wv_ref, pos_ref, ko_ref, vo_ref, xk_ref, xv_ref):
    s = kc_ref.shape[1]
    ncmp = ko_ref.shape[2]
    xk_ref[pl.ds(0, s), :] = kc_ref[0].astype(F32)
    xv_ref[pl.ds(0, s), :] = vc_ref[0].astype(F32)
    xk_ref[pl.ds(s, CMP_LEN), :] = jnp.zeros((CMP_LEN, HEAD_DIM), F32)
    xv_ref[pl.ds(s, CMP_LEN), :] = jnp.zeros((CMP_LEN, HEAD_DIM), F32)
    acck = jnp.zeros((ncmp, HEAD_DIM), F32)
    accv = jnp.zeros((ncmp, HEAD_DIM), F32)
    for l in range(CMP_LEN):
        pos = pos_ref[pl.ds(l, 1), :]
        rk = (xk_ref[pl.ds(l, ncmp, stride=CMP_STRIDE), :] + pos).astype(BF16)
        rv = (xv_ref[pl.ds(l, ncmp, stride=CMP_STRIDE), :] + pos).astype(BF16)
        acck = acck + jnp.dot(rk, wk_ref[l], preferred_element_type=F32)
        accv = accv + jnp.dot(rv, wv_ref[l], preferred_element_type=F32)
    ko_ref[0, 0] = acck.astype(BF16)
    vo_ref[0, 0] = accv.astype(BF16)


def _compress(proj, cmp_wk, cmp_wv, cmp_pos):
    b, s, _ = proj.shape
    ncmp = s // CMP_STRIDE
    out = jax.ShapeDtypeStruct((b, NSA_KV, ncmp, HEAD_DIM), BF16)
    return pl.pallas_call(
        _compress_body,
        grid=(b, NSA_KV),
        in_specs=[pl.BlockSpec((1, s, LANE), lambda bi, g: (bi, 0, CB_KC + g)),
                  pl.BlockSpec((1, s, LANE), lambda bi, g: (bi, 0, CB_VC + g)),
                  pl.BlockSpec((CMP_LEN, HEAD_DIM, HEAD_DIM), lambda bi, g: (0, 0, 0)),
                  pl.BlockSpec((CMP_LEN, HEAD_DIM, HEAD_DIM), lambda bi, g: (0, 0, 0)),
                  pl.BlockSpec((CMP_LEN, HEAD_DIM), lambda bi, g: (0, 0))],
        out_specs=[pl.BlockSpec((1, 1, ncmp, HEAD_DIM), lambda bi, g: (bi, g, 0, 0)),
                   pl.BlockSpec((1, 1, ncmp, HEAD_DIM), lambda bi, g: (bi, g, 0, 0))],
        out_shape=[out, out],
        scratch_shapes=[pltpu.VMEM((s + CMP_LEN, HEAD_DIM), F32),
                        pltpu.VMEM((s + CMP_LEN, HEAD_DIM), F32)],
        compiler_params=_params("parallel", "parallel"),
        name="nsa_compress",
    )(proj, proj, cmp_wk.astype(BF16), cmp_wv.astype(BF16), cmp_pos)


def _nsa_body(q_ref, kcmp_ref, vcmp_ref, ks_ref, vs_ref, kw_ref, vw_ref, gt_ref, bcq_ref, bct_ref,
              covt_ref, tzs_ref, tzw_ref, o_ref, mask_ref):
    qi = pl.program_id(2)
    scale = HEAD_DIM ** -0.5
    nslc = covt_ref.shape[0]
    nkt = mask_ref.shape[0]
    kcmp = kcmp_ref[0, 0]
    vcmp = vcmp_ref[0, 0]
    gates = jax.nn.sigmoid(gt_ref[0])

    psum = jnp.zeros((kcmp.shape[0], BQ), F32)
    for r in range(NSA_GROUP):
        q = q_ref[0, :, r * LANE:(r + 1) * LANE]
        p = _masked_softmax(_nt(q, kcmp) * scale + bcq_ref[r], axis=1)
        o_cmp = jnp.dot(p.astype(BF16), vcmp, preferred_element_type=F32)
        o_ref[0, :, r * LANE:(r + 1) * LANE] = gates[:, 3 * r:3 * r + 1] * o_cmp
        psum = psum + _masked_softmax(_nt(kcmp, q) * scale + bct_ref[r], axis=0)
    imp = jnp.dot(covt_ref[...], psum, preferred_element_type=F32,
                  precision=lax.Precision.HIGHEST)
    jb = lax.broadcasted_iota(I32, (nslc, BQ), 0)
    qpos = qi * BQ + lax.broadcasted_iota(I32, (nslc, BQ), 1)
    qblk = lax.shift_right_logical(qpos, int(math.log2(SLC_LEN)))
    forced = (jb == 0) | (jb == qblk) | (jb == qblk - 1)
    imp = jnp.where(forced, FORCE, jnp.where(jb > qblk, -FORCE, imp))
    beaten = jnp.zeros((nslc, BQ), F32)
    for jp in range(nslc):
        row = imp[jp:jp + 1, :]
        wins = (row > imp) | ((row == imp) & (jp < jb))
        beaten = beaten + wins.astype(F32)
    selt = (beaten < SLC_TOPN).astype(BF16)
    eye = (lax.broadcasted_iota(I32, (BQ, BQ), 0) == lax.broadcasted_iota(I32, (BQ, BQ), 1)).astype(BF16)
    selq = _nt(eye, selt).astype(BF16)
    for t in range(nkt):
        kblk = (t * BK + lax.broadcasted_iota(I32, (nslc, BK), 1)) // SLC_LEN
        expand = (kblk == lax.broadcasted_iota(I32, (nslc, BK), 0)).astype(BF16)
        mask_ref[t] = jnp.dot(selq, expand, preferred_element_type=F32)

    for r in range(NSA_GROUP):
        q = q_ref[0, :, r * LANE:(r + 1) * LANE]

        def score_sel(kj, s, r=r):
            return jnp.where(mask_ref[kj] > 0.5, s + tzs_ref[r, qi - kj], NEG)

        def score_win(kj, s, r=r):
            return s + tzw_ref[r, qi - kj]

        o_sel = _flash(q, ks_ref, vs_ref, 0, qi + 1, score_sel, scale)
        nwin = tzw_ref.shape[1]
        o_win = _flash(q, kw_ref, vw_ref, jnp.maximum(qi - (nwin - 1), 0), qi + 1, score_win, scale)
        o_ref[0, :, r * LANE:(r + 1) * LANE] += (gates[:, 3 * r + 1:3 * r + 2] * o_sel
                                                 + gates[:, 3 * r + 2:3 * r + 3] * o_win)


def _nsa(proj, gates, kcmp, vcmp, bcq, bct, covt, tzs, tzw):
    b, s, _ = proj.shape
    nd = s // BK
    ncmp = kcmp.shape[2]
    nslc = covt.shape[0]
    nwin = tzw.shape[1]
    gw = NSA_GROUP * LANE
    kv = lambda cb: pl.BlockSpec((1, s, LANE), lambda bi, g, qi: (bi, 0, cb + g))
    return pl.pallas_call(
        _nsa_body,
        grid=(b, NSA_KV, s // BQ),
        in_specs=[pl.BlockSpec((1, BQ, gw), lambda bi, g, qi: (bi, qi, CB_QB // NSA_GROUP + g)),
                  pl.BlockSpec((1, 1, ncmp, HEAD_DIM), lambda bi, g, qi: (bi, g, 0, 0)),
                  pl.BlockSpec((1, 1, ncmp, HEAD_DIM), lambda bi, g, qi: (bi, g, 0, 0)),
                  kv(CB_KS), kv(CB_VS), kv(CB_KW), kv(CB_VW),
                  pl.BlockSpec((1, BQ, LANE), lambda bi, g, qi: (bi, qi, g)),
                  pl.BlockSpec((NSA_GROUP, BQ, ncmp), lambda bi, g, qi: (g, qi, 0)),
                  pl.BlockSpec((NSA_GROUP, ncmp, BQ), lambda bi, g, qi: (g, 0, qi)),
                  pl.BlockSpec((nslc, ncmp), lambda bi, g, qi: (0, 0)),
                  pl.BlockSpec((NSA_GROUP, nd, BQ, BK), lambda bi, g, qi: (g, 0, 0, 0)),
                  pl.BlockSpec((NSA_GROUP, nwin, BQ, BK), lambda bi, g, qi: (g, 0, 0, 0))],
        out_specs=pl.BlockSpec((1, BQ, gw), lambda bi, g, qi: (bi, qi, g)),
        out_shape=jax.ShapeDtypeStruct((b, s, B_HEADS * LANE), F32),
        scratch_shapes=[pltpu.VMEM((nd, BQ, BK), F32)],
        compiler_params=_params("parallel", "parallel", "arbitrary"),
        name="nsa_attention",
    )(proj, kcmp, vcmp, proj, proj, proj, proj, gates, bcq, bct, covt, tzs, tzw)


def _moba_body(q_ref, k_ref, v_ref, avg_ref, tz_ref, o_ref, mask_ref):
    qi = pl.program_id(2)
    scale = HEAD_DIM ** -0.5
    nblk = avg_ref.shape[0]
    q = q_ref[0]
    kmean = jnp.dot(avg_ref[...], k_ref[0], preferred_element_type=F32)
    gate = _nt(kmean, q.astype(F32), precision=lax.Precision.HIGHEST)
    nb = lax.broadcasted_iota(I32, (nblk, BQ), 0)
    past = nb < qi
    beaten = jnp.zeros((nblk, BQ), F32)
    for np_ in range(nblk):
        row = gate[np_:np_ + 1, :]
        wins = (np_ < qi) & ((row > gate) | ((row == gate) & (np_ < nb)))
        beaten = beaten + wins.astype(F32)
    selt = (past & (beaten < MOBA_TOPK)).astype(BF16)
    eye = (lax.broadcasted_iota(I32, (BQ, BQ), 0) == lax.broadcasted_iota(I32, (BQ, BQ), 1)).astype(BF16)
    selq = _nt(eye, selt)
    for n in range(nblk):
        mask_ref[n] = jnp.broadcast_to(selq[:, n:n + 1], (BQ, BK))

    def score(kj, s):
        keep = (mask_ref[kj] > 0.5) | (kj == qi)
        return jnp.where(keep, s + tz_ref[0, qi - kj], NEG)

    o_ref[0] = _flash(q, k_ref, v_ref, 0, qi + 1, score, scale)


def _moba(proj, avg, tzc):
    b, s, _ = proj.shape
    nd = s // BK
    nblk = avg.shape[0]
    return pl.pallas_call(
        _moba_body,
        grid=(b, C_HEADS, s // BQ),
        in_specs=[pl.BlockSpec((1, BQ, LANE), lambda bi, h, qi: (bi, qi, CB_QC + h)),
                  pl.BlockSpec((1, s, LANE), lambda bi, h, qi: (bi, 0, CB_KCC + h)),
                  pl.BlockSpec((1, s, LANE), lambda bi, h, qi: (bi, 0, CB_VCC + h)),
                  pl.BlockSpec((nblk, s), lambda bi, h, qi: (0, 0)),
                  pl.BlockSpec((1, nd, BQ, BK), lambda bi, h, qi: (h, 0, 0, 0))],
        out_specs=pl.BlockSpec((1, BQ, LANE), lambda bi, h, qi: (bi, qi, h)),
        out_shape=jax.ShapeDtypeStruct((b, s, C_HEADS * LANE), F32),
        scratch_shapes=[pltpu.VMEM((nblk, BQ, BK), F32)],
        compiler_params=_params("parallel", "parallel", "arbitrary"),
        name="moba_attention",
    )(proj, proj, proj, avg, tzc)


def _outproj_body(oa_ref, ob_ref, oc_ref, g_ref, w_ref, h_ref, o_ref, y_ref):
    @pl.when(pl.program_id(1) == 0)
    def _():
        c0 = 0
        for ref in (oa_ref, ob_ref, oc_ref):
            x = ref[...]
            wd = x.shape[1]
            ms = jnp.mean(x * x, axis=-1, keepdims=True)
            y_ref[:, c0:c0 + wd] = (x * lax.rsqrt(ms + EPS) * g_ref[:, c0:c0 + wd]).astype(BF16)
            c0 += wd

    o_ref[...] = h_ref[...] + jnp.dot(y_ref[...], w_ref[...], preferred_element_type=F32)


def _outproj(oa, ob, oc, g, w, h, tm, tn):
    t, d = h.shape
    row = lambda a: pl.BlockSpec((tm, a.shape[1]), lambda i, j: (i, 0))
    return pl.pallas_call(
        _outproj_body,
        grid=(t // tm, d // tn),
        in_specs=[row(oa), row(ob), row(oc),
                  pl.BlockSpec((1, d), lambda i, j: (0, 0)),
                  pl.BlockSpec((d, tn), lambda i, j: (0, j)),
                  pl.BlockSpec((tm, tn), lambda i, j: (i, j))],
        out_specs=pl.BlockSpec((tm, tn), lambda i, j: (i, j)),
        out_shape=jax.ShapeDtypeStruct((t, d), F32),
        scratch_shapes=[pltpu.VMEM((tm, d), BF16)],
        compiler_params=_params("parallel", "arbitrary"),
        name="out_projection",
    )(oa, ob, oc, g.reshape(1, d), w, h)


def _topk_rows(x, k, payload=None):
    n, tm = x.shape
    rows = lax.broadcasted_iota(I32, x.shape, 0)
    slot = lax.broadcasted_iota(I32, (k, tm), 0)
    vals = jnp.zeros((k, tm), F32)
    idxs = jnp.zeros((k, tm), I32)
    for it in range(k):
        mx = jnp.max(x, axis=0, keepdims=True)
        idx = jnp.min(jnp.where(x == mx, rows, n), axis=0, keepdims=True)
        hit = rows == idx
        if payload is not None:
            idx = jnp.sum(jnp.where(hit, payload, 0), axis=0, keepdims=True)
        vals = jnp.where(slot == it, mx, vals)
        idxs = jnp.where(slot == it, idx, idxs)
        x = jnp.where(hit, -jnp.inf, x)
    return vals, idxs


def _peer_topk_body(q_ref, keys_ref, e_ref, g_ref):
    for h in range(PEER_HEADS):
        tops = []
        for c in range(2):
            hc = 2 * h + c
            sc = _nt(keys_ref[hc], q_ref[:, hc * LANE:(hc + 1) * LANE])
            tops.append(_topk_rows(sc, PEER_TOPK))
        (s0, i0), (s1, i1) = tops
        cand = jnp.concatenate([s0[a:a + 1] + s1 for a in range(PEER_TOPK)], axis=0)
        cand_i = jnp.concatenate([i0[a:a + 1] * PEER_NKEYS + i1 for a in range(PEER_TOPK)], axis=0)
        bs, be = _topk_rows(cand, PEER_TOPK, payload=cand_i)
        e = jnp.exp(bs - jnp.max(bs, axis=0, keepdims=True))
        g_ref[h * PEER_TOPK:(h + 1) * PEER_TOPK, :] = e / jnp.sum(e, axis=0, keepdims=True)
        e_ref[h * PEER_TOPK:(h + 1) * PEER_TOPK, :] = be


def _peer_topk(qp, keys, tm):
    t, d = qp.shape
    hk = PEER_HEADS * PEER_TOPK
    return pl.pallas_call(
        _peer_topk_body,
        grid=(t // tm,),
        in_specs=[pl.BlockSpec((tm, d), lambda i: (i, 0)),
                  pl.BlockSpec(keys.shape, lambda i: (0, 0, 0))],
        out_specs=[pl.BlockSpec((hk, tm), lambda i: (0, i)),
                   pl.BlockSpec((hk, tm), lambda i: (0, i))],
        out_shape=[jax.ShapeDtypeStruct((hk, t), I32), jax.ShapeDtypeStruct((hk, t), F32)],
        compiler_params=_params("parallel"),
        name="peer_topk",
    )(qp, keys)


PEER_NBUF = 8
PEER_AHEAD = 6


def _gelu(x):
    return 0.5 * x * (1.0 + lax.erf(x * (2.0 ** -0.5)))


def _hi_lo(x):
    hi = x.astype(BF16)
    lo = (x - hi.astype(F32)).astype(BF16)
    return jnp.concatenate([hi, lo], axis=1)


def _peer_expert_body(ids0_ref, idsn_ref, h_ref, ln_ref, g_ref, uvw_ref, o_ref, xn_ref, acc_ref, lhs_ref,
                      buf_ref, sem_ref):
    tb, d = h_ref.shape
    hk = idsn_ref.shape[1]
    nrg = hk // 8
    nlt = d // LANE
    assert nrg == nlt and hk % (2 * nlt) == 0
    step = pl.program_id(0)
    x = h_ref[...]
    ms = jnp.mean(x * x, axis=-1, keepdims=True)
    xn_ref[...] = x * lax.rsqrt(ms + EPS) * ln_ref[...]

    def issue(ids_ref, row, slot, k0, k1):
        for k in range(k0, k1):
            e = ids_ref[row, k]
            pltpu.make_async_copy(uvw_ref.at[e], buf_ref.at[slot, :, k, :],
                                  sem_ref.at[slot]).start(priority=k % 2)

    def wait_all(slot):
        pltpu.make_async_copy(buf_ref.at[slot], buf_ref.at[slot], sem_ref.at[slot]).wait()

    def x_tiles(t):
        xrow = xn_ref[pl.ds(t, 1), :]
        return [jnp.broadcast_to(xrow[:, j * LANE:(j + 1) * LANE], (8, LANE)) for j in range(nlt)]

    def dots(xb, slot, r):
        a = None
        for j in range(nlt):
            w = buf_ref[slot, j, r * 8:(r + 1) * 8, :]
            pr = lax.bitcast_convert_type(lax.shift_left(w, jnp.uint32(16)), F32) * xb[j]
            a = pr if a is None else a + pr
        return a

    def coefficients(t, slot):
        act = jnp.sum(acc_ref[slot % 2].T, axis=0, keepdims=True)
        coef = jnp.broadcast_to(_gelu(act) * g_ref[pl.ds(t, 1), :], (8, hk))
        chi = coef.astype(BF16).astype(F32)
        return jnp.concatenate([chi, coef - chi], axis=0).astype(BF16)

    def weighted(lhs, slot, j):
        w = buf_ref[slot, j]
        v = lax.bitcast_convert_type(w & jnp.uint32(0xFFFF0000), F32).astype(BF16)
        yj = jnp.dot(lhs, v, preferred_element_type=F32)
        return yj[0:1, :] + yj[8:9, :]

    def flush(done):
        ya, yb, tp = done
        y = jnp.concatenate([ya[j:j + 1, :] for j in range(8)] + [yb[j:j + 1, :] for j in range(8)], axis=1)
        o_ref[pl.ds(tp, 1), :] = h_ref[pl.ds(tp, 1), :] + y

    def turn(t, slot, stages, done):
        tgt = (slot + PEER_AHEAD) % PEER_NBUF
        far = (slot + 2) % PEER_NBUF
        per = hk // (2 * nlt)
        if stages >= 3:
            wait_all(far)
            xb = x_tiles(t + 2)
        lhs = lhs_ref[slot % 2]
        ys, parts = [], []
        for c in range(nlt):
            if stages >= 3:
                parts.append(dots(xb, far, c))
            issue(idsn_ref, t, tgt, 2 * c * per, (2 * c + 1) * per)
            if 2 * c < nlt:
                ys.append(weighted(lhs, slot, 2 * c))
                ys.append(weighted(lhs, slot, 2 * c + 1))
            if 2 * c == nlt and stages >= 2:
                lhs_next = coefficients(t + 1, slot + 1)
            issue(idsn_ref, t, tgt, (2 * c + 1) * per, (2 * c + 2) * per)
        if stages >= 3:
            acc_ref[slot % 2] = jnp.concatenate(parts, axis=0)
        if stages >= 2:
            lhs_ref[(slot + 1) % 2] = lhs_next
        flush(done)
        return (jnp.concatenate(ys[:8], axis=0), jnp.concatenate(ys[8:], axis=0), t)

    @pl.when(step == 0)
    def _():
        for t0 in range(PEER_AHEAD):
            issue(ids0_ref, t0, t0, 0, hk)

    for t0 in range(2):
        wait_all(t0)
        xb0 = x_tiles(t0)
        acc_ref[t0] = jnp.concatenate([dots(xb0, t0, r) for r in range(nrg)], axis=0)
    lhs_ref[0] = coefficients(0, 0)

    def ring(i, stacks):
        done = stacks + (jnp.maximum(i * PEER_NBUF - 1, 0),)
        for slot in range(PEER_NBUF):
            done = turn(i * PEER_NBUF + slot, slot, 3, done)
        return done[:2]

    zero = jnp.zeros((8, LANE), F32)
    nring = (tb - 2) // PEER_NBUF
    done = lax.fori_loop(0, nring, ring, (zero, zero)) + (nring * PEER_NBUF - 1,)
    for t in range(nring * PEER_NBUF, tb - 2):
        done = turn(t, t % PEER_NBUF, 3, done)
    done = turn(tb - 2, (tb - 2) % PEER_NBUF, 2, done)
    done = turn(tb - 1, (tb - 1) % PEER_NBUF, 1, done)
    flush(done)

    @pl.when(step == pl.num_programs(0) - 1)
    def _():
        for t0 in range(PEER_AHEAD):
            wait_all((tb + t0) % PEER_NBUF)


def _peer_experts(ids, gates, h, ln, uvw, tb):
    t, d = h.shape
    hk = ids.shape[1]
    assert hk == LANE and tb % PEER_NBUF == 0 and PEER_NBUF % 2 == 0 and d % LANE == 0
    assert PEER_AHEAD <= min(8, PEER_NBUF - 2)
    ids_next = jnp.concatenate([ids[PEER_AHEAD:], ids[:PEER_AHEAD]], axis=0)
    slabs = uvw.reshape(uvw.shape[0], d // LANE, LANE)
    return pl.pallas_call(
        _peer_expert_body,
        grid=(t // tb,),
        in_specs=[pl.BlockSpec((8, hk), lambda i: (0, 0), memory_space=pltpu.SMEM),
                  pl.BlockSpec((tb, hk), lambda i: (i, 0), memory_space=pltpu.SMEM),
                  pl.BlockSpec((tb, d), lambda i: (i, 0)),
                  pl.BlockSpec((1, d), lambda i: (0, 0)),
                  pl.BlockSpec((tb, hk), lambda i: (i, 0)),
                  pl.BlockSpec(memory_space=pl.ANY)],
        out_specs=pl.BlockSpec((tb, d), lambda i: (i, 0)),
        out_shape=jax.ShapeDtypeStruct((t, d), F32),
        scratch_shapes=[pltpu.VMEM((tb, d), F32),
                        pltpu.VMEM((2, hk, LANE), F32),
                        pltpu.VMEM((2, 16, hk), BF16),
                        pltpu.VMEM((PEER_NBUF, d // LANE, hk, LANE), jnp.uint32),
                        pltpu.SemaphoreType.DMA((PEER_NBUF,))],
        compiler_params=_params("arbitrary"),
        name="peer_experts",
    )(ids[:8], ids_next, h, ln.reshape(1, d), gates, slabs)


def _ple_body(x_ref, g_ref, wg_ref, p_ref, wp_ref, h_ref, o_ref, xn_ref):
    @pl.when(pl.program_id(1) == 0)
    def _():
        x = x_ref[...]
        ms = jnp.mean(x * x, axis=-1, keepdims=True)
        xn_ref[...] = (x * lax.rsqrt(ms + EPS) * g_ref[...]).astype(BF16)

    z = jnp.dot(xn_ref[...], wg_ref[...], preferred_element_type=F32)
    pp = jnp.dot(p_ref[...].astype(BF16), wp_ref[...], preferred_element_type=F32)
    o_ref[...] = h_ref[...] + jax.nn.sigmoid(z) * pp


def _ple(h, g, wg, p, wp, tm, tn):
    t, d = h.shape
    pd = p.shape[1]
    return pl.pallas_call(
        _ple_body,
        grid=(t // tm, d // tn),
        in_specs=[pl.BlockSpec((tm, d), lambda i, j: (i, 0)),
                  pl.BlockSpec((1, d), lambda i, j: (0, 0)),
                  pl.BlockSpec((d, tn), lambda i, j: (0, j)),
                  pl.BlockSpec((tm, pd), lambda i, j: (i, 0)),
                  pl.BlockSpec((pd, tn), lambda i, j: (0, j)),
                  pl.BlockSpec((tm, tn), lambda i, j: (i, j))],
        out_specs=pl.BlockSpec((tm, tn), lambda i, j: (i, j)),
        out_shape=jax.ShapeDtypeStruct((t, d), F32),
        scratch_shapes=[pltpu.VMEM((tm, d), BF16)],
        compiler_params=_params("parallel", "arbitrary"),
        name="ple_gate",
    )(h, g.reshape(1, d), wg, p, wp, h)


def _rmsnorm_body(x_ref, g_ref, o_ref):
    x = x_ref[...]
    ms = jnp.mean(x * x, axis=-1, keepdims=True)
    o_ref[...] = x * lax.rsqrt(ms + EPS) * g_ref[...]


def _rmsnorm(x, g, tm):
    t, d = x.shape
    return pl.pallas_call(
        _rmsnorm_body,
        grid=(t // tm,),
        in_specs=[pl.BlockSpec((tm, d), lambda i: (i, 0)), pl.BlockSpec((1, d), lambda i: (0, 0))],
        out_specs=pl.BlockSpec((tm, d), lambda i: (i, 0)),
        out_shape=jax.ShapeDtypeStruct((t, d), F32),
        compiler_params=_params("parallel"),
        name="final_rmsnorm",
    )(x, g.reshape(1, d))


def _toeplitz(vec, nq, nk):
    n = nq + nk - 1
    lead = vec.shape[:-1]
    p = jnp.concatenate([vec[..., ::-1], jnp.zeros(lead + (1,), vec.dtype)], axis=-1)
    flat = jnp.broadcast_to(p[..., None, :], lead + (nq, n + 1)).reshape(lead + (nq * (n + 1),))
    skew = flat[..., :nq * n].reshape(lead + (nq, n))
    return skew[..., nq - 1:nq - 1 + nk]


def _bias_tables(rel_bias, s):
    nd = s // BK
    heads = rel_bias.shape[1]
    bdt = rel_bias[_rel_bucket(jnp.arange(s))].astype(F32).T
    dd = (jnp.arange(nd)[:, None, None] * BK + jnp.arange(BQ)[None, :, None]
          - jnp.arange(BK)[None, None, :])
    ext = jnp.pad(bdt, ((0, 0), (BK - 1, 0)))
    segs = jnp.stack([ext[:, dl * BK:dl * BK + BQ + BK - 1] for dl in range(nd)], axis=1)
    tz = _toeplitz(segs, BQ, BK)
    causal = dd >= 0
    mult = sum(((dd % dil == 0) & (dd // dil <= window // dil)).astype(F32) for window, dil in DIL_PATTERNS)
    ok = causal & (mult > 0)
    tzd = jnp.where(ok, tz[:A_HEADS] + jnp.log(jnp.where(ok, mult, 1.0)), NEG)
    tzb = tz[A_HEADS:A_HEADS + B_HEADS]
    tzs = jnp.where(causal, tzb, NEG)
    nwin = -(-(NSA_WINDOW - 1) // BK) + 1
    tzw = jnp.where(causal & (dd <= NSA_WINDOW - 1), tzb, NEG)[:, :nwin]
    tzc = jnp.where(causal, tz[A_HEADS + B_HEADS:], NEG)
    ncmp = s // CMP_STRIDE
    na = s // CMP_STRIDE
    nvec = na + ncmp - 1
    lo = CMP_STRIDE * (ncmp - 1) + CMP_LEN - 1
    bdb = bdt[A_HEADS:A_HEADS + B_HEADS]
    gext = jnp.concatenate([jnp.full((B_HEADS, lo), NEG, F32), bdb], axis=1)
    vecs = gext[:, :CMP_STRIDE * nvec].reshape(B_HEADS, nvec, CMP_STRIDE).transpose(0, 2, 1)
    bcq = _toeplitz(vecs, na, ncmp)
    bcq = bcq.transpose(0, 2, 1, 3).reshape(B_HEADS, s, ncmp)
    bct = jnp.transpose(bcq, (0, 2, 1))
    nslc = s // SLC_LEN
    cstart = jnp.arange(ncmp) * CMP_STRIDE
    sstart = jnp.arange(nslc) * SLC_LEN
    covt = ((cstart[None, :] < sstart[:, None] + SLC_LEN)
            & (cstart[None, :] + CMP_LEN > sstart[:, None])).astype(F32)
    nblk = s // MOBA_BLK
    avg = ((jnp.arange(s)[None, :] // MOBA_BLK == jnp.arange(nblk)[:, None]).astype(F32)
           / MOBA_BLK).astype(BF16)
    return tzd, tzs, tzw, tzc, bcq, bct, covt, avg


def _pack_uv(u, v):
    ub = lax.bitcast_convert_type(u.astype(BF16), jnp.uint16).astype(jnp.uint32)
    vb = lax.bitcast_convert_type(v.astype(BF16), jnp.uint16).astype(jnp.uint32)
    return ub | (vb << 16)


def _reorder_w_in(w):
    g0 = 3 * A_HEADS * LANE + B_HEADS * LANE + 6 * NSA_KV * LANE
    ng = 3 * B_HEADS
    main = jnp.concatenate([w[:, :g0], w[:, g0 + ng:]], axis=1)
    per = ng // NSA_KV
    gate = jnp.concatenate(
        [jnp.pad(w[:, g0 + g * per:g0 + (g + 1) * per], ((0, 0), (0, LANE - per))) for g in range(NSA_KV)],
        axis=1)
    return main.astype(BF16), gate.astype(BF16)


def kernel(x, p, ln_mix, w_in, cmp_wk, cmp_wv, cmp_pos, out_norm, w_out, rel_bias, ln_ffn, peer_wq,
           peer_keys, peer_u, peer_v, ln_ple, ple_gate, ple_proj, ln_final):
    b, s, d = x.shape
    t = b * s
    depth = w_in.shape[0]
    assert s % BQ == 0 and BQ == BK == MOBA_BLK and d % LANE == 0
    tm = 512
    tzd, tzs, tzw, tzc, bcq, bct, covt, avg = _bias_tables(rel_bias, s)
    h = x.reshape(t, d)
    for i in range(depth):
        w_main, w_gate = _reorder_w_in(w_in[i])
        proj = _normmm(h, ln_mix[i], w_main, BF16, tm, 1024, "in_projection").reshape(b, s, MAIN_COLS)
        gates = _normmm(h, ln_mix[i], w_gate, F32, tm, w_gate.shape[1], "gate_projection")
        gates = gates.reshape(b, s, NSA_KV * LANE)
        oa = _dilated(proj, tzd)
        kcmp, vcmp = _compress(proj, cmp_wk[i], cmp_wv[i], cmp_pos[i])
        ob = _nsa(proj, gates, kcmp, vcmp, bcq, bct, covt, tzs, tzw)
        oc = _moba(proj, avg, tzc)
        h = _outproj(oa.reshape(t, -1), ob.reshape(t, -1), oc.reshape(t, -1), out_norm[i],
                     w_out[i].astype(BF16), h, tm, 1024)
        qp = _normmm(h, ln_ffn[i], peer_wq[i].astype(BF16), F32, tm, 1024, "peer_query")
        keys = peer_keys[i].reshape(PEER_HEADS * 2, PEER_NKEYS, -1)
        e_t, g_t = _peer_topk(qp, keys, 256)
        h = _peer_experts(e_t.T, g_t.T, h, ln_ffn[i], _pack_uv(peer_u[i], peer_v[i]), LANE)
        h = _ple(h, ln_ple[i], ple_gate[i].astype(BF16), p[i].reshape(t, -1), ple_proj[i].astype(BF16),
                 tm, 1024)
    return _rmsnorm(h, ln_final, tm).reshape(b, s, d)
```

```python
import functools
import math

import numpy as np
import jax
import jax.numpy as jnp
from jax import lax
from jax.experimental import pallas as pl
from jax.experimental.pallas import tpu as pltpu

F32 = jnp.float32
BF16 = jnp.bfloat16
I32 = jnp.int32

HEAD_DIM = 128
A_HEADS, B_HEADS, C_HEADS = 6, 6, 4
NSA_KV, NSA_GROUP = 2, 3
DIL_PATTERNS = ((128, 1), (512, 4), (2048, 16))
CMP_LEN, CMP_STRIDE = 32, 16
SLC_LEN, SLC_TOPN = 64, 8
NSA_WINDOW = 512
FORCE = 1e4
MOBA_BLK, MOBA_TOPK = 256, 3
REL_BUCKETS, REL_MAX_DIST = 32, 1024
PEER_HEADS, PEER_NKEYS, PEER_TOPK = 8, 128, 16
EPS = 1e-6

LANE = 128
BQ = 256
BK = 256
NEG = -1e30
HALF_NEG = -5e29
VMEM_LIMIT = 56 * 1024 * 1024

CB_QA, CB_KA, CB_VA = 0, 6, 12
CB_QB, CB_KC, CB_VC, CB_KS, CB_VS, CB_KW, CB_VW = 18, 24, 26, 28, 30, 32, 34
CB_QC, CB_KCC, CB_VCC = 36, 40, 44
MAIN_COLS = 48 * LANE


def _nt(a, b, precision=None):
    return lax.dot_general(a, b, (((1,), (1,)), ((), ())), preferred_element_type=F32,
                           precision=precision)


def _rel_bucket(dist):
    exact = REL_BUCKETS // 2
    d = jnp.maximum(dist, 0)
    logd = jnp.log(jnp.maximum(d, 1).astype(F32) / exact)
    large = exact + (logd / math.log(REL_MAX_DIST / exact) * (REL_BUCKETS - exact)).astype(I32)
    large = jnp.clip(large, exact, REL_BUCKETS - 1)
    return jnp.where(d < exact, d, large)


def _masked_softmax(s, axis):
    valid = s > HALF_NEG
    m = jnp.max(s, axis=axis, keepdims=True)
    e = jnp.where(valid, jnp.exp(s - m), 0.0)
    z = jnp.sum(e, axis=axis, keepdims=True)
    zs = jnp.where(z > 0, z, 1.0)
    return e / zs


def _params(*sem):
    return pltpu.CompilerParams(dimension_semantics=sem, vmem_limit_bytes=VMEM_LIMIT)


def _normmm_body(x_ref, g_ref, w_ref, o_ref, xn_ref):
    @pl.when(pl.program_id(1) == 0)
    def _():
        x = x_ref[...]
        ms = jnp.mean(x * x, axis=-1, keepdims=True)
        xn_ref[...] = (x * lax.rsqrt(ms + EPS) * g_ref[...]).astype(BF16)

    o_ref[...] = jnp.dot(xn_ref[...], w_ref[...], preferred_element_type=F32).astype(o_ref.dtype)


def _normmm(x, g, w, out_dtype, tm, tn, name):
    t, d = x.shape
    n = w.shape[1]
    return pl.pallas_call(
        _normmm_body,
        grid=(t // tm, n // tn),
        in_specs=[pl.BlockSpec((tm, d), lambda i, j: (i, 0)),
                  pl.BlockSpec((1, d), lambda i, j: (0, 0)),
                  pl.BlockSpec((d, tn), lambda i, j: (0, j))],
        out_specs=pl.BlockSpec((tm, tn), lambda i, j: (i, j)),
        out_shape=jax.ShapeDtypeStruct((t, n), out_dtype),
        scratch_shapes=[pltpu.VMEM((tm, d), BF16)],
        compiler_params=_params("parallel", "arbitrary"),
        name=name,
    )(x, g.reshape(1, d), w)


def _flash(qs, k_ref, v_ref, lo, hi, score_fns, scale):
    def tile(kj):
        off = pl.multiple_of(kj * BK, BK)
        return k_ref[0, pl.ds(off, BK), :], v_ref[0, pl.ds(off, BK), :]

    def update(state, blocks):
        m, l, acc = state
        m_new = m
        for s, _ in blocks:
            m_new = jnp.maximum(m_new, jnp.max(s, axis=1, keepdims=True))
        alpha = jnp.exp(m - m_new)
        l = alpha * l
        acc = alpha * acc
        for s, v in blocks:
            p = jnp.exp(s - m_new)
            l = l + jnp.sum(p, axis=1, keepdims=True)
            acc = acc + jnp.dot(p.astype(BF16), v, preferred_element_type=F32)
        return m_new, l, acc

    def step(kjs, states):
        kv = [tile(kj) for kj in kjs]
        return tuple(
            update(states[i], [(score_fns[i](kj, _nt(q, k) * scale), v) for kj, (k, v) in zip(kjs, kv)])
            for i, q in enumerate(qs))

    states = tuple((jnp.full((BQ, 1), NEG, F32), jnp.zeros((BQ, 1), F32), jnp.zeros((BQ, HEAD_DIM), F32))
                   for _ in qs)
    start = lo
    for width in ((4, 2, 1) if len(qs) == 1 else (2, 1)):
        shift = width.bit_length() - 1
        count = lax.shift_right_logical(hi - start, shift)

        def group(p, st, start=start, width=width):
            return step([start + width * p + i for i in range(width)], st)

        states = lax.fori_loop(0, count, group, states)
        start = start + count * width
    return [acc / l for _, l, acc in states]


def _dilated_body(q_ref, k_ref, v_ref, tz_ref, o_ref):
    qi = pl.program_id(2)
    scale = HEAD_DIM ** -0.5

    def score(kj, s):
        return s + tz_ref[0, qi - kj]

    o_ref[0] = _flash([q_ref[0]], k_ref, v_ref, 0, qi + 1, [score], scale)[0]


def _dilated(proj, tzd):
    b, s, _ = proj.shape
    nd = s // BK
    return pl.pallas_call(
        _dilated_body,
        grid=(b, A_HEADS, s // BQ),
        in_specs=[pl.BlockSpec((1, BQ, LANE), lambda bi, h, qi: (bi, qi, CB_QA + h)),
                  pl.BlockSpec((1, s, LANE), lambda bi, h, qi: (bi, 0, CB_KA + h)),
                  pl.BlockSpec((1, s, LANE), lambda bi, h, qi: (bi, 0, CB_VA + h)),
                  pl.BlockSpec((1, nd, BQ, BK), lambda bi, h, qi: (h, 0, 0, 0))],
        out_specs=pl.BlockSpec((1, BQ, LANE), lambda bi, h, qi: (bi, qi, h)),
        out_shape=jax.ShapeDtypeStruct((b, s, A_HEADS * LANE), F32),
        compiler_params=_params("parallel", "parallel", "arbitrary"),
        name="dilated_attention",
    )(proj, proj, proj, tzd)


def _compress_body(kc_ref, vc_ref, wk_ref, wv_ref, pos_ref, ko_ref, vo_ref, xk_ref, xv_ref):
    s = kc_ref.shape[1]
    ncmp = ko_ref.shape[2]
    xk_ref[pl.ds(0, s), :] = kc_ref[0].astype(F32)
    xv_ref[pl.ds(0, s), :] = vc_ref[0].astype(F32)
    xk_ref[pl.ds(s, CMP_LEN), :] = jnp.zeros((CMP_LEN, HEAD_DIM), F32)
    xv_ref[pl.ds(s, CMP_LEN), :] = jnp.zeros((CMP_LEN, HEAD_DIM), F32)
    acck = jnp.zeros((ncmp, HEAD_DIM), F32)
    accv = jnp.zeros((ncmp, HEAD_DIM), F32)
    for l in range(CMP_LEN):
        pos = pos_ref[pl.ds(l, 1), :]
        rk = (xk_ref[pl.ds(l, ncmp, stride=CMP_STRIDE), :] + pos).astype(BF16)
        rv = (xv_ref[pl.ds(l, ncmp, stride=CMP_STRIDE), :] + pos).astype(BF16)
        acck = acck + jnp.dot(rk, wk_ref[l], preferred_element_type=F32)
        accv = accv + jnp.dot(rv, wv_ref[l], preferred_element_type=F32)
    ko_ref[0, 0] = acck.astype(BF16)
    vo_ref[0, 0] = accv.astype(BF16)


def _compress(proj, cmp_wk, cmp_wv, cmp_pos):
    b, s, _ = proj.shape
    ncmp = s // CMP_STRIDE
    out = jax.ShapeDtypeStruct((b, NSA_KV, ncmp, HEAD_DIM), BF16)
    return pl.pallas_call(
        _compress_body,
        grid=(b, NSA_KV),
        in_specs=[pl.BlockSpec((1, s, LANE), lambda bi, g: (bi, 0, CB_KC + g)),
                  pl.BlockSpec((1, s, LANE), lambda bi, g: (bi, 0, CB_VC + g)),
                  pl.BlockSpec((CMP_LEN, HEAD_DIM, HEAD_DIM), lambda bi, g: (0, 0, 0)),
                  pl.BlockSpec((CMP_LEN, HEAD_DIM, HEAD_DIM), lambda bi, g: (0, 0, 0)),
                  pl.BlockSpec((CMP_LEN, HEAD_DIM), lambda bi, g: (0, 0))],
        out_specs=[pl.BlockSpec((1, 1, ncmp, HEAD_DIM), lambda bi, g: (bi, g, 0, 0)),
                   pl.BlockSpec((1, 1, ncmp, HEAD_DIM), lambda bi, g: (bi, g, 0, 0))],
        out_shape=[out, out],
        scratch_shapes=[pltpu.VMEM((s + CMP_LEN, HEAD_DIM), F32),
                        pltpu.VMEM((s + CMP_LEN, HEAD_DIM), F32)],
        compiler_params=_params("parallel", "parallel"),
        name="nsa_compress",
    )(proj, proj, cmp_wk.astype(BF16), cmp_wv.astype(BF16), cmp_pos)


def _nsa_body(q_ref, kcmp_ref, vcmp_ref, ks_ref, vs_ref, kw_ref, vw_ref, gt_ref, bcq_ref, bct_ref,
              covt_ref, tzs_ref, tzw_ref, o_ref, mask_ref):
    qi = pl.program_id(2)
    scale = HEAD_DIM ** -0.5
    nslc = covt_ref.shape[0]
    nkt = mask_ref.shape[0]
    kcmp = kcmp_ref[0, 0]
    vcmp = vcmp_ref[0, 0]
    gates = jax.nn.sigmoid(gt_ref[0])

    psum = jnp.zeros((kcmp.shape[0], BQ), F32)
    for r in range(NSA_GROUP):
        q = q_ref[0, :, r * LANE:(r + 1) * LANE]
        p = _masked_softmax(_nt(q, kcmp) * scale + bcq_ref[r], axis=1)
        o_cmp = jnp.dot(p.astype(BF16), vcmp, preferred_element_type=F32)
        o_ref[0, :, r * LANE:(r + 1) * LANE] = gates[:, 3 * r:3 * r + 1] * o_cmp
        psum = psum + _masked_softmax(_nt(kcmp, q) * scale + bct_ref[r], axis=0)
    imp = jnp.dot(covt_ref[...], psum, preferred_element_type=F32,
                  precision=lax.Precision.HIGHEST)
    jb = lax.broadcasted_iota(I32, (nslc, BQ), 0)
    qpos = qi * BQ + lax.broadcasted_iota(I32, (nslc, BQ), 1)
    qblk = lax.shift_right_logical(qpos, int(math.log2(SLC_LEN)))
    forced = (jb == 0) | (jb == qblk) | (jb == qblk - 1)
    imp = jnp.where(forced, FORCE, jnp.where(jb > qblk, -FORCE, imp))
    beaten = jnp.zeros((nslc, BQ), F32)
    for jp in range(nslc):
        row = imp[jp:jp + 1, :]
        wins = (row > imp) | ((row == imp) & (jp < jb))
        beaten = beaten + wins.astype(F32)
    selt = (beaten < SLC_TOPN).astype(BF16)
    eye = (lax.broadcasted_iota(I32, (BQ, BQ), 0) == lax.broadcasted_iota(I32, (BQ, BQ), 1)).astype(BF16)
    selq = _nt(eye, selt).astype(BF16)
    for t in range(nkt):
        kblk = (t * BK + lax.broadcasted_iota(I32, (nslc, BK), 1)) // SLC_LEN
        expand = (kblk == lax.broadcasted_iota(I32, (nslc, BK), 0)).astype(BF16)
        mask_ref[t] = jnp.dot(selq, expand, preferred_element_type=F32)

    qs = [q_ref[0, :, r * LANE:(r + 1) * LANE] for r in range(NSA_GROUP)]
    sel_fns = [lambda kj, s, r=r: jnp.where(mask_ref[kj] > 0.5, s + tzs_ref[r, qi - kj], NEG)
               for r in range(NSA_GROUP)]
    win_fns = [lambda kj, s, r=r: s + tzw_ref[r, qi - kj] for r in range(NSA_GROUP)]
    nwin = tzw_ref.shape[1]
    o_sel = _flash(qs, ks_ref, vs_ref, 0, qi + 1, sel_fns, scale)
    o_win = _flash(qs, kw_ref, vw_ref, jnp.maximum(qi - (nwin - 1), 0), qi + 1, win_fns, scale)
    for r in range(NSA_GROUP):
        o_ref[0, :, r * LANE:(r + 1) * LANE] += (gates[:, 3 * r + 1:3 * r + 2] * o_sel[r]
                                                 + gates[:, 3 * r + 2:3 * r + 3] * o_win[r])


def _nsa(proj, gates, kcmp, vcmp, bcq, bct, covt, tzs, tzw):
    b, s, _ = proj.shape
    nd = s // BK
    ncmp = kcmp.shape[2]
    nslc = covt.shape[0]
    nwin = tzw.shape[1]
    gw = NSA_GROUP * LANE
    kv = lambda cb: pl.BlockSpec((1, s, LANE), lambda bi, g, qi: (bi, 0, cb + g))
    return pl.pallas_call(
        _nsa_body,
        grid=(b, NSA_KV, s // BQ),
        in_specs=[pl.BlockSpec((1, BQ, gw), lambda bi, g, qi: (bi, qi, CB_QB // NSA_GROUP + g)),
                  pl.BlockSpec((1, 1, ncmp, HEAD_DIM), lambda bi, g, qi: (bi, g, 0, 0)),
                  pl.BlockSpec((1, 1, ncmp, HEAD_DIM), lambda bi, g, qi: (bi, g, 0, 0)),
                  kv(CB_KS), kv(CB_VS), kv(CB_KW), kv(CB_VW),
                  pl.BlockSpec((1, BQ, LANE), lambda bi, g, qi: (bi, qi, g)),
                  pl.BlockSpec((NSA_GROUP, BQ, ncmp), lambda bi, g, qi: (g, qi, 0)),
                  pl.BlockSpec((NSA_GROUP, ncmp, BQ), lambda bi, g, qi: (g, 0, qi)),
                  pl.BlockSpec((nslc, ncmp), lambda bi, g, qi: (0, 0)),
                  pl.BlockSpec((NSA_GROUP, nd, BQ, BK), lambda bi, g, qi: (g, 0, 0, 0)),
                  pl.BlockSpec((NSA_GROUP, nwin, BQ, BK), lambda bi, g, qi: (g, 0, 0, 0))],
        out_specs=pl.BlockSpec((1, BQ, gw), lambda bi, g, qi: (bi, qi, g)),
        out_shape=jax.ShapeDtypeStruct((b, s, B_HEADS * LANE), F32),
        scratch_shapes=[pltpu.VMEM((nd, BQ, BK), F32)],
        compiler_params=_params("parallel", "parallel", "arbitrary"),
        name="nsa_attention",
    )(proj, kcmp, vcmp, proj, proj, proj, proj, gates, bcq, bct, covt, tzs, tzw)


def _moba_body(q_ref, k_ref, v_ref, avg_ref, tz_ref, o_ref, mask_ref):
    qi = pl.program_id(2)
    scale = HEAD_DIM ** -0.5
    nblk = avg_ref.shape[0]
    q = q_ref[0]
    kmean = jnp.dot(avg_ref[...], k_ref[0], preferred_element_type=F32)
    gate = _nt(kmean, q.astype(F32), precision=lax.Precision.HIGHEST)
    nb = lax.broadcasted_iota(I32, (nblk, BQ), 0)
    past = nb < qi
    beaten = jnp.zeros((nblk, BQ), F32)
    for np_ in range(nblk):
        row = gate[np_:np_ + 1, :]
        wins = (np_ < qi) & ((row > gate) | ((row == gate) & (np_ < nb)))
        beaten = beaten + wins.astype(F32)
    selt = (past & (beaten < MOBA_TOPK)).astype(BF16)
    eye = (lax.broadcasted_iota(I32, (BQ, BQ), 0) == lax.broadcasted_iota(I32, (BQ, BQ), 1)).astype(BF16)
    selq = _nt(eye, selt)
    for n in range(nblk):
        mask_ref[n] = jnp.broadcast_to(selq[:, n:n + 1], (BQ, BK))

    def score(kj, s):
        keep = (mask_ref[kj] > 0.5) | (kj == qi)
        return jnp.where(keep, s + tz_ref[0, qi - kj], NEG)

    o_ref[0] = _flash([q], k_ref, v_ref, 0, qi + 1, [score], scale)[0]


def _moba(proj, avg, tzc):
    b, s, _ = proj.shape
    nd = s // BK
    nblk = avg.shape[0]
    return pl.pallas_call(
        _moba_body,
        grid=(b, C_HEADS, s // BQ),
        in_specs=[pl.BlockSpec((1, BQ, LANE), lambda bi, h, qi: (bi, qi, CB_QC + h)),
                  pl.BlockSpec((1, s, LANE), lambda bi, h, qi: (bi, 0, CB_KCC + h)),
                  pl.BlockSpec((1, s, LANE), lambda bi, h, qi: (bi, 0, CB_VCC + h)),
                  pl.BlockSpec((nblk, s), lambda bi, h, qi: (0, 0)),
                  pl.BlockSpec((1, nd, BQ, BK), lambda bi, h, qi: (h, 0, 0, 0))],
        out_specs=pl.BlockSpec((1, BQ, LANE), lambda bi, h, qi: (bi, qi, h)),
        out_shape=jax.ShapeDtypeStruct((b, s, C_HEADS * LANE), F32),
        scratch_shapes=[pltpu.VMEM((nblk, BQ, BK), F32)],
        compiler_params=_params("parallel", "parallel", "arbitrary"),
        name="moba_attention",
    )(proj, proj, proj, avg, tzc)


def _outproj_body(oa_ref, ob_ref, oc_ref, g_ref, w_ref, h_ref, o_ref, y_ref):
    @pl.when(pl.program_id(1) == 0)
    def _():
        c0 = 0
        for ref in (oa_ref, ob_ref, oc_ref):
            x = ref[...]
            wd = x.shape[1]
            ms = jnp.mean(x * x, axis=-1, keepdims=True)
            y_ref[:, c0:c0 + wd] = (x * lax.rsqrt(ms + EPS) * g_ref[:, c0:c0 + wd]).astype(BF16)
            c0 += wd

    o_ref[...] = h_ref[...] + jnp.dot(y_ref[...], w_ref[...], preferred_element_type=F32)


def _outproj(oa, ob, oc, g, w, h, tm, tn):
    t, d = h.shape
    row = lambda a: pl.BlockSpec((tm, a.shape[1]), lambda i, j: (i, 0))
    return pl.pallas_call(
        _outproj_body,
        grid=(t // tm, d // tn),
        in_specs=[row(oa), row(ob), row(oc),
                  pl.BlockSpec((1, d), lambda i, j: (0, 0)),
                  pl.BlockSpec((d, tn), lambda i, j: (0, j)),
                  pl.BlockSpec((tm, tn), lambda i, j: (i, j))],
        out_specs=pl.BlockSpec((tm, tn), lambda i, j: (i, j)),
        out_shape=jax.ShapeDtypeStruct((t, d), F32),
        scratch_shapes=[pltpu.VMEM((tm, d), BF16)],
        compiler_params=_params("parallel", "arbitrary"),
        name="out_projection",
    )(oa, ob, oc, g.reshape(1, d), w, h)


def _topk_rows(x, k, payload=None):
    n, tm = x.shape
    rows = lax.broadcasted_iota(I32, x.shape, 0)
    slot = lax.broadcasted_iota(I32, (k, tm), 0)
    vals = jnp.zeros((k, tm), F32)
    idxs = jnp.zeros((k, tm), I32)
    for it in range(k):
        mx = jnp.max(x, axis=0, keepdims=True)
        idx = jnp.min(jnp.where(x == mx, rows, n), axis=0, keepdims=True)
        hit = rows == idx
        if payload is not None:
            idx = jnp.sum(jnp.where(hit, payload, 0), axis=0, keepdims=True)
        vals = jnp.where(slot == it, mx, vals)
        idxs = jnp.where(slot == it, idx, idxs)
        x = jnp.where(hit, -jnp.inf, x)
    return vals, idxs


def _peer_topk_body(q_ref, keys_ref, e_ref, g_ref):
    for h in range(PEER_HEADS):
        tops = []
        for c in range(2):
            hc = 2 * h + c
            sc = _nt(keys_ref[hc], q_ref[:, hc * LANE:(hc + 1) * LANE])
            tops.append(_topk_rows(sc, PEER_TOPK))
        (s0, i0), (s1, i1) = tops
        half = PEER_TOPK // 2
        sub = lax.broadcasted_iota(I32, (half, s0.shape[1]), 0)
        pieces = [s0[0:1] + s1]
        pieces_i = [i0[0:1] * PEER_NKEYS + i1]
        for a in range(1, half):
            keep = sub < PEER_TOPK // (a + 1)
            pieces.append(jnp.where(keep, s0[a:a + 1] + s1[0:half], -jnp.inf))
            pieces_i.append(i0[a:a + 1] * PEER_NKEYS + i1[0:half])
        pieces.append(s0[half:] + s1[0:1])
        pieces_i.append(i0[half:] * PEER_NKEYS + i1[0:1])
        cand = jnp.concatenate(pieces, axis=0)
        cand_i = jnp.concatenate(pieces_i, axis=0)
        bs, be = _topk_rows(cand, PEER_TOPK, payload=cand_i)
        e = jnp.exp(bs - jnp.max(bs, axis=0, keepdims=True))
        g_ref[h * PEER_TOPK:(h + 1) * PEER_TOPK, :] = e / jnp.sum(e, axis=0, keepdims=True)
        e_ref[h * PEER_TOPK:(h + 1) * PEER_TOPK, :] = be


def _peer_topk(qp, keys, tm):
    t, d = qp.shape
    hk = PEER_HEADS * PEER_TOPK
    return pl.pallas_call(
        _peer_topk_body,
        grid=(t // tm,),
        in_specs=[pl.BlockSpec((tm, d), lambda i: (i, 0)),
                  pl.BlockSpec(keys.shape, lambda i: (0, 0, 0))],
        out_specs=[pl.BlockSpec((hk, tm), lambda i: (0, i)),
                   pl.BlockSpec((hk, tm), lambda i: (0, i))],
        out_shape=[jax.ShapeDtypeStruct((hk, t), I32), jax.ShapeDtypeStruct((hk, t), F32)],
        compiler_params=_params("parallel"),
        name="peer_topk",
    )(qp, keys)


PEER_NBUF = 8
PEER_AHEAD = 6


def _gelu(x):
    return 0.5 * x * (1.0 + lax.erf(x * (2.0 ** -0.5)))


def _hi_lo(x):
    hi = x.astype(BF16)
    lo = (x - hi.astype(F32)).astype(BF16)
    return jnp.concatenate([hi, lo], axis=1)


def _peer_expert_body(ids0_ref, idsn_ref, h_ref, ln_ref, g_ref, uvw_ref, o_ref, xn_ref, acc_ref, lhs_ref,
                      buf_ref, sem_ref):
    tb, d = h_ref.shape
    hk = idsn_ref.shape[1]
    nrg = hk // 8
    nlt = d // LANE
    assert nrg == nlt and hk % (2 * nlt) == 0
    step = pl.program_id(0)
    x = h_ref[...]
    ms = jnp.mean(x * x, axis=-1, keepdims=True)
    xn_ref[...] = x * lax.rsqrt(ms + EPS) * ln_ref[...]

    def issue(ids_ref, row, slot, k0, k1):
        for k in range(k0, k1):
            e = ids_ref[row, k]
            pltpu.make_async_copy(uvw_ref.at[e], buf_ref.at[slot, :, k, :],
                                  sem_ref.at[slot]).start(priority=k % 2)

    def wait_all(slot):
        pltpu.make_async_copy(buf_ref.at[slot], buf_ref.at[slot], sem_ref.at[slot]).wait()

    def x_tiles(t):
        xrow = xn_ref[pl.ds(t, 1), :]
        return [jnp.broadcast_to(xrow[:, j * LANE:(j + 1) * LANE], (8, LANE)) for j in range(nlt)]

    def dots(xb, slot, r):
        a = None
        for j in range(nlt):
            w = buf_ref[slot, j, r * 8:(r + 1) * 8, :]
            pr = lax.bitcast_convert_type(lax.shift_left(w, jnp.uint32(16)), F32) * xb[j]
            a = pr if a is None else a + pr
        return a

    def coefficients(t, slot):
        act = jnp.sum(acc_ref[slot % 2].T, axis=0, keepdims=True)
        coef = jnp.broadcast_to(_gelu(act) * g_ref[pl.ds(t, 1), :], (8, hk))
        chi = coef.astype(BF16).astype(F32)
        return jnp.concatenate([chi, coef - chi], axis=0).astype(BF16)

    def weighted(lhs, slot, j):
        w = buf_ref[slot, j]
        v = lax.bitcast_convert_type(w & jnp.uint32(0xFFFF0000), F32).astype(BF16)
        yj = jnp.dot(lhs, v, preferred_element_type=F32)
        return yj[0:1, :] + yj[8:9, :]

    def flush(done):
        ya, yb, tp = done
        y = jnp.concatenate([ya[j:j + 1, :] for j in range(8)] + [yb[j:j + 1, :] for j in range(8)], axis=1)
        o_ref[pl.ds(tp, 1), :] = h_ref[pl.ds(tp, 1), :] + y

    def turn(t, slot, stages, done):
        tgt = (slot + PEER_AHEAD) % PEER_NBUF
        far = (slot + 2) % PEER_NBUF
        per = hk // (2 * nlt)
        if stages >= 3:
            wait_all(far)
            xb = x_tiles(t + 2)
        lhs = lhs_ref[slot % 2]
        ys, parts = [], []
        for c in range(nlt):
            if stages >= 3:
                parts.append(dots(xb, far, c))
            issue(idsn_ref, t, tgt, 2 * c * per, (2 * c + 1) * per)
            if 2 * c < nlt:
                ys.append(weighted(lhs, slot, 2 * c))
                ys.append(weighted(lhs, slot, 2 * c + 1))
            if 2 * c == nlt and stages >= 2:
                lhs_next = coefficients(t + 1, slot + 1)
            issue(idsn_ref, t, tgt, (2 * c + 1) * per, (2 * c + 2) * per)
        if stages >= 3:
            acc_ref[slot % 2] = jnp.concatenate(parts, axis=0)
        if stages >= 2:
            lhs_ref[(slot + 1) % 2] = lhs_next
        flush(done)
        return (jnp.concatenate(ys[:8], axis=0), jnp.concatenate(ys[8:], axis=0), t)

    @pl.when(step == 0)
    def _():
        for t0 in range(PEER_AHEAD):
            issue(ids0_ref, t0, t0, 0, hk)

    for t0 in range(2):
        wait_all(t0)
        xb0 = x_tiles(t0)
        acc_ref[t0] = jnp.concatenate([dots(xb0, t0, r) for r in range(nrg)], axis=0)
    lhs_ref[0] = coefficients(0, 0)

    def ring(i, stacks):
        done = stacks + (jnp.maximum(i * PEER_NBUF - 1, 0),)
        for slot in range(PEER_NBUF):
            done = turn(i * PEER_NBUF + slot, slot, 3, done)
        return done[:2]

    zero = jnp.zeros((8, LANE), F32)
    nring = (tb - 2) // PEER_NBUF
    done = lax.fori_loop(0, nring, ring, (zero, zero)) + (nring * PEER_NBUF - 1,)
    for t in range(nring * PEER_NBUF, tb - 2):
        done = turn(t, t % PEER_NBUF, 3, done)
    done = turn(tb - 2, (tb - 2) % PEER_NBUF, 2, done)
    done = turn(tb - 1, (tb - 1) % PEER_NBUF, 1, done)
    flush(done)

    @pl.when(step == pl.num_programs(0) - 1)
    def _():
        for t0 in range(PEER_AHEAD):
            wait_all((tb + t0) % PEER_NBUF)


def _peer_experts(ids, gates, h, ln, uvw, tb):
    t, d = h.shape
    hk = ids.shape[1]
    assert hk == LANE and tb % PEER_NBUF == 0 and PEER_NBUF % 2 == 0 and d % LANE == 0
    assert PEER_AHEAD <= min(8, PEER_NBUF - 2)
    ids_next = jnp.concatenate([ids[PEER_AHEAD:], ids[:PEER_AHEAD]], axis=0)
    slabs = uvw.reshape(uvw.shape[0], d // LANE, LANE)
    return pl.pallas_call(
        _peer_expert_body,
        grid=(t // tb,),
        in_specs=[pl.BlockSpec((8, hk), lambda i: (0, 0), memory_space=pltpu.SMEM),
                  pl.BlockSpec((tb, hk), lambda i: (i, 0), memory_space=pltpu.SMEM),
                  pl.BlockSpec((tb, d), lambda i: (i, 0)),
                  pl.BlockSpec((1, d), lambda i: (0, 0)),
                  pl.BlockSpec((tb, hk), lambda i: (i, 0)),
                  pl.BlockSpec(memory_space=pl.ANY)],
        out_specs=pl.BlockSpec((tb, d), lambda i: (i, 0)),
        out_shape=jax.ShapeDtypeStruct((t, d), F32),
        scratch_shapes=[pltpu.VMEM((tb, d), F32),
                        pltpu.VMEM((2, hk, LANE), F32),
                        pltpu.VMEM((2, 16, hk), BF16),
                        pltpu.VMEM((PEER_NBUF, d // LANE, hk, LANE), jnp.uint32),
                        pltpu.SemaphoreType.DMA((PEER_NBUF,))],
        compiler_params=_params("arbitrary"),
        name="peer_experts",
    )(ids[:8], ids_next, h, ln.reshape(1, d), gates, slabs)


def _ple_body(x_ref, g_ref, wg_ref, p_ref, wp_ref, h_ref, o_ref, xn_ref):
    @pl.when(pl.program_id(1) == 0)
    def _():
        x = x_ref[...]
        ms = jnp.mean(x * x, axis=-1, keepdims=True)
        xn_ref[...] = (x * lax.rsqrt(ms + EPS) * g_ref[...]).astype(BF16)

    z = jnp.dot(xn_ref[...], wg_ref[...], preferred_element_type=F32)
    pp = jnp.dot(p_ref[...].astype(BF16), wp_ref[...], preferred_element_type=F32)
    o_ref[...] = h_ref[...] + jax.nn.sigmoid(z) * pp


def _ple(h, g, wg, p, wp, tm, tn):
    t, d = h.shape
    pd = p.shape[1]
    return pl.pallas_call(
        _ple_body,
        grid=(t // tm, d // tn),
        in_specs=[pl.BlockSpec((tm, d), lambda i, j: (i, 0)),
                  pl.BlockSpec((1, d), lambda i, j: (0, 0)),
                  pl.BlockSpec((d, tn), lambda i, j: (0, j)),
                  pl.BlockSpec((tm, pd), lambda i, j: (i, 0)),
                  pl.BlockSpec((pd, tn), lambda i, j: (0, j)),
                  pl.BlockSpec((tm, tn), lambda i, j: (i, j))],
        out_specs=pl.BlockSpec((tm, tn), lambda i, j: (i, j)),
        out_shape=jax.ShapeDtypeStruct((t, d), F32),
        scratch_shapes=[pltpu.VMEM((tm, d), BF16)],
        compiler_params=_params("parallel", "arbitrary"),
        name="ple_gate",
    )(h, g.reshape(1, d), wg, p, wp, h)


def _rmsnorm_body(x_ref, g_ref, o_ref):
    x = x_ref[...]
    ms = jnp.mean(x * x, axis=-1, keepdims=True)
    o_ref[...] = x * lax.rsqrt(ms + EPS) * g_ref[...]


def _rmsnorm(x, g, tm):
    t, d = x.shape
    return pl.pallas_call(
        _rmsnorm_body,
        grid=(t // tm,),
        in_specs=[pl.BlockSpec((tm, d), lambda i: (i, 0)), pl.BlockSpec((1, d), lambda i: (0, 0))],
        out_specs=pl.BlockSpec((tm, d), lambda i: (i, 0)),
        out_shape=jax.ShapeDtypeStruct((t, d), F32),
        compiler_params=_params("parallel"),
        name="final_rmsnorm",
    )(x, g.reshape(1, d))


def _toeplitz(vec, nq, nk):
    assert nq == nk
    lead = vec.shape[:-1]
    g = int(np.prod(lead))
    w = jnp.concatenate([jnp.zeros(lead + (1,), vec.dtype), vec[..., ::-1]], axis=-1).reshape(g, 1, 2 * nk)

    def body(w_ref, o_ref):
        rows = jnp.broadcast_to(w_ref[0], (nq, 2 * nk))
        o_ref[0] = pltpu.roll(rows, 0, 1, stride=1, stride_axis=0)[:, nk:]

    out = pl.pallas_call(
        body,
        grid=(g,),
        in_specs=[pl.BlockSpec((1, 1, 2 * nk), lambda i: (i, 0, 0))],
        out_specs=pl.BlockSpec((1, nq, nk), lambda i: (i, 0, 0)),
        out_shape=jax.ShapeDtypeStruct((g, nq, nk), vec.dtype),
        compiler_params=_params("parallel"),
        name="toeplitz_tiles",
    )(w)
    return out.reshape(lead + (nq, nk))


def _bias_tables(rel_bias, s):
    nd = s // BK
    heads = rel_bias.shape[1]
    bdt = rel_bias[_rel_bucket(jnp.arange(s))].astype(F32).T
    dd = (jnp.arange(nd)[:, None, None] * BK + jnp.arange(BQ)[None, :, None]
          - jnp.arange(BK)[None, None, :])
    ext = jnp.pad(bdt, ((0, 0), (BK - 1, 0)))
    segs = jnp.stack([ext[:, dl * BK:dl * BK + BQ + BK - 1] for dl in range(nd)], axis=1)
    tz = _toeplitz(segs, BQ, BK)
    causal = dd >= 0
    mult = sum(((dd % dil == 0) & (dd // dil <= window // dil)).astype(F32) for window, dil in DIL_PATTERNS)
    ok = causal & (mult > 0)
    tzd = jnp.where(ok, tz[:A_HEADS] + jnp.log(jnp.where(ok, mult, 1.0)), NEG)
    tzb = tz[A_HEADS:A_HEADS + B_HEADS]
    tzs = jnp.where(causal, tzb, NEG)
    nwin = -(-(NSA_WINDOW - 1) // BK) + 1
    tzw = jnp.where(causal & (dd <= NSA_WINDOW - 1), tzb, NEG)[:, :nwin]
    tzc = jnp.where(causal, tz[A_HEADS + B_HEADS:], NEG)
    ncmp = s // CMP_STRIDE
    na = s // CMP_STRIDE
    nvec = na + ncmp - 1
    lo = CMP_STRIDE * (ncmp - 1) + CMP_LEN - 1
    bdb = bdt[A_HEADS:A_HEADS + B_HEADS]
    gext = jnp.concatenate([jnp.full((B_HEADS, lo), NEG, F32), bdb], axis=1)
    vecs = gext[:, :CMP_STRIDE * nvec].reshape(B_HEADS, nvec, CMP_STRIDE).transpose(0, 2, 1)
    bcq = _toeplitz(vecs, na, ncmp)
    bcq = bcq.transpose(0, 2, 1, 3).reshape(B_HEADS, s, ncmp)
    bct = jnp.transpose(bcq, (0, 2, 1))
    nslc = s // SLC_LEN
    cstart = jnp.arange(ncmp) * CMP_STRIDE
    sstart = jnp.arange(nslc) * SLC_LEN
    covt = ((cstart[None, :] < sstart[:, None] + SLC_LEN)
            & (cstart[None, :] + CMP_LEN > sstart[:, None])).astype(F32)
    nblk = s // MOBA_BLK
    avg = ((jnp.arange(s)[None, :] // MOBA_BLK == jnp.arange(nblk)[:, None]).astype(F32)
           / MOBA_BLK).astype(BF16)
    return tzd, tzs, tzw, tzc, bcq, bct, covt, avg


def _pack_uv(u, v):
    ub = lax.bitcast_convert_type(u.astype(BF16), jnp.uint16).astype(jnp.uint32)
    vb = lax.bitcast_convert_type(v.astype(BF16), jnp.uint16).astype(jnp.uint32)
    return ub | (vb << 16)


def _reorder_w_in(w):
    g0 = 3 * A_HEADS * LANE + B_HEADS * LANE + 6 * NSA_KV * LANE
    ng = 3 * B_HEADS
    main = jnp.concatenate([w[:, :g0], w[:, g0 + ng:]], axis=1)
    per = ng // NSA_KV
    gate = jnp.concatenate(
        [jnp.pad(w[:, g0 + g * per:g0 + (g + 1) * per], ((0, 0), (0, LANE - per))) for g in range(NSA_KV)],
        axis=1)
    return main.astype(BF16), gate.astype(BF16)


def kernel(x, p, ln_mix, w_in, cmp_wk, cmp_wv, cmp_pos, out_norm, w_out, rel_bias, ln_ffn, peer_wq,
           peer_keys, peer_u, peer_v, ln_ple, ple_gate, ple_proj, ln_final):
    b, s, d = x.shape
    t = b * s
    depth = w_in.shape[0]
    assert s % BQ == 0 and BQ == BK == MOBA_BLK and d % LANE == 0
    tm = 512
    tzd, tzs, tzw, tzc, bcq, bct, covt, avg = _bias_tables(rel_bias, s)
    h = x.reshape(t, d)
    for i in range(depth):
        w_main, w_gate = _reorder_w_in(w_in[i])
        proj = _normmm(h, ln_mix[i], w_main, BF16, tm, 1024, "in_projection").reshape(b, s, MAIN_COLS)
        gates = _normmm(h, ln_mix[i], w_gate, F32, tm, w_gate.shape[1], "gate_projection")
        gates = gates.reshape(b, s, NSA_KV * LANE)
        oa = _dilated(proj, tzd)
        kcmp, vcmp = _compress(proj, cmp_wk[i], cmp_wv[i], cmp_pos[i])
        ob = _nsa(proj, gates, kcmp, vcmp, bcq, bct, covt, tzs, tzw)
        oc = _moba(proj, avg, tzc)
        h = _outproj(oa.reshape(t, -1), ob.reshape(t, -1), oc.reshape(t, -1), out_norm[i],
                     w_out[i].astype(BF16), h, tm, 1024)
        qp = _normmm(h, ln_ffn[i], peer_wq[i].astype(BF16), F32, tm, 1024, "peer_query")
        keys = peer_keys[i].reshape(PEER_HEADS * 2, PEER_NKEYS, -1)
        e_t, g_t = _peer_topk(qp, keys, 256)
        h = _peer_experts(e_t.T, g_t.T, h, ln_ffn[i], _pack_uv(peer_u[i], peer_v[i]), LANE)
        h = _ple(h, ln_ple[i], ple_gate[i].astype(BF16), p[i].reshape(t, -1), ple_proj[i].astype(BF16),
                 tm, 1024)
    return _rmsnorm(h, ln_final, tm).reshape(b, s, d)
```

```python
import functools
import math

import numpy as np
import jax
import jax.numpy as jnp
from jax import lax
from jax.experimental import pallas as pl
from jax.experimental.pallas import tpu as pltpu

F32 = jnp.float32
BF16 = jnp.bfloat16
I32 = jnp.int32

HEAD_DIM = 128
A_HEADS, B_HEADS, C_HEADS = 6, 6, 4
NSA_KV, NSA_GROUP = 2, 3
DIL_PATTERNS = ((128, 1), (512, 4), (2048, 16))
CMP_LEN, CMP_STRIDE = 32, 16
SLC_LEN, SLC_TOPN = 64, 8
NSA_WINDOW = 512
FORCE = 1e4
MOBA_BLK, MOBA_TOPK = 256, 3
REL_BUCKETS, REL_MAX_DIST = 32, 1024
PEER_HEADS, PEER_NKEYS, PEER_TOPK = 8, 128, 16
EPS = 1e-6

LANE = 128
BQ = 256
BK = 256
NEG = -1e30
HALF_NEG = -5e29
VMEM_LIMIT = 56 * 1024 * 1024

CB_QA, CB_KA, CB_VA = 0, 6, 12
CB_QB, CB_KC, CB_VC, CB_KS, CB_VS, CB_KW, CB_VW = 18, 24, 26, 28, 30, 32, 34
CB_QC, CB_KCC, CB_VCC = 36, 40, 44
MAIN_COLS = 48 * LANE


def _nt(a, b, precision=None):
    return lax.dot_general(a, b, (((1,), (1,)), ((), ())), preferred_element_type=F32,
                           precision=precision)


def _rel_bucket(dist):
    exact = REL_BUCKETS // 2
    d = jnp.maximum(dist, 0)
    logd = jnp.log(jnp.maximum(d, 1).astype(F32) / exact)
    large = exact + (logd / math.log(REL_MAX_DIST / exact) * (REL_BUCKETS - exact)).astype(I32)
    large = jnp.clip(large, exact, REL_BUCKETS - 1)
    return jnp.where(d < exact, d, large)


def _masked_softmax(s, axis):
    valid = s > HALF_NEG
    m = jnp.max(s, axis=axis, keepdims=True)
    e = jnp.where(valid, jnp.exp(s - m), 0.0)
    z = jnp.sum(e, axis=axis, keepdims=True)
    zs = jnp.where(z > 0, z, 1.0)
    return e / zs


def _params(*sem):
    return pltpu.CompilerParams(dimension_semantics=sem, vmem_limit_bytes=VMEM_LIMIT)


def _normmm_body(x_ref, g_ref, w_ref, o_ref, xn_ref):
    @pl.when(pl.program_id(1) == 0)
    def _():
        x = x_ref[...]
        ms = jnp.mean(x * x, axis=-1, keepdims=True)
        xn_ref[...] = (x * lax.rsqrt(ms + EPS) * g_ref[...]).astype(BF16)

    o_ref[...] = jnp.dot(xn_ref[...], w_ref[...], preferred_element_type=F32).astype(o_ref.dtype)


def _normmm(x, g, w, out_dtype, tm, tn, name):
    t, d = x.shape
    n = w.shape[1]
    return pl.pallas_call(
        _normmm_body,
        grid=(t // tm, n // tn),
        in_specs=[pl.BlockSpec((tm, d), lambda i, j: (i, 0)),
                  pl.BlockSpec((1, d), lambda i, j: (0, 0)),
                  pl.BlockSpec((d, tn), lambda i, j: (0, j))],
        out_specs=pl.BlockSpec((tm, tn), lambda i, j: (i, j)),
        out_shape=jax.ShapeDtypeStruct((t, n), out_dtype),
        scratch_shapes=[pltpu.VMEM((tm, d), BF16)],
        compiler_params=_params("parallel", "arbitrary"),
        name=name,
    )(x, g.reshape(1, d), w)


def _flash(qs, k_ref, v_ref, lo, hi, score_fns, scale):
    def tile(kj):
        off = pl.multiple_of(kj * BK, BK)
        return k_ref[0, pl.ds(off, BK), :], v_ref[0, pl.ds(off, BK), :]

    def update(state, blocks):
        m, l, acc = state
        m_new = m
        for s, _ in blocks:
            m_new = jnp.maximum(m_new, jnp.max(s, axis=1, keepdims=True))
        alpha = jnp.exp(m - m_new)
        l = alpha * l
        acc = alpha * acc
        for s, v in blocks:
            p = jnp.exp(s - m_new)
            l = l + jnp.sum(p, axis=1, keepdims=True)
            acc = acc + jnp.dot(p.astype(BF16), v, preferred_element_type=F32)
        return m_new, l, acc

    def step(kjs, states):
        kv = [tile(kj) for kj in kjs]
        return tuple(
            update(states[i], [(score_fns[i](kj, _nt(q, k) * scale), v) for kj, (k, v) in zip(kjs, kv)])
            for i, q in enumerate(qs))

    states = tuple((jnp.full((BQ, 1), NEG, F32), jnp.zeros((BQ, 1), F32), jnp.zeros((BQ, HEAD_DIM), F32))
                   for _ in qs)
    start = lo
    for width in ((4, 2, 1) if len(qs) == 1 else (2, 1)):
        shift = width.bit_length() - 1
        count = lax.shift_right_logical(hi - start, shift)

        def group(p, st, start=start, width=width):
            return step([start + width * p + i for i in range(width)], st)

        states = lax.fori_loop(0, count, group, states)
        start = start + count * width
    return [acc / l for _, l, acc in states]


def _dilated_body(q_ref, k_ref, v_ref, tz_ref, o_ref):
    qi = pl.program_id(2)
    scale = HEAD_DIM ** -0.5

    def score(kj, s):
        return s + tz_ref[0, qi - kj]

    o_ref[0] = _flash([q_ref[0]], k_ref, v_ref, 0, qi + 1, [score], scale)[0]


def _dilated(proj, tzd):
    b, s, _ = proj.shape
    nd = s // BK
    return pl.pallas_call(
        _dilated_body,
        grid=(b, A_HEADS, s // BQ),
        in_specs=[pl.BlockSpec((1, BQ, LANE), lambda bi, h, qi: (bi, qi, CB_QA + h)),
                  pl.BlockSpec((1, s, LANE), lambda bi, h, qi: (bi, 0, CB_KA + h)),
                  pl.BlockSpec((1, s, LANE), lambda bi, h, qi: (bi, 0, CB_VA + h)),
                  pl.BlockSpec((1, nd, BQ, BK), lambda bi, h, qi: (h, 0, 0, 0))],
        out_specs=pl.BlockSpec((1, BQ, LANE), lambda bi, h, qi: (bi, qi, h)),
        out_shape=jax.ShapeDtypeStruct((b, s, A_HEADS * LANE), F32),
        compiler_params=_params("parallel", "parallel", "arbitrary"),
        name="dilated_attention",
    )(proj, proj, proj, tzd)


def _compress_body(kc_ref, vc_ref, wk_ref, wv_ref, pos_ref, ko_ref, vo_ref, xk_ref, xv_ref):
    s = kc_ref.shape[1]
    ncmp = ko_ref.shape[2]
    xk_ref[pl.ds(0, s), :] = kc_ref[0].astype(F32)
    xv_ref[pl.ds(0, s), :] = vc_ref[0].astype(F32)
    xk_ref[pl.ds(s, CMP_LEN), :] = jnp.zeros((CMP_LEN, HEAD_DIM), F32)
    xv_ref[pl.ds(s, CMP_LEN), :] = jnp.zeros((CMP_LEN, HEAD_DIM), F32)
    acck = jnp.zeros((ncmp, HEAD_DIM), F32)
    accv = jnp.zeros((ncmp, HEAD_DIM), F32)
    for l in range(CMP_LEN):
        pos = pos_ref[pl.ds(l, 1), :]
        rk = (xk_ref[pl.ds(l, ncmp, stride=CMP_STRIDE), :] + pos).astype(BF16)
        rv = (xv_ref[pl.ds(l, ncmp, stride=CMP_STRIDE), :] + pos).astype(BF16)
        acck = acck + jnp.dot(rk, wk_ref[l], preferred_element_type=F32)
        accv = accv + jnp.dot(rv, wv_ref[l], preferred_element_type=F32)
    ko_ref[0, 0] = acck.astype(BF16)
    vo_ref[0, 0] = accv.astype(BF16)


def _compress(proj, cmp_wk, cmp_wv, cmp_pos):
    b, s, _ = proj.shape
    ncmp = s // CMP_STRIDE
    out = jax.ShapeDtypeStruct((b, NSA_KV, ncmp, HEAD_DIM), BF16)
    return pl.pallas_call(
        _compress_body,
        grid=(b, NSA_KV),
        in_specs=[pl.BlockSpec((1, s, LANE), lambda bi, g: (bi, 0, CB_KC + g)),
                  pl.BlockSpec((1, s, LANE), lambda bi, g: (bi, 0, CB_VC + g)),
                  pl.BlockSpec((CMP_LEN, HEAD_DIM, HEAD_DIM), lambda bi, g: (0, 0, 0)),
                  pl.BlockSpec((CMP_LEN, HEAD_DIM, HEAD_DIM), lambda bi, g: (0, 0, 0)),
                  pl.BlockSpec((CMP_LEN, HEAD_DIM), lambda bi, g: (0, 0))],
        out_specs=[pl.BlockSpec((1, 1, ncmp, HEAD_DIM), lambda bi, g: (bi, g, 0, 0)),
                   pl.BlockSpec((1, 1, ncmp, HEAD_DIM), lambda bi, g: (bi, g, 0, 0))],
        out_shape=[out, out],
        scratch_shapes=[pltpu.VMEM((s + CMP_LEN, HEAD_DIM), F32),
                        pltpu.VMEM((s + CMP_LEN, HEAD_DIM), F32)],
        compiler_params=_params("parallel", "parallel"),
        name="nsa_compress",
    )(proj, proj, cmp_wk.astype(BF16), cmp_wv.astype(BF16), cmp_pos)


def _nsa_body(q_ref, kcmp_ref, vcmp_ref, ks_ref, vs_ref, kw_ref, vw_ref, gt_ref, bcq_ref, bct_ref,
              covt_ref, tzs_ref, tzw_ref, o_ref, mask_ref):
    qi = pl.program_id(2)
    scale = HEAD_DIM ** -0.5
    nslc = covt_ref.shape[0]
    nkt = mask_ref.shape[0]
    kcmp = kcmp_ref[0, 0]
    vcmp = vcmp_ref[0, 0]
    gates = jax.nn.sigmoid(gt_ref[0])

    psum = jnp.zeros((kcmp.shape[0], BQ), F32)
    for r in range(NSA_GROUP):
        q = q_ref[0, :, r * LANE:(r + 1) * LANE]
        p = _masked_softmax(_nt(q, kcmp) * scale + bcq_ref[r], axis=1)
        o_cmp = jnp.dot(p.astype(BF16), vcmp, preferred_element_type=F32)
        o_ref[0, :, r * LANE:(r + 1) * LANE] = gates[:, 3 * r:3 * r + 1] * o_cmp
        psum = psum + _masked_softmax(_nt(kcmp, q) * scale + bct_ref[r], axis=0)
    imp = jnp.dot(covt_ref[...], psum, preferred_element_type=F32,
                  precision=lax.Precision.HIGHEST)
    jb = lax.broadcasted_iota(I32, (nslc, BQ), 0)
    qpos = qi * BQ + lax.broadcasted_iota(I32, (nslc, BQ), 1)
    qblk = lax.shift_right_logical(qpos, int(math.log2(SLC_LEN)))
    forced = (jb == 0) | (jb == qblk) | (jb == qblk - 1)
    imp = jnp.where(forced, FORCE, jnp.where(jb > qblk, -FORCE, imp))
    beaten = jnp.zeros((nslc, BQ), F32)
    for jp in range(nslc):
        row = imp[jp:jp + 1, :]
        wins = (row > imp) | ((row == imp) & (jp < jb))
        beaten = beaten + wins.astype(F32)
    selt = (beaten < SLC_TOPN).astype(BF16)
    eye = (lax.broadcasted_iota(I32, (BQ, BQ), 0) == lax.broadcasted_iota(I32, (BQ, BQ), 1)).astype(BF16)
    selq = _nt(eye, selt).astype(BF16)
    for t in range(nkt):
        kblk = (t * BK + lax.broadcasted_iota(I32, (nslc, BK), 1)) // SLC_LEN
        expand = (kblk == lax.broadcasted_iota(I32, (nslc, BK), 0)).astype(BF16)
        mask_ref[t] = jnp.dot(selq, expand, preferred_element_type=F32)

    qs = [q_ref[0, :, r * LANE:(r + 1) * LANE] for r in range(NSA_GROUP)]
    sel_fns = [lambda kj, s, r=r: jnp.where(mask_ref[kj] > 0.5, s + tzs_ref[r, qi - kj], NEG)
               for r in range(NSA_GROUP)]
    win_fns = [lambda kj, s, r=r: s + tzw_ref[r, qi - kj] for r in range(NSA_GROUP)]
    nwin = tzw_ref.shape[1]
    o_sel = _flash(qs, ks_ref, vs_ref, 0, qi + 1, sel_fns, scale)
    o_win = _flash(qs, kw_ref, vw_ref, jnp.maximum(qi - (nwin - 1), 0), qi + 1, win_fns, scale)
    for r in range(NSA_GROUP):
        o_ref[0, :, r * LANE:(r + 1) * LANE] += (gates[:, 3 * r + 1:3 * r + 2] * o_sel[r]
                                                 + gates[:, 3 * r + 2:3 * r + 3] * o_win[r])


def _nsa(proj, gates, kcmp, vcmp, bcq, bct, covt, tzs, tzw):
    b, s, _ = proj.shape
    nd = s // BK
    ncmp = kcmp.shape[2]
    nslc = covt.shape[0]
    nwin = tzw.shape[1]
    gw = NSA_GROUP * LANE
    kv = lambda cb: pl.BlockSpec((1, s, LANE), lambda bi, g, qi: (bi, 0, cb + g))
    return pl.pallas_call(
        _nsa_body,
        grid=(b, NSA_KV, s // BQ),
        in_specs=[pl.BlockSpec((1, BQ, gw), lambda bi, g, qi: (bi, qi, CB_QB // NSA_GROUP + g)),
                  pl.BlockSpec((1, 1, ncmp, HEAD_DIM), lambda bi, g, qi: (bi, g, 0, 0)),
                  pl.BlockSpec((1, 1, ncmp, HEAD_DIM), lambda bi, g, qi: (bi, g, 0, 0)),
                  kv(CB_KS), kv(CB_VS), kv(CB_KW), kv(CB_VW),
                  pl.BlockSpec((1, BQ, LANE), lambda bi, g, qi: (bi, qi, g)),
                  pl.BlockSpec((NSA_GROUP, BQ, ncmp), lambda bi, g, qi: (g, qi, 0)),
                  pl.BlockSpec((NSA_GROUP, ncmp, BQ), lambda bi, g, qi: (g, 0, qi)),
                  pl.BlockSpec((nslc, ncmp), lambda bi, g, qi: (0, 0)),
                  pl.BlockSpec((NSA_GROUP, nd, BQ, BK), lambda bi, g, qi: (g, 0, 0, 0)),
                  pl.BlockSpec((NSA_GROUP, nwin, BQ, BK), lambda bi, g, qi: (g, 0, 0, 0))],
        out_specs=pl.BlockSpec((1, BQ, gw), lambda bi, g, qi: (bi, qi, g)),
        out_shape=jax.ShapeDtypeStruct((b, s, B_HEADS * LANE), F32),
        scratch_shapes=[pltpu.VMEM((nd, BQ, BK), F32)],
        compiler_params=_params("parallel", "parallel", "arbitrary"),
        name="nsa_attention",
    )(proj, kcmp, vcmp, proj, proj, proj, proj, gates, bcq, bct, covt, tzs, tzw)


def _moba_body(q_ref, k_ref, v_ref, avg_ref, tz_ref, o_ref, mask_ref):
    qi = pl.program_id(2)
    scale = HEAD_DIM ** -0.5
    nblk = avg_ref.shape[0]
    q = q_ref[0]
    kmean = jnp.dot(avg_ref[...], k_ref[0], preferred_element_type=F32)
    gate = _nt(kmean, q.astype(F32), precision=lax.Precision.HIGHEST)
    nb = lax.broadcasted_iota(I32, (nblk, BQ), 0)
    past = nb < qi
    beaten = jnp.zeros((nblk, BQ), F32)
    for np_ in range(nblk):
        row = gate[np_:np_ + 1, :]
        wins = (np_ < qi) & ((row > gate) | ((row == gate) & (np_ < nb)))
        beaten = beaten + wins.astype(F32)
    selt = (past & (beaten < MOBA_TOPK)).astype(BF16)
    eye = (lax.broadcasted_iota(I32, (BQ, BQ), 0) == lax.broadcasted_iota(I32, (BQ, BQ), 1)).astype(BF16)
    selq = _nt(eye, selt)
    for n in range(nblk):
        mask_ref[n] = jnp.broadcast_to(selq[:, n:n + 1], (BQ, BK))

    def score(kj, s):
        keep = (mask_ref[kj] > 0.5) | (kj == qi)
        return jnp.where(keep, s + tz_ref[0, qi - kj], NEG)

    o_ref[0] = _flash([q], k_ref, v_ref, 0, qi + 1, [score], scale)[0]


def _moba(proj, avg, tzc):
    b, s, _ = proj.shape
    nd = s // BK
    nblk = avg.shape[0]
    return pl.pallas_call(
        _moba_body,
        grid=(b, C_HEADS, s // BQ),
        in_specs=[pl.BlockSpec((1, BQ, LANE), lambda bi, h, qi: (bi, qi, CB_QC + h)),
                  pl.BlockSpec((1, s, LANE), lambda bi, h, qi: (bi, 0, CB_KCC + h)),
                  pl.BlockSpec((1, s, LANE), lambda bi, h, qi: (bi, 0, CB_VCC + h)),
                  pl.BlockSpec((nblk, s), lambda bi, h, qi: (0, 0)),
                  pl.BlockSpec((1, nd, BQ, BK), lambda bi, h, qi: (h, 0, 0, 0))],
        out_specs=pl.BlockSpec((1, BQ, LANE), lambda bi, h, qi: (bi, qi, h)),
        out_shape=jax.ShapeDtypeStruct((b, s, C_HEADS * LANE), F32),
        scratch_shapes=[pltpu.VMEM((nblk, BQ, BK), F32)],
        compiler_params=_params("parallel", "parallel", "arbitrary"),
        name="moba_attention",
    )(proj, proj, proj, avg, tzc)


def _outproj_body(oa_ref, ob_ref, oc_ref, g_ref, w_ref, h_ref, o_ref, y_ref):
    @pl.when(pl.program_id(1) == 0)
    def _():
        c0 = 0
        for ref in (oa_ref, ob_ref, oc_ref):
            x = ref[...]
            wd = x.shape[1]
            ms = jnp.mean(x * x, axis=-1, keepdims=True)
            y_ref[:, c0:c0 + wd] = (x * lax.rsqrt(ms + EPS) * g_ref[:, c0:c0 + wd]).astype(BF16)
            c0 += wd

    o_ref[...] = h_ref[...] + jnp.dot(y_ref[...], w_ref[...], preferred_element_type=F32)


def _outproj(oa, ob, oc, g, w, h, tm, tn):
    t, d = h.shape
    row = lambda a: pl.BlockSpec((tm, a.shape[1]), lambda i, j: (i, 0))
    return pl.pallas_call(
        _outproj_body,
        grid=(t // tm, d // tn),
        in_specs=[row(oa), row(ob), row(oc),
                  pl.BlockSpec((1, d), lambda i, j: (0, 0)),
                  pl.BlockSpec((d, tn), lambda i, j: (0, j)),
                  pl.BlockSpec((tm, tn), lambda i, j: (i, j))],
        out_specs=pl.BlockSpec((tm, tn), lambda i, j: (i, j)),
        out_shape=jax.ShapeDtypeStruct((t, d), F32),
        scratch_shapes=[pltpu.VMEM((tm, d), BF16)],
        compiler_params=_params("parallel", "arbitrary"),
        name="out_projection",
    )(oa, ob, oc, g.reshape(1, d), w, h)


def _topk_rows(x, k, payload=None):
    n, tm = x.shape
    rows = lax.broadcasted_iota(I32, x.shape, 0)
    slot = lax.broadcasted_iota(I32, (k, tm), 0)
    vals = jnp.zeros((k, tm), F32)
    idxs = jnp.zeros((k, tm), I32)
    for it in range(k):
        mx = jnp.max(x, axis=0, keepdims=True)
        idx = jnp.min(jnp.where(x == mx, rows, n), axis=0, keepdims=True)
        hit = rows == idx
        if payload is not None:
            idx = jnp.sum(jnp.where(hit, payload, 0), axis=0, keepdims=True)
        vals = jnp.where(slot == it, mx, vals)
        idxs = jnp.where(slot == it, idx, idxs)
        x = jnp.where(hit, -jnp.inf, x)
    return vals, idxs


def _peer_topk_body(q_ref, keys_ref, e_ref, g_ref):
    for h in range(PEER_HEADS):
        tops = []
        for c in range(2):
            hc = 2 * h + c
            sc = _nt(keys_ref[hc], q_ref[:, hc * LANE:(hc + 1) * LANE])
            tops.append(_topk_rows(sc, PEER_TOPK))
        (s0, i0), (s1, i1) = tops
        half = PEER_TOPK // 2
        sub = lax.broadcasted_iota(I32, (half, s0.shape[1]), 0)
        pieces = [s0[0:1] + s1]
        pieces_i = [i0[0:1] * PEER_NKEYS + i1]
        for a in range(1, half):
            keep = sub < PEER_TOPK // (a + 1)
            pieces.append(jnp.where(keep, s0[a:a + 1] + s1[0:half], -jnp.inf))
            pieces_i.append(i0[a:a + 1] * PEER_NKEYS + i1[0:half])
        pieces.append(s0[half:] + s1[0:1])
        pieces_i.append(i0[half:] * PEER_NKEYS + i1[0:1])
        cand = jnp.concatenate(pieces, axis=0)
        cand_i = jnp.concatenate(pieces_i, axis=0)
        bs, be = _topk_rows(cand, PEER_TOPK, payload=cand_i)
        e = jnp.exp(bs - jnp.max(bs, axis=0, keepdims=True))
        g_ref[h * PEER_TOPK:(h + 1) * PEER_TOPK, :] = e / jnp.sum(e, axis=0, keepdims=True)
        e_ref[h * PEER_TOPK:(h + 1) * PEER_TOPK, :] = be


def _peer_topk(qp, keys, tm):
    t, d = qp.shape
    hk = PEER_HEADS * PEER_TOPK
    return pl.pallas_call(
        _peer_topk_body,
        grid=(t // tm,),
        in_specs=[pl.BlockSpec((tm, d), lambda i: (i, 0)),
                  pl.BlockSpec(keys.shape, lambda i: (0, 0, 0))],
        out_specs=[pl.BlockSpec((hk, tm), lambda i: (0, i)),
                   pl.BlockSpec((hk, tm), lambda i: (0, i))],
        out_shape=[jax.ShapeDtypeStruct((hk, t), I32), jax.ShapeDtypeStruct((hk, t), F32)],
        compiler_params=_params("parallel"),
        name="peer_topk",
    )(qp, keys)


PEER_NBUF = 8
PEER_AHEAD = 6


def _gelu(x):
    return 0.5 * x * (1.0 + lax.erf(x * (2.0 ** -0.5)))


def _hi_lo(x):
    hi = x.astype(BF16)
    lo = (x - hi.astype(F32)).astype(BF16)
    return jnp.concatenate([hi, lo], axis=1)


def _peer_expert_body(ids0_ref, idsn_ref, h_ref, ln_ref, g_ref, uvw_ref, o_ref, xn_ref, acc_ref, lhs_ref,
                      buf_ref, sem_ref):
    tb, d = h_ref.shape
    hk = idsn_ref.shape[1]
    nrg = hk // 8
    nlt = d // LANE
    assert nrg == nlt and hk % (2 * nlt) == 0
    step = pl.program_id(0)
    x = h_ref[...]
    ms = jnp.mean(x * x, axis=-1, keepdims=True)
    xn_ref[...] = x * lax.rsqrt(ms + EPS) * ln_ref[...]

    def issue(ids_ref, row, slot, k0, k1):
        for k in range(k0, k1):
            e = ids_ref[row, k]
            pltpu.make_async_copy(uvw_ref.at[e], buf_ref.at[slot, :, k, :],
                                  sem_ref.at[slot]).start(priority=k % 2)

    def wait_all(slot):
        pltpu.make_async_copy(buf_ref.at[slot], buf_ref.at[slot], sem_ref.at[slot]).wait()

    def x_tiles(t):
        xrow = xn_ref[pl.ds(t, 1), :]
        return [jnp.broadcast_to(xrow[:, j * LANE:(j + 1) * LANE], (8, LANE)) for j in range(nlt)]

    def dots(xb, slot, r):
        a = None
        for j in range(nlt):
            w = buf_ref[slot, j, r * 8:(r + 1) * 8, :]
            pr = lax.bitcast_convert_type(lax.shift_left(w, jnp.uint32(16)), F32) * xb[j]
            a = pr if a is None else a + pr
        return a

    def coefficients(t, slot):
        act = jnp.sum(acc_ref[slot % 2].T, axis=0, keepdims=True)
        coef = jnp.broadcast_to(_gelu(act) * g_ref[pl.ds(t, 1), :], (8, hk))
        chi = coef.astype(BF16).astype(F32)
        return jnp.concatenate([chi, coef - chi], axis=0).astype(BF16)

    def weighted(lhs, slot, j):
        w = buf_ref[slot, j]
        v = lax.bitcast_convert_type(w & jnp.uint32(0xFFFF0000), F32).astype(BF16)
        yj = jnp.dot(lhs, v, preferred_element_type=F32)
        return yj[0:1, :] + yj[8:9, :]

    def flush(done):
        ya, yb, tp = done
        y = jnp.concatenate([ya[j:j + 1, :] for j in range(8)] + [yb[j:j + 1, :] for j in range(8)], axis=1)
        o_ref[pl.ds(tp, 1), :] = h_ref[pl.ds(tp, 1), :] + y

    def turn(t, slot, stages, done):
        tgt = (slot + PEER_AHEAD) % PEER_NBUF
        far = (slot + 2) % PEER_NBUF
        per = hk // (2 * nlt)
        if stages >= 3:
            wait_all(far)
            xb = x_tiles(t + 2)
        lhs = lhs_ref[slot % 2]
        ys, parts = [], []
        for c in range(nlt):
            if stages >= 3:
                parts.append(dots(xb, far, c))
            issue(idsn_ref, t, tgt, 2 * c * per, (2 * c + 1) * per)
            if 2 * c < nlt:
                ys.append(weighted(lhs, slot, 2 * c))
                ys.append(weighted(lhs, slot, 2 * c + 1))
            if 2 * c == nlt and stages >= 2:
                lhs_next = coefficients(t + 1, slot + 1)
            issue(idsn_ref, t, tgt, (2 * c + 1) * per, (2 * c + 2) * per)
        if stages >= 3:
            acc_ref[slot % 2] = jnp.concatenate(parts, axis=0)
        if stages >= 2:
            lhs_ref[(slot + 1) % 2] = lhs_next
        flush(done)
        return (jnp.concatenate(ys[:8], axis=0), jnp.concatenate(ys[8:], axis=0), t)

    @pl.when(step == 0)
    def _():
        for t0 in range(PEER_AHEAD):
            issue(ids0_ref, t0, t0, 0, hk)

    for t0 in range(2):
        wait_all(t0)
        xb0 = x_tiles(t0)
        acc_ref[t0] = jnp.concatenate([dots(xb0, t0, r) for r in range(nrg)], axis=0)
    lhs_ref[0] = coefficients(0, 0)

    def ring(i, stacks):
        done = stacks + (jnp.maximum(i * PEER_NBUF - 1, 0),)
        for slot in range(PEER_NBUF):
            done = turn(i * PEER_NBUF + slot, slot, 3, done)
        return done[:2]

    zero = jnp.zeros((8, LANE), F32)
    nring = (tb - 2) // PEER_NBUF
    done = lax.fori_loop(0, nring, ring, (zero, zero)) + (nring * PEER_NBUF - 1,)
    for t in range(nring * PEER_NBUF, tb - 2):
        done = turn(t, t % PEER_NBUF, 3, done)
    done = turn(tb - 2, (tb - 2) % PEER_NBUF, 2, done)
    done = turn(tb - 1, (tb - 1) % PEER_NBUF, 1, done)
    flush(done)

    @pl.when(step == pl.num_programs(0) - 1)
    def _():
        for t0 in range(PEER_AHEAD):
            wait_all((tb + t0) % PEER_NBUF)


def _peer_experts(ids, gates, h, ln, slabs, tb):
    t, d = h.shape
    hk = ids.shape[1]
    assert hk == LANE and tb % PEER_NBUF == 0 and PEER_NBUF % 2 == 0 and d % LANE == 0
    assert PEER_AHEAD <= min(8, PEER_NBUF - 2)
    ids_next = jnp.concatenate([ids[PEER_AHEAD:], ids[:PEER_AHEAD]], axis=0)
    assert slabs.shape[1:] == (d // LANE, LANE)
    return pl.pallas_call(
        _peer_expert_body,
        grid=(t // tb,),
        in_specs=[pl.BlockSpec((8, hk), lambda i: (0, 0), memory_space=pltpu.SMEM),
                  pl.BlockSpec((tb, hk), lambda i: (i, 0), memory_space=pltpu.SMEM),
                  pl.BlockSpec((tb, d), lambda i: (i, 0)),
                  pl.BlockSpec((1, d), lambda i: (0, 0)),
                  pl.BlockSpec((tb, hk), lambda i: (i, 0)),
                  pl.BlockSpec(memory_space=pl.ANY)],
        out_specs=pl.BlockSpec((tb, d), lambda i: (i, 0)),
        out_shape=jax.ShapeDtypeStruct((t, d), F32),
        scratch_shapes=[pltpu.VMEM((tb, d), F32),
                        pltpu.VMEM((2, hk, LANE), F32),
                        pltpu.VMEM((2, 16, hk), BF16),
                        pltpu.VMEM((PEER_NBUF, d // LANE, hk, LANE), jnp.uint32),
                        pltpu.SemaphoreType.DMA((PEER_NBUF,))],
        compiler_params=_params("arbitrary"),
        name="peer_experts",
    )(ids[:8], ids_next, h, ln.reshape(1, d), gates, slabs)


def _ple_body(x_ref, g_ref, wg_ref, p_ref, wp_ref, h_ref, o_ref, xn_ref):
    @pl.when(pl.program_id(1) == 0)
    def _():
        x = x_ref[...]
        ms = jnp.mean(x * x, axis=-1, keepdims=True)
        xn_ref[...] = (x * lax.rsqrt(ms + EPS) * g_ref[...]).astype(BF16)

    z = jnp.dot(xn_ref[...], wg_ref[...], preferred_element_type=F32)
    pp = jnp.dot(p_ref[...].astype(BF16), wp_ref[...], preferred_element_type=F32)
    o_ref[...] = h_ref[...] + jax.nn.sigmoid(z) * pp


def _ple(h, g, wg, p, wp, tm, tn):
    t, d = h.shape
    pd = p.shape[1]
    return pl.pallas_call(
        _ple_body,
        grid=(t // tm, d // tn),
        in_specs=[pl.BlockSpec((tm, d), lambda i, j: (i, 0)),
                  pl.BlockSpec((1, d), lambda i, j: (0, 0)),
                  pl.BlockSpec((d, tn), lambda i, j: (0, j)),
                  pl.BlockSpec((tm, pd), lambda i, j: (i, 0)),
                  pl.BlockSpec((pd, tn), lambda i, j: (0, j)),
                  pl.BlockSpec((tm, tn), lambda i, j: (i, j))],
        out_specs=pl.BlockSpec((tm, tn), lambda i, j: (i, j)),
        out_shape=jax.ShapeDtypeStruct((t, d), F32),
        scratch_shapes=[pltpu.VMEM((tm, d), BF16)],
        compiler_params=_params("parallel", "arbitrary"),
        name="ple_gate",
    )(h, g.reshape(1, d), wg, p, wp, h)


def _rmsnorm_body(x_ref, g_ref, o_ref):
    x = x_ref[...]
    ms = jnp.mean(x * x, axis=-1, keepdims=True)
    o_ref[...] = x * lax.rsqrt(ms + EPS) * g_ref[...]


def _rmsnorm(x, g, tm):
    t, d = x.shape
    return pl.pallas_call(
        _rmsnorm_body,
        grid=(t // tm,),
        in_specs=[pl.BlockSpec((tm, d), lambda i: (i, 0)), pl.BlockSpec((1, d), lambda i: (0, 0))],
        out_specs=pl.BlockSpec((tm, d), lambda i: (i, 0)),
        out_shape=jax.ShapeDtypeStruct((t, d), F32),
        compiler_params=_params("parallel"),
        name="final_rmsnorm",
    )(x, g.reshape(1, d))


def _toeplitz(vec, nq, nk):
    assert nq == nk
    lead = vec.shape[:-1]
    g = int(np.prod(lead))
    w = jnp.concatenate([jnp.zeros(lead + (1,), vec.dtype), vec[..., ::-1]], axis=-1).reshape(g, 1, 2 * nk)

    def body(w_ref, o_ref):
        rows = jnp.broadcast_to(w_ref[0], (nq, 2 * nk))
        o_ref[0] = pltpu.roll(rows, 0, 1, stride=1, stride_axis=0)[:, nk:]

    out = pl.pallas_call(
        body,
        grid=(g,),
        in_specs=[pl.BlockSpec((1, 1, 2 * nk), lambda i: (i, 0, 0))],
        out_specs=pl.BlockSpec((1, nq, nk), lambda i: (i, 0, 0)),
        out_shape=jax.ShapeDtypeStruct((g, nq, nk), vec.dtype),
        compiler_params=_params("parallel"),
        name="toeplitz_tiles",
    )(w)
    return out.reshape(lead + (nq, nk))


def _bias_tables(rel_bias, s):
    nd = s // BK
    heads = rel_bias.shape[1]
    bdt = rel_bias[_rel_bucket(jnp.arange(s))].astype(F32).T
    dd = (jnp.arange(nd)[:, None, None] * BK + jnp.arange(BQ)[None, :, None]
          - jnp.arange(BK)[None, None, :])
    ext = jnp.pad(bdt, ((0, 0), (BK - 1, 0)))
    segs = jnp.stack([ext[:, dl * BK:dl * BK + BQ + BK - 1] for dl in range(nd)], axis=1)
    tz = _toeplitz(segs, BQ, BK)
    causal = dd >= 0
    mult = sum(((dd % dil == 0) & (dd // dil <= window // dil)).astype(F32) for window, dil in DIL_PATTERNS)
    ok = causal & (mult > 0)
    tzd = jnp.where(ok, tz[:A_HEADS] + jnp.log(jnp.where(ok, mult, 1.0)), NEG)
    tzb = tz[A_HEADS:A_HEADS + B_HEADS]
    tzs = jnp.where(causal, tzb, NEG)
    nwin = -(-(NSA_WINDOW - 1) // BK) + 1
    tzw = jnp.where(causal & (dd <= NSA_WINDOW - 1), tzb, NEG)[:, :nwin]
    tzc = jnp.where(causal, tz[A_HEADS + B_HEADS:], NEG)
    ncmp = s // CMP_STRIDE
    na = s // CMP_STRIDE
    nvec = na + ncmp - 1
    lo = CMP_STRIDE * (ncmp - 1) + CMP_LEN - 1
    bdb = bdt[A_HEADS:A_HEADS + B_HEADS]
    gext = jnp.concatenate([jnp.full((B_HEADS, lo), NEG, F32), bdb], axis=1)
    vecs = gext[:, :CMP_STRIDE * nvec].reshape(B_HEADS, nvec, CMP_STRIDE).transpose(0, 2, 1)
    bcq = _toeplitz(vecs, na, ncmp)
    bcq = bcq.transpose(0, 2, 1, 3).reshape(B_HEADS, s, ncmp)
    bct = jnp.transpose(bcq, (0, 2, 1))
    nslc = s // SLC_LEN
    cstart = jnp.arange(ncmp) * CMP_STRIDE
    sstart = jnp.arange(nslc) * SLC_LEN
    covt = ((cstart[None, :] < sstart[:, None] + SLC_LEN)
            & (cstart[None, :] + CMP_LEN > sstart[:, None])).astype(F32)
    nblk = s // MOBA_BLK
    avg = ((jnp.arange(s)[None, :] // MOBA_BLK == jnp.arange(nblk)[:, None]).astype(F32)
           / MOBA_BLK).astype(BF16)
    return tzd, tzs, tzw, tzc, bcq, bct, covt, avg


def _pack_body(u_ref, v_ref, o_ref):
    te, d = u_ref.shape
    nlt = d // LANE
    ub = lax.bitcast_convert_type(u_ref[...].astype(BF16).astype(F32), jnp.uint32)
    vb = lax.bitcast_convert_type(v_ref[...].astype(BF16).astype(F32), jnp.uint32)
    word = lax.shift_right_logical(ub, jnp.uint32(16)) | vb
    for j in range(nlt):
        o_ref[pl.ds(j, te, stride=nlt), :] = word[:, j * LANE:(j + 1) * LANE]


def _pack_uv(u, v, te=256):
    e, d = u.shape
    nlt = d // LANE
    out = pl.pallas_call(
        _pack_body,
        grid=(e // te,),
        in_specs=[pl.BlockSpec((te, d), lambda i: (i, 0)), pl.BlockSpec((te, d), lambda i: (i, 0))],
        out_specs=pl.BlockSpec((te * nlt, LANE), lambda i: (i, 0)),
        out_shape=jax.ShapeDtypeStruct((e * nlt, LANE), jnp.uint32),
        compiler_params=_params("parallel"),
        name="pack_experts",
    )(u, v)
    return out.reshape(e, nlt, LANE)


def _reorder_w_in(w):
    g0 = 3 * A_HEADS * LANE + B_HEADS * LANE + 6 * NSA_KV * LANE
    ng = 3 * B_HEADS
    main = jnp.concatenate([w[:, :g0], w[:, g0 + ng:]], axis=1)
    per = ng // NSA_KV
    gate = jnp.concatenate(
        [jnp.pad(w[:, g0 + g * per:g0 + (g + 1) * per], ((0, 0), (0, LANE - per))) for g in range(NSA_KV)],
        axis=1)
    return main.astype(BF16), gate.astype(BF16)


def kernel(x, p, ln_mix, w_in, cmp_wk, cmp_wv, cmp_pos, out_norm, w_out, rel_bias, ln_ffn, peer_wq,
           peer_keys, peer_u, peer_v, ln_ple, ple_gate, ple_proj, ln_final):
    b, s, d = x.shape
    t = b * s
    depth = w_in.shape[0]
    assert s % BQ == 0 and BQ == BK == MOBA_BLK and d % LANE == 0
    tm = 1024 if t % 1024 == 0 else 512
    tzd, tzs, tzw, tzc, bcq, bct, covt, avg = _bias_tables(rel_bias, s)
    h = x.reshape(t, d)
    for i in range(depth):
        w_main, w_gate = _reorder_w_in(w_in[i])
        proj = _normmm(h, ln_mix[i], w_main, BF16, tm, 1024, "in_projection").reshape(b, s, MAIN_COLS)
        gates = _normmm(h, ln_mix[i], w_gate, F32, tm, w_gate.shape[1], "gate_projection")
        gates = gates.reshape(b, s, NSA_KV * LANE)
        oa = _dilated(proj, tzd)
        kcmp, vcmp = _compress(proj, cmp_wk[i], cmp_wv[i], cmp_pos[i])
        ob = _nsa(proj, gates, kcmp, vcmp, bcq, bct, covt, tzs, tzw)
        oc = _moba(proj, avg, tzc)
        h = _outproj(oa.reshape(t, -1), ob.reshape(t, -1), oc.reshape(t, -1), out_norm[i],
                     w_out[i].astype(BF16), h, tm, 1024)
        qp = _normmm(h, ln_ffn[i], peer_wq[i].astype(BF16), F32, tm, 1024, "peer_query")
        keys = peer_keys[i].reshape(PEER_HEADS * 2, PEER_NKEYS, -1)
        e_t, g_t = _peer_topk(qp, keys, 256)
        h = _peer_experts(e_t.T, g_t.T, h, ln_ffn[i], _pack_uv(peer_u[i], peer_v[i]), LANE)
        h = _ple(h, ln_ple[i], ple_gate[i].astype(BF16), p[i].reshape(t, -1), ple_proj[i].astype(BF16),
                 tm, 1024)
    return _rmsnorm(h, ln_final, tm).reshape(b, s, d)
```

```python
import functools
import math

import numpy as np
import jax
import jax.numpy as jnp
from jax import lax
from jax.experimental import pallas as pl
from jax.experimental.pallas import tpu as pltpu

F32 = jnp.float32
BF16 = jnp.bfloat16
I32 = jnp.int32

HEAD_DIM = 128
A_HEADS, B_HEADS, C_HEADS = 6, 6, 4
NSA_KV, NSA_GROUP = 2, 3
DIL_PATTERNS = ((128, 1), (512, 4), (2048, 16))
CMP_LEN, CMP_STRIDE = 32, 16
SLC_LEN, SLC_TOPN = 64, 8
NSA_WINDOW = 512
FORCE = 1e4
MOBA_BLK, MOBA_TOPK = 256, 3
REL_BUCKETS, REL_MAX_DIST = 32, 1024
PEER_HEADS, PEER_NKEYS, PEER_TOPK = 8, 128, 16
EPS = 1e-6

LANE = 128
BQ = 256
BK = 256
NEG = -1e30
HALF_NEG = -5e29
VMEM_LIMIT = 56 * 1024 * 1024

CB_QA, CB_KA, CB_VA = 0, 6, 12
CB_QB, CB_KC, CB_VC, CB_KS, CB_VS, CB_KW, CB_VW = 18, 24, 26, 28, 30, 32, 34
CB_QC, CB_KCC, CB_VCC = 36, 40, 44
MAIN_COLS = 48 * LANE


def _nt(a, b, precision=None):
    return lax.dot_general(a, b, (((1,), (1,)), ((), ())), preferred_element_type=F32,
                           precision=precision)


def _rel_bucket(dist):
    exact = REL_BUCKETS // 2
    d = jnp.maximum(dist, 0)
    logd = jnp.log(jnp.maximum(d, 1).astype(F32) / exact)
    large = exact + (logd / math.log(REL_MAX_DIST / exact) * (REL_BUCKETS - exact)).astype(I32)
    large = jnp.clip(large, exact, REL_BUCKETS - 1)
    return jnp.where(d < exact, d, large)


def _masked_softmax(s, axis):
    valid = s > HALF_NEG
    m = jnp.max(s, axis=axis, keepdims=True)
    e = jnp.where(valid, jnp.exp(s - m), 0.0)
    z = jnp.sum(e, axis=axis, keepdims=True)
    zs = jnp.where(z > 0, z, 1.0)
    return e / zs


def _params(*sem):
    return pltpu.CompilerParams(dimension_semantics=sem, vmem_limit_bytes=VMEM_LIMIT)


def _normmm_body(x_ref, g_ref, w_ref, o_ref, xn_ref):
    @pl.when(pl.program_id(1) == 0)
    def _():
        x = x_ref[...]
        ms = jnp.mean(x * x, axis=-1, keepdims=True)
        xn_ref[...] = (x * lax.rsqrt(ms + EPS) * g_ref[...]).astype(BF16)

    o_ref[...] = jnp.dot(xn_ref[...], w_ref[...], preferred_element_type=F32).astype(o_ref.dtype)


def _normmm(x, g, w, out_dtype, tm, tn, name):
    t, d = x.shape
    n = w.shape[1]
    return pl.pallas_call(
        _normmm_body,
        grid=(t // tm, n // tn),
        in_specs=[pl.BlockSpec((tm, d), lambda i, j: (i, 0)),
                  pl.BlockSpec((1, d), lambda i, j: (0, 0)),
                  pl.BlockSpec((d, tn), lambda i, j: (0, j))],
        out_specs=pl.BlockSpec((tm, tn), lambda i, j: (i, j)),
        out_shape=jax.ShapeDtypeStruct((t, n), out_dtype),
        scratch_shapes=[pltpu.VMEM((tm, d), BF16)],
        compiler_params=_params("parallel", "arbitrary"),
        name=name,
    )(x, g.reshape(1, d), w)


def _flash(qs, k_ref, v_ref, lo, hi, score_fns, scale):
    def tile(kj):
        off = pl.multiple_of(kj * BK, BK)
        return k_ref[0, pl.ds(off, BK), :], v_ref[0, pl.ds(off, BK), :]

    def update(state, blocks):
        m, l, acc = state
        m_new = m
        for s, _ in blocks:
            m_new = jnp.maximum(m_new, jnp.max(s, axis=1, keepdims=True))
        alpha = jnp.exp(m - m_new)
        l = alpha * l
        acc = alpha * acc
        for s, v in blocks:
            p = jnp.exp(s - m_new)
            l = l + jnp.sum(p, axis=1, keepdims=True)
            acc = acc + jnp.dot(p.astype(BF16), v, preferred_element_type=F32)
        return m_new, l, acc

    def step(kjs, states):
        kv = [tile(kj) for kj in kjs]
        return tuple(
            update(states[i], [(score_fns[i](kj, _nt(q, k) * scale), v) for kj, (k, v) in zip(kjs, kv)])
            for i, q in enumerate(qs))

    states = tuple((jnp.full((BQ, 1), NEG, F32), jnp.zeros((BQ, 1), F32), jnp.zeros((BQ, HEAD_DIM), F32))
                   for _ in qs)
    start = lo
    for width in ((4, 2, 1) if len(qs) == 1 else (2, 1)):
        shift = width.bit_length() - 1
        count = lax.shift_right_logical(hi - start, shift)

        def group(p, st, start=start, width=width):
            return step([start + width * p + i for i in range(width)], st)

        states = lax.fori_loop(0, count, group, states)
        start = start + count * width
    return [acc / l for _, l, acc in states]


def _dilated_body(q_ref, k_ref, v_ref, tz_ref, o_ref):
    qi = pl.program_id(2)
    scale = HEAD_DIM ** -0.5

    def score(kj, s):
        return s + tz_ref[0, qi - kj]

    o_ref[0] = _flash([q_ref[0]], k_ref, v_ref, 0, qi + 1, [score], scale)[0]


def _dilated(proj, tzd):
    b, s, _ = proj.shape
    nd = s // BK
    return pl.pallas_call(
        _dilated_body,
        grid=(b, A_HEADS, s // BQ),
        in_specs=[pl.BlockSpec((1, BQ, LANE), lambda bi, h, qi: (bi, qi, CB_QA + h)),
                  pl.BlockSpec((1, s, LANE), lambda bi, h, qi: (bi, 0, CB_KA + h)),
                  pl.BlockSpec((1, s, LANE), lambda bi, h, qi: (bi, 0, CB_VA + h)),
                  pl.BlockSpec((1, nd, BQ, BK), lambda bi, h, qi: (h, 0, 0, 0))],
        out_specs=pl.BlockSpec((1, BQ, LANE), lambda bi, h, qi: (bi, qi, h)),
        out_shape=jax.ShapeDtypeStruct((b, s, A_HEADS * LANE), F32),
        compiler_params=_params("parallel", "parallel", "arbitrary"),
        name="dilated_attention",
    )(proj, proj, proj, tzd)


def _compress_body(kc_ref, vc_ref, wk_ref, wv_ref, pos_ref, ko_ref, vo_ref, xk_ref, xv_ref):
    s = kc_ref.shape[1]
    ncmp = ko_ref.shape[2]
    xk_ref[pl.ds(0, s), :] = kc_ref[0].astype(F32)
    xv_ref[pl.ds(0, s), :] = vc_ref[0].astype(F32)
    xk_ref[pl.ds(s, CMP_LEN), :] = jnp.zeros((CMP_LEN, HEAD_DIM), F32)
    xv_ref[pl.ds(s, CMP_LEN), :] = jnp.zeros((CMP_LEN, HEAD_DIM), F32)
    acck = jnp.zeros((ncmp, HEAD_DIM), F32)
    accv = jnp.zeros((ncmp, HEAD_DIM), F32)
    for l in range(CMP_LEN):
        pos = pos_ref[pl.ds(l, 1), :]
        rk = (xk_ref[pl.ds(l, ncmp, stride=CMP_STRIDE), :] + pos).astype(BF16)
        rv = (xv_ref[pl.ds(l, ncmp, stride=CMP_STRIDE), :] + pos).astype(BF16)
        acck = acck + jnp.dot(rk, wk_ref[l], preferred_element_type=F32)
        accv = accv + jnp.dot(rv, wv_ref[l], preferred_element_type=F32)
    ko_ref[0, 0] = acck.astype(BF16)
    vo_ref[0, 0] = accv.astype(BF16)


def _compress(proj, cmp_wk, cmp_wv, cmp_pos):
    b, s, _ = proj.shape
    ncmp = s // CMP_STRIDE
    out = jax.ShapeDtypeStruct((b, NSA_KV, ncmp, HEAD_DIM), BF16)
    return pl.pallas_call(
        _compress_body,
        grid=(b, NSA_KV),
        in_specs=[pl.BlockSpec((1, s, LANE), lambda bi, g: (bi, 0, CB_KC + g)),
                  pl.BlockSpec((1, s, LANE), lambda bi, g: (bi, 0, CB_VC + g)),
                  pl.BlockSpec((CMP_LEN, HEAD_DIM, HEAD_DIM), lambda bi, g: (0, 0, 0)),
                  pl.BlockSpec((CMP_LEN, HEAD_DIM, HEAD_DIM), lambda bi, g: (0, 0, 0)),
                  pl.BlockSpec((CMP_LEN, HEAD_DIM), lambda bi, g: (0, 0))],
        out_specs=[pl.BlockSpec((1, 1, ncmp, HEAD_DIM), lambda bi, g: (bi, g, 0, 0)),
                   pl.BlockSpec((1, 1, ncmp, HEAD_DIM), lambda bi, g: (bi, g, 0, 0))],
        out_shape=[out, out],
        scratch_shapes=[pltpu.VMEM((s + CMP_LEN, HEAD_DIM), F32),
                        pltpu.VMEM((s + CMP_LEN, HEAD_DIM), F32)],
        compiler_params=_params("parallel", "parallel"),
        name="nsa_compress",
    )(proj, proj, cmp_wk.astype(BF16), cmp_wv.astype(BF16), cmp_pos)


def _nsa_body(q_ref, kcmp_ref, vcmp_ref, ks_ref, vs_ref, kw_ref, vw_ref, gt_ref, bcq_ref, bct_ref,
              covt_ref, tzs_ref, tzw_ref, o_ref, mask_ref):
    qi = pl.program_id(2)
    scale = HEAD_DIM ** -0.5
    nslc = covt_ref.shape[0]
    nkt = mask_ref.shape[0]
    kcmp = kcmp_ref[0, 0]
    vcmp = vcmp_ref[0, 0]
    gates = jax.nn.sigmoid(gt_ref[0])

    psum = jnp.zeros((kcmp.shape[0], BQ), F32)
    for r in range(NSA_GROUP):
        q = q_ref[0, :, r * LANE:(r + 1) * LANE]
        p = _masked_softmax(_nt(q, kcmp) * scale + bcq_ref[r], axis=1)
        o_cmp = jnp.dot(p.astype(BF16), vcmp, preferred_element_type=F32)
        o_ref[0, :, r * LANE:(r + 1) * LANE] = gates[:, 3 * r:3 * r + 1] * o_cmp
        psum = psum + _masked_softmax(_nt(kcmp, q) * scale + bct_ref[r], axis=0)
    imp = jnp.dot(covt_ref[...], psum, preferred_element_type=F32,
                  precision=lax.Precision.HIGHEST)
    jb = lax.broadcasted_iota(I32, (nslc, BQ), 0)
    qpos = qi * BQ + lax.broadcasted_iota(I32, (nslc, BQ), 1)
    qblk = lax.shift_right_logical(qpos, int(math.log2(SLC_LEN)))
    forced = (jb == 0) | (jb == qblk) | (jb == qblk - 1)
    imp = jnp.where(forced, FORCE, jnp.where(jb > qblk, -FORCE, imp))
    beaten = jnp.zeros((nslc, BQ), F32)
    for jp in range(nslc):
        row = imp[jp:jp + 1, :]
        wins = (row > imp) | ((row == imp) & (jp < jb))
        beaten = beaten + wins.astype(F32)
    selt = (beaten < SLC_TOPN).astype(BF16)
    eye = (lax.broadcasted_iota(I32, (BQ, BQ), 0) == lax.broadcasted_iota(I32, (BQ, BQ), 1)).astype(BF16)
    selq = _nt(eye, selt).astype(BF16)
    for t in range(nkt):
        kblk = (t * BK + lax.broadcasted_iota(I32, (nslc, BK), 1)) // SLC_LEN
        expand = (kblk == lax.broadcasted_iota(I32, (nslc, BK), 0)).astype(BF16)
        mask_ref[t] = jnp.dot(selq, expand, preferred_element_type=F32)

    qs = [q_ref[0, :, r * LANE:(r + 1) * LANE] for r in range(NSA_GROUP)]
    sel_fns = [lambda kj, s, r=r: jnp.where(mask_ref[kj] > 0.5, s + tzs_ref[r, qi - kj], NEG)
               for r in range(NSA_GROUP)]
    win_fns = [lambda kj, s, r=r: s + tzw_ref[r, qi - kj] for r in range(NSA_GROUP)]
    nwin = tzw_ref.shape[1]
    o_sel = _flash(qs, ks_ref, vs_ref, 0, qi + 1, sel_fns, scale)
    o_win = _flash(qs, kw_ref, vw_ref, jnp.maximum(qi - (nwin - 1), 0), qi + 1, win_fns, scale)
    for r in range(NSA_GROUP):
        o_ref[0, :, r * LANE:(r + 1) * LANE] += (gates[:, 3 * r + 1:3 * r + 2] * o_sel[r]
                                                 + gates[:, 3 * r + 2:3 * r + 3] * o_win[r])


def _nsa(proj, gates, kcmp, vcmp, bcq, bct, covt, tzs, tzw):
    b, s, _ = proj.shape
    nd = s // BK
    ncmp = kcmp.shape[2]
    nslc = covt.shape[0]
    nwin = tzw.shape[1]
    gw = NSA_GROUP * LANE
    kv = lambda cb: pl.BlockSpec((1, s, LANE), lambda bi, g, qi: (bi, 0, cb + g))
    return pl.pallas_call(
        _nsa_body,
        grid=(b, NSA_KV, s // BQ),
        in_specs=[pl.BlockSpec((1, BQ, gw), lambda bi, g, qi: (bi, qi, CB_QB // NSA_GROUP + g)),
                  pl.BlockSpec((1, 1, ncmp, HEAD_DIM), lambda bi, g, qi: (bi, g, 0, 0)),
                  pl.BlockSpec((1, 1, ncmp, HEAD_DIM), lambda bi, g, qi: (bi, g, 0, 0)),
                  kv(CB_KS), kv(CB_VS), kv(CB_KW), kv(CB_VW),
                  pl.BlockSpec((1, BQ, LANE), lambda bi, g, qi: (bi, qi, g)),
                  pl.BlockSpec((NSA_GROUP, BQ, ncmp), lambda bi, g, qi: (g, qi, 0)),
                  pl.BlockSpec((NSA_GROUP, ncmp, BQ), lambda bi, g, qi: (g, 0, qi)),
                  pl.BlockSpec((nslc, ncmp), lambda bi, g, qi: (0, 0)),
                  pl.BlockSpec((NSA_GROUP, nd, BQ, BK), lambda bi, g, qi: (g, 0, 0, 0)),
                  pl.BlockSpec((NSA_GROUP, nwin, BQ, BK), lambda bi, g, qi: (g, 0, 0, 0))],
        out_specs=pl.BlockSpec((1, BQ, gw), lambda bi, g, qi: (bi, qi, g)),
        out_shape=jax.ShapeDtypeStruct((b, s, B_HEADS * LANE), F32),
        scratch_shapes=[pltpu.VMEM((nd, BQ, BK), F32)],
        compiler_params=_params("parallel", "parallel", "arbitrary"),
        name="nsa_attention",
    )(proj, kcmp, vcmp, proj, proj, proj, proj, gates, bcq, bct, covt, tzs, tzw)


def _moba_body(q_ref, k_ref, v_ref, avg_ref, tz_ref, o_ref, mask_ref):
    qi = pl.program_id(2)
    scale = HEAD_DIM ** -0.5
    nblk = avg_ref.shape[0]
    q = q_ref[0]
    kmean = jnp.dot(avg_ref[...], k_ref[0], preferred_element_type=F32)
    gate = _nt(kmean, q.astype(F32), precision=lax.Precision.HIGHEST)
    nb = lax.broadcasted_iota(I32, (nblk, BQ), 0)
    past = nb < qi
    beaten = jnp.zeros((nblk, BQ), F32)
    for np_ in range(nblk):
        row = gate[np_:np_ + 1, :]
        wins = (np_ < qi) & ((row > gate) | ((row == gate) & (np_ < nb)))
        beaten = beaten + wins.astype(F32)
    selt = (past & (beaten < MOBA_TOPK)).astype(BF16)
    eye = (lax.broadcasted_iota(I32, (BQ, BQ), 0) == lax.broadcasted_iota(I32, (BQ, BQ), 1)).astype(BF16)
    selq = _nt(eye, selt)
    for n in range(nblk):
        mask_ref[n] = jnp.broadcast_to(selq[:, n:n + 1], (BQ, BK))

    def score(kj, s):
        keep = (mask_ref[kj] > 0.5) | (kj == qi)
        return jnp.where(keep, s + tz_ref[0, qi - kj], NEG)

    o_ref[0] = _flash([q], k_ref, v_ref, 0, qi + 1, [score], scale)[0]


def _moba(proj, avg, tzc):
    b, s, _ = proj.shape
    nd = s // BK
    nblk = avg.shape[0]
    return pl.pallas_call(
        _moba_body,
        grid=(b, C_HEADS, s // BQ),
        in_specs=[pl.BlockSpec((1, BQ, LANE), lambda bi, h, qi: (bi, qi, CB_QC + h)),
                  pl.BlockSpec((1, s, LANE), lambda bi, h, qi: (bi, 0, CB_KCC + h)),
                  pl.BlockSpec((1, s, LANE), lambda bi, h, qi: (bi, 0, CB_VCC + h)),
                  pl.BlockSpec((nblk, s), lambda bi, h, qi: (0, 0)),
                  pl.BlockSpec((1, nd, BQ, BK), lambda bi, h, qi: (h, 0, 0, 0))],
        out_specs=pl.BlockSpec((1, BQ, LANE), lambda bi, h, qi: (bi, qi, h)),
        out_shape=jax.ShapeDtypeStruct((b, s, C_HEADS * LANE), F32),
        scratch_shapes=[pltpu.VMEM((nblk, BQ, BK), F32)],
        compiler_params=_params("parallel", "parallel", "arbitrary"),
        name="moba_attention",
    )(proj, proj, proj, avg, tzc)


def _outproj_body(oa_ref, ob_ref, oc_ref, g_ref, w_ref, h_ref, o_ref, y_ref):
    @pl.when(pl.program_id(1) == 0)
    def _():
        c0 = 0
        for ref in (oa_ref, ob_ref, oc_ref):
            x = ref[...]
            wd = x.shape[1]
            ms = jnp.mean(x * x, axis=-1, keepdims=True)
            y_ref[:, c0:c0 + wd] = (x * lax.rsqrt(ms + EPS) * g_ref[:, c0:c0 + wd]).astype(BF16)
            c0 += wd

    o_ref[...] = h_ref[...] + jnp.dot(y_ref[...], w_ref[...], preferred_element_type=F32)


def _outproj(oa, ob, oc, g, w, h, tm, tn):
    t, d = h.shape
    row = lambda a: pl.BlockSpec((tm, a.shape[1]), lambda i, j: (i, 0))
    return pl.pallas_call(
        _outproj_body,
        grid=(t // tm, d // tn),
        in_specs=[row(oa), row(ob), row(oc),
                  pl.BlockSpec((1, d), lambda i, j: (0, 0)),
                  pl.BlockSpec((d, tn), lambda i, j: (0, j)),
                  pl.BlockSpec((tm, tn), lambda i, j: (i, j))],
        out_specs=pl.BlockSpec((tm, tn), lambda i, j: (i, j)),
        out_shape=jax.ShapeDtypeStruct((t, d), F32),
        scratch_shapes=[pltpu.VMEM((tm, d), BF16)],
        compiler_params=_params("parallel", "arbitrary"),
        name="out_projection",
    )(oa, ob, oc, g.reshape(1, d), w, h)


def _topk_rows(x, k, payload=None):
    n, tm = x.shape
    rows = lax.broadcasted_iota(I32, x.shape, 0)
    slot = lax.broadcasted_iota(I32, (k, tm), 0)
    vals = jnp.zeros((k, tm), F32)
    idxs = jnp.zeros((k, tm), I32)
    for it in range(k):
        mx = jnp.max(x, axis=0, keepdims=True)
        idx = jnp.min(jnp.where(x == mx, rows, n), axis=0, keepdims=True)
        hit = rows == idx
        if payload is not None:
            idx = jnp.sum(jnp.where(hit, payload, 0), axis=0, keepdims=True)
        vals = jnp.where(slot == it, mx, vals)
        idxs = jnp.where(slot == it, idx, idxs)
        x = jnp.where(hit, -jnp.inf, x)
    return vals, idxs


def _peer_topk_body(q_ref, keys_ref, e_ref, g_ref):
    for h in range(PEER_HEADS):
        tops = []
        for c in range(2):
            hc = 2 * h + c
            sc = _nt(keys_ref[hc], q_ref[:, hc * LANE:(hc + 1) * LANE])
            tops.append(_topk_rows(sc, PEER_TOPK))
        (s0, i0), (s1, i1) = tops
        half = PEER_TOPK // 2
        sub = lax.broadcasted_iota(I32, (half, s0.shape[1]), 0)
        pieces = [s0[0:1] + s1]
        pieces_i = [i0[0:1] * PEER_NKEYS + i1]
        for a in range(1, half):
            keep = sub < PEER_TOPK // (a + 1)
            pieces.append(jnp.where(keep, s0[a:a + 1] + s1[0:half], -jnp.inf))
            pieces_i.append(i0[a:a + 1] * PEER_NKEYS + i1[0:half])
        pieces.append(s0[half:] + s1[0:1])
        pieces_i.append(i0[half:] * PEER_NKEYS + i1[0:1])
        cand = jnp.concatenate(pieces, axis=0)
        cand_i = jnp.concatenate(pieces_i, axis=0)
        bs, be = _topk_rows(cand, PEER_TOPK, payload=cand_i)
        e = jnp.exp(bs - jnp.max(bs, axis=0, keepdims=True))
        g_ref[h * PEER_TOPK:(h + 1) * PEER_TOPK, :] = e / jnp.sum(e, axis=0, keepdims=True)
        e_ref[h * PEER_TOPK:(h + 1) * PEER_TOPK, :] = be


def _peer_topk(qp, keys, tm):
    t, d = qp.shape
    hk = PEER_HEADS * PEER_TOPK
    return pl.pallas_call(
        _peer_topk_body,
        grid=(t // tm,),
        in_specs=[pl.BlockSpec((tm, d), lambda i: (i, 0)),
                  pl.BlockSpec(keys.shape, lambda i: (0, 0, 0))],
        out_specs=[pl.BlockSpec((hk, tm), lambda i: (0, i)),
                   pl.BlockSpec((hk, tm), lambda i: (0, i))],
        out_shape=[jax.ShapeDtypeStruct((hk, t), I32), jax.ShapeDtypeStruct((hk, t), F32)],
        compiler_params=_params("parallel"),
        name="peer_topk",
    )(qp, keys)


PEER_NBUF = 8
PEER_AHEAD = 6


def _gelu(x):
    return 0.5 * x * (1.0 + lax.erf(x * (2.0 ** -0.5)))


def _hi_lo(x):
    hi = x.astype(BF16)
    lo = (x - hi.astype(F32)).astype(BF16)
    return jnp.concatenate([hi, lo], axis=1)


def _peer_expert_body(ids0_ref, idsn_ref, h_ref, ln_ref, g_ref, uvw_ref, o_ref, xn_ref, acc_ref, lhs_ref,
                      buf_ref, sem_ref):
    tb, d = h_ref.shape
    hk = idsn_ref.shape[1]
    nrg = hk // 8
    nlt = d // LANE
    assert nrg == nlt and hk % (2 * nlt) == 0
    step = pl.program_id(0)
    x = h_ref[...]
    ms = jnp.mean(x * x, axis=-1, keepdims=True)
    xn_ref[...] = x * lax.rsqrt(ms + EPS) * ln_ref[...]

    def issue(ids_ref, row, slot, k0, k1):
        for k in range(k0, k1):
            e = ids_ref[row, k]
            pltpu.make_async_copy(uvw_ref.at[e], buf_ref.at[slot, :, k, :],
                                  sem_ref.at[slot]).start(priority=k % 2)

    def wait_all(slot):
        pltpu.make_async_copy(buf_ref.at[slot], buf_ref.at[slot], sem_ref.at[slot]).wait()

    def x_tiles(t):
        xrow = xn_ref[pl.ds(t, 1), :]
        return [jnp.broadcast_to(xrow[:, j * LANE:(j + 1) * LANE], (8, LANE)) for j in range(nlt)]

    def dots(xb, slot, r):
        a = None
        for j in range(nlt):
            w = buf_ref[slot, j, r * 8:(r + 1) * 8, :]
            pr = lax.bitcast_convert_type(lax.shift_left(w, jnp.uint32(16)), F32) * xb[j]
            a = pr if a is None else a + pr
        return a

    def coefficients(t, slot):
        act = jnp.sum(acc_ref[slot % 2].T, axis=0, keepdims=True)
        coef = jnp.broadcast_to(_gelu(act) * g_ref[pl.ds(t, 1), :], (8, hk))
        chi = coef.astype(BF16).astype(F32)
        return jnp.concatenate([chi, coef - chi], axis=0).astype(BF16)

    def weighted(lhs, slot, j):
        w = buf_ref[slot, j]
        v = lax.bitcast_convert_type(w & jnp.uint32(0xFFFF0000), F32).astype(BF16)
        yj = jnp.dot(lhs, v, preferred_element_type=F32)
        return yj[0:1, :] + yj[8:9, :]

    def flush(done):
        ya, yb, tp = done
        y = jnp.concatenate([ya[j:j + 1, :] for j in range(8)] + [yb[j:j + 1, :] for j in range(8)], axis=1)
        o_ref[pl.ds(tp, 1), :] = h_ref[pl.ds(tp, 1), :] + y

    def turn(t, slot, stages, done):
        tgt = (slot + PEER_AHEAD) % PEER_NBUF
        far = (slot + 2) % PEER_NBUF
        per = hk // (2 * nlt)
        if stages >= 3:
            wait_all(far)
            xb = x_tiles(t + 2)
        lhs = lhs_ref[slot % 2]
        ys, parts = [], []
        for c in range(nlt):
            if stages >= 3:
                parts.append(dots(xb, far, c))
            issue(idsn_ref, t, tgt, 2 * c * per, (2 * c + 1) * per)
            if 2 * c < nlt:
                ys.append(weighted(lhs, slot, 2 * c))
                ys.append(weighted(lhs, slot, 2 * c + 1))
            if 2 * c == nlt and stages >= 2:
                lhs_next = coefficients(t + 1, slot + 1)
            issue(idsn_ref, t, tgt, (2 * c + 1) * per, (2 * c + 2) * per)
        if stages >= 3:
            acc_ref[slot % 2] = jnp.concatenate(parts, axis=0)
        if stages >= 2:
            lhs_ref[(slot + 1) % 2] = lhs_next
        flush(done)
        return (jnp.concatenate(ys[:8], axis=0), jnp.concatenate(ys[8:], axis=0), t)

    @pl.when(step == 0)
    def _():
        for t0 in range(PEER_AHEAD):
            issue(ids0_ref, t0, t0, 0, hk)

    for t0 in range(2):
        wait_all(t0)
        xb0 = x_tiles(t0)
        acc_ref[t0] = jnp.concatenate([dots(xb0, t0, r) for r in range(nrg)], axis=0)
    lhs_ref[0] = coefficients(0, 0)

    def ring(i, stacks):
        done = stacks + (jnp.maximum(i * PEER_NBUF - 1, 0),)
        for slot in range(PEER_NBUF):
            done = turn(i * PEER_NBUF + slot, slot, 3, done)
        return done[:2]

    zero = jnp.zeros((8, LANE), F32)
    nring = (tb - 2) // PEER_NBUF
    done = lax.fori_loop(0, nring, ring, (zero, zero)) + (nring * PEER_NBUF - 1,)
    for t in range(nring * PEER_NBUF, tb - 2):
        done = turn(t, t % PEER_NBUF, 3, done)
    done = turn(tb - 2, (tb - 2) % PEER_NBUF, 2, done)
    done = turn(tb - 1, (tb - 1) % PEER_NBUF, 1, done)
    flush(done)

    @pl.when(step == pl.num_programs(0) - 1)
    def _():
        for t0 in range(PEER_AHEAD):
            wait_all((tb + t0) % PEER_NBUF)


def _peer_experts(ids, gates, h, ln, slabs, tb):
    t, d = h.shape
    hk = ids.shape[1]
    assert hk == LANE and tb % PEER_NBUF == 0 and PEER_NBUF % 2 == 0 and d % LANE == 0
    assert PEER_AHEAD <= min(8, PEER_NBUF - 2)
    ids_next = jnp.concatenate([ids[PEER_AHEAD:], ids[:PEER_AHEAD]], axis=0)
    assert slabs.shape[1:] == (d // LANE, LANE)
    return pl.pallas_call(
        _peer_expert_body,
        grid=(t // tb,),
        in_specs=[pl.BlockSpec((8, hk), lambda i: (0, 0), memory_space=pltpu.SMEM),
                  pl.BlockSpec((tb, hk), lambda i: (i, 0), memory_space=pltpu.SMEM),
                  pl.BlockSpec((tb, d), lambda i: (i, 0)),
                  pl.BlockSpec((1, d), lambda i: (0, 0)),
                  pl.BlockSpec((tb, hk), lambda i: (i, 0)),
                  pl.BlockSpec(memory_space=pl.ANY)],
        out_specs=pl.BlockSpec((tb, d), lambda i: (i, 0)),
        out_shape=jax.ShapeDtypeStruct((t, d), F32),
        scratch_shapes=[pltpu.VMEM((tb, d), F32),
                        pltpu.VMEM((2, hk, LANE), F32),
                        pltpu.VMEM((2, 16, hk), BF16),
                        pltpu.VMEM((PEER_NBUF, d // LANE, hk, LANE), jnp.uint32),
                        pltpu.SemaphoreType.DMA((PEER_NBUF,))],
        compiler_params=_params("arbitrary"),
        name="peer_experts",
    )(ids[:8], ids_next, h, ln.reshape(1, d), gates, slabs)


def _ple_body(x_ref, g_ref, wg_ref, p_ref, wp_ref, h_ref, o_ref, xn_ref):
    @pl.when(pl.program_id(1) == 0)
    def _():
        x = x_ref[...]
        ms = jnp.mean(x * x, axis=-1, keepdims=True)
        xn_ref[...] = (x * lax.rsqrt(ms + EPS) * g_ref[...]).astype(BF16)

    z = jnp.dot(xn_ref[...], wg_ref[...], preferred_element_type=F32)
    pp = jnp.dot(p_ref[...].astype(BF16), wp_ref[...], preferred_element_type=F32)
    o_ref[...] = h_ref[...] + jax.nn.sigmoid(z) * pp


def _ple(h, g, wg, p, wp, tm, tn):
    t, d = h.shape
    pd = p.shape[1]
    return pl.pallas_call(
        _ple_body,
        grid=(t // tm, d // tn),
        in_specs=[pl.BlockSpec((tm, d), lambda i, j: (i, 0)),
                  pl.BlockSpec((1, d), lambda i, j: (0, 0)),
                  pl.BlockSpec((d, tn), lambda i, j: (0, j)),
                  pl.BlockSpec((tm, pd), lambda i, j: (i, 0)),
                  pl.BlockSpec((pd, tn), lambda i, j: (0, j)),
                  pl.BlockSpec((tm, tn), lambda i, j: (i, j))],
        out_specs=pl.BlockSpec((tm, tn), lambda i, j: (i, j)),
        out_shape=jax.ShapeDtypeStruct((t, d), F32),
        scratch_shapes=[pltpu.VMEM((tm, d), BF16)],
        compiler_params=_params("parallel", "arbitrary"),
        name="ple_gate",
    )(h, g.reshape(1, d), wg, p, wp, h)


def _rmsnorm_body(x_ref, g_ref, o_ref):
    x = x_ref[...]
    ms = jnp.mean(x * x, axis=-1, keepdims=True)
    o_ref[...] = x * lax.rsqrt(ms + EPS) * g_ref[...]


def _rmsnorm(x, g, tm):
    t, d = x.shape
    return pl.pallas_call(
        _rmsnorm_body,
        grid=(t // tm,),
        in_specs=[pl.BlockSpec((tm, d), lambda i: (i, 0)), pl.BlockSpec((1, d), lambda i: (0, 0))],
        out_specs=pl.BlockSpec((tm, d), lambda i: (i, 0)),
        out_shape=jax.ShapeDtypeStruct((t, d), F32),
        compiler_params=_params("parallel"),
        name="final_rmsnorm",
    )(x, g.reshape(1, d))


def _toeplitz(vec, nq, nk):
    assert nq == nk
    lead = vec.shape[:-1]
    g = int(np.prod(lead))
    w = jnp.concatenate([jnp.zeros(lead + (1,), vec.dtype), vec[..., ::-1]], axis=-1).reshape(g, 1, 2 * nk)

    def body(w_ref, o_ref):
        rows = jnp.broadcast_to(w_ref[0], (nq, 2 * nk))
        o_ref[0] = pltpu.roll(rows, 0, 1, stride=1, stride_axis=0)[:, nk:]

    out = pl.pallas_call(
        body,
        grid=(g,),
        in_specs=[pl.BlockSpec((1, 1, 2 * nk), lambda i: (i, 0, 0))],
        out_specs=pl.BlockSpec((1, nq, nk), lambda i: (i, 0, 0)),
        out_shape=jax.ShapeDtypeStruct((g, nq, nk), vec.dtype),
        compiler_params=_params("parallel"),
        name="toeplitz_tiles",
    )(w)
    return out.reshape(lead + (nq, nk))


def _bias_tables(rel_bias, s):
    nd = s // BK
    heads = rel_bias.shape[1]
    bdt = rel_bias[_rel_bucket(jnp.arange(s))].astype(F32).T
    dd = (jnp.arange(nd)[:, None, None] * BK + jnp.arange(BQ)[None, :, None]
          - jnp.arange(BK)[None, None, :])
    ext = jnp.pad(bdt, ((0, 0), (BK - 1, 0)))
    segs = jnp.stack([ext[:, dl * BK:dl * BK + BQ + BK - 1] for dl in range(nd)], axis=1)
    tz = _toeplitz(segs, BQ, BK)
    causal = dd >= 0
    mult = sum(((dd % dil == 0) & (dd // dil <= window // dil)).astype(F32) for window, dil in DIL_PATTERNS)
    ok = causal & (mult > 0)
    tzd = jnp.where(ok, tz[:A_HEADS] + jnp.log(jnp.where(ok, mult, 1.0)), NEG)
    tzb = tz[A_HEADS:A_HEADS + B_HEADS]
    tzs = jnp.where(causal, tzb, NEG)
    nwin = -(-(NSA_WINDOW - 1) // BK) + 1
    tzw = jnp.where(causal & (dd <= NSA_WINDOW - 1), tzb, NEG)[:, :nwin]
    tzc = jnp.where(causal, tz[A_HEADS + B_HEADS:], NEG)
    ncmp = s // CMP_STRIDE
    na = s // CMP_STRIDE
    nvec = na + ncmp - 1
    lo = CMP_STRIDE * (ncmp - 1) + CMP_LEN - 1
    bdb = bdt[A_HEADS:A_HEADS + B_HEADS]
    gext = jnp.concatenate([jnp.full((B_HEADS, lo), NEG, F32), bdb], axis=1)
    vecs = gext[:, :CMP_STRIDE * nvec].reshape(B_HEADS, nvec, CMP_STRIDE).transpose(0, 2, 1)
    bcq = _toeplitz(vecs, na, ncmp)
    bcq = bcq.transpose(0, 2, 1, 3).reshape(B_HEADS, s, ncmp)
    bct = jnp.transpose(bcq, (0, 2, 1))
    nslc = s // SLC_LEN
    cstart = jnp.arange(ncmp) * CMP_STRIDE
    sstart = jnp.arange(nslc) * SLC_LEN
    covt = ((cstart[None, :] < sstart[:, None] + SLC_LEN)
            & (cstart[None, :] + CMP_LEN > sstart[:, None])).astype(F32)
    nblk = s // MOBA_BLK
    avg = ((jnp.arange(s)[None, :] // MOBA_BLK == jnp.arange(nblk)[:, None]).astype(F32)
           / MOBA_BLK).astype(BF16)
    return tzd, tzs, tzw, tzc, bcq, bct, covt, avg


def _pack_body(u_ref, v_ref, o_ref):
    _, te, d = u_ref.shape
    nlt = d // LANE
    ub = lax.bitcast_convert_type(u_ref[0].astype(BF16).astype(F32), jnp.uint32)
    vb = lax.bitcast_convert_type(v_ref[0].astype(BF16).astype(F32), jnp.uint32)
    word = lax.shift_right_logical(ub, jnp.uint32(16)) | vb
    for j in range(nlt):
        o_ref[pl.ds(j, te, stride=nlt), :] = word[:, j * LANE:(j + 1) * LANE]


def _pack_uv(u, v, layer, te=256):
    _, e, d = u.shape
    nlt = d // LANE
    spec = pl.BlockSpec((1, te, d), lambda i: (layer, i, 0))
    out = pl.pallas_call(
        _pack_body,
        grid=(e // te,),
        in_specs=[spec, spec],
        out_specs=pl.BlockSpec((te * nlt, LANE), lambda i: (i, 0)),
        out_shape=jax.ShapeDtypeStruct((e * nlt, LANE), jnp.uint32),
        compiler_params=_params("parallel"),
        name="pack_experts",
    )(u, v)
    return out.reshape(e, nlt, LANE)


GATE_COL0 = 3 * A_HEADS * LANE + B_HEADS * LANE + 6 * NSA_KV * LANE
GATE_COLS = 3 * B_HEADS


def _w_main_body(w_ref, o_ref):
    x = w_ref[0]
    o_ref[:, :GATE_COL0] = x[:, :GATE_COL0].astype(BF16)
    o_ref[:, GATE_COL0:] = x[:, GATE_COL0 + GATE_COLS:].astype(BF16)


def _reorder_w_in(w_in, layer, tk=256):
    _, d, cols = w_in.shape
    main = pl.pallas_call(
        _w_main_body,
        grid=(d // tk,),
        in_specs=[pl.BlockSpec((1, tk, cols), lambda i: (layer, i, 0))],
        out_specs=pl.BlockSpec((tk, MAIN_COLS), lambda i: (i, 0)),
        out_shape=jax.ShapeDtypeStruct((d, MAIN_COLS), BF16),
        compiler_params=_params("parallel"),
        name="reorder_w_in",
    )(w_in)
    per = GATE_COLS // NSA_KV
    wg = w_in[layer, :, GATE_COL0:GATE_COL0 + GATE_COLS]
    gate = jnp.concatenate([jnp.pad(wg[:, g * per:(g + 1) * per], ((0, 0), (0, LANE - per))) for g in range(NSA_KV)],
                           axis=1)
    return main, gate.astype(BF16)


def kernel(x, p, ln_mix, w_in, cmp_wk, cmp_wv, cmp_pos, out_norm, w_out, rel_bias, ln_ffn, peer_wq,
           peer_keys, peer_u, peer_v, ln_ple, ple_gate, ple_proj, ln_final):
    b, s, d = x.shape
    t = b * s
    depth = w_in.shape[0]
    assert s % BQ == 0 and BQ == BK == MOBA_BLK and d % LANE == 0
    tm = 1024 if t % 1024 == 0 else 512
    tzd, tzs, tzw, tzc, bcq, bct, covt, avg = _bias_tables(rel_bias, s)
    h = x.reshape(t, d)
    for i in range(depth):
        w_main, w_gate = _reorder_w_in(w_in, i)
        proj = _normmm(h, ln_mix[i], w_main, BF16, tm, 1024, "in_projection").reshape(b, s, MAIN_COLS)
        gates = _normmm(h, ln_mix[i], w_gate, F32, tm, w_gate.shape[1], "gate_projection")
        gates = gates.reshape(b, s, NSA_KV * LANE)
        oa = _dilated(proj, tzd)
        kcmp, vcmp = _compress(proj, cmp_wk[i], cmp_wv[i], cmp_pos[i])
        ob = _nsa(proj, gates, kcmp, vcmp, bcq, bct, covt, tzs, tzw)
        oc = _moba(proj, avg, tzc)
        h = _outproj(oa.reshape(t, -1), ob.reshape(t, -1), oc.reshape(t, -1), out_norm[i],
                     w_out[i].astype(BF16), h, tm, 1024)
        qp = _normmm(h, ln_ffn[i], peer_wq[i].astype(BF16), F32, tm, 1024, "peer_query")
        keys = peer_keys[i].reshape(PEER_HEADS * 2, PEER_NKEYS, -1)
        e_t, g_t = _peer_topk(qp, keys, 256)
        h = _peer_experts(e_t.T, g_t.T, h, ln_ffn[i], _pack_uv(peer_u, peer_v, i), LANE)
        h = _ple(h, ln_ple[i], ple_gate[i].astype(BF16), p[i].reshape(t, -1), ple_proj[i].astype(BF16),
                 tm, 1024)
    return _rmsnorm(h, ln_final, tm).reshape(b, s, d)
```

```python
import functools
import math

import numpy as np
import jax
import jax.numpy as jnp
from jax import lax
from jax.experimental import pallas as pl
from jax.experimental.pallas import tpu as pltpu

F32 = jnp.float32
BF16 = jnp.bfloat16
I32 = jnp.int32

HEAD_DIM = 128
A_HEADS, B_HEADS, C_HEADS = 6, 6, 4
NSA_KV, NSA_GROUP = 2, 3
DIL_PATTERNS = ((128, 1), (512, 4), (2048, 16))
CMP_LEN, CMP_STRIDE = 32, 16
SLC_LEN, SLC_TOPN = 64, 8
NSA_WINDOW = 512
FORCE = 1e4
MOBA_BLK, MOBA_TOPK = 256, 3
REL_BUCKETS, REL_MAX_DIST = 32, 1024
PEER_HEADS, PEER_NKEYS, PEER_TOPK = 8, 128, 16
EPS = 1e-6

LANE = 128
BQ = 256
BK = 256
NEG = -1e30
HALF_NEG = -5e29
VMEM_LIMIT = 56 * 1024 * 1024

CB_QA, CB_KA, CB_VA = 0, 6, 12
CB_QB, CB_KC, CB_VC, CB_KS, CB_VS, CB_KW, CB_VW = 18, 24, 26, 28, 30, 32, 34
CB_QC, CB_KCC, CB_VCC = 36, 40, 44
MAIN_COLS = 48 * LANE


def _nt(a, b, precision=None):
    return lax.dot_general(a, b, (((1,), (1,)), ((), ())), preferred_element_type=F32,
                           precision=precision)


def _rel_bucket(dist):
    exact = REL_BUCKETS // 2
    d = jnp.maximum(dist, 0)
    logd = jnp.log(jnp.maximum(d, 1).astype(F32) / exact)
    large = exact + (logd / math.log(REL_MAX_DIST / exact) * (REL_BUCKETS - exact)).astype(I32)
    large = jnp.clip(large, exact, REL_BUCKETS - 1)
    return jnp.where(d < exact, d, large)


def _masked_softmax(s, axis):
    valid = s > HALF_NEG
    m = jnp.max(s, axis=axis, keepdims=True)
    e = jnp.where(valid, jnp.exp(s - m), 0.0)
    z = jnp.sum(e, axis=axis, keepdims=True)
    zs = jnp.where(z > 0, z, 1.0)
    return e * (1.0 / zs)


def _params(*sem):
    return pltpu.CompilerParams(dimension_semantics=sem, vmem_limit_bytes=VMEM_LIMIT)


def _normmm_body(x_ref, g_ref, w_ref, o_ref, xn_ref):
    @pl.when(pl.program_id(1) == 0)
    def _():
        x = x_ref[...]
        ms = jnp.mean(x * x, axis=-1, keepdims=True)
        xn_ref[...] = (x * lax.rsqrt(ms + EPS) * g_ref[...]).astype(BF16)

    o_ref[...] = jnp.dot(xn_ref[...], w_ref[...], preferred_element_type=F32).astype(o_ref.dtype)


def _normmm(x, g, w, out_dtype, tm, tn, name):
    t, d = x.shape
    n = w.shape[1]
    return pl.pallas_call(
        _normmm_body,
        grid=(t // tm, n // tn),
        in_specs=[pl.BlockSpec((tm, d), lambda i, j: (i, 0)),
                  pl.BlockSpec((1, d), lambda i, j: (0, 0)),
                  pl.BlockSpec((d, tn), lambda i, j: (0, j))],
        out_specs=pl.BlockSpec((tm, tn), lambda i, j: (i, j)),
        out_shape=jax.ShapeDtypeStruct((t, n), out_dtype),
        scratch_shapes=[pltpu.VMEM((tm, d), BF16)],
        compiler_params=_params("parallel", "arbitrary"),
        name=name,
    )(x, g.reshape(1, d), w)


def _flash(qs, k_ref, v_ref, lo, hi, score_fns, scale):
    def tile(kj):
        off = pl.multiple_of(kj * BK, BK)
        return k_ref[0, pl.ds(off, BK), :], v_ref[0, pl.ds(off, BK), :]

    def update(state, blocks):
        m, l, acc = state
        m_new = m
        for s, _ in blocks:
            m_new = jnp.maximum(m_new, jnp.max(s, axis=1, keepdims=True))
        alpha = jnp.exp(m - m_new)
        l = alpha * l
        acc = alpha * acc
        for s, v in blocks:
            p = jnp.exp(s - m_new)
            l = l + jnp.sum(p, axis=1, keepdims=True)
            acc = acc + jnp.dot(p.astype(BF16), v, preferred_element_type=F32)
        return m_new, l, acc

    def step(kjs, states):
        kv = [tile(kj) for kj in kjs]
        return tuple(
            update(states[i], [(score_fns[i](kj, _nt(q, k) * scale), v) for kj, (k, v) in zip(kjs, kv)])
            for i, q in enumerate(qs))

    states = tuple((jnp.full((BQ, 1), NEG, F32), jnp.zeros((BQ, 1), F32), jnp.zeros((BQ, HEAD_DIM), F32))
                   for _ in qs)
    start = lo
    for width in ((4, 2, 1) if len(qs) == 1 else (2, 1)):
        shift = width.bit_length() - 1
        count = lax.shift_right_logical(hi - start, shift)

        def group(p, st, start=start, width=width):
            return step([start + width * p + i for i in range(width)], st)

        states = lax.fori_loop(0, count, group, states)
        start = start + count * width
    return [acc / l for _, l, acc in states]


def _dilated_body(q_ref, k_ref, v_ref, tz_ref, o_ref):
    qi = pl.program_id(2)
    scale = HEAD_DIM ** -0.5

    def score(kj, s):
        return s + tz_ref[0, qi - kj]

    o_ref[0] = _flash([q_ref[0]], k_ref, v_ref, 0, qi + 1, [score], scale)[0]


def _dilated(proj, tzd):
    b, s, _ = proj.shape
    nd = s // BK
    return pl.pallas_call(
        _dilated_body,
        grid=(b, A_HEADS, s // BQ),
        in_specs=[pl.BlockSpec((1, BQ, LANE), lambda bi, h, qi: (bi, qi, CB_QA + h)),
                  pl.BlockSpec((1, s, LANE), lambda bi, h, qi: (bi, 0, CB_KA + h)),
                  pl.BlockSpec((1, s, LANE), lambda bi, h, qi: (bi, 0, CB_VA + h)),
                  pl.BlockSpec((1, nd, BQ, BK), lambda bi, h, qi: (h, 0, 0, 0))],
        out_specs=pl.BlockSpec((1, BQ, LANE), lambda bi, h, qi: (bi, qi, h)),
        out_shape=jax.ShapeDtypeStruct((b, s, A_HEADS * LANE), F32),
        compiler_params=_params("parallel", "parallel", "arbitrary"),
        name="dilated_attention",
    )(proj, proj, proj, tzd)


def _compress_body(kc_ref, vc_ref, wk_ref, wv_ref, pos_ref, ko_ref, vo_ref, xk_ref, xv_ref):
    s = kc_ref.shape[1]
    ncmp = ko_ref.shape[2]
    xk_ref[pl.ds(0, s), :] = kc_ref[0].astype(F32)
    xv_ref[pl.ds(0, s), :] = vc_ref[0].astype(F32)
    xk_ref[pl.ds(s, CMP_LEN), :] = jnp.zeros((CMP_LEN, HEAD_DIM), F32)
    xv_ref[pl.ds(s, CMP_LEN), :] = jnp.zeros((CMP_LEN, HEAD_DIM), F32)
    acck = jnp.zeros((ncmp, HEAD_DIM), F32)
    accv = jnp.zeros((ncmp, HEAD_DIM), F32)
    for l in range(CMP_LEN):
        pos = pos_ref[pl.ds(l, 1), :]
        rk = (xk_ref[pl.ds(l, ncmp, stride=CMP_STRIDE), :] + pos).astype(BF16)
        rv = (xv_ref[pl.ds(l, ncmp, stride=CMP_STRIDE), :] + pos).astype(BF16)
        acck = acck + jnp.dot(rk, wk_ref[l], preferred_element_type=F32)
        accv = accv + jnp.dot(rv, wv_ref[l], preferred_element_type=F32)
    ko_ref[0, 0] = acck.astype(BF16)
    vo_ref[0, 0] = accv.astype(BF16)


def _compress(proj, cmp_wk, cmp_wv, cmp_pos):
    b, s, _ = proj.shape
    ncmp = s // CMP_STRIDE
    out = jax.ShapeDtypeStruct((b, NSA_KV, ncmp, HEAD_DIM), BF16)
    return pl.pallas_call(
        _compress_body,
        grid=(b, NSA_KV),
        in_specs=[pl.BlockSpec((1, s, LANE), lambda bi, g: (bi, 0, CB_KC + g)),
                  pl.BlockSpec((1, s, LANE), lambda bi, g: (bi, 0, CB_VC + g)),
                  pl.BlockSpec((CMP_LEN, HEAD_DIM, HEAD_DIM), lambda bi, g: (0, 0, 0)),
                  pl.BlockSpec((CMP_LEN, HEAD_DIM, HEAD_DIM), lambda bi, g: (0, 0, 0)),
                  pl.BlockSpec((CMP_LEN, HEAD_DIM), lambda bi, g: (0, 0))],
        out_specs=[pl.BlockSpec((1, 1, ncmp, HEAD_DIM), lambda bi, g: (bi, g, 0, 0)),
                   pl.BlockSpec((1, 1, ncmp, HEAD_DIM), lambda bi, g: (bi, g, 0, 0))],
        out_shape=[out, out],
        scratch_shapes=[pltpu.VMEM((s + CMP_LEN, HEAD_DIM), F32),
                        pltpu.VMEM((s + CMP_LEN, HEAD_DIM), F32)],
        compiler_params=_params("parallel", "parallel"),
        name="nsa_compress",
    )(proj, proj, cmp_wk.astype(BF16), cmp_wv.astype(BF16), cmp_pos)


def _nsa_body(q_ref, kcmp_ref, vcmp_ref, ks_ref, vs_ref, kw_ref, vw_ref, gt_ref, bct_ref,
              covt_ref, tzs_ref, tzw_ref, o_ref, mask_ref):
    qi = pl.program_id(2)
    scale = HEAD_DIM ** -0.5
    nslc = covt_ref.shape[0]
    nkt = mask_ref.shape[0]
    kcmp = kcmp_ref[0, 0]
    vcmp = vcmp_ref[0, 0]
    gates = jax.nn.sigmoid(gt_ref[0])

    psum = jnp.zeros((kcmp.shape[0], BQ), F32)
    for r in range(NSA_GROUP):
        q = q_ref[0, :, r * LANE:(r + 1) * LANE]
        pt = _masked_softmax(_nt(kcmp, q) * scale + bct_ref[r], axis=0)
        o_cmp = lax.dot_general(pt.astype(BF16), vcmp, (((0,), (0,)), ((), ())),
                                preferred_element_type=F32)
        o_ref[0, :, r * LANE:(r + 1) * LANE] = gates[:, 3 * r:3 * r + 1] * o_cmp
        psum = psum + pt
    imp = jnp.dot(covt_ref[...], psum, preferred_element_type=F32,
                  precision=lax.Precision.HIGHEST)
    jb = lax.broadcasted_iota(I32, (nslc, BQ), 0)
    qpos = qi * BQ + lax.broadcasted_iota(I32, (nslc, BQ), 1)
    qblk = lax.shift_right_logical(qpos, int(math.log2(SLC_LEN)))
    forced = (jb == 0) | (jb == qblk) | (jb == qblk - 1)
    imp = jnp.where(forced, FORCE, jnp.where(jb > qblk, -FORCE, imp))
    beaten = jnp.zeros((nslc, BQ), F32)
    for jp in range(nslc):
        row = imp[jp:jp + 1, :]
        wins = (row > imp) | ((row == imp) & (jp < jb))
        beaten = beaten + wins.astype(F32)
    selt = (beaten < SLC_TOPN).astype(BF16)
    eye = (lax.broadcasted_iota(I32, (BQ, BQ), 0) == lax.broadcasted_iota(I32, (BQ, BQ), 1)).astype(BF16)
    selq = _nt(eye, selt).astype(BF16)
    kpos = lax.broadcasted_iota(I32, (nslc, BK), 1)
    kblk_row = lax.broadcasted_iota(I32, (nslc, BK), 0)

    for t in range(nkt):
        expand = (lax.shift_right_logical(t * BK + kpos, int(math.log2(SLC_LEN))) == kblk_row).astype(BF16)
        mask_ref[t] = (jnp.dot(selq, expand, preferred_element_type=F32) - 1.0) * (-NEG)

    qs = [q_ref[0, :, r * LANE:(r + 1) * LANE] for r in range(NSA_GROUP)]
    sel_fns = [lambda kj, s, r=r: s + tzs_ref[r, qi - kj] + mask_ref[kj] for r in range(NSA_GROUP)]
    win_fns = [lambda kj, s, r=r: s + tzw_ref[r, qi - kj] for r in range(NSA_GROUP)]
    nwin = tzw_ref.shape[1]
    o_sel = _flash(qs, ks_ref, vs_ref, 0, qi + 1, sel_fns, scale)
    o_win = _flash(qs, kw_ref, vw_ref, jnp.maximum(qi - (nwin - 1), 0), qi + 1, win_fns, scale)
    for r in range(NSA_GROUP):
        o_ref[0, :, r * LANE:(r + 1) * LANE] += (gates[:, 3 * r + 1:3 * r + 2] * o_sel[r]
                                                 + gates[:, 3 * r + 2:3 * r + 3] * o_win[r])


def _nsa(proj, gates, kcmp, vcmp, bct, covt, tzs, tzw):
    b, s, _ = proj.shape
    nd = s // BK
    ncmp = kcmp.shape[2]
    nslc = covt.shape[0]
    nwin = tzw.shape[1]
    gw = NSA_GROUP * LANE
    kv = lambda cb: pl.BlockSpec((1, s, LANE), lambda bi, g, qi: (bi, 0, cb + g))
    return pl.pallas_call(
        _nsa_body,
        grid=(b, NSA_KV, s // BQ),
        in_specs=[pl.BlockSpec((1, BQ, gw), lambda bi, g, qi: (bi, qi, CB_QB // NSA_GROUP + g)),
                  pl.BlockSpec((1, 1, ncmp, HEAD_DIM), lambda bi, g, qi: (bi, g, 0, 0)),
                  pl.BlockSpec((1, 1, ncmp, HEAD_DIM), lambda bi, g, qi: (bi, g, 0, 0)),
                  kv(CB_KS), kv(CB_VS), kv(CB_KW), kv(CB_VW),
                  pl.BlockSpec((1, BQ, LANE), lambda bi, g, qi: (bi, qi, g)),
                  pl.BlockSpec((NSA_GROUP, ncmp, BQ), lambda bi, g, qi: (g, 0, qi)),
                  pl.BlockSpec((nslc, ncmp), lambda bi, g, qi: (0, 0)),
                  pl.BlockSpec((NSA_GROUP, nd, BQ, BK), lambda bi, g, qi: (g, 0, 0, 0)),
                  pl.BlockSpec((NSA_GROUP, nwin, BQ, BK), lambda bi, g, qi: (g, 0, 0, 0))],
        out_specs=pl.BlockSpec((1, BQ, gw), lambda bi, g, qi: (bi, qi, g)),
        out_shape=jax.ShapeDtypeStruct((b, s, B_HEADS * LANE), F32),
        scratch_shapes=[pltpu.VMEM((nd, BQ, BK), F32)],
        compiler_params=_params("parallel", "parallel", "arbitrary"),
        name="nsa_attention",
    )(proj, kcmp, vcmp, proj, proj, proj, proj, gates, bct, covt, tzs, tzw)


def _moba_body(q_ref, k_ref, v_ref, avg_ref, tz_ref, o_ref, mask_ref):
    qi = pl.program_id(2)
    scale = HEAD_DIM ** -0.5
    nblk = avg_ref.shape[0]
    q = q_ref[0]
    kmean = jnp.dot(avg_ref[...], k_ref[0], preferred_element_type=F32)
    gate = _nt(kmean, q.astype(F32), precision=lax.Precision.HIGHEST)
    nb = lax.broadcasted_iota(I32, (nblk, BQ), 0)
    past = nb < qi
    beaten = jnp.zeros((nblk, BQ), F32)
    for np_ in range(nblk):
        row = gate[np_:np_ + 1, :]
        wins = (np_ < qi) & ((row > gate) | ((row == gate) & (np_ < nb)))
        beaten = beaten + wins.astype(F32)
    selt = (past & (beaten < MOBA_TOPK)).astype(BF16)
    eye = (lax.broadcasted_iota(I32, (BQ, BQ), 0) == lax.broadcasted_iota(I32, (BQ, BQ), 1)).astype(BF16)
    selq = _nt(eye, selt)
    for n in range(nblk):
        mask_ref[n] = jnp.broadcast_to(selq[:, n:n + 1], (BQ, BK))

    def score(kj, s):
        keep = (mask_ref[kj] > 0.5) | (kj == qi)
        return jnp.where(keep, s + tz_ref[0, qi - kj], NEG)

    o_ref[0] = _flash([q], k_ref, v_ref, 0, qi + 1, [score], scale)[0]


def _moba(proj, avg, tzc):
    b, s, _ = proj.shape
    nd = s // BK
    nblk = avg.shape[0]
    return pl.pallas_call(
        _moba_body,
        grid=(b, C_HEADS, s // BQ),
        in_specs=[pl.BlockSpec((1, BQ, LANE), lambda bi, h, qi: (bi, qi, CB_QC + h)),
                  pl.BlockSpec((1, s, LANE), lambda bi, h, qi: (bi, 0, CB_KCC + h)),
                  pl.BlockSpec((1, s, LANE), lambda bi, h, qi: (bi, 0, CB_VCC + h)),
                  pl.BlockSpec((nblk, s), lambda bi, h, qi: (0, 0)),
                  pl.BlockSpec((1, nd, BQ, BK), lambda bi, h, qi: (h, 0, 0, 0))],
        out_specs=pl.BlockSpec((1, BQ, LANE), lambda bi, h, qi: (bi, qi, h)),
        out_shape=jax.ShapeDtypeStruct((b, s, C_HEADS * LANE), F32),
        scratch_shapes=[pltpu.VMEM((nblk, BQ, BK), F32)],
        compiler_params=_params("parallel", "parallel", "arbitrary"),
        name="moba_attention",
    )(proj, proj, proj, avg, tzc)


def _outproj_body(oa_ref, ob_ref, oc_ref, g_ref, w_ref, h_ref, o_ref, y_ref):
    @pl.when(pl.program_id(1) == 0)
    def _():
        c0 = 0
        for ref in (oa_ref, ob_ref, oc_ref):
            x = ref[...]
            wd = x.shape[1]
            ms = jnp.mean(x * x, axis=-1, keepdims=True)
            y_ref[:, c0:c0 + wd] = (x * lax.rsqrt(ms + EPS) * g_ref[:, c0:c0 + wd]).astype(BF16)
            c0 += wd

    o_ref[...] = h_ref[...] + jnp.dot(y_ref[...], w_ref[...], preferred_element_type=F32)


def _outproj(oa, ob, oc, g, w, h, tm, tn):
    t, d = h.shape
    row = lambda a: pl.BlockSpec((tm, a.shape[1]), lambda i, j: (i, 0))
    return pl.pallas_call(
        _outproj_body,
        grid=(t // tm, d // tn),
        in_specs=[row(oa), row(ob), row(oc),
                  pl.BlockSpec((1, d), lambda i, j: (0, 0)),
                  pl.BlockSpec((d, tn), lambda i, j: (0, j)),
                  pl.BlockSpec((tm, tn), lambda i, j: (i, j))],
        out_specs=pl.BlockSpec((tm, tn), lambda i, j: (i, j)),
        out_shape=jax.ShapeDtypeStruct((t, d), F32),
        scratch_shapes=[pltpu.VMEM((tm, d), BF16)],
        compiler_params=_params("parallel", "arbitrary"),
        name="out_projection",
    )(oa, ob, oc, g.reshape(1, d), w, h)


def _topk_rows(x, k, payload=None):
    n, tm = x.shape
    rows = lax.broadcasted_iota(I32, x.shape, 0)
    slot = lax.broadcasted_iota(I32, (k, tm), 0)
    vals = jnp.zeros((k, tm), F32)
    idxs = jnp.zeros((k, tm), I32)
    for it in range(k):
        mx = jnp.max(x, axis=0, keepdims=True)
        idx = jnp.min(jnp.where(x == mx, rows, n), axis=0, keepdims=True)
        hit = rows == idx
        if payload is not None:
            idx = jnp.sum(jnp.where(hit, payload, 0), axis=0, keepdims=True)
        vals = jnp.where(slot == it, mx, vals)
        idxs = jnp.where(slot == it, idx, idxs)
        x = jnp.where(hit, -jnp.inf, x)
    return vals, idxs


def _peer_topk_body(q_ref, keys_ref, e_ref, g_ref):
    for h in range(PEER_HEADS):
        tops = []
        for c in range(2):
            hc = 2 * h + c
            sc = _nt(keys_ref[hc], q_ref[:, hc * LANE:(hc + 1) * LANE])
            tops.append(_topk_rows(sc, PEER_TOPK))
        (s0, i0), (s1, i1) = tops
        half = PEER_TOPK // 2
        sub = lax.broadcasted_iota(I32, (half, s0.shape[1]), 0)
        pieces = [s0[0:1] + s1]
        pieces_i = [i0[0:1] * PEER_NKEYS + i1]
        for a in range(1, half):
            keep = sub < PEER_TOPK // (a + 1)
            pieces.append(jnp.where(keep, s0[a:a + 1] + s1[0:half], -jnp.inf))
            pieces_i.append(i0[a:a + 1] * PEER_NKEYS + i1[0:half])
        pieces.append(s0[half:] + s1[0:1])
        pieces_i.append(i0[half:] * PEER_NKEYS + i1[0:1])
        cand = jnp.concatenate(pieces, axis=0)
        cand_i = jnp.concatenate(pieces_i, axis=0)
        bs, be = _topk_rows(cand, PEER_TOPK, payload=cand_i)
        e = jnp.exp(bs - jnp.max(bs, axis=0, keepdims=True))
        g_ref[h * PEER_TOPK:(h + 1) * PEER_TOPK, :] = e / jnp.sum(e, axis=0, keepdims=True)
        e_ref[h * PEER_TOPK:(h + 1) * PEER_TOPK, :] = be


def _peer_topk(qp, keys, tm):
    t, d = qp.shape
    hk = PEER_HEADS * PEER_TOPK
    return pl.pallas_call(
        _peer_topk_body,
        grid=(t // tm,),
        in_specs=[pl.BlockSpec((tm, d), lambda i: (i, 0)),
                  pl.BlockSpec(keys.shape, lambda i: (0, 0, 0))],
        out_specs=[pl.BlockSpec((hk, tm), lambda i: (0, i)),
                   pl.BlockSpec((hk, tm), lambda i: (0, i))],
        out_shape=[jax.ShapeDtypeStruct((hk, t), I32), jax.ShapeDtypeStruct((hk, t), F32)],
        compiler_params=_params("parallel"),
        name="peer_topk",
    )(qp, keys)


PEER_NBUF = 16
PEER_AHEAD = 14


def _gelu(x):
    return 0.5 * x * (1.0 + lax.erf(x * (2.0 ** -0.5)))


def _hi_lo(x):
    hi = x.astype(BF16)
    lo = (x - hi.astype(F32)).astype(BF16)
    return jnp.concatenate([hi, lo], axis=1)


def _peer_expert_body(ids0_ref, idsn_ref, h_ref, ln_ref, g_ref, uvw_ref, o_ref, xn_ref, acc_ref, lhs_ref,
                      buf_ref, sem_ref):
    tb, d = h_ref.shape
    hk = idsn_ref.shape[1]
    nrg = hk // 8
    nlt = d // LANE
    assert nrg == nlt and hk % (2 * nlt) == 0
    step = pl.program_id(0)
    x = h_ref[...]
    ms = jnp.mean(x * x, axis=-1, keepdims=True)
    xn_ref[...] = x * lax.rsqrt(ms + EPS) * ln_ref[...]

    def issue(ids_ref, row, slot, k0, k1):
        for k in range(k0, k1):
            e = ids_ref[row, k]
            pltpu.make_async_copy(uvw_ref.at[e], buf_ref.at[slot, :, k, :],
                                  sem_ref.at[slot]).start(priority=k % 2)

    def wait_all(slot):
        pltpu.make_async_copy(buf_ref.at[slot], buf_ref.at[slot], sem_ref.at[slot]).wait()

    def x_tiles(t):
        xrow = xn_ref[pl.ds(t, 1), :]
        return [jnp.broadcast_to(xrow[:, j * LANE:(j + 1) * LANE], (8, LANE)) for j in range(nlt)]

    def dots(xb, slot, r):
        a = None
        for j in range(nlt):
            w = buf_ref[slot, j, r * 8:(r + 1) * 8, :]
            pr = lax.bitcast_convert_type(lax.shift_left(w, jnp.uint32(16)), F32) * xb[j]
            a = pr if a is None else a + pr
        return a

    def coefficients(t, slot):
        act = jnp.sum(acc_ref[slot % 2].T, axis=0, keepdims=True)
        coef = jnp.broadcast_to(_gelu(act) * g_ref[pl.ds(t, 1), :], (8, hk))
        chi = coef.astype(BF16).astype(F32)
        return jnp.concatenate([chi, coef - chi], axis=0).astype(BF16)

    def weighted(lhs, slot, j):
        w = buf_ref[slot, j]
        v = lax.bitcast_convert_type(w & jnp.uint32(0xFFFF0000), F32).astype(BF16)
        yj = jnp.dot(lhs, v, preferred_element_type=F32)
        return yj[0:1, :] + yj[8:9, :]

    def flush(done):
        ya, yb, tp = done
        y = jnp.concatenate([ya[j:j + 1, :] for j in range(8)] + [yb[j:j + 1, :] for j in range(8)], axis=1)
        o_ref[pl.ds(tp, 1), :] = h_ref[pl.ds(tp, 1), :] + y

    def turn(t, slot, stages, done):
        tgt = (slot + PEER_AHEAD) % PEER_NBUF
        far = (slot + 2) % PEER_NBUF
        per = hk // (2 * nlt)
        if stages >= 3:
            wait_all(far)
            xb = x_tiles(t + 2)
        lhs = lhs_ref[slot % 2]
        ys, parts = [], []
        for c in range(nlt):
            if stages >= 3:
                parts.append(dots(xb, far, c))
            issue(idsn_ref, t, tgt, 2 * c * per, (2 * c + 1) * per)
            if 2 * c < nlt:
                ys.append(weighted(lhs, slot, 2 * c))
                ys.append(weighted(lhs, slot, 2 * c + 1))
            if 2 * c == nlt and stages >= 2:
                lhs_next = coefficients(t + 1, slot + 1)
            issue(idsn_ref, t, tgt, (2 * c + 1) * per, (2 * c + 2) * per)
        if stages >= 3:
            acc_ref[slot % 2] = jnp.concatenate(parts, axis=0)
        if stages >= 2:
            lhs_ref[(slot + 1) % 2] = lhs_next
        flush(done)
        return (jnp.concatenate(ys[:8], axis=0), jnp.concatenate(ys[8:], axis=0), t)

    @pl.when(step == 0)
    def _():
        for t0 in range(PEER_AHEAD):
            issue(ids0_ref, t0, t0, 0, hk)

    for t0 in range(2):
        wait_all(t0)
        xb0 = x_tiles(t0)
        acc_ref[t0] = jnp.concatenate([dots(xb0, t0, r) for r in range(nrg)], axis=0)
    lhs_ref[0] = coefficients(0, 0)

    def ring(i, stacks):
        done = stacks + (jnp.maximum(i * PEER_NBUF - 1, 0),)
        for slot in range(PEER_NBUF):
            done = turn(i * PEER_NBUF + slot, slot, 3, done)
        return done[:2]

    zero = jnp.zeros((8, LANE), F32)
    nring = (tb - 2) // PEER_NBUF
    done = lax.fori_loop(0, nring, ring, (zero, zero)) + (nring * PEER_NBUF - 1,)
    for t in range(nring * PEER_NBUF, tb - 2):
        done = turn(t, t % PEER_NBUF, 3, done)
    done = turn(tb - 2, (tb - 2) % PEER_NBUF, 2, done)
    done = turn(tb - 1, (tb - 1) % PEER_NBUF, 1, done)
    flush(done)

    @pl.when(step == pl.num_programs(0) - 1)
    def _():
        for t0 in range(PEER_AHEAD):
            wait_all((tb + t0) % PEER_NBUF)


def _peer_experts(ids, gates, h, ln, slabs, tb):
    t, d = h.shape
    hk = ids.shape[1]
    assert hk == LANE and tb % PEER_NBUF == 0 and PEER_NBUF % 2 == 0 and d % LANE == 0
    assert PEER_AHEAD <= PEER_NBUF - 2
    ids_next = jnp.concatenate([ids[PEER_AHEAD:], ids[:PEER_AHEAD]], axis=0)
    assert slabs.shape[1:] == (d // LANE, LANE)
    return pl.pallas_call(
        _peer_expert_body,
        grid=(t // tb,),
        in_specs=[pl.BlockSpec((PEER_NBUF, hk), lambda i: (0, 0), memory_space=pltpu.SMEM),
                  pl.BlockSpec((tb, hk), lambda i: (i, 0), memory_space=pltpu.SMEM),
                  pl.BlockSpec((tb, d), lambda i: (i, 0)),
                  pl.BlockSpec((1, d), lambda i: (0, 0)),
                  pl.BlockSpec((tb, hk), lambda i: (i, 0)),
                  pl.BlockSpec(memory_space=pl.ANY)],
        out_specs=pl.BlockSpec((tb, d), lambda i: (i, 0)),
        out_shape=jax.ShapeDtypeStruct((t, d), F32),
        scratch_shapes=[pltpu.VMEM((tb, d), F32),
                        pltpu.VMEM((2, hk, LANE), F32),
                        pltpu.VMEM((2, 16, hk), BF16),
                        pltpu.VMEM((PEER_NBUF, d // LANE, hk, LANE), jnp.uint32),
                        pltpu.SemaphoreType.DMA((PEER_NBUF,))],
        compiler_params=_params("arbitrary"),
        name="peer_experts",
    )(ids[:PEER_NBUF], ids_next, h, ln.reshape(1, d), gates, slabs)


def _ple_body(x_ref, g_ref, wg_ref, p_ref, wp_ref, h_ref, o_ref, xn_ref):
    @pl.when(pl.program_id(1) == 0)
    def _():
        x = x_ref[...]
        ms = jnp.mean(x * x, axis=-1, keepdims=True)
        xn_ref[...] = (x * lax.rsqrt(ms + EPS) * g_ref[...]).astype(BF16)

    z = jnp.dot(xn_ref[...], wg_ref[...], preferred_element_type=F32)
    pp = jnp.dot(p_ref[...].astype(BF16), wp_ref[...], preferred_element_type=F32)
    o_ref[...] = h_ref[...] + jax.nn.sigmoid(z) * pp


def _ple(h, g, wg, p, wp, tm, tn):
    t, d = h.shape
    pd = p.shape[1]
    return pl.pallas_call(
        _ple_body,
        grid=(t // tm, d // tn),
        in_specs=[pl.BlockSpec((tm, d), lambda i, j: (i, 0)),
                  pl.BlockSpec((1, d), lambda i, j: (0, 0)),
                  pl.BlockSpec((d, tn), lambda i, j: (0, j)),
                  pl.BlockSpec((tm, pd), lambda i, j: (i, 0)),
                  pl.BlockSpec((pd, tn), lambda i, j: (0, j)),
                  pl.BlockSpec((tm, tn), lambda i, j: (i, j))],
        out_specs=pl.BlockSpec((tm, tn), lambda i, j: (i, j)),
        out_shape=jax.ShapeDtypeStruct((t, d), F32),
        scratch_shapes=[pltpu.VMEM((tm, d), BF16)],
        compiler_params=_params("parallel", "arbitrary"),
        name="ple_gate",
    )(h, g.reshape(1, d), wg, p, wp, h)


def _rmsnorm_body(x_ref, g_ref, o_ref):
    x = x_ref[...]
    ms = jnp.mean(x * x, axis=-1, keepdims=True)
    o_ref[...] = x * lax.rsqrt(ms + EPS) * g_ref[...]


def _rmsnorm(x, g, tm):
    t, d = x.shape
    return pl.pallas_call(
        _rmsnorm_body,
        grid=(t // tm,),
        in_specs=[pl.BlockSpec((tm, d), lambda i: (i, 0)), pl.BlockSpec((1, d), lambda i: (0, 0))],
        out_specs=pl.BlockSpec((tm, d), lambda i: (i, 0)),
        out_shape=jax.ShapeDtypeStruct((t, d), F32),
        compiler_params=_params("parallel"),
        name="final_rmsnorm",
    )(x, g.reshape(1, d))


def _toeplitz(vec, nq, nk):
    assert nq == nk
    lead = vec.shape[:-1]
    g = int(np.prod(lead))
    w = jnp.concatenate([jnp.zeros(lead + (1,), vec.dtype), vec[..., ::-1]], axis=-1).reshape(g, 1, 2 * nk)

    def body(w_ref, o_ref):
        rows = jnp.broadcast_to(w_ref[0], (nq, 2 * nk))
        o_ref[0] = pltpu.roll(rows, 0, 1, stride=1, stride_axis=0)[:, nk:]

    out = pl.pallas_call(
        body,
        grid=(g,),
        in_specs=[pl.BlockSpec((1, 1, 2 * nk), lambda i: (i, 0, 0))],
        out_specs=pl.BlockSpec((1, nq, nk), lambda i: (i, 0, 0)),
        out_shape=jax.ShapeDtypeStruct((g, nq, nk), vec.dtype),
        compiler_params=_params("parallel"),
        name="toeplitz_tiles",
    )(w)
    return out.reshape(lead + (nq, nk))


def _bias_tables(rel_bias, s):
    nd = s // BK
    heads = rel_bias.shape[1]
    bdt = rel_bias[_rel_bucket(jnp.arange(s))].astype(F32).T
    dd = (jnp.arange(nd)[:, None, None] * BK + jnp.arange(BQ)[None, :, None]
          - jnp.arange(BK)[None, None, :])
    ext = jnp.pad(bdt, ((0, 0), (BK - 1, 0)))
    segs = jnp.stack([ext[:, dl * BK:dl * BK + BQ + BK - 1] for dl in range(nd)], axis=1)
    tz = _toeplitz(segs, BQ, BK)
    causal = dd >= 0
    mult = sum(((dd % dil == 0) & (dd // dil <= window // dil)).astype(F32) for window, dil in DIL_PATTERNS)
    ok = causal & (mult > 0)
    tzd = jnp.where(ok, tz[:A_HEADS] + jnp.log(jnp.where(ok, mult, 1.0)), NEG)
    tzb = tz[A_HEADS:A_HEADS + B_HEADS]
    tzs = jnp.where(causal, tzb, NEG)
    nwin = -(-(NSA_WINDOW - 1) // BK) + 1
    tzw = jnp.where(causal & (dd <= NSA_WINDOW - 1), tzb, NEG)[:, :nwin]
    tzc = jnp.where(causal, tz[A_HEADS + B_HEADS:], NEG)
    ncmp = s // CMP_STRIDE
    na = s // CMP_STRIDE
    nvec = na + ncmp - 1
    lo = CMP_STRIDE * (ncmp - 1) + CMP_LEN - 1
    bdb = bdt[A_HEADS:A_HEADS + B_HEADS]
    gext = jnp.concatenate([jnp.full((B_HEADS, lo), NEG, F32), bdb], axis=1)
    vecs = gext[:, :CMP_STRIDE * nvec].reshape(B_HEADS, nvec, CMP_STRIDE).transpose(0, 2, 1)
    bcq = _toeplitz(vecs, na, ncmp)
    bcq = bcq.transpose(0, 2, 1, 3).reshape(B_HEADS, s, ncmp)
    bct = jnp.transpose(bcq, (0, 2, 1))
    nslc = s // SLC_LEN
    cstart = jnp.arange(ncmp) * CMP_STRIDE
    sstart = jnp.arange(nslc) * SLC_LEN
    covt = ((cstart[None, :] < sstart[:, None] + SLC_LEN)
            & (cstart[None, :] + CMP_LEN > sstart[:, None])).astype(F32)
    nblk = s // MOBA_BLK
    avg = ((jnp.arange(s)[None, :] // MOBA_BLK == jnp.arange(nblk)[:, None]).astype(F32)
           / MOBA_BLK).astype(BF16)
    return tzd, tzs, tzw, tzc, bct, covt, avg


def _pack_body(u_ref, v_ref, o_ref):
    _, te, d = u_ref.shape
    nlt = d // LANE
    ub = lax.bitcast_convert_type(u_ref[0].astype(BF16).astype(F32), jnp.uint32)
    vb = lax.bitcast_convert_type(v_ref[0].astype(BF16).astype(F32), jnp.uint32)
    word = lax.shift_right_logical(ub, jnp.uint32(16)) | vb
    for j in range(nlt):
        o_ref[pl.ds(j, te, stride=nlt), :] = word[:, j * LANE:(j + 1) * LANE]


def _pack_uv(u, v, layer, te=256):
    _, e, d = u.shape
    nlt = d // LANE
    spec = pl.BlockSpec((1, te, d), lambda i: (layer, i, 0))
    out = pl.pallas_call(
        _pack_body,
        grid=(e // te,),
        in_specs=[spec, spec],
        out_specs=pl.BlockSpec((te * nlt, LANE), lambda i: (i, 0)),
        out_shape=jax.ShapeDtypeStruct((e * nlt, LANE), jnp.uint32),
        compiler_params=_params("parallel"),
        name="pack_experts",
    )(u, v)
    return out.reshape(e, nlt, LANE)


GATE_COL0 = 3 * A_HEADS * LANE + B_HEADS * LANE + 6 * NSA_KV * LANE
GATE_COLS = 3 * B_HEADS


def _w_main_body(w_ref, o_ref):
    x = w_ref[0]
    o_ref[:, :GATE_COL0] = x[:, :GATE_COL0].astype(BF16)
    o_ref[:, GATE_COL0:] = x[:, GATE_COL0 + GATE_COLS:].astype(BF16)


def _reorder_w_in(w_in, layer, tk=256):
    _, d, cols = w_in.shape
    main = pl.pallas_call(
        _w_main_body,
        grid=(d // tk,),
        in_specs=[pl.BlockSpec((1, tk, cols), lambda i: (layer, i, 0))],
        out_specs=pl.BlockSpec((tk, MAIN_COLS), lambda i: (i, 0)),
        out_shape=jax.ShapeDtypeStruct((d, MAIN_COLS), BF16),
        compiler_params=_params("parallel"),
        name="reorder_w_in",
    )(w_in)
    per = GATE_COLS // NSA_KV
    wg = w_in[layer, :, GATE_COL0:GATE_COL0 + GATE_COLS]
    gate = jnp.concatenate([jnp.pad(wg[:, g * per:(g + 1) * per], ((0, 0), (0, LANE - per))) for g in range(NSA_KV)],
                           axis=1)
    return main, gate.astype(BF16)


def kernel(x, p, ln_mix, w_in, cmp_wk, cmp_wv, cmp_pos, out_norm, w_out, rel_bias, ln_ffn, peer_wq,
           peer_keys, peer_u, peer_v, ln_ple, ple_gate, ple_proj, ln_final):
    b, s, d = x.shape
    t = b * s
    depth = w_in.shape[0]
    assert s % BQ == 0 and BQ == BK == MOBA_BLK and d % LANE == 0
    tm = 1024 if t % 1024 == 0 else 512
    tzd, tzs, tzw, tzc, bct, covt, avg = _bias_tables(rel_bias, s)
    h = x.reshape(t, d)
    for i in range(depth):
        w_main, w_gate = _reorder_w_in(w_in, i)
        proj = _normmm(h, ln_mix[i], w_main, BF16, tm, 1024, "in_projection").reshape(b, s, MAIN_COLS)
        gates = _normmm(h, ln_mix[i], w_gate, F32, tm, w_gate.shape[1], "gate_projection")
        gates = gates.reshape(b, s, NSA_KV * LANE)
        oa = _dilated(proj, tzd)
        kcmp, vcmp = _compress(proj, cmp_wk[i], cmp_wv[i], cmp_pos[i])
        ob = _nsa(proj, gates, kcmp, vcmp, bct, covt, tzs, tzw)
        oc = _moba(proj, avg, tzc)
        h = _outproj(oa.reshape(t, -1), ob.reshape(t, -1), oc.reshape(t, -1), out_norm[i],
                     w_out[i].astype(BF16), h, tm, 1024)
        qp = _normmm(h, ln_ffn[i], peer_wq[i].astype(BF16), F32, tm, 1024, "peer_query")
        keys = peer_keys[i].reshape(PEER_HEADS * 2, PEER_NKEYS, -1)
        e_t, g_t = _peer_topk(qp, keys, 256)
        h = _peer_experts(e_t.T, g_t.T, h, ln_ffn[i], _pack_uv(peer_u, peer_v, i), LANE)
        h = _ple(h, ln_ple[i], ple_gate[i].astype(BF16), p[i].reshape(t, -1), ple_proj[i].astype(BF16),
                 tm, 1024)
    return _rmsnorm(h, ln_final, tm).reshape(b, s, d)
```

```python
import functools
import math

import numpy as np
import jax
import jax.numpy as jnp
from jax import lax
from jax.experimental import pallas as pl
from jax.experimental.pallas import tpu as pltpu

F32 = jnp.float32
BF16 = jnp.bfloat16
I32 = jnp.int32

HEAD_DIM = 128
A_HEADS, B_HEADS, C_HEADS = 6, 6, 4
NSA_KV, NSA_GROUP = 2, 3
DIL_PATTERNS = ((128, 1), (512, 4), (2048, 16))
CMP_LEN, CMP_STRIDE = 32, 16
SLC_LEN, SLC_TOPN = 64, 8
NSA_WINDOW = 512
FORCE = 1e4
MOBA_BLK, MOBA_TOPK = 256, 3
REL_BUCKETS, REL_MAX_DIST = 32, 1024
PEER_HEADS, PEER_NKEYS, PEER_TOPK = 8, 128, 16
EPS = 1e-6

LANE = 128
BQ = 256
BK = 256
NEG = -1e30
HALF_NEG = -5e29
VMEM_LIMIT = 56 * 1024 * 1024

CB_QA, CB_KA, CB_VA = 0, 6, 12
CB_QB, CB_KC, CB_VC, CB_KS, CB_VS, CB_KW, CB_VW = 18, 24, 26, 28, 30, 32, 34
CB_QC, CB_KCC, CB_VCC = 36, 40, 44
MAIN_COLS = 48 * LANE


def _nt(a, b, precision=None):
    return lax.dot_general(a, b, (((1,), (1,)), ((), ())), preferred_element_type=F32,
                           precision=precision)


def _rel_bucket(dist):
    exact = REL_BUCKETS // 2
    d = jnp.maximum(dist, 0)
    logd = jnp.log(jnp.maximum(d, 1).astype(F32) / exact)
    large = exact + (logd / math.log(REL_MAX_DIST / exact) * (REL_BUCKETS - exact)).astype(I32)
    large = jnp.clip(large, exact, REL_BUCKETS - 1)
    return jnp.where(d < exact, d, large)


def _masked_softmax(s, axis):
    valid = s > HALF_NEG
    m = jnp.max(s, axis=axis, keepdims=True)
    e = jnp.where(valid, jnp.exp(s - m), 0.0)
    z = jnp.sum(e, axis=axis, keepdims=True)
    zs = jnp.where(z > 0, z, 1.0)
    return e * (1.0 / zs)


def _params(*sem):
    return pltpu.CompilerParams(dimension_semantics=sem, vmem_limit_bytes=VMEM_LIMIT)


def _normmm_body(x_ref, g_ref, w_ref, o_ref, xn_ref):
    @pl.when(pl.program_id(1) == 0)
    def _():
        x = x_ref[...]
        ms = jnp.mean(x * x, axis=-1, keepdims=True)
        xn_ref[...] = (x * lax.rsqrt(ms + EPS) * g_ref[...]).astype(BF16)

    o_ref[...] = jnp.dot(xn_ref[...], w_ref[...], preferred_element_type=F32).astype(o_ref.dtype)


def _normmm(x, g, w, out_dtype, tm, tn, name):
    t, d = x.shape
    n = w.shape[1]
    return pl.pallas_call(
        _normmm_body,
        grid=(t // tm, n // tn),
        in_specs=[pl.BlockSpec((tm, d), lambda i, j: (i, 0)),
                  pl.BlockSpec((1, d), lambda i, j: (0, 0)),
                  pl.BlockSpec((d, tn), lambda i, j: (0, j))],
        out_specs=pl.BlockSpec((tm, tn), lambda i, j: (i, j)),
        out_shape=jax.ShapeDtypeStruct((t, n), out_dtype),
        scratch_shapes=[pltpu.VMEM((tm, d), BF16)],
        compiler_params=_params("parallel", "arbitrary"),
        name=name,
    )(x, g.reshape(1, d), w)


def _flash(qs, k_ref, v_ref, lo, hi, score_fns, scale):
    def tile(kj):
        off = pl.multiple_of(kj * BK, BK)
        return k_ref[0, pl.ds(off, BK), :], v_ref[0, pl.ds(off, BK), :]

    def update(state, blocks):
        m, l, acc = state
        m_new = m
        for s, _ in blocks:
            m_new = jnp.maximum(m_new, jnp.max(s, axis=1, keepdims=True))
        alpha = jnp.exp(m - m_new)
        l = alpha * l
        acc = alpha * acc
        for s, v in blocks:
            p = jnp.exp(s - m_new)
            l = l + jnp.sum(p, axis=1, keepdims=True)
            acc = acc + jnp.dot(p.astype(BF16), v, preferred_element_type=F32)
        return m_new, l, acc

    def step(kjs, states):
        kv = [tile(kj) for kj in kjs]
        return tuple(
            update(states[i], [(score_fns[i](kj, _nt(q, k) * scale), v) for kj, (k, v) in zip(kjs, kv)])
            for i, q in enumerate(qs))

    states = tuple((jnp.full((BQ, 1), NEG, F32), jnp.zeros((BQ, 1), F32), jnp.zeros((BQ, HEAD_DIM), F32))
                   for _ in qs)
    start = lo
    for width in ((4, 2, 1) if len(qs) == 1 else (2, 1)):
        shift = width.bit_length() - 1
        count = lax.shift_right_logical(hi - start, shift)

        def group(p, st, start=start, width=width):
            return step([start + width * p + i for i in range(width)], st)

        states = lax.fori_loop(0, count, group, states)
        start = start + count * width
    return [acc / l for _, l, acc in states]


def _dilated_body(q_ref, k_ref, v_ref, tz_ref, o_ref):
    qi = pl.program_id(2)
    scale = HEAD_DIM ** -0.5

    def score(kj, s):
        return s + tz_ref[0, qi - kj]

    o_ref[0] = _flash([q_ref[0]], k_ref, v_ref, 0, qi + 1, [score], scale)[0]


def _dilated(proj, tzd):
    b, s, _ = proj.shape
    nd = s // BK
    return pl.pallas_call(
        _dilated_body,
        grid=(b, A_HEADS, s // BQ),
        in_specs=[pl.BlockSpec((1, BQ, LANE), lambda bi, h, qi: (bi, qi, CB_QA + h)),
                  pl.BlockSpec((1, s, LANE), lambda bi, h, qi: (bi, 0, CB_KA + h)),
                  pl.BlockSpec((1, s, LANE), lambda bi, h, qi: (bi, 0, CB_VA + h)),
                  pl.BlockSpec((1, nd, BQ, BK), lambda bi, h, qi: (h, 0, 0, 0))],
        out_specs=pl.BlockSpec((1, BQ, LANE), lambda bi, h, qi: (bi, qi, h)),
        out_shape=jax.ShapeDtypeStruct((b, s, A_HEADS * LANE), F32),
        compiler_params=_params("parallel", "parallel", "arbitrary"),
        name="dilated_attention",
    )(proj, proj, proj, tzd)


def _compress_body(kc_ref, vc_ref, wk_ref, wv_ref, pos_ref, ko_ref, vo_ref, xk_ref, xv_ref):
    s = kc_ref.shape[1]
    ncmp = ko_ref.shape[2]
    xk_ref[pl.ds(0, s), :] = kc_ref[0].astype(F32)
    xv_ref[pl.ds(0, s), :] = vc_ref[0].astype(F32)
    xk_ref[pl.ds(s, CMP_LEN), :] = jnp.zeros((CMP_LEN, HEAD_DIM), F32)
    xv_ref[pl.ds(s, CMP_LEN), :] = jnp.zeros((CMP_LEN, HEAD_DIM), F32)
    acck = jnp.zeros((ncmp, HEAD_DIM), F32)
    accv = jnp.zeros((ncmp, HEAD_DIM), F32)
    for l in range(CMP_LEN):
        pos = pos_ref[pl.ds(l, 1), :]
        rk = (xk_ref[pl.ds(l, ncmp, stride=CMP_STRIDE), :] + pos).astype(BF16)
        rv = (xv_ref[pl.ds(l, ncmp, stride=CMP_STRIDE), :] + pos).astype(BF16)
        acck = acck + jnp.dot(rk, wk_ref[l], preferred_element_type=F32)
        accv = accv + jnp.dot(rv, wv_ref[l], preferred_element_type=F32)
    ko_ref[0, 0] = acck.astype(BF16)
    vo_ref[0, 0] = accv.astype(BF16)


def _compress(proj, cmp_wk, cmp_wv, cmp_pos):
    b, s, _ = proj.shape
    ncmp = s // CMP_STRIDE
    out = jax.ShapeDtypeStruct((b, NSA_KV, ncmp, HEAD_DIM), BF16)
    return pl.pallas_call(
        _compress_body,
        grid=(b, NSA_KV),
        in_specs=[pl.BlockSpec((1, s, LANE), lambda bi, g: (bi, 0, CB_KC + g)),
                  pl.BlockSpec((1, s, LANE), lambda bi, g: (bi, 0, CB_VC + g)),
                  pl.BlockSpec((CMP_LEN, HEAD_DIM, HEAD_DIM), lambda bi, g: (0, 0, 0)),
                  pl.BlockSpec((CMP_LEN, HEAD_DIM, HEAD_DIM), lambda bi, g: (0, 0, 0)),
                  pl.BlockSpec((CMP_LEN, HEAD_DIM), lambda bi, g: (0, 0))],
        out_specs=[pl.BlockSpec((1, 1, ncmp, HEAD_DIM), lambda bi, g: (bi, g, 0, 0)),
                   pl.BlockSpec((1, 1, ncmp, HEAD_DIM), lambda bi, g: (bi, g, 0, 0))],
        out_shape=[out, out],
        scratch_shapes=[pltpu.VMEM((s + CMP_LEN, HEAD_DIM), F32),
                        pltpu.VMEM((s + CMP_LEN, HEAD_DIM), F32)],
        compiler_params=_params("parallel", "parallel"),
        name="nsa_compress",
    )(proj, proj, cmp_wk.astype(BF16), cmp_wv.astype(BF16), cmp_pos)


def _nsa_body(q_ref, kcmp_ref, vcmp_ref, ks_ref, vs_ref, kw_ref, vw_ref, gt_ref, bct_ref,
              covt_ref, tzs_ref, tzw_ref, o_ref, mask_ref):
    qi = pl.program_id(2)
    scale = HEAD_DIM ** -0.5
    nslc = covt_ref.shape[0]
    nkt = mask_ref.shape[0]
    kcmp = kcmp_ref[0, 0]
    vcmp = vcmp_ref[0, 0]
    gates = jax.nn.sigmoid(gt_ref[0])

    psum = jnp.zeros((kcmp.shape[0], BQ), F32)
    for r in range(NSA_GROUP):
        q = q_ref[0, :, r * LANE:(r + 1) * LANE]
        pt = _masked_softmax(_nt(kcmp, q) * scale + bct_ref[r], axis=0)
        o_cmp = lax.dot_general(pt.astype(BF16), vcmp, (((0,), (0,)), ((), ())),
                                preferred_element_type=F32)
        o_ref[0, :, r * LANE:(r + 1) * LANE] = gates[:, 3 * r:3 * r + 1] * o_cmp
        psum = psum + pt
    imp = jnp.dot(covt_ref[...], psum, preferred_element_type=F32,
                  precision=lax.Precision.HIGHEST)
    jb = lax.broadcasted_iota(I32, (nslc, BQ), 0)
    qpos = qi * BQ + lax.broadcasted_iota(I32, (nslc, BQ), 1)
    qblk = lax.shift_right_logical(qpos, int(math.log2(SLC_LEN)))
    forced = (jb == 0) | (jb == qblk) | (jb == qblk - 1)
    imp = jnp.where(forced, FORCE, jnp.where(jb > qblk, -FORCE, imp))
    beaten = jnp.zeros((nslc, BQ), F32)
    for jp in range(nslc):
        row = imp[jp:jp + 1, :]
        wins = (row > imp) | ((row == imp) & (jp < jb))
        beaten = beaten + wins.astype(F32)
    selt = (beaten < SLC_TOPN).astype(BF16)
    eye = (lax.broadcasted_iota(I32, (BQ, BQ), 0) == lax.broadcasted_iota(I32, (BQ, BQ), 1)).astype(BF16)
    selq = _nt(eye, selt).astype(BF16)
    kpos = lax.broadcasted_iota(I32, (nslc, BK), 1)
    kblk_row = lax.broadcasted_iota(I32, (nslc, BK), 0)

    for t in range(nkt):
        expand = (lax.shift_right_logical(t * BK + kpos, int(math.log2(SLC_LEN))) == kblk_row).astype(BF16)
        mask_ref[t] = (jnp.dot(selq, expand, preferred_element_type=F32) - 1.0) * (-NEG)

    qs = [q_ref[0, :, r * LANE:(r + 1) * LANE] for r in range(NSA_GROUP)]
    sel_fns = [lambda kj, s, r=r: s + tzs_ref[r, qi - kj] + mask_ref[kj] for r in range(NSA_GROUP)]
    win_fns = [lambda kj, s, r=r: s + tzw_ref[r, qi - kj] for r in range(NSA_GROUP)]
    nwin = tzw_ref.shape[1]
    o_sel = _flash(qs, ks_ref, vs_ref, 0, qi + 1, sel_fns, scale)
    o_win = _flash(qs, kw_ref, vw_ref, jnp.maximum(qi - (nwin - 1), 0), qi + 1, win_fns, scale)
    for r in range(NSA_GROUP):
        o_ref[0, :, r * LANE:(r + 1) * LANE] += (gates[:, 3 * r + 1:3 * r + 2] * o_sel[r]
                                                 + gates[:, 3 * r + 2:3 * r + 3] * o_win[r])


def _nsa(proj, gates, kcmp, vcmp, bct, covt, tzs, tzw):
    b, s, _ = proj.shape
    nd = s // BK
    ncmp = kcmp.shape[2]
    nslc = covt.shape[0]
    nwin = tzw.shape[1]
    gw = NSA_GROUP * LANE
    kv = lambda cb: pl.BlockSpec((1, s, LANE), lambda bi, g, qi: (bi, 0, cb + g))
    return pl.pallas_call(
        _nsa_body,
        grid=(b, NSA_KV, s // BQ),
        in_specs=[pl.BlockSpec((1, BQ, gw), lambda bi, g, qi: (bi, qi, CB_QB // NSA_GROUP + g)),
                  pl.BlockSpec((1, 1, ncmp, HEAD_DIM), lambda bi, g, qi: (bi, g, 0, 0)),
                  pl.BlockSpec((1, 1, ncmp, HEAD_DIM), lambda bi, g, qi: (bi, g, 0, 0)),
                  kv(CB_KS), kv(CB_VS), kv(CB_KW), kv(CB_VW),
                  pl.BlockSpec((1, BQ, LANE), lambda bi, g, qi: (bi, qi, g)),
                  pl.BlockSpec((NSA_GROUP, ncmp, BQ), lambda bi, g, qi: (g, 0, qi)),
                  pl.BlockSpec((nslc, ncmp), lambda bi, g, qi: (0, 0)),
                  pl.BlockSpec((NSA_GROUP, nd, BQ, BK), lambda bi, g, qi: (g, 0, 0, 0)),
                  pl.BlockSpec((NSA_GROUP, nwin, BQ, BK), lambda bi, g, qi: (g, 0, 0, 0))],
        out_specs=pl.BlockSpec((1, BQ, gw), lambda bi, g, qi: (bi, qi, g)),
        out_shape=jax.ShapeDtypeStruct((b, s, B_HEADS * LANE), F32),
        scratch_shapes=[pltpu.VMEM((nd, BQ, BK), F32)],
        compiler_params=_params("parallel", "parallel", "arbitrary"),
        name="nsa_attention",
    )(proj, kcmp, vcmp, proj, proj, proj, proj, gates, bct, covt, tzs, tzw)


def _moba_body(q_ref, k_ref, v_ref, avg_ref, tz_ref, o_ref, mask_ref):
    qi = pl.program_id(2)
    scale = HEAD_DIM ** -0.5
    nblk = avg_ref.shape[0]
    q = q_ref[0]
    kmean = jnp.dot(avg_ref[...], k_ref[0], preferred_element_type=F32)
    gate = _nt(kmean, q.astype(F32), precision=lax.Precision.HIGHEST)
    nb = lax.broadcasted_iota(I32, (nblk, BQ), 0)
    past = nb < qi
    beaten = jnp.zeros((nblk, BQ), F32)
    for np_ in range(nblk):
        row = gate[np_:np_ + 1, :]
        wins = (np_ < qi) & ((row > gate) | ((row == gate) & (np_ < nb)))
        beaten = beaten + wins.astype(F32)
    selt = (past & (beaten < MOBA_TOPK)).astype(BF16)
    eye = (lax.broadcasted_iota(I32, (BQ, BQ), 0) == lax.broadcasted_iota(I32, (BQ, BQ), 1)).astype(BF16)
    selq = _nt(eye, selt)
    for n in range(nblk):
        mask_ref[n] = jnp.broadcast_to(selq[:, n:n + 1], (BQ, BK))

    def score(kj, s):
        keep = (mask_ref[kj] > 0.5) | (kj == qi)
        return jnp.where(keep, s + tz_ref[0, qi - kj], NEG)

    o_ref[0] = _flash([q], k_ref, v_ref, 0, qi + 1, [score], scale)[0]


def _moba(proj, avg, tzc):
    b, s, _ = proj.shape
    nd = s // BK
    nblk = avg.shape[0]
    return pl.pallas_call(
        _moba_body,
        grid=(b, C_HEADS, s // BQ),
        in_specs=[pl.BlockSpec((1, BQ, LANE), lambda bi, h, qi: (bi, qi, CB_QC + h)),
                  pl.BlockSpec((1, s, LANE), lambda bi, h, qi: (bi, 0, CB_KCC + h)),
                  pl.BlockSpec((1, s, LANE), lambda bi, h, qi: (bi, 0, CB_VCC + h)),
                  pl.BlockSpec((nblk, s), lambda bi, h, qi: (0, 0)),
                  pl.BlockSpec((1, nd, BQ, BK), lambda bi, h, qi: (h, 0, 0, 0))],
        out_specs=pl.BlockSpec((1, BQ, LANE), lambda bi, h, qi: (bi, qi, h)),
        out_shape=jax.ShapeDtypeStruct((b, s, C_HEADS * LANE), F32),
        scratch_shapes=[pltpu.VMEM((nblk, BQ, BK), F32)],
        compiler_params=_params("parallel", "parallel", "arbitrary"),
        name="moba_attention",
    )(proj, proj, proj, avg, tzc)


def _outproj_body(oa_ref, ob_ref, oc_ref, g_ref, w_ref, h_ref, o_ref, y_ref):
    @pl.when(pl.program_id(1) == 0)
    def _():
        c0 = 0
        for ref in (oa_ref, ob_ref, oc_ref):
            x = ref[...]
            wd = x.shape[1]
            ms = jnp.mean(x * x, axis=-1, keepdims=True)
            y_ref[:, c0:c0 + wd] = (x * lax.rsqrt(ms + EPS) * g_ref[:, c0:c0 + wd]).astype(BF16)
            c0 += wd

    o_ref[...] = h_ref[...] + jnp.dot(y_ref[...], w_ref[...], preferred_element_type=F32)


def _outproj(oa, ob, oc, g, w, h, tm, tn):
    t, d = h.shape
    row = lambda a: pl.BlockSpec((tm, a.shape[1]), lambda i, j: (i, 0))
    return pl.pallas_call(
        _outproj_body,
        grid=(t // tm, d // tn),
        in_specs=[row(oa), row(ob), row(oc),
                  pl.BlockSpec((1, d), lambda i, j: (0, 0)),
                  pl.BlockSpec((d, tn), lambda i, j: (0, j)),
                  pl.BlockSpec((tm, tn), lambda i, j: (i, j))],
        out_specs=pl.BlockSpec((tm, tn), lambda i, j: (i, j)),
        out_shape=jax.ShapeDtypeStruct((t, d), F32),
        scratch_shapes=[pltpu.VMEM((tm, d), BF16)],
        compiler_params=_params("parallel", "arbitrary"),
        name="out_projection",
    )(oa, ob, oc, g.reshape(1, d), w, h)


def _topk_rows(x, k, payload=None):
    n, tm = x.shape
    rows = lax.broadcasted_iota(I32, x.shape, 0)
    slot = lax.broadcasted_iota(I32, (k, tm), 0)
    vals = jnp.zeros((k, tm), F32)
    idxs = jnp.zeros((k, tm), I32)
    for it in range(k):
        mx = jnp.max(x, axis=0, keepdims=True)
        idx = jnp.min(jnp.where(x == mx, rows, n), axis=0, keepdims=True)
        hit = rows == idx
        if payload is not None:
            idx = jnp.sum(jnp.where(hit, payload, 0), axis=0, keepdims=True)
        vals = jnp.where(slot == it, mx, vals)
        idxs = jnp.where(slot == it, idx, idxs)
        x = jnp.where(hit, -jnp.inf, x)
    return vals, idxs


def _peer_topk_body(h_ref, ln_ref, wq_ref, keys_ref, e_ref, g_ref):
    x = h_ref[...]
    ms = jnp.mean(x * x, axis=-1, keepdims=True)
    xn = (x * lax.rsqrt(ms + EPS) * ln_ref[...]).astype(BF16)
    dq = keys_ref.shape[2]
    for h in range(PEER_HEADS):
        qh = jnp.dot(xn, wq_ref[:, 2 * h * dq:2 * (h + 1) * dq], preferred_element_type=F32)
        tops = []
        for c in range(2):
            sc = _nt(keys_ref[2 * h + c], qh[:, c * dq:(c + 1) * dq])
            tops.append(_topk_rows(sc, PEER_TOPK))
        (s0, i0), (s1, i1) = tops
        half = PEER_TOPK // 2
        sub = lax.broadcasted_iota(I32, (half, s0.shape[1]), 0)
        pieces = [s0[0:1] + s1]
        pieces_i = [i0[0:1] * PEER_NKEYS + i1]
        for a in range(1, half):
            keep = sub < PEER_TOPK // (a + 1)
            pieces.append(jnp.where(keep, s0[a:a + 1] + s1[0:half], -jnp.inf))
            pieces_i.append(i0[a:a + 1] * PEER_NKEYS + i1[0:half])
        pieces.append(s0[half:] + s1[0:1])
        pieces_i.append(i0[half:] * PEER_NKEYS + i1[0:1])
        cand = jnp.concatenate(pieces, axis=0)
        cand_i = jnp.concatenate(pieces_i, axis=0)
        bs, be = _topk_rows(cand, PEER_TOPK, payload=cand_i)
        e = jnp.exp(bs - jnp.max(bs, axis=0, keepdims=True))
        g_ref[h * PEER_TOPK:(h + 1) * PEER_TOPK, :] = e / jnp.sum(e, axis=0, keepdims=True)
        e_ref[h * PEER_TOPK:(h + 1) * PEER_TOPK, :] = be


def _peer_topk(h, ln, wq, keys, tm):
    t, d = h.shape
    hk = PEER_HEADS * PEER_TOPK
    assert wq.shape == (d, keys.shape[0] * keys.shape[2]) and keys.shape[2] == LANE
    return pl.pallas_call(
        _peer_topk_body,
        grid=(t // tm,),
        in_specs=[pl.BlockSpec((tm, d), lambda i: (i, 0)),
                  pl.BlockSpec((1, d), lambda i: (0, 0)),
                  pl.BlockSpec(wq.shape, lambda i: (0, 0)),
                  pl.BlockSpec(keys.shape, lambda i: (0, 0, 0))],
        out_specs=[pl.BlockSpec((hk, tm), lambda i: (0, i)),
                   pl.BlockSpec((hk, tm), lambda i: (0, i))],
        out_shape=[jax.ShapeDtypeStruct((hk, t), I32), jax.ShapeDtypeStruct((hk, t), F32)],
        compiler_params=_params("parallel"),
        name="peer_topk",
    )(h, ln.reshape(1, d), wq, keys)


PEER_NBUF = 16
PEER_AHEAD = 14


def _gelu(x):
    return 0.5 * x * (1.0 + lax.erf(x * (2.0 ** -0.5)))


def _hi_lo(x):
    hi = x.astype(BF16)
    lo = (x - hi.astype(F32)).astype(BF16)
    return jnp.concatenate([hi, lo], axis=1)


def _peer_expert_body(ids0_ref, idsn_ref, h_ref, ln_ref, g_ref, uvw_ref, o_ref, xn_ref, acc_ref, lhs_ref,
                      buf_ref, sem_ref):
    tb, d = h_ref.shape
    hk = idsn_ref.shape[1]
    nrg = hk // 8
    nlt = d // LANE
    assert nrg == nlt and hk % (2 * nlt) == 0
    step = pl.program_id(0)
    x = h_ref[...]
    ms = jnp.mean(x * x, axis=-1, keepdims=True)
    xn_ref[...] = x * lax.rsqrt(ms + EPS) * ln_ref[...]

    def issue(ids_ref, row, slot, k0, k1):
        for k in range(k0, k1):
            e = ids_ref[row, k]
            pltpu.make_async_copy(uvw_ref.at[e], buf_ref.at[slot, :, k, :],
                                  sem_ref.at[slot]).start(priority=k % 2)

    def wait_all(slot):
        pltpu.make_async_copy(buf_ref.at[slot], buf_ref.at[slot], sem_ref.at[slot]).wait()

    def x_tiles(t):
        xrow = xn_ref[pl.ds(t, 1), :]
        return [jnp.broadcast_to(xrow[:, j * LANE:(j + 1) * LANE], (8, LANE)) for j in range(nlt)]

    def dots(xb, slot, r):
        a = None
        for j in range(nlt):
            w = buf_ref[slot, j, r * 8:(r + 1) * 8, :]
            pr = lax.bitcast_convert_type(lax.shift_left(w, jnp.uint32(16)), F32) * xb[j]
            a = pr if a is None else a + pr
        return a

    def coefficients(t, slot):
        act = jnp.sum(acc_ref[slot % 2].T, axis=0, keepdims=True)
        coef = jnp.broadcast_to(_gelu(act) * g_ref[pl.ds(t, 1), :], (8, hk))
        chi = coef.astype(BF16).astype(F32)
        return jnp.concatenate([chi, coef - chi], axis=0).astype(BF16)

    def weighted(lhs, slot, j):
        w = buf_ref[slot, j]
        v = lax.bitcast_convert_type(w & jnp.uint32(0xFFFF0000), F32).astype(BF16)
        yj = jnp.dot(lhs, v, preferred_element_type=F32)
        return yj[0:1, :] + yj[8:9, :]

    def flush(done):
        ya, yb, tp = done
        y = jnp.concatenate([ya[j:j + 1, :] for j in range(8)] + [yb[j:j + 1, :] for j in range(8)], axis=1)
        o_ref[pl.ds(tp, 1), :] = h_ref[pl.ds(tp, 1), :] + y

    def turn(t, slot, stages, done):
        tgt = (slot + PEER_AHEAD) % PEER_NBUF
        far = (slot + 2) % PEER_NBUF
        per = hk // (2 * nlt)
        if stages >= 3:
            wait_all(far)
            xb = x_tiles(t + 2)
        lhs = lhs_ref[slot % 2]
        ys, parts = [], []
        for c in range(nlt):
            if stages >= 3:
                parts.append(dots(xb, far, c))
            issue(idsn_ref, t, tgt, 2 * c * per, (2 * c + 1) * per)
            if 2 * c < nlt:
                ys.append(weighted(lhs, slot, 2 * c))
                ys.append(weighted(lhs, slot, 2 * c + 1))
            if 2 * c == nlt and stages >= 2:
                lhs_next = coefficients(t + 1, slot + 1)
            issue(idsn_ref, t, tgt, (2 * c + 1) * per, (2 * c + 2) * per)
        if stages >= 3:
            acc_ref[slot % 2] = jnp.concatenate(parts, axis=0)
        if stages >= 2:
            lhs_ref[(slot + 1) % 2] = lhs_next
        flush(done)
        return (jnp.concatenate(ys[:8], axis=0), jnp.concatenate(ys[8:], axis=0), t)

    @pl.when(step == 0)
    def _():
        for t0 in range(PEER_AHEAD):
            issue(ids0_ref, t0, t0, 0, hk)

    for t0 in range(2):
        wait_all(t0)
        xb0 = x_tiles(t0)
        acc_ref[t0] = jnp.concatenate([dots(xb0, t0, r) for r in range(nrg)], axis=0)
    lhs_ref[0] = coefficients(0, 0)

    def ring(i, stacks):
        done = stacks + (jnp.maximum(i * PEER_NBUF - 1, 0),)
        for slot in range(PEER_NBUF):
            done = turn(i * PEER_NBUF + slot, slot, 3, done)
        return done[:2]

    zero = jnp.zeros((8, LANE), F32)
    nring = (tb - 2) // PEER_NBUF
    done = lax.fori_loop(0, nring, ring, (zero, zero)) + (nring * PEER_NBUF - 1,)
    for t in range(nring * PEER_NBUF, tb - 2):
        done = turn(t, t % PEER_NBUF, 3, done)
    done = turn(tb - 2, (tb - 2) % PEER_NBUF, 2, done)
    done = turn(tb - 1, (tb - 1) % PEER_NBUF, 1, done)
    flush(done)

    @pl.when(step == pl.num_programs(0) - 1)
    def _():
        for t0 in range(PEER_AHEAD):
            wait_all((tb + t0) % PEER_NBUF)


def _peer_experts(ids, gates, h, ln, slabs, tb):
    t, d = h.shape
    hk = ids.shape[1]
    assert hk == LANE and tb % PEER_NBUF == 0 and PEER_NBUF % 2 == 0 and d % LANE == 0
    assert PEER_AHEAD <= PEER_NBUF - 2
    ids_next = jnp.concatenate([ids[PEER_AHEAD:], ids[:PEER_AHEAD]], axis=0)
    assert slabs.shape[1:] == (d // LANE, LANE)
    return pl.pallas_call(
        _peer_expert_body,
        grid=(t // tb,),
        in_specs=[pl.BlockSpec((PEER_NBUF, hk), lambda i: (0, 0), memory_space=pltpu.SMEM),
                  pl.BlockSpec((tb, hk), lambda i: (i, 0), memory_space=pltpu.SMEM),
                  pl.BlockSpec((tb, d), lambda i: (i, 0)),
                  pl.BlockSpec((1, d), lambda i: (0, 0)),
                  pl.BlockSpec((tb, hk), lambda i: (i, 0)),
                  pl.BlockSpec(memory_space=pl.ANY)],
        out_specs=pl.BlockSpec((tb, d), lambda i: (i, 0)),
        out_shape=jax.ShapeDtypeStruct((t, d), F32),
        scratch_shapes=[pltpu.VMEM((tb, d), F32),
                        pltpu.VMEM((2, hk, LANE), F32),
                        pltpu.VMEM((2, 16, hk), BF16),
                        pltpu.VMEM((PEER_NBUF, d // LANE, hk, LANE), jnp.uint32),
                        pltpu.SemaphoreType.DMA((PEER_NBUF,))],
        compiler_params=_params("arbitrary"),
        name="peer_experts",
    )(ids[:PEER_NBUF], ids_next, h, ln.reshape(1, d), gates, slabs)


def _ple_body(x_ref, g_ref, wg_ref, p_ref, wp_ref, h_ref, o_ref, xn_ref):
    @pl.when(pl.program_id(1) == 0)
    def _():
        x = x_ref[...]
        ms = jnp.mean(x * x, axis=-1, keepdims=True)
        xn_ref[...] = (x * lax.rsqrt(ms + EPS) * g_ref[...]).astype(BF16)

    z = jnp.dot(xn_ref[...], wg_ref[...], preferred_element_type=F32)
    pp = jnp.dot(p_ref[...].astype(BF16), wp_ref[...], preferred_element_type=F32)
    o_ref[...] = h_ref[...] + jax.nn.sigmoid(z) * pp


def _ple(h, g, wg, p, wp, tm, tn):
    t, d = h.shape
    pd = p.shape[1]
    return pl.pallas_call(
        _ple_body,
        grid=(t // tm, d // tn),
        in_specs=[pl.BlockSpec((tm, d), lambda i, j: (i, 0)),
                  pl.BlockSpec((1, d), lambda i, j: (0, 0)),
                  pl.BlockSpec((d, tn), lambda i, j: (0, j)),
                  pl.BlockSpec((tm, pd), lambda i, j: (i, 0)),
                  pl.BlockSpec((pd, tn), lambda i, j: (0, j)),
                  pl.BlockSpec((tm, tn), lambda i, j: (i, j))],
        out_specs=pl.BlockSpec((tm, tn), lambda i, j: (i, j)),
        out_shape=jax.ShapeDtypeStruct((t, d), F32),
        scratch_shapes=[pltpu.VMEM((tm, d), BF16)],
        compiler_params=_params("parallel", "arbitrary"),
        name="ple_gate",
    )(h, g.reshape(1, d), wg, p, wp, h)


def _rmsnorm_body(x_ref, g_ref, o_ref):
    x = x_ref[...]
    ms = jnp.mean(x * x, axis=-1, keepdims=True)
    o_ref[...] = x * lax.rsqrt(ms + EPS) * g_ref[...]


def _rmsnorm(x, g, tm):
    t, d = x.shape
    return pl.pallas_call(
        _rmsnorm_body,
        grid=(t // tm,),
        in_specs=[pl.BlockSpec((tm, d), lambda i: (i, 0)), pl.BlockSpec((1, d), lambda i: (0, 0))],
        out_specs=pl.BlockSpec((tm, d), lambda i: (i, 0)),
        out_shape=jax.ShapeDtypeStruct((t, d), F32),
        compiler_params=_params("parallel"),
        name="final_rmsnorm",
    )(x, g.reshape(1, d))


def _toeplitz(vec, nq, nk):
    assert nq == nk
    lead = vec.shape[:-1]
    g = int(np.prod(lead))
    w = jnp.concatenate([jnp.zeros(lead + (1,), vec.dtype), vec[..., ::-1]], axis=-1).reshape(g, 1, 2 * nk)

    def body(w_ref, o_ref):
        rows = jnp.broadcast_to(w_ref[0], (nq, 2 * nk))
        o_ref[0] = pltpu.roll(rows, 0, 1, stride=1, stride_axis=0)[:, nk:]

    out = pl.pallas_call(
        body,
        grid=(g,),
        in_specs=[pl.BlockSpec((1, 1, 2 * nk), lambda i: (i, 0, 0))],
        out_specs=pl.BlockSpec((1, nq, nk), lambda i: (i, 0, 0)),
        out_shape=jax.ShapeDtypeStruct((g, nq, nk), vec.dtype),
        compiler_params=_params("parallel"),
        name="toeplitz_tiles",
    )(w)
    return out.reshape(lead + (nq, nk))


def _bias_tables(rel_bias, s):
    nd = s // BK
    heads = rel_bias.shape[1]
    bdt = rel_bias[_rel_bucket(jnp.arange(s))].astype(F32).T
    dd = (jnp.arange(nd)[:, None, None] * BK + jnp.arange(BQ)[None, :, None]
          - jnp.arange(BK)[None, None, :])
    ext = jnp.pad(bdt, ((0, 0), (BK - 1, 0)))
    segs = jnp.stack([ext[:, dl * BK:dl * BK + BQ + BK - 1] for dl in range(nd)], axis=1)
    tz = _toeplitz(segs, BQ, BK)
    causal = dd >= 0
    mult = sum(((dd % dil == 0) & (dd // dil <= window // dil)).astype(F32) for window, dil in DIL_PATTERNS)
    ok = causal & (mult > 0)
    tzd = jnp.where(ok, tz[:A_HEADS] + jnp.log(jnp.where(ok, mult, 1.0)), NEG)
    tzb = tz[A_HEADS:A_HEADS + B_HEADS]
    tzs = jnp.where(causal, tzb, NEG)
    nwin = -(-(NSA_WINDOW - 1) // BK) + 1
    tzw = jnp.where(causal & (dd <= NSA_WINDOW - 1), tzb, NEG)[:, :nwin]
    tzc = jnp.where(causal, tz[A_HEADS + B_HEADS:], NEG)
    ncmp = s // CMP_STRIDE
    na = s // CMP_STRIDE
    nvec = na + ncmp - 1
    lo = CMP_STRIDE * (ncmp - 1) + CMP_LEN - 1
    bdb = bdt[A_HEADS:A_HEADS + B_HEADS]
    gext = jnp.concatenate([jnp.full((B_HEADS, lo), NEG, F32), bdb], axis=1)
    vecs = gext[:, :CMP_STRIDE * nvec].reshape(B_HEADS, nvec, CMP_STRIDE).transpose(0, 2, 1)
    bcq = _toeplitz(vecs, na, ncmp)
    bcq = bcq.transpose(0, 2, 1, 3).reshape(B_HEADS, s, ncmp)
    bct = jnp.transpose(bcq, (0, 2, 1))
    nslc = s // SLC_LEN
    cstart = jnp.arange(ncmp) * CMP_STRIDE
    sstart = jnp.arange(nslc) * SLC_LEN
    covt = ((cstart[None, :] < sstart[:, None] + SLC_LEN)
            & (cstart[None, :] + CMP_LEN > sstart[:, None])).astype(F32)
    nblk = s // MOBA_BLK
    avg = ((jnp.arange(s)[None, :] // MOBA_BLK == jnp.arange(nblk)[:, None]).astype(F32)
           / MOBA_BLK).astype(BF16)
    return tzd, tzs, tzw, tzc, bct, covt, avg


def _pack_body(u_ref, v_ref, o_ref):
    _, te, d = u_ref.shape
    nlt = d // LANE
    ub = lax.bitcast_convert_type(u_ref[0].astype(BF16).astype(F32), jnp.uint32)
    vb = lax.bitcast_convert_type(v_ref[0].astype(BF16).astype(F32), jnp.uint32)
    word = lax.shift_right_logical(ub, jnp.uint32(16)) | vb
    for j in range(nlt):
        o_ref[pl.ds(j, te, stride=nlt), :] = word[:, j * LANE:(j + 1) * LANE]


def _pack_uv(u, v, layer, te=256):
    _, e, d = u.shape
    nlt = d // LANE
    spec = pl.BlockSpec((1, te, d), lambda i: (layer, i, 0))
    out = pl.pallas_call(
        _pack_body,
        grid=(e // te,),
        in_specs=[spec, spec],
        out_specs=pl.BlockSpec((te * nlt, LANE), lambda i: (i, 0)),
        out_shape=jax.ShapeDtypeStruct((e * nlt, LANE), jnp.uint32),
        compiler_params=_params("parallel"),
        name="pack_experts",
    )(u, v)
    return out.reshape(e, nlt, LANE)


GATE_COL0 = 3 * A_HEADS * LANE + B_HEADS * LANE + 6 * NSA_KV * LANE
GATE_COLS = 3 * B_HEADS


def _w_main_body(w_ref, o_ref):
    x = w_ref[0]
    o_ref[:, :GATE_COL0] = x[:, :GATE_COL0].astype(BF16)
    o_ref[:, GATE_COL0:] = x[:, GATE_COL0 + GATE_COLS:].astype(BF16)


def _reorder_w_in(w_in, layer, tk=256):
    _, d, cols = w_in.shape
    main = pl.pallas_call(
        _w_main_body,
        grid=(d // tk,),
        in_specs=[pl.BlockSpec((1, tk, cols), lambda i: (layer, i, 0))],
        out_specs=pl.BlockSpec((tk, MAIN_COLS), lambda i: (i, 0)),
        out_shape=jax.ShapeDtypeStruct((d, MAIN_COLS), BF16),
        compiler_params=_params("parallel"),
        name="reorder_w_in",
    )(w_in)
    per = GATE_COLS // NSA_KV
    wg = w_in[layer, :, GATE_COL0:GATE_COL0 + GATE_COLS]
    gate = jnp.concatenate([jnp.pad(wg[:, g * per:(g + 1) * per], ((0, 0), (0, LANE - per))) for g in range(NSA_KV)],
                           axis=1)
    return main, gate.astype(BF16)


def kernel(x, p, ln_mix, w_in, cmp_wk, cmp_wv, cmp_pos, out_norm, w_out, rel_bias, ln_ffn, peer_wq,
           peer_keys, peer_u, peer_v, ln_ple, ple_gate, ple_proj, ln_final):
    b, s, d = x.shape
    t = b * s
    depth = w_in.shape[0]
    assert s % BQ == 0 and BQ == BK == MOBA_BLK and d % LANE == 0
    tm = 1024 if t % 1024 == 0 else 512
    tzd, tzs, tzw, tzc, bct, covt, avg = _bias_tables(rel_bias, s)
    h = x.reshape(t, d)
    for i in range(depth):
        w_main, w_gate = _reorder_w_in(w_in, i)
        proj = _normmm(h, ln_mix[i], w_main, BF16, tm, 1024, "in_projection").reshape(b, s, MAIN_COLS)
        gates = _normmm(h, ln_mix[i], w_gate, F32, tm, w_gate.shape[1], "gate_projection")
        gates = gates.reshape(b, s, NSA_KV * LANE)
        oa = _dilated(proj, tzd)
        kcmp, vcmp = _compress(proj, cmp_wk[i], cmp_wv[i], cmp_pos[i])
        ob = _nsa(proj, gates, kcmp, vcmp, bct, covt, tzs, tzw)
        oc = _moba(proj, avg, tzc)
        h = _outproj(oa.reshape(t, -1), ob.reshape(t, -1), oc.reshape(t, -1), out_norm[i],
                     w_out[i].astype(BF16), h, tm, 1024)
        keys = peer_keys[i].reshape(PEER_HEADS * 2, PEER_NKEYS, -1)
        e_t, g_t = _peer_topk(h, ln_ffn[i], peer_wq[i].astype(BF16), keys, 256)
        h = _peer_experts(e_t.T, g_t.T, h, ln_ffn[i], _pack_uv(peer_u, peer_v, i), LANE)
        h = _ple(h, ln_ple[i], ple_gate[i].astype(BF16), p[i].reshape(t, -1), ple_proj[i].astype(BF16),
                 tm, 1024)
    return _rmsnorm(h, ln_final, tm).reshape(b, s, d)
```

```python
import functools
import math

import numpy as np
import jax
import jax.numpy as jnp
from jax import lax
from jax.experimental import pallas as pl
from jax.experimental.pallas import tpu as pltpu

F32 = jnp.float32
BF16 = jnp.bfloat16
I32 = jnp.int32

HEAD_DIM = 128
A_HEADS, B_HEADS, C_HEADS = 6, 6, 4
NSA_KV, NSA_GROUP = 2, 3
DIL_PATTERNS = ((128, 1), (512, 4), (2048, 16))
CMP_LEN, CMP_STRIDE = 32, 16
SLC_LEN, SLC_TOPN = 64, 8
NSA_WINDOW = 512
FORCE = 1e4
MOBA_BLK, MOBA_TOPK = 256, 3
REL_BUCKETS, REL_MAX_DIST = 32, 1024
PEER_HEADS, PEER_NKEYS, PEER_TOPK = 8, 128, 16
EPS = 1e-6

LANE = 128
BQ = 256
BK = 256
NEG = -1e30
HALF_NEG = -5e29
VMEM_LIMIT = 56 * 1024 * 1024

CB_QA, CB_KA, CB_VA = 0, 6, 12
CB_QB, CB_KC, CB_VC, CB_KS, CB_VS, CB_KW, CB_VW = 18, 24, 26, 28, 30, 32, 34
CB_QC, CB_KCC, CB_VCC = 36, 40, 44
MAIN_COLS = 48 * LANE


def _nt(a, b, precision=None):
    return lax.dot_general(a, b, (((1,), (1,)), ((), ())), preferred_element_type=F32,
                           precision=precision)


def _rel_bucket(dist):
    exact = REL_BUCKETS // 2
    d = jnp.maximum(dist, 0)
    logd = jnp.log(jnp.maximum(d, 1).astype(F32) / exact)
    large = exact + (logd / math.log(REL_MAX_DIST / exact) * (REL_BUCKETS - exact)).astype(I32)
    large = jnp.clip(large, exact, REL_BUCKETS - 1)
    return jnp.where(d < exact, d, large)


def _masked_softmax(s, axis):
    valid = s > HALF_NEG
    m = jnp.max(s, axis=axis, keepdims=True)
    e = jnp.where(valid, jnp.exp(s - m), 0.0)
    z = jnp.sum(e, axis=axis, keepdims=True)
    zs = jnp.where(z > 0, z, 1.0)
    return e * (1.0 / zs)


def _params(*sem):
    return pltpu.CompilerParams(dimension_semantics=sem, vmem_limit_bytes=VMEM_LIMIT)


def _normmm_body(x_ref, g_ref, w_ref, o_ref, xn_ref):
    @pl.when(pl.program_id(1) == 0)
    def _():
        x = x_ref[...]
        ms = jnp.mean(x * x, axis=-1, keepdims=True)
        xn_ref[...] = (x * lax.rsqrt(ms + EPS) * g_ref[...]).astype(BF16)

    o_ref[...] = jnp.dot(xn_ref[...], w_ref[...], preferred_element_type=F32).astype(o_ref.dtype)


def _normmm(x, g, w, out_dtype, tm, tn, name):
    t, d = x.shape
    n = w.shape[1]
    return pl.pallas_call(
        _normmm_body,
        grid=(t // tm, n // tn),
        in_specs=[pl.BlockSpec((tm, d), lambda i, j: (i, 0)),
                  pl.BlockSpec((1, d), lambda i, j: (0, 0)),
                  pl.BlockSpec((d, tn), lambda i, j: (0, j))],
        out_specs=pl.BlockSpec((tm, tn), lambda i, j: (i, j)),
        out_shape=jax.ShapeDtypeStruct((t, n), out_dtype),
        scratch_shapes=[pltpu.VMEM((tm, d), BF16)],
        compiler_params=_params("parallel", "arbitrary"),
        name=name,
    )(x, g.reshape(1, d), w)


def _flash(qs, k_ref, v_ref, lo, hi, score_fns, scale):
    def tile(kj):
        off = pl.multiple_of(kj * BK, BK)
        return k_ref[0, pl.ds(off, BK), :], v_ref[0, pl.ds(off, BK), :]

    def update(state, blocks):
        m, l, acc = state
        m_new = m
        for s, _ in blocks:
            m_new = jnp.maximum(m_new, jnp.max(s, axis=1, keepdims=True))
        alpha = jnp.exp(m - m_new)
        l = alpha * l
        acc = alpha * acc
        for s, v in blocks:
            p = jnp.exp(s - m_new)
            l = l + jnp.sum(p, axis=1, keepdims=True)
            acc = acc + jnp.dot(p.astype(BF16), v, preferred_element_type=F32)
        return m_new, l, acc

    def step(kjs, states):
        kv = [tile(kj) for kj in kjs]
        return tuple(
            update(states[i], [(score_fns[i](kj, _nt(q, k) * scale), v) for kj, (k, v) in zip(kjs, kv)])
            for i, q in enumerate(qs))

    states = tuple((jnp.full((BQ, 1), NEG, F32), jnp.zeros((BQ, 1), F32), jnp.zeros((BQ, HEAD_DIM), F32))
                   for _ in qs)
    start = lo
    for width in ((4, 2, 1) if len(qs) == 1 else (2, 1)):
        shift = width.bit_length() - 1
        count = lax.shift_right_logical(hi - start, shift)

        def group(p, st, start=start, width=width):
            return step([start + width * p + i for i in range(width)], st)

        states = lax.fori_loop(0, count, group, states)
        start = start + count * width
    return [acc / l for _, l, acc in states]


def _dilated_body(q_ref, k_ref, v_ref, tz_ref, o_ref):
    qi = pl.program_id(2)
    scale = HEAD_DIM ** -0.5

    def score(kj, s):
        return s + tz_ref[0, qi - kj]

    o_ref[0] = _flash([q_ref[0]], k_ref, v_ref, 0, qi + 1, [score], scale)[0]


def _dilated(proj, tzd):
    b, s, _ = proj.shape
    nd = s // BK
    return pl.pallas_call(
        _dilated_body,
        grid=(b, A_HEADS, s // BQ),
        in_specs=[pl.BlockSpec((1, BQ, LANE), lambda bi, h, qi: (bi, qi, CB_QA + h)),
                  pl.BlockSpec((1, s, LANE), lambda bi, h, qi: (bi, 0, CB_KA + h)),
                  pl.BlockSpec((1, s, LANE), lambda bi, h, qi: (bi, 0, CB_VA + h)),
                  pl.BlockSpec((1, nd, BQ, BK), lambda bi, h, qi: (h, 0, 0, 0))],
        out_specs=pl.BlockSpec((1, BQ, LANE), lambda bi, h, qi: (bi, qi, h)),
        out_shape=jax.ShapeDtypeStruct((b, s, A_HEADS * LANE), F32),
        compiler_params=_params("parallel", "parallel", "arbitrary"),
        name="dilated_attention",
    )(proj, proj, proj, tzd)


def _compress_body(kc_ref, vc_ref, wk_ref, wv_ref, pos_ref, ko_ref, vo_ref, xk_ref, xv_ref):
    s = kc_ref.shape[1]
    ncmp = ko_ref.shape[2]
    xk_ref[pl.ds(0, s), :] = kc_ref[0].astype(F32)
    xv_ref[pl.ds(0, s), :] = vc_ref[0].astype(F32)
    xk_ref[pl.ds(s, CMP_LEN), :] = jnp.zeros((CMP_LEN, HEAD_DIM), F32)
    xv_ref[pl.ds(s, CMP_LEN), :] = jnp.zeros((CMP_LEN, HEAD_DIM), F32)
    acck = jnp.zeros((ncmp, HEAD_DIM), F32)
    accv = jnp.zeros((ncmp, HEAD_DIM), F32)
    for l in range(CMP_LEN):
        pos = pos_ref[pl.ds(l, 1), :]
        rk = (xk_ref[pl.ds(l, ncmp, stride=CMP_STRIDE), :] + pos).astype(BF16)
        rv = (xv_ref[pl.ds(l, ncmp, stride=CMP_STRIDE), :] + pos).astype(BF16)
        acck = acck + jnp.dot(rk, wk_ref[l], preferred_element_type=F32)
        accv = accv + jnp.dot(rv, wv_ref[l], preferred_element_type=F32)
    ko_ref[0, 0] = acck.astype(BF16)
    vo_ref[0, 0] = accv.astype(BF16)


def _compress(proj, cmp_wk, cmp_wv, cmp_pos):
    b, s, _ = proj.shape
    ncmp = s // CMP_STRIDE
    out = jax.ShapeDtypeStruct((b, NSA_KV, ncmp, HEAD_DIM), BF16)
    return pl.pallas_call(
        _compress_body,
        grid=(b, NSA_KV),
        in_specs=[pl.BlockSpec((1, s, LANE), lambda bi, g: (bi, 0, CB_KC + g)),
                  pl.BlockSpec((1, s, LANE), lambda bi, g: (bi, 0, CB_VC + g)),
                  pl.BlockSpec((CMP_LEN, HEAD_DIM, HEAD_DIM), lambda bi, g: (0, 0, 0)),
                  pl.BlockSpec((CMP_LEN, HEAD_DIM, HEAD_DIM), lambda bi, g: (0, 0, 0)),
                  pl.BlockSpec((CMP_LEN, HEAD_DIM), lambda bi, g: (0, 0))],
        out_specs=[pl.BlockSpec((1, 1, ncmp, HEAD_DIM), lambda bi, g: (bi, g, 0, 0)),
                   pl.BlockSpec((1, 1, ncmp, HEAD_DIM), lambda bi, g: (bi, g, 0, 0))],
        out_shape=[out, out],
        scratch_shapes=[pltpu.VMEM((s + CMP_LEN, HEAD_DIM), F32),
                        pltpu.VMEM((s + CMP_LEN, HEAD_DIM), F32)],
        compiler_params=_params("parallel", "parallel"),
        name="nsa_compress",
    )(proj, proj, cmp_wk.astype(BF16), cmp_wv.astype(BF16), cmp_pos)


def _nsa_body(q_ref, kcmp_ref, vcmp_ref, ks_ref, vs_ref, kw_ref, vw_ref, gt_ref, bct_ref,
              covt_ref, tzs_ref, tzw_ref, o_ref, mask_ref):
    qi = pl.program_id(2)
    scale = HEAD_DIM ** -0.5
    nslc = covt_ref.shape[0]
    nkt = mask_ref.shape[0]
    kcmp = kcmp_ref[0, 0]
    vcmp = vcmp_ref[0, 0]
    gates = jax.nn.sigmoid(gt_ref[0])

    psum = jnp.zeros((kcmp.shape[0], BQ), F32)
    for r in range(NSA_GROUP):
        q = q_ref[0, :, r * LANE:(r + 1) * LANE]
        pt = _masked_softmax(_nt(kcmp, q) * scale + bct_ref[r], axis=0)
        o_cmp = lax.dot_general(pt.astype(BF16), vcmp, (((0,), (0,)), ((), ())),
                                preferred_element_type=F32)
        o_ref[0, :, r * LANE:(r + 1) * LANE] = gates[:, 3 * r:3 * r + 1] * o_cmp
        psum = psum + pt
    imp = jnp.dot(covt_ref[...], psum, preferred_element_type=F32,
                  precision=lax.Precision.HIGHEST)
    jb = lax.broadcasted_iota(I32, (nslc, BQ), 0)
    qpos = qi * BQ + lax.broadcasted_iota(I32, (nslc, BQ), 1)
    qblk = lax.shift_right_logical(qpos, int(math.log2(SLC_LEN)))
    forced = (jb == 0) | (jb == qblk) | (jb == qblk - 1)
    imp = jnp.where(forced, FORCE, jnp.where(jb > qblk, -FORCE, imp))
    beaten = jnp.zeros((nslc, BQ), F32)
    for jp in range(nslc):
        row = imp[jp:jp + 1, :]
        wins = (row > imp) | ((row == imp) & (jp < jb))
        beaten = beaten + wins.astype(F32)
    selt = (beaten < SLC_TOPN).astype(BF16)
    eye = (lax.broadcasted_iota(I32, (BQ, BQ), 0) == lax.broadcasted_iota(I32, (BQ, BQ), 1)).astype(BF16)
    selq = _nt(eye, selt).astype(BF16)
    kpos = lax.broadcasted_iota(I32, (nslc, BK), 1)
    kblk_row = lax.broadcasted_iota(I32, (nslc, BK), 0)

    for t in range(nkt):
        expand = (lax.shift_right_logical(t * BK + kpos, int(math.log2(SLC_LEN))) == kblk_row).astype(BF16)
        mask_ref[t] = (jnp.dot(selq, expand, preferred_element_type=F32) - 1.0) * (-NEG)

    qs = [q_ref[0, :, r * LANE:(r + 1) * LANE] for r in range(NSA_GROUP)]
    sel_fns = [lambda kj, s, r=r: s + tzs_ref[r, qi - kj] + mask_ref[kj] for r in range(NSA_GROUP)]
    win_fns = [lambda kj, s, r=r: s + tzw_ref[r, qi - kj] for r in range(NSA_GROUP)]
    nwin = tzw_ref.shape[1]
    o_sel = _flash(qs, ks_ref, vs_ref, 0, qi + 1, sel_fns, scale)
    o_win = _flash(qs, kw_ref, vw_ref, jnp.maximum(qi - (nwin - 1), 0), qi + 1, win_fns, scale)
    for r in range(NSA_GROUP):
        o_ref[0, :, r * LANE:(r + 1) * LANE] += (gates[:, 3 * r + 1:3 * r + 2] * o_sel[r]
                                                 + gates[:, 3 * r + 2:3 * r + 3] * o_win[r])


def _nsa(proj, gates, kcmp, vcmp, bct, covt, tzs, tzw):
    b, s, _ = proj.shape
    nd = s // BK
    ncmp = kcmp.shape[2]
    nslc = covt.shape[0]
    nwin = tzw.shape[1]
    gw = NSA_GROUP * LANE
    kv = lambda cb: pl.BlockSpec((1, s, LANE), lambda bi, g, qi: (bi, 0, cb + g))
    return pl.pallas_call(
        _nsa_body,
        grid=(b, NSA_KV, s // BQ),
        in_specs=[pl.BlockSpec((1, BQ, gw), lambda bi, g, qi: (bi, qi, CB_QB // NSA_GROUP + g)),
                  pl.BlockSpec((1, 1, ncmp, HEAD_DIM), lambda bi, g, qi: (bi, g, 0, 0)),
                  pl.BlockSpec((1, 1, ncmp, HEAD_DIM), lambda bi, g, qi: (bi, g, 0, 0)),
                  kv(CB_KS), kv(CB_VS), kv(CB_KW), kv(CB_VW),
                  pl.BlockSpec((1, BQ, LANE), lambda bi, g, qi: (bi, qi, g)),
                  pl.BlockSpec((NSA_GROUP, ncmp, BQ), lambda bi, g, qi: (g, 0, qi)),
                  pl.BlockSpec((nslc, ncmp), lambda bi, g, qi: (0, 0)),
                  pl.BlockSpec((NSA_GROUP, nd, BQ, BK), lambda bi, g, qi: (g, 0, 0, 0)),
                  pl.BlockSpec((NSA_GROUP, nwin, BQ, BK), lambda bi, g, qi: (g, 0, 0, 0))],
        out_specs=pl.BlockSpec((1, BQ, gw), lambda bi, g, qi: (bi, qi, g)),
        out_shape=jax.ShapeDtypeStruct((b, s, B_HEADS * LANE), F32),
        scratch_shapes=[pltpu.VMEM((nd, BQ, BK), F32)],
        compiler_params=_params("parallel", "parallel", "arbitrary"),
        name="nsa_attention",
    )(proj, kcmp, vcmp, proj, proj, proj, proj, gates, bct, covt, tzs, tzw)


def _moba_body(q_ref, k_ref, v_ref, avg_ref, tz_ref, o_ref, mask_ref):
    qi = pl.program_id(2)
    scale = HEAD_DIM ** -0.5
    nblk = avg_ref.shape[0]
    q = q_ref[0]
    kmean = jnp.dot(avg_ref[...], k_ref[0], preferred_element_type=F32)
    gate = _nt(kmean, q.astype(F32), precision=lax.Precision.HIGHEST)
    nb = lax.broadcasted_iota(I32, (nblk, BQ), 0)
    past = nb < qi
    beaten = jnp.zeros((nblk, BQ), F32)
    for np_ in range(nblk):
        row = gate[np_:np_ + 1, :]
        wins = (np_ < qi) & ((row > gate) | ((row == gate) & (np_ < nb)))
        beaten = beaten + wins.astype(F32)
    selt = (past & (beaten < MOBA_TOPK)).astype(BF16)
    eye = (lax.broadcasted_iota(I32, (BQ, BQ), 0) == lax.broadcasted_iota(I32, (BQ, BQ), 1)).astype(BF16)
    selq = _nt(eye, selt)
    for n in range(nblk):
        mask_ref[n] = jnp.broadcast_to(selq[:, n:n + 1], (BQ, BK))

    def score(kj, s):
        keep = (mask_ref[kj] > 0.5) | (kj == qi)
        return jnp.where(keep, s + tz_ref[0, qi - kj], NEG)

    o_ref[0] = _flash([q], k_ref, v_ref, 0, qi + 1, [score], scale)[0]


def _moba(proj, avg, tzc):
    b, s, _ = proj.shape
    nd = s // BK
    nblk = avg.shape[0]
    return pl.pallas_call(
        _moba_body,
        grid=(b, C_HEADS, s // BQ),
        in_specs=[pl.BlockSpec((1, BQ, LANE), lambda bi, h, qi: (bi, qi, CB_QC + h)),
                  pl.BlockSpec((1, s, LANE), lambda bi, h, qi: (bi, 0, CB_KCC + h)),
                  pl.BlockSpec((1, s, LANE), lambda bi, h, qi: (bi, 0, CB_VCC + h)),
                  pl.BlockSpec((nblk, s), lambda bi, h, qi: (0, 0)),
                  pl.BlockSpec((1, nd, BQ, BK), lambda bi, h, qi: (h, 0, 0, 0))],
        out_specs=pl.BlockSpec((1, BQ, LANE), lambda bi, h, qi: (bi, qi, h)),
        out_shape=jax.ShapeDtypeStruct((b, s, C_HEADS * LANE), F32),
        scratch_shapes=[pltpu.VMEM((nblk, BQ, BK), F32)],
        compiler_params=_params("parallel", "parallel", "arbitrary"),
        name="moba_attention",
    )(proj, proj, proj, avg, tzc)


def _outproj_body(oa_ref, ob_ref, oc_ref, g_ref, w_ref, h_ref, o_ref, y_ref):
    @pl.when(pl.program_id(1) == 0)
    def _():
        c0 = 0
        for ref in (oa_ref, ob_ref, oc_ref):
            x = ref[...]
            wd = x.shape[1]
            ms = jnp.mean(x * x, axis=-1, keepdims=True)
            y_ref[:, c0:c0 + wd] = (x * lax.rsqrt(ms + EPS) * g_ref[:, c0:c0 + wd]).astype(BF16)
            c0 += wd

    o_ref[...] = h_ref[...] + jnp.dot(y_ref[...], w_ref[...], preferred_element_type=F32)


def _outproj(oa, ob, oc, g, w, h, tm, tn):
    t, d = h.shape
    row = lambda a: pl.BlockSpec((tm, a.shape[1]), lambda i, j: (i, 0))
    return pl.pallas_call(
        _outproj_body,
        grid=(t // tm, d // tn),
        in_specs=[row(oa), row(ob), row(oc),
                  pl.BlockSpec((1, d), lambda i, j: (0, 0)),
                  pl.BlockSpec((d, tn), lambda i, j: (0, j)),
                  pl.BlockSpec((tm, tn), lambda i, j: (i, j))],
        out_specs=pl.BlockSpec((tm, tn), lambda i, j: (i, j)),
        out_shape=jax.ShapeDtypeStruct((t, d), F32),
        scratch_shapes=[pltpu.VMEM((tm, d), BF16)],
        compiler_params=_params("parallel", "arbitrary"),
        name="out_projection",
    )(oa, ob, oc, g.reshape(1, d), w, h)


def _topk_rows(x, k, payload=None):
    n, tm = x.shape
    rows = lax.broadcasted_iota(I32, x.shape, 0)
    slot = lax.broadcasted_iota(I32, (k, tm), 0)
    vals = jnp.zeros((k, tm), F32)
    idxs = jnp.zeros((k, tm), I32)
    for it in range(k):
        mx = jnp.max(x, axis=0, keepdims=True)
        idx = jnp.min(jnp.where(x == mx, rows, n), axis=0, keepdims=True)
        hit = rows == idx
        if payload is not None:
            idx = jnp.sum(jnp.where(hit, payload, 0), axis=0, keepdims=True)
        vals = jnp.where(slot == it, mx, vals)
        idxs = jnp.where(slot == it, idx, idxs)
        x = jnp.where(hit, -jnp.inf, x)
    return vals, idxs


def _retrieve(x, ln_ref, wq_ref, keys_ref, e_ref, g_ref):
    ms = jnp.mean(x * x, axis=-1, keepdims=True)
    xn = (x * lax.rsqrt(ms + EPS) * ln_ref[...]).astype(BF16)
    dq = keys_ref.shape[2]
    for h in range(PEER_HEADS):
        qh = jnp.dot(xn, wq_ref[:, 2 * h * dq:2 * (h + 1) * dq], preferred_element_type=F32)
        tops = []
        for c in range(2):
            sc = _nt(keys_ref[2 * h + c], qh[:, c * dq:(c + 1) * dq])
            tops.append(_topk_rows(sc, PEER_TOPK))
        (s0, i0), (s1, i1) = tops
        half = PEER_TOPK // 2
        sub = lax.broadcasted_iota(I32, (half, s0.shape[1]), 0)
        pieces = [s0[0:1] + s1]
        pieces_i = [i0[0:1] * PEER_NKEYS + i1]
        for a in range(1, half):
            keep = sub < PEER_TOPK // (a + 1)
            pieces.append(jnp.where(keep, s0[a:a + 1] + s1[0:half], -jnp.inf))
            pieces_i.append(i0[a:a + 1] * PEER_NKEYS + i1[0:half])
        pieces.append(s0[half:] + s1[0:1])
        pieces_i.append(i0[half:] * PEER_NKEYS + i1[0:1])
        cand = jnp.concatenate(pieces, axis=0)
        cand_i = jnp.concatenate(pieces_i, axis=0)
        bs, be = _topk_rows(cand, PEER_TOPK, payload=cand_i)
        e = jnp.exp(bs - jnp.max(bs, axis=0, keepdims=True))
        g_ref[h * PEER_TOPK:(h + 1) * PEER_TOPK, :] = e / jnp.sum(e, axis=0, keepdims=True)
        e_ref[h * PEER_TOPK:(h + 1) * PEER_TOPK, :] = be


PEER_NBUF = 16
PEER_AHEAD = 14


def _gelu(x):
    return 0.5 * x * (1.0 + lax.erf(x * (2.0 ** -0.5)))


def _hi_lo(x):
    hi = x.astype(BF16)
    lo = (x - hi.astype(F32)).astype(BF16)
    return jnp.concatenate([hi, lo], axis=1)


def _peer_expert_body(h_ref, hn_ref, ln_ref, wq_ref, keys_ref, uvw_ref, o_ref, xn_ref, acc_ref, lhs_ref,
                      buf_ref, sem_ref, idv_ref, gk_ref, ids_ref, gat_ref, csem):
    tb, d = h_ref.shape
    hk = gk_ref.shape[0]
    nrg = hk // 8
    nlt = d // LANE
    assert nrg == nlt and hk % (2 * nlt) == 0
    step = pl.program_id(0)
    cur, nxt = 0, 1
    x = h_ref[...]
    ms = jnp.mean(x * x, axis=-1, keepdims=True)
    xn_ref[...] = x * lax.rsqrt(ms + EPS) * ln_ref[...]
    g_ref = gat_ref.at[cur]

    def retrieve_next(x_ref):
        _retrieve(x_ref[...], ln_ref, wq_ref, keys_ref, idv_ref, gk_ref)
        gat_ref[nxt] = gk_ref[...].T

    def publish(par):
        copy = pltpu.make_async_copy(idv_ref, ids_ref.at[par], csem)
        copy.start()
        copy.wait()

    def issue(par, row, slot, k0, k1):
        for k in range(k0, k1):
            e = ids_ref[par, k, row]
            pltpu.make_async_copy(uvw_ref.at[e], buf_ref.at[slot, :, k, :],
                                  sem_ref.at[slot]).start(priority=k % 2)

    def issue_ahead(t, slot, k0, k1):
        row = t + PEER_AHEAD
        if isinstance(row, int) and row >= tb:
            issue(nxt, row - tb, slot, k0, k1)
        else:
            issue(cur, row, slot, k0, k1)

    def wait_all(slot):
        pltpu.make_async_copy(buf_ref.at[slot], buf_ref.at[slot], sem_ref.at[slot]).wait()

    def x_tiles(t):
        xrow = xn_ref[pl.ds(t, 1), :]
        return [jnp.broadcast_to(xrow[:, j * LANE:(j + 1) * LANE], (8, LANE)) for j in range(nlt)]

    def dots(xb, slot, r):
        a = None
        for j in range(nlt):
            w = buf_ref[slot, j, r * 8:(r + 1) * 8, :]
            pr = lax.bitcast_convert_type(lax.shift_left(w, jnp.uint32(16)), F32) * xb[j]
            a = pr if a is None else a + pr
        return a

    def coefficients(t, slot):
        act = jnp.sum(acc_ref[slot % 2].T, axis=0, keepdims=True)
        coef = jnp.broadcast_to(_gelu(act) * g_ref[pl.ds(t, 1), :], (8, hk))
        chi = coef.astype(BF16).astype(F32)
        return jnp.concatenate([chi, coef - chi], axis=0).astype(BF16)

    def weighted(lhs, slot, j):
        w = buf_ref[slot, j]
        v = lax.bitcast_convert_type(w & jnp.uint32(0xFFFF0000), F32).astype(BF16)
        yj = jnp.dot(lhs, v, preferred_element_type=F32)
        return yj[0:1, :] + yj[8:9, :]

    def flush(done):
        ya, yb, tp = done
        y = jnp.concatenate([ya[j:j + 1, :] for j in range(8)] + [yb[j:j + 1, :] for j in range(8)], axis=1)
        o_ref[pl.ds(tp, 1), :] = h_ref[pl.ds(tp, 1), :] + y

    def turn(t, slot, stages, done):
        tgt = (slot + PEER_AHEAD) % PEER_NBUF
        far = (slot + 2) % PEER_NBUF
        per = hk // (2 * nlt)
        if stages >= 3:
            wait_all(far)
            xb = x_tiles(t + 2)
        lhs = lhs_ref[slot % 2]
        ys, parts = [], []
        for c in range(nlt):
            if stages >= 3:
                parts.append(dots(xb, far, c))
            issue_ahead(t, tgt, 2 * c * per, (2 * c + 1) * per)
            if 2 * c < nlt:
                ys.append(weighted(lhs, slot, 2 * c))
                ys.append(weighted(lhs, slot, 2 * c + 1))
            if 2 * c == nlt and stages >= 2:
                lhs_next = coefficients(t + 1, slot + 1)
            issue_ahead(t, tgt, (2 * c + 1) * per, (2 * c + 2) * per)
        if stages >= 3:
            acc_ref[slot % 2] = jnp.concatenate(parts, axis=0)
        if stages >= 2:
            lhs_ref[(slot + 1) % 2] = lhs_next
        flush(done)
        return (jnp.concatenate(ys[:8], axis=0), jnp.concatenate(ys[8:], axis=0), t)

    @pl.when(step == 0)
    def _():
        retrieve_next(h_ref)

    publish(cur)
    gat_ref[cur] = gat_ref[nxt]

    @pl.when(step == 0)
    def _():
        for t0 in range(PEER_AHEAD):
            issue(cur, t0, t0, 0, hk)

    retrieve_next(hn_ref)
    publish(nxt)

    for t0 in range(2):
        wait_all(t0)
        xb0 = x_tiles(t0)
        acc_ref[t0] = jnp.concatenate([dots(xb0, t0, r) for r in range(nrg)], axis=0)
    lhs_ref[0] = coefficients(0, 0)

    def ring(i, stacks):
        done = stacks + (jnp.maximum(i * PEER_NBUF - 1, 0),)
        for slot in range(PEER_NBUF):
            done = turn(i * PEER_NBUF + slot, slot, 3, done)
        return done[:2]

    zero = jnp.zeros((8, LANE), F32)
    nring = (tb - 2) // PEER_NBUF
    assert nring * PEER_NBUF - 1 + PEER_AHEAD < tb
    done = lax.fori_loop(0, nring, ring, (zero, zero)) + (nring * PEER_NBUF - 1,)
    for t in range(nring * PEER_NBUF, tb - 2):
        done = turn(t, t % PEER_NBUF, 3, done)
    done = turn(tb - 2, (tb - 2) % PEER_NBUF, 2, done)
    done = turn(tb - 1, (tb - 1) % PEER_NBUF, 1, done)
    flush(done)

    @pl.when(step == pl.num_programs(0) - 1)
    def _():
        for t0 in range(PEER_AHEAD):
            wait_all((tb + t0) % PEER_NBUF)


def _peer_ffn(h, ln, wq, keys, slabs, tb):
    t, d = h.shape
    hk = PEER_HEADS * PEER_TOPK
    nblk = t // tb
    assert hk == LANE and tb == LANE and tb % PEER_NBUF == 0 and PEER_NBUF % 2 == 0 and d % LANE == 0
    assert PEER_AHEAD <= PEER_NBUF - 2
    assert slabs.shape[1:] == (d // LANE, LANE)
    assert wq.shape == (d, keys.shape[0] * keys.shape[2]) and keys.shape[2] == LANE
    return pl.pallas_call(
        _peer_expert_body,
        grid=(nblk,),
        in_specs=[pl.BlockSpec((tb, d), lambda i: (i, 0)),
                  pl.BlockSpec((tb, d), lambda i: (jnp.minimum(i + 1, nblk - 1), 0)),
                  pl.BlockSpec((1, d), lambda i: (0, 0)),
                  pl.BlockSpec(wq.shape, lambda i: (0, 0)),
                  pl.BlockSpec(keys.shape, lambda i: (0, 0, 0)),
                  pl.BlockSpec(memory_space=pl.ANY)],
        out_specs=pl.BlockSpec((tb, d), lambda i: (i, 0)),
        out_shape=jax.ShapeDtypeStruct((t, d), F32),
        scratch_shapes=[pltpu.VMEM((tb, d), F32),
                        pltpu.VMEM((2, hk, LANE), F32),
                        pltpu.VMEM((2, 16, hk), BF16),
                        pltpu.VMEM((PEER_NBUF, d // LANE, hk, LANE), jnp.uint32),
                        pltpu.SemaphoreType.DMA((PEER_NBUF,)),
                        pltpu.VMEM((hk, tb), I32),
                        pltpu.VMEM((hk, tb), F32),
                        pltpu.SMEM((2, hk, tb), I32),
                        pltpu.VMEM((2, tb, hk), F32),
                        pltpu.SemaphoreType.DMA(())],
        compiler_params=_params("arbitrary"),
        name="peer_ffn",
    )(h, h, ln.reshape(1, d), wq, keys, slabs)


def _ple_body(x_ref, g_ref, wg_ref, p_ref, wp_ref, h_ref, o_ref, xn_ref):
    @pl.when(pl.program_id(1) == 0)
    def _():
        x = x_ref[...]
        ms = jnp.mean(x * x, axis=-1, keepdims=True)
        xn_ref[...] = (x * lax.rsqrt(ms + EPS) * g_ref[...]).astype(BF16)

    z = jnp.dot(xn_ref[...], wg_ref[...], preferred_element_type=F32)
    pp = jnp.dot(p_ref[...].astype(BF16), wp_ref[...], preferred_element_type=F32)
    o_ref[...] = h_ref[...] + jax.nn.sigmoid(z) * pp


def _ple(h, g, wg, p, wp, tm, tn):
    t, d = h.shape
    pd = p.shape[1]
    return pl.pallas_call(
        _ple_body,
        grid=(t // tm, d // tn),
        in_specs=[pl.BlockSpec((tm, d), lambda i, j: (i, 0)),
                  pl.BlockSpec((1, d), lambda i, j: (0, 0)),
                  pl.BlockSpec((d, tn), lambda i, j: (0, j)),
                  pl.BlockSpec((tm, pd), lambda i, j: (i, 0)),
                  pl.BlockSpec((pd, tn), lambda i, j: (0, j)),
                  pl.BlockSpec((tm, tn), lambda i, j: (i, j))],
        out_specs=pl.BlockSpec((tm, tn), lambda i, j: (i, j)),
        out_shape=jax.ShapeDtypeStruct((t, d), F32),
        scratch_shapes=[pltpu.VMEM((tm, d), BF16)],
        compiler_params=_params("parallel", "arbitrary"),
        name="ple_gate",
    )(h, g.reshape(1, d), wg, p, wp, h)


def _rmsnorm_body(x_ref, g_ref, o_ref):
    x = x_ref[...]
    ms = jnp.mean(x * x, axis=-1, keepdims=True)
    o_ref[...] = x * lax.rsqrt(ms + EPS) * g_ref[...]


def _rmsnorm(x, g, tm):
    t, d = x.shape
    return pl.pallas_call(
        _rmsnorm_body,
        grid=(t // tm,),
        in_specs=[pl.BlockSpec((tm, d), lambda i: (i, 0)), pl.BlockSpec((1, d), lambda i: (0, 0))],
        out_specs=pl.BlockSpec((tm, d), lambda i: (i, 0)),
        out_shape=jax.ShapeDtypeStruct((t, d), F32),
        compiler_params=_params("parallel"),
        name="final_rmsnorm",
    )(x, g.reshape(1, d))


def _toeplitz(vec, nq, nk):
    assert nq == nk
    lead = vec.shape[:-1]
    g = int(np.prod(lead))
    w = jnp.concatenate([jnp.zeros(lead + (1,), vec.dtype), vec[..., ::-1]], axis=-1).reshape(g, 1, 2 * nk)

    def body(w_ref, o_ref):
        rows = jnp.broadcast_to(w_ref[0], (nq, 2 * nk))
        o_ref[0] = pltpu.roll(rows, 0, 1, stride=1, stride_axis=0)[:, nk:]

    out = pl.pallas_call(
        body,
        grid=(g,),
        in_specs=[pl.BlockSpec((1, 1, 2 * nk), lambda i: (i, 0, 0))],
        out_specs=pl.BlockSpec((1, nq, nk), lambda i: (i, 0, 0)),
        out_shape=jax.ShapeDtypeStruct((g, nq, nk), vec.dtype),
        compiler_params=_params("parallel"),
        name="toeplitz_tiles",
    )(w)
    return out.reshape(lead + (nq, nk))


def _bias_tables(rel_bias, s):
    nd = s // BK
    heads = rel_bias.shape[1]
    bdt = rel_bias[_rel_bucket(jnp.arange(s))].astype(F32).T
    dd = (jnp.arange(nd)[:, None, None] * BK + jnp.arange(BQ)[None, :, None]
          - jnp.arange(BK)[None, None, :])
    ext = jnp.pad(bdt, ((0, 0), (BK - 1, 0)))
    segs = jnp.stack([ext[:, dl * BK:dl * BK + BQ + BK - 1] for dl in range(nd)], axis=1)
    tz = _toeplitz(segs, BQ, BK)
    causal = dd >= 0
    mult = sum(((dd % dil == 0) & (dd // dil <= window // dil)).astype(F32) for window, dil in DIL_PATTERNS)
    ok = causal & (mult > 0)
    tzd = jnp.where(ok, tz[:A_HEADS] + jnp.log(jnp.where(ok, mult, 1.0)), NEG)
    tzb = tz[A_HEADS:A_HEADS + B_HEADS]
    tzs = jnp.where(causal, tzb, NEG)
    nwin = -(-(NSA_WINDOW - 1) // BK) + 1
    tzw = jnp.where(causal & (dd <= NSA_WINDOW - 1), tzb, NEG)[:, :nwin]
    tzc = jnp.where(causal, tz[A_HEADS + B_HEADS:], NEG)
    ncmp = s // CMP_STRIDE
    na = s // CMP_STRIDE
    nvec = na + ncmp - 1
    lo = CMP_STRIDE * (ncmp - 1) + CMP_LEN - 1
    bdb = bdt[A_HEADS:A_HEADS + B_HEADS]
    gext = jnp.concatenate([jnp.full((B_HEADS, lo), NEG, F32), bdb], axis=1)
    vecs = gext[:, :CMP_STRIDE * nvec].reshape(B_HEADS, nvec, CMP_STRIDE).transpose(0, 2, 1)
    bcq = _toeplitz(vecs, na, ncmp)
    bcq = bcq.transpose(0, 2, 1, 3).reshape(B_HEADS, s, ncmp)
    bct = jnp.transpose(bcq, (0, 2, 1))
    nslc = s // SLC_LEN
    cstart = jnp.arange(ncmp) * CMP_STRIDE
    sstart = jnp.arange(nslc) * SLC_LEN
    covt = ((cstart[None, :] < sstart[:, None] + SLC_LEN)
            & (cstart[None, :] + CMP_LEN > sstart[:, None])).astype(F32)
    nblk = s // MOBA_BLK
    avg = ((jnp.arange(s)[None, :] // MOBA_BLK == jnp.arange(nblk)[:, None]).astype(F32)
           / MOBA_BLK).astype(BF16)
    return tzd, tzs, tzw, tzc, bct, covt, avg


def _pack_body(u_ref, v_ref, o_ref):
    _, te, d = u_ref.shape
    nlt = d // LANE
    ub = lax.bitcast_convert_type(u_ref[0].astype(BF16).astype(F32), jnp.uint32)
    vb = lax.bitcast_convert_type(v_ref[0].astype(BF16).astype(F32), jnp.uint32)
    word = lax.shift_right_logical(ub, jnp.uint32(16)) | vb
    for j in range(nlt):
        o_ref[pl.ds(j, te, stride=nlt), :] = word[:, j * LANE:(j + 1) * LANE]


def _pack_uv(u, v, layer, te=256):
    _, e, d = u.shape
    nlt = d // LANE
    spec = pl.BlockSpec((1, te, d), lambda i: (layer, i, 0))
    out = pl.pallas_call(
        _pack_body,
        grid=(e // te,),
        in_specs=[spec, spec],
        out_specs=pl.BlockSpec((te * nlt, LANE), lambda i: (i, 0)),
        out_shape=jax.ShapeDtypeStruct((e * nlt, LANE), jnp.uint32),
        compiler_params=_params("parallel"),
        name="pack_experts",
    )(u, v)
    return out.reshape(e, nlt, LANE)


GATE_COL0 = 3 * A_HEADS * LANE + B_HEADS * LANE + 6 * NSA_KV * LANE
GATE_COLS = 3 * B_HEADS


def _w_main_body(w_ref, o_ref):
    x = w_ref[0]
    o_ref[:, :GATE_COL0] = x[:, :GATE_COL0].astype(BF16)
    o_ref[:, GATE_COL0:] = x[:, GATE_COL0 + GATE_COLS:].astype(BF16)


def _reorder_w_in(w_in, layer, tk=256):
    _, d, cols = w_in.shape
    main = pl.pallas_call(
        _w_main_body,
        grid=(d // tk,),
        in_specs=[pl.BlockSpec((1, tk, cols), lambda i: (layer, i, 0))],
        out_specs=pl.BlockSpec((tk, MAIN_COLS), lambda i: (i, 0)),
        out_shape=jax.ShapeDtypeStruct((d, MAIN_COLS), BF16),
        compiler_params=_params("parallel"),
        name="reorder_w_in",
    )(w_in)
    per = GATE_COLS // NSA_KV
    wg = w_in[layer, :, GATE_COL0:GATE_COL0 + GATE_COLS]
    gate = jnp.concatenate([jnp.pad(wg[:, g * per:(g + 1) * per], ((0, 0), (0, LANE - per))) for g in range(NSA_KV)],
                           axis=1)
    return main, gate.astype(BF16)


def kernel(x, p, ln_mix, w_in, cmp_wk, cmp_wv, cmp_pos, out_norm, w_out, rel_bias, ln_ffn, peer_wq,
           peer_keys, peer_u, peer_v, ln_ple, ple_gate, ple_proj, ln_final):
    b, s, d = x.shape
    t = b * s
    depth = w_in.shape[0]
    assert s % BQ == 0 and BQ == BK == MOBA_BLK and d % LANE == 0
    tm = 1024 if t % 1024 == 0 else 512
    tzd, tzs, tzw, tzc, bct, covt, avg = _bias_tables(rel_bias, s)
    h = x.reshape(t, d)
    for i in range(depth):
        w_main, w_gate = _reorder_w_in(w_in, i)
        proj = _normmm(h, ln_mix[i], w_main, BF16, tm, 1024, "in_projection").reshape(b, s, MAIN_COLS)
        gates = _normmm(h, ln_mix[i], w_gate, F32, tm, w_gate.shape[1], "gate_projection")
        gates = gates.reshape(b, s, NSA_KV * LANE)
        oa = _dilated(proj, tzd)
        kcmp, vcmp = _compress(proj, cmp_wk[i], cmp_wv[i], cmp_pos[i])
        ob = _nsa(proj, gates, kcmp, vcmp, bct, covt, tzs, tzw)
        oc = _moba(proj, avg, tzc)
        h = _outproj(oa.reshape(t, -1), ob.reshape(t, -1), oc.reshape(t, -1), out_norm[i],
                     w_out[i].astype(BF16), h, tm, 1024)
        keys = peer_keys[i].reshape(PEER_HEADS * 2, PEER_NKEYS, -1)
        h = _peer_ffn(h, ln_ffn[i], peer_wq[i].astype(BF16), keys, _pack_uv(peer_u, peer_v, i), LANE)
        h = _ple(h, ln_ple[i], ple_gate[i].astype(BF16), p[i].reshape(t, -1), ple_proj[i].astype(BF16),
                 tm, 1024)
    return _rmsnorm(h, ln_final, tm).reshape(b, s, d)
```

```python
import functools
import math

import numpy as np
import jax
import jax.numpy as jnp
from jax import lax
from jax.experimental import pallas as pl
from jax.experimental.pallas import tpu as pltpu

F32 = jnp.float32
BF16 = jnp.bfloat16
I32 = jnp.int32

HEAD_DIM = 128
A_HEADS, B_HEADS, C_HEADS = 6, 6, 4
NSA_KV, NSA_GROUP = 2, 3
DIL_PATTERNS = ((128, 1), (512, 4), (2048, 16))
CMP_LEN, CMP_STRIDE = 32, 16
SLC_LEN, SLC_TOPN = 64, 8
NSA_WINDOW = 512
FORCE = 1e4
MOBA_BLK, MOBA_TOPK = 256, 3
REL_BUCKETS, REL_MAX_DIST = 32, 1024
PEER_HEADS, PEER_NKEYS, PEER_TOPK = 8, 128, 16
EPS = 1e-6

LANE = 128
BQ = 256
BK = 256
NEG = -1e30
HALF_NEG = -5e29
VMEM_LIMIT = 56 * 1024 * 1024

CB_QA, CB_KA, CB_VA = 0, 6, 12
CB_QB, CB_KC, CB_VC, CB_KS, CB_VS, CB_KW, CB_VW = 18, 24, 26, 28, 30, 32, 34
CB_QC, CB_KCC, CB_VCC = 36, 40, 44
MAIN_COLS = 48 * LANE


def _nt(a, b, precision=None):
    return lax.dot_general(a, b, (((1,), (1,)), ((), ())), preferred_element_type=F32,
                           precision=precision)


def _rel_bucket(dist):
    exact = REL_BUCKETS // 2
    d = jnp.maximum(dist, 0)
    logd = jnp.log(jnp.maximum(d, 1).astype(F32) / exact)
    large = exact + (logd / math.log(REL_MAX_DIST / exact) * (REL_BUCKETS - exact)).astype(I32)
    large = jnp.clip(large, exact, REL_BUCKETS - 1)
    return jnp.where(d < exact, d, large)


def _masked_softmax(s, axis):
    valid = s > HALF_NEG
    m = jnp.max(s, axis=axis, keepdims=True)
    e = jnp.where(valid, jnp.exp(s - m), 0.0)
    z = jnp.sum(e, axis=axis, keepdims=True)
    zs = jnp.where(z > 0, z, 1.0)
    return e * (1.0 / zs)


def _params(*sem):
    return pltpu.CompilerParams(dimension_semantics=sem, vmem_limit_bytes=VMEM_LIMIT)


def _normmm_body(x_ref, g_ref, w_ref, o_ref, xn_ref):
    @pl.when(pl.program_id(1) == 0)
    def _():
        x = x_ref[...]
        ms = jnp.mean(x * x, axis=-1, keepdims=True)
        xn_ref[...] = (x * lax.rsqrt(ms + EPS) * g_ref[...]).astype(BF16)

    o_ref[...] = jnp.dot(xn_ref[...], w_ref[...], preferred_element_type=F32).astype(o_ref.dtype)


def _normmm(x, g, w, out_dtype, tm, tn, name):
    t, d = x.shape
    n = w.shape[1]
    return pl.pallas_call(
        _normmm_body,
        grid=(t // tm, n // tn),
        in_specs=[pl.BlockSpec((tm, d), lambda i, j: (i, 0)),
                  pl.BlockSpec((1, d), lambda i, j: (0, 0)),
                  pl.BlockSpec((d, tn), lambda i, j: (0, j))],
        out_specs=pl.BlockSpec((tm, tn), lambda i, j: (i, j)),
        out_shape=jax.ShapeDtypeStruct((t, n), out_dtype),
        scratch_shapes=[pltpu.VMEM((tm, d), BF16)],
        compiler_params=_params("parallel", "arbitrary"),
        name=name,
    )(x, g.reshape(1, d), w)


def _flash(qs, k_ref, v_ref, lo, hi, score_fns, scale):
    def tile(kj):
        off = pl.multiple_of(kj * BK, BK)
        return k_ref[0, pl.ds(off, BK), :], v_ref[0, pl.ds(off, BK), :]

    def update(state, blocks):
        m, l, acc = state
        m_new = m
        for s, _ in blocks:
            m_new = jnp.maximum(m_new, jnp.max(s, axis=1, keepdims=True))
        alpha = jnp.exp(m - m_new)
        l = alpha * l
        acc = alpha * acc
        for s, v in blocks:
            p = jnp.exp(s - m_new)
            l = l + jnp.sum(p, axis=1, keepdims=True)
            acc = acc + jnp.dot(p.astype(BF16), v, preferred_element_type=F32)
        return m_new, l, acc

    def step(kjs, states):
        kv = [tile(kj) for kj in kjs]
        return tuple(
            update(states[i], [(score_fns[i](kj, _nt(q, k) * scale), v) for kj, (k, v) in zip(kjs, kv)])
            for i, q in enumerate(qs))

    states = tuple((jnp.full((BQ, 1), NEG, F32), jnp.zeros((BQ, 1), F32), jnp.zeros((BQ, HEAD_DIM), F32))
                   for _ in qs)
    start = lo
    for width in ((4, 2, 1) if len(qs) == 1 else (2, 1)):
        shift = width.bit_length() - 1
        count = lax.shift_right_logical(hi - start, shift)

        def group(p, st, start=start, width=width):
            return step([start + width * p + i for i in range(width)], st)

        states = lax.fori_loop(0, count, group, states)
        start = start + count * width
    return [acc / l for _, l, acc in states]


def _dilated_body(q_ref, k_ref, v_ref, tz_ref, o_ref):
    qi = pl.program_id(2)
    scale = HEAD_DIM ** -0.5

    def score(kj, s):
        return s + tz_ref[0, qi - kj]

    o_ref[0] = _flash([q_ref[0]], k_ref, v_ref, 0, qi + 1, [score], scale)[0]


def _dilated(proj, tzd):
    b, s, _ = proj.shape
    nd = s // BK
    return pl.pallas_call(
        _dilated_body,
        grid=(b, A_HEADS, s // BQ),
        in_specs=[pl.BlockSpec((1, BQ, LANE), lambda bi, h, qi: (bi, qi, CB_QA + h)),
                  pl.BlockSpec((1, s, LANE), lambda bi, h, qi: (bi, 0, CB_KA + h)),
                  pl.BlockSpec((1, s, LANE), lambda bi, h, qi: (bi, 0, CB_VA + h)),
                  pl.BlockSpec((1, nd, BQ, BK), lambda bi, h, qi: (h, 0, 0, 0))],
        out_specs=pl.BlockSpec((1, BQ, LANE), lambda bi, h, qi: (bi, qi, h)),
        out_shape=jax.ShapeDtypeStruct((b, s, A_HEADS * LANE), F32),
        compiler_params=_params("parallel", "parallel", "arbitrary"),
        name="dilated_attention",
    )(proj, proj, proj, tzd)


def _compress_body(kc_ref, vc_ref, wk_ref, wv_ref, pos_ref, ko_ref, vo_ref, xk_ref, xv_ref):
    s = kc_ref.shape[1]
    ncmp = ko_ref.shape[2]
    xk_ref[pl.ds(0, s), :] = kc_ref[0].astype(F32)
    xv_ref[pl.ds(0, s), :] = vc_ref[0].astype(F32)
    xk_ref[pl.ds(s, CMP_LEN), :] = jnp.zeros((CMP_LEN, HEAD_DIM), F32)
    xv_ref[pl.ds(s, CMP_LEN), :] = jnp.zeros((CMP_LEN, HEAD_DIM), F32)
    acck = jnp.zeros((ncmp, HEAD_DIM), F32)
    accv = jnp.zeros((ncmp, HEAD_DIM), F32)
    for l in range(CMP_LEN):
        pos = pos_ref[pl.ds(l, 1), :]
        rk = (xk_ref[pl.ds(l, ncmp, stride=CMP_STRIDE), :] + pos).astype(BF16)
        rv = (xv_ref[pl.ds(l, ncmp, stride=CMP_STRIDE), :] + pos).astype(BF16)
        acck = acck + jnp.dot(rk, wk_ref[l], preferred_element_type=F32)
        accv = accv + jnp.dot(rv, wv_ref[l], preferred_element_type=F32)
    ko_ref[0, 0] = acck.astype(BF16)
    vo_ref[0, 0] = accv.astype(BF16)


def _compress(proj, cmp_wk, cmp_wv, cmp_pos):
    b, s, _ = proj.shape
    ncmp = s // CMP_STRIDE
    out = jax.ShapeDtypeStruct((b, NSA_KV, ncmp, HEAD_DIM), BF16)
    return pl.pallas_call(
        _compress_body,
        grid=(b, NSA_KV),
        in_specs=[pl.BlockSpec((1, s, LANE), lambda bi, g: (bi, 0, CB_KC + g)),
                  pl.BlockSpec((1, s, LANE), lambda bi, g: (bi, 0, CB_VC + g)),
                  pl.BlockSpec((CMP_LEN, HEAD_DIM, HEAD_DIM), lambda bi, g: (0, 0, 0)),
                  pl.BlockSpec((CMP_LEN, HEAD_DIM, HEAD_DIM), lambda bi, g: (0, 0, 0)),
                  pl.BlockSpec((CMP_LEN, HEAD_DIM), lambda bi, g: (0, 0))],
        out_specs=[pl.BlockSpec((1, 1, ncmp, HEAD_DIM), lambda bi, g: (bi, g, 0, 0)),
                   pl.BlockSpec((1, 1, ncmp, HEAD_DIM), lambda bi, g: (bi, g, 0, 0))],
        out_shape=[out, out],
        scratch_shapes=[pltpu.VMEM((s + CMP_LEN, HEAD_DIM), F32),
                        pltpu.VMEM((s + CMP_LEN, HEAD_DIM), F32)],
        compiler_params=_params("parallel", "parallel"),
        name="nsa_compress",
    )(proj, proj, cmp_wk.astype(BF16), cmp_wv.astype(BF16), cmp_pos)


def _nsa_body(q_ref, kcmp_ref, vcmp_ref, ks_ref, vs_ref, kw_ref, vw_ref, gt_ref, bct_ref,
              covt_ref, tzs_ref, tzw_ref, o_ref, mask_ref):
    qi = pl.program_id(2)
    scale = HEAD_DIM ** -0.5
    nslc = covt_ref.shape[0]
    nkt = mask_ref.shape[0]
    kcmp = kcmp_ref[0, 0]
    vcmp = vcmp_ref[0, 0]
    gates = jax.nn.sigmoid(gt_ref[0])

    psum = jnp.zeros((kcmp.shape[0], BQ), F32)
    for r in range(NSA_GROUP):
        q = q_ref[0, :, r * LANE:(r + 1) * LANE]
        pt = _masked_softmax(_nt(kcmp, q) * scale + bct_ref[r], axis=0)
        o_cmp = lax.dot_general(pt.astype(BF16), vcmp, (((0,), (0,)), ((), ())),
                                preferred_element_type=F32)
        o_ref[0, :, r * LANE:(r + 1) * LANE] = gates[:, 3 * r:3 * r + 1] * o_cmp
        psum = psum + pt
    imp = jnp.dot(covt_ref[...], psum, preferred_element_type=F32,
                  precision=lax.Precision.HIGHEST)
    jb = lax.broadcasted_iota(I32, (nslc, BQ), 0)
    qpos = qi * BQ + lax.broadcasted_iota(I32, (nslc, BQ), 1)
    qblk = lax.shift_right_logical(qpos, int(math.log2(SLC_LEN)))
    forced = (jb == 0) | (jb == qblk) | (jb == qblk - 1)
    imp = jnp.where(forced, FORCE, jnp.where(jb > qblk, -FORCE, imp))
    beaten = jnp.zeros((nslc, BQ), F32)
    for jp in range(nslc):
        row = imp[jp:jp + 1, :]
        wins = (row > imp) | ((row == imp) & (jp < jb))
        beaten = beaten + wins.astype(F32)
    selt = (beaten < SLC_TOPN).astype(BF16)
    eye = (lax.broadcasted_iota(I32, (BQ, BQ), 0) == lax.broadcasted_iota(I32, (BQ, BQ), 1)).astype(BF16)
    selq = _nt(eye, selt).astype(BF16)
    kpos = lax.broadcasted_iota(I32, (nslc, BK), 1)
    kblk_row = lax.broadcasted_iota(I32, (nslc, BK), 0)

    for t in range(nkt):
        expand = (lax.shift_right_logical(t * BK + kpos, int(math.log2(SLC_LEN))) == kblk_row).astype(BF16)
        mask_ref[t] = (jnp.dot(selq, expand, preferred_element_type=F32) - 1.0) * (-NEG)

    qs = [q_ref[0, :, r * LANE:(r + 1) * LANE] for r in range(NSA_GROUP)]
    sel_fns = [lambda kj, s, r=r: s + tzs_ref[r, qi - kj] + mask_ref[kj] for r in range(NSA_GROUP)]
    win_fns = [lambda kj, s, r=r: s + tzw_ref[r, qi - kj] for r in range(NSA_GROUP)]
    nwin = tzw_ref.shape[1]
    o_sel = _flash(qs, ks_ref, vs_ref, 0, qi + 1, sel_fns, scale)
    o_win = _flash(qs, kw_ref, vw_ref, jnp.maximum(qi - (nwin - 1), 0), qi + 1, win_fns, scale)
    for r in range(NSA_GROUP):
        o_ref[0, :, r * LANE:(r + 1) * LANE] += (gates[:, 3 * r + 1:3 * r + 2] * o_sel[r]
                                                 + gates[:, 3 * r + 2:3 * r + 3] * o_win[r])


def _nsa(proj, gates, kcmp, vcmp, bct, covt, tzs, tzw):
    b, s, _ = proj.shape
    nd = s // BK
    ncmp = kcmp.shape[2]
    nslc = covt.shape[0]
    nwin = tzw.shape[1]
    gw = NSA_GROUP * LANE
    kv = lambda cb: pl.BlockSpec((1, s, LANE), lambda bi, g, qi: (bi, 0, cb + g))
    return pl.pallas_call(
        _nsa_body,
        grid=(b, NSA_KV, s // BQ),
        in_specs=[pl.BlockSpec((1, BQ, gw), lambda bi, g, qi: (bi, qi, CB_QB // NSA_GROUP + g)),
                  pl.BlockSpec((1, 1, ncmp, HEAD_DIM), lambda bi, g, qi: (bi, g, 0, 0)),
                  pl.BlockSpec((1, 1, ncmp, HEAD_DIM), lambda bi, g, qi: (bi, g, 0, 0)),
                  kv(CB_KS), kv(CB_VS), kv(CB_KW), kv(CB_VW),
                  pl.BlockSpec((1, BQ, LANE), lambda bi, g, qi: (bi, qi, g)),
                  pl.BlockSpec((NSA_GROUP, ncmp, BQ), lambda bi, g, qi: (g, 0, qi)),
                  pl.BlockSpec((nslc, ncmp), lambda bi, g, qi: (0, 0)),
                  pl.BlockSpec((NSA_GROUP, nd, BQ, BK), lambda bi, g, qi: (g, 0, 0, 0)),
                  pl.BlockSpec((NSA_GROUP, nwin, BQ, BK), lambda bi, g, qi: (g, 0, 0, 0))],
        out_specs=pl.BlockSpec((1, BQ, gw), lambda bi, g, qi: (bi, qi, g)),
        out_shape=jax.ShapeDtypeStruct((b, s, B_HEADS * LANE), F32),
        scratch_shapes=[pltpu.VMEM((nd, BQ, BK), F32)],
        compiler_params=_params("parallel", "parallel", "arbitrary"),
        name="nsa_attention",
    )(proj, kcmp, vcmp, proj, proj, proj, proj, gates, bct, covt, tzs, tzw)


def _moba_body(q_ref, k_ref, v_ref, avg_ref, tz_ref, o_ref, mask_ref):
    qi = pl.program_id(2)
    scale = HEAD_DIM ** -0.5
    nblk = avg_ref.shape[0]
    q = q_ref[0]
    kmean = jnp.dot(avg_ref[...], k_ref[0], preferred_element_type=F32)
    gate = _nt(kmean, q.astype(F32), precision=lax.Precision.HIGHEST)
    nb = lax.broadcasted_iota(I32, (nblk, BQ), 0)
    past = nb < qi
    beaten = jnp.zeros((nblk, BQ), F32)
    for np_ in range(nblk):
        row = gate[np_:np_ + 1, :]
        wins = (np_ < qi) & ((row > gate) | ((row == gate) & (np_ < nb)))
        beaten = beaten + wins.astype(F32)
    selt = (past & (beaten < MOBA_TOPK)).astype(BF16)
    eye = (lax.broadcasted_iota(I32, (BQ, BQ), 0) == lax.broadcasted_iota(I32, (BQ, BQ), 1)).astype(BF16)
    selq = _nt(eye, selt)
    for n in range(nblk):
        mask_ref[n] = jnp.broadcast_to(selq[:, n:n + 1], (BQ, BK))

    def score(kj, s):
        keep = (mask_ref[kj] > 0.5) | (kj == qi)
        return jnp.where(keep, s + tz_ref[0, qi - kj], NEG)

    o_ref[0] = _flash([q], k_ref, v_ref, 0, qi + 1, [score], scale)[0]


def _moba(proj, avg, tzc):
    b, s, _ = proj.shape
    nd = s // BK
    nblk = avg.shape[0]
    return pl.pallas_call(
        _moba_body,
        grid=(b, C_HEADS, s // BQ),
        in_specs=[pl.BlockSpec((1, BQ, LANE), lambda bi, h, qi: (bi, qi, CB_QC + h)),
                  pl.BlockSpec((1, s, LANE), lambda bi, h, qi: (bi, 0, CB_KCC + h)),
                  pl.BlockSpec((1, s, LANE), lambda bi, h, qi: (bi, 0, CB_VCC + h)),
                  pl.BlockSpec((nblk, s), lambda bi, h, qi: (0, 0)),
                  pl.BlockSpec((1, nd, BQ, BK), lambda bi, h, qi: (h, 0, 0, 0))],
        out_specs=pl.BlockSpec((1, BQ, LANE), lambda bi, h, qi: (bi, qi, h)),
        out_shape=jax.ShapeDtypeStruct((b, s, C_HEADS * LANE), F32),
        scratch_shapes=[pltpu.VMEM((nblk, BQ, BK), F32)],
        compiler_params=_params("parallel", "parallel", "arbitrary"),
        name="moba_attention",
    )(proj, proj, proj, avg, tzc)


def _outproj_body(oa_ref, ob_ref, oc_ref, g_ref, w_ref, h_ref, o_ref, y_ref):
    @pl.when(pl.program_id(1) == 0)
    def _():
        c0 = 0
        for ref in (oa_ref, ob_ref, oc_ref):
            x = ref[...]
            wd = x.shape[1]
            ms = jnp.mean(x * x, axis=-1, keepdims=True)
            y_ref[:, c0:c0 + wd] = (x * lax.rsqrt(ms + EPS) * g_ref[:, c0:c0 + wd]).astype(BF16)
            c0 += wd

    o_ref[...] = h_ref[...] + jnp.dot(y_ref[...], w_ref[...], preferred_element_type=F32)


def _outproj(oa, ob, oc, g, w, h, tm, tn):
    t, d = h.shape
    row = lambda a: pl.BlockSpec((tm, a.shape[1]), lambda i, j: (i, 0))
    return pl.pallas_call(
        _outproj_body,
        grid=(t // tm, d // tn),
        in_specs=[row(oa), row(ob), row(oc),
                  pl.BlockSpec((1, d), lambda i, j: (0, 0)),
                  pl.BlockSpec((d, tn), lambda i, j: (0, j)),
                  pl.BlockSpec((tm, tn), lambda i, j: (i, j))],
        out_specs=pl.BlockSpec((tm, tn), lambda i, j: (i, j)),
        out_shape=jax.ShapeDtypeStruct((t, d), F32),
        scratch_shapes=[pltpu.VMEM((tm, d), BF16)],
        compiler_params=_params("parallel", "arbitrary"),
        name="out_projection",
    )(oa, ob, oc, g.reshape(1, d), w, h)


def _topk_rows(x, k, payload=None):
    n, tm = x.shape
    rows = lax.broadcasted_iota(I32, x.shape, 0)
    slot = lax.broadcasted_iota(I32, (k, tm), 0)
    vals = jnp.zeros((k, tm), F32)
    idxs = jnp.zeros((k, tm), I32)
    for it in range(k):
        mx = jnp.max(x, axis=0, keepdims=True)
        idx = jnp.min(jnp.where(x == mx, rows, n), axis=0, keepdims=True)
        hit = rows == idx
        if payload is not None:
            idx = jnp.sum(jnp.where(hit, payload, 0), axis=0, keepdims=True)
        vals = jnp.where(slot == it, mx, vals)
        idxs = jnp.where(slot == it, idx, idxs)
        x = jnp.where(hit, -jnp.inf, x)
    return vals, idxs


def _retrieve(xn, wq_ref, keys_ref, e_ref, g_ref, heads):
    dq = keys_ref.shape[2]
    for h in heads:
        qh = jnp.dot(xn, wq_ref[:, 2 * h * dq:2 * (h + 1) * dq], preferred_element_type=F32)
        tops = []
        for c in range(2):
            sc = _nt(keys_ref[2 * h + c], qh[:, c * dq:(c + 1) * dq])
            tops.append(_topk_rows(sc, PEER_TOPK))
        (s0, i0), (s1, i1) = tops
        half = PEER_TOPK // 2
        sub = lax.broadcasted_iota(I32, (half, s0.shape[1]), 0)
        pieces = [s0[0:1] + s1]
        pieces_i = [i0[0:1] * PEER_NKEYS + i1]
        for a in range(1, half):
            keep = sub < PEER_TOPK // (a + 1)
            pieces.append(jnp.where(keep, s0[a:a + 1] + s1[0:half], -jnp.inf))
            pieces_i.append(i0[a:a + 1] * PEER_NKEYS + i1[0:half])
        pieces.append(s0[half:] + s1[0:1])
        pieces_i.append(i0[half:] * PEER_NKEYS + i1[0:1])
        cand = jnp.concatenate(pieces, axis=0)
        cand_i = jnp.concatenate(pieces_i, axis=0)
        bs, be = _topk_rows(cand, PEER_TOPK, payload=cand_i)
        e = jnp.exp(bs - jnp.max(bs, axis=0, keepdims=True))
        g_ref[h * PEER_TOPK:(h + 1) * PEER_TOPK, :] = e / jnp.sum(e, axis=0, keepdims=True)
        e_ref[h * PEER_TOPK:(h + 1) * PEER_TOPK, :] = be


PEER_NBUF = 16
PEER_AHEAD = 14


def _gelu(x):
    return 0.5 * x * (1.0 + lax.erf(x * (2.0 ** -0.5)))


def _hi_lo(x):
    hi = x.astype(BF16)
    lo = (x - hi.astype(F32)).astype(BF16)
    return jnp.concatenate([hi, lo], axis=1)


def _peer_expert_body(h_ref, hn_ref, ln_ref, wq_ref, keys_ref, uvw_ref, o_ref, xn_ref, acc_ref, lhs_ref,
                      buf_ref, sem_ref, idv_ref, gk_ref, ids_ref, gat_ref, csem, xq_ref):
    tb, d = h_ref.shape
    hk = gk_ref.shape[0]
    nrg = hk // 8
    nlt = d // LANE
    assert nrg == nlt and hk % (2 * nlt) == 0
    step = pl.program_id(0)
    cur, nxt = 0, 1
    x = h_ref[...]
    ms = jnp.mean(x * x, axis=-1, keepdims=True)
    xn_ref[...] = x * lax.rsqrt(ms + EPS) * ln_ref[...]
    g_ref = gat_ref.at[cur]

    def normalised(x_ref):
        xx = x_ref[...]
        return (xx * lax.rsqrt(jnp.mean(xx * xx, axis=-1, keepdims=True) + EPS) * ln_ref[...]).astype(BF16)

    def retrieve(xn, heads):
        _retrieve(xn, wq_ref, keys_ref, idv_ref, gk_ref, heads)

    def publish(par):
        copy = pltpu.make_async_copy(idv_ref, ids_ref.at[par], csem)
        copy.start()
        copy.wait()

    def issue(par, row, slot, k0, k1):
        for k in range(k0, k1):
            e = ids_ref[par, k, row]
            pltpu.make_async_copy(uvw_ref.at[e], buf_ref.at[slot, :, k, :],
                                  sem_ref.at[slot]).start(priority=k % 2)

    def issue_ahead(t, slot, k0, k1):
        row = t + PEER_AHEAD
        if isinstance(row, int) and row >= tb:
            issue(nxt, row - tb, slot, k0, k1)
        else:
            issue(cur, row, slot, k0, k1)

    def wait_all(slot):
        pltpu.make_async_copy(buf_ref.at[slot], buf_ref.at[slot], sem_ref.at[slot]).wait()

    def x_tiles(t):
        xrow = xn_ref[pl.ds(t, 1), :]
        return [jnp.broadcast_to(xrow[:, j * LANE:(j + 1) * LANE], (8, LANE)) for j in range(nlt)]

    def dots(xb, slot, r):
        a = None
        for j in range(nlt):
            w = buf_ref[slot, j, r * 8:(r + 1) * 8, :]
            pr = lax.bitcast_convert_type(lax.shift_left(w, jnp.uint32(16)), F32) * xb[j]
            a = pr if a is None else a + pr
        return a

    def coefficients(t, slot):
        act = jnp.sum(acc_ref[slot % 2].T, axis=0, keepdims=True)
        coef = jnp.broadcast_to(_gelu(act) * g_ref[pl.ds(t, 1), :], (8, hk))
        chi = coef.astype(BF16).astype(F32)
        return jnp.concatenate([chi, coef - chi], axis=0).astype(BF16)

    def weighted(lhs, slot, j):
        w = buf_ref[slot, j]
        v = lax.bitcast_convert_type(w & jnp.uint32(0xFFFF0000), F32).astype(BF16)
        yj = jnp.dot(lhs, v, preferred_element_type=F32)
        return yj[0:1, :] + yj[8:9, :]

    def flush(done):
        ya, yb, tp = done
        y = jnp.concatenate([ya[j:j + 1, :] for j in range(8)] + [yb[j:j + 1, :] for j in range(8)], axis=1)
        o_ref[pl.ds(tp, 1), :] = h_ref[pl.ds(tp, 1), :] + y

    def turn(t, slot, stages, done):
        tgt = (slot + PEER_AHEAD) % PEER_NBUF
        far = (slot + 2) % PEER_NBUF
        per = hk // (2 * nlt)
        if stages >= 3:
            wait_all(far)
            xb = x_tiles(t + 2)
        lhs = lhs_ref[slot % 2]
        ys, parts = [], []
        for c in range(nlt):
            if stages >= 3:
                parts.append(dots(xb, far, c))
            issue_ahead(t, tgt, 2 * c * per, (2 * c + 1) * per)
            if 2 * c < nlt:
                ys.append(weighted(lhs, slot, 2 * c))
                ys.append(weighted(lhs, slot, 2 * c + 1))
            if 2 * c == nlt and stages >= 2:
                lhs_next = coefficients(t + 1, slot + 1)
            issue_ahead(t, tgt, (2 * c + 1) * per, (2 * c + 2) * per)
        if stages >= 3:
            acc_ref[slot % 2] = jnp.concatenate(parts, axis=0)
        if stages >= 2:
            lhs_ref[(slot + 1) % 2] = lhs_next
        flush(done)
        return (jnp.concatenate(ys[:8], axis=0), jnp.concatenate(ys[8:], axis=0), t)

    @pl.when(step == 0)
    def _():
        retrieve(normalised(h_ref), range(PEER_HEADS))
        gat_ref[nxt] = gk_ref[...].T

    publish(cur)
    gat_ref[cur] = gat_ref[nxt]

    @pl.when(step == 0)
    def _():
        for t0 in range(PEER_AHEAD):
            issue(cur, t0, t0, 0, hk)

    xq_ref[...] = normalised(hn_ref)
    nring = (tb - 2) // PEER_NBUF
    nseg = min(PEER_HEADS // 2, nring)
    head_parts = [range(s * PEER_HEADS // nseg, (s + 1) * PEER_HEADS // nseg) for s in range(nseg)]
    ring_cuts = [s * nring // nseg for s in range(nseg)] + [nring]
    retrieve(xq_ref[...], head_parts[0])

    for t0 in range(2):
        wait_all(t0)
        xb0 = x_tiles(t0)
        acc_ref[t0] = jnp.concatenate([dots(xb0, t0, r) for r in range(nrg)], axis=0)
    lhs_ref[0] = coefficients(0, 0)

    def ring(i, stacks):
        done = stacks + (jnp.maximum(i * PEER_NBUF - 1, 0),)
        for slot in range(PEER_NBUF):
            done = turn(i * PEER_NBUF + slot, slot, 3, done)
        return done[:2]

    zero = jnp.zeros((8, LANE), F32)
    assert nring * PEER_NBUF - 1 + PEER_AHEAD < tb
    stacks = (zero, zero)
    for s in range(nseg):
        stacks = lax.fori_loop(ring_cuts[s], ring_cuts[s + 1], ring, stacks)
        if s + 1 < nseg:
            retrieve(xq_ref[...], head_parts[s + 1])
    gat_ref[nxt] = gk_ref[...].T
    publish(nxt)
    done = stacks + (nring * PEER_NBUF - 1,)
    for t in range(nring * PEER_NBUF, tb - 2):
        done = turn(t, t % PEER_NBUF, 3, done)
    done = turn(tb - 2, (tb - 2) % PEER_NBUF, 2, done)
    done = turn(tb - 1, (tb - 1) % PEER_NBUF, 1, done)
    flush(done)

    @pl.when(step == pl.num_programs(0) - 1)
    def _():
        for t0 in range(PEER_AHEAD):
            wait_all((tb + t0) % PEER_NBUF)


def _peer_ffn(h, ln, wq, keys, slabs, tb):
    t, d = h.shape
    hk = PEER_HEADS * PEER_TOPK
    nblk = t // tb
    assert hk == LANE and tb == LANE and tb % PEER_NBUF == 0 and PEER_NBUF % 2 == 0 and d % LANE == 0
    assert PEER_AHEAD <= PEER_NBUF - 2
    assert slabs.shape[1:] == (d // LANE, LANE)
    assert wq.shape == (d, keys.shape[0] * keys.shape[2]) and keys.shape[2] == LANE
    return pl.pallas_call(
        _peer_expert_body,
        grid=(nblk,),
        in_specs=[pl.BlockSpec((tb, d), lambda i: (i, 0)),
                  pl.BlockSpec((tb, d), lambda i: (jnp.minimum(i + 1, nblk - 1), 0)),
                  pl.BlockSpec((1, d), lambda i: (0, 0)),
                  pl.BlockSpec(wq.shape, lambda i: (0, 0)),
                  pl.BlockSpec(keys.shape, lambda i: (0, 0, 0)),
                  pl.BlockSpec(memory_space=pl.ANY)],
        out_specs=pl.BlockSpec((tb, d), lambda i: (i, 0)),
        out_shape=jax.ShapeDtypeStruct((t, d), F32),
        scratch_shapes=[pltpu.VMEM((tb, d), F32),
                        pltpu.VMEM((2, hk, LANE), F32),
                        pltpu.VMEM((2, 16, hk), BF16),
                        pltpu.VMEM((PEER_NBUF, d // LANE, hk, LANE), jnp.uint32),
                        pltpu.SemaphoreType.DMA((PEER_NBUF,)),
                        pltpu.VMEM((hk, tb), I32),
                        pltpu.VMEM((hk, tb), F32),
                        pltpu.SMEM((2, hk, tb), I32),
                        pltpu.VMEM((2, tb, hk), F32),
                        pltpu.SemaphoreType.DMA(()),
                        pltpu.VMEM((tb, d), BF16)],
        compiler_params=_params("arbitrary"),
        name="peer_ffn",
    )(h, h, ln.reshape(1, d), wq, keys, slabs)


def _ple_body(x_ref, g_ref, wg_ref, p_ref, wp_ref, h_ref, o_ref, xn_ref):
    @pl.when(pl.program_id(1) == 0)
    def _():
        x = x_ref[...]
        ms = jnp.mean(x * x, axis=-1, keepdims=True)
        xn_ref[...] = (x * lax.rsqrt(ms + EPS) * g_ref[...]).astype(BF16)

    z = jnp.dot(xn_ref[...], wg_ref[...], preferred_element_type=F32)
    pp = jnp.dot(p_ref[...].astype(BF16), wp_ref[...], preferred_element_type=F32)
    o_ref[...] = h_ref[...] + jax.nn.sigmoid(z) * pp


def _ple(h, g, wg, p, wp, tm, tn):
    t, d = h.shape
    pd = p.shape[1]
    return pl.pallas_call(
        _ple_body,
        grid=(t // tm, d // tn),
        in_specs=[pl.BlockSpec((tm, d), lambda i, j: (i, 0)),
                  pl.BlockSpec((1, d), lambda i, j: (0, 0)),
                  pl.BlockSpec((d, tn), lambda i, j: (0, j)),
                  pl.BlockSpec((tm, pd), lambda i, j: (i, 0)),
                  pl.BlockSpec((pd, tn), lambda i, j: (0, j)),
                  pl.BlockSpec((tm, tn), lambda i, j: (i, j))],
        out_specs=pl.BlockSpec((tm, tn), lambda i, j: (i, j)),
        out_shape=jax.ShapeDtypeStruct((t, d), F32),
        scratch_shapes=[pltpu.VMEM((tm, d), BF16)],
        compiler_params=_params("parallel", "arbitrary"),
        name="ple_gate",
    )(h, g.reshape(1, d), wg, p, wp, h)


def _rmsnorm_body(x_ref, g_ref, o_ref):
    x = x_ref[...]
    ms = jnp.mean(x * x, axis=-1, keepdims=True)
    o_ref[...] = x * lax.rsqrt(ms + EPS) * g_ref[...]


def _rmsnorm(x, g, tm):
    t, d = x.shape
    return pl.pallas_call(
        _rmsnorm_body,
        grid=(t // tm,),
        in_specs=[pl.BlockSpec((tm, d), lambda i: (i, 0)), pl.BlockSpec((1, d), lambda i: (0, 0))],
        out_specs=pl.BlockSpec((tm, d), lambda i: (i, 0)),
        out_shape=jax.ShapeDtypeStruct((t, d), F32),
        compiler_params=_params("parallel"),
        name="final_rmsnorm",
    )(x, g.reshape(1, d))


def _toeplitz(vec, nq, nk):
    assert nq == nk
    lead = vec.shape[:-1]
    g = int(np.prod(lead))
    w = jnp.concatenate([jnp.zeros(lead + (1,), vec.dtype), vec[..., ::-1]], axis=-1).reshape(g, 1, 2 * nk)

    def body(w_ref, o_ref):
        rows = jnp.broadcast_to(w_ref[0], (nq, 2 * nk))
        o_ref[0] = pltpu.roll(rows, 0, 1, stride=1, stride_axis=0)[:, nk:]

    out = pl.pallas_call(
        body,
        grid=(g,),
        in_specs=[pl.BlockSpec((1, 1, 2 * nk), lambda i: (i, 0, 0))],
        out_specs=pl.BlockSpec((1, nq, nk), lambda i: (i, 0, 0)),
        out_shape=jax.ShapeDtypeStruct((g, nq, nk), vec.dtype),
        compiler_params=_params("parallel"),
        name="toeplitz_tiles",
    )(w)
    return out.reshape(lead + (nq, nk))


def _bias_tables(rel_bias, s):
    nd = s // BK
    heads = rel_bias.shape[1]
    bdt = rel_bias[_rel_bucket(jnp.arange(s))].astype(F32).T
    dd = (jnp.arange(nd)[:, None, None] * BK + jnp.arange(BQ)[None, :, None]
          - jnp.arange(BK)[None, None, :])
    ext = jnp.pad(bdt, ((0, 0), (BK - 1, 0)))
    segs = jnp.stack([ext[:, dl * BK:dl * BK + BQ + BK - 1] for dl in range(nd)], axis=1)
    tz = _toeplitz(segs, BQ, BK)
    causal = dd >= 0
    mult = sum(((dd % dil == 0) & (dd // dil <= window // dil)).astype(F32) for window, dil in DIL_PATTERNS)
    ok = causal & (mult > 0)
    tzd = jnp.where(ok, tz[:A_HEADS] + jnp.log(jnp.where(ok, mult, 1.0)), NEG)
    tzb = tz[A_HEADS:A_HEADS + B_HEADS]
    tzs = jnp.where(causal, tzb, NEG)
    nwin = -(-(NSA_WINDOW - 1) // BK) + 1
    tzw = jnp.where(causal & (dd <= NSA_WINDOW - 1), tzb, NEG)[:, :nwin]
    tzc = jnp.where(causal, tz[A_HEADS + B_HEADS:], NEG)
    ncmp = s // CMP_STRIDE
    na = s // CMP_STRIDE
    nvec = na + ncmp - 1
    lo = CMP_STRIDE * (ncmp - 1) + CMP_LEN - 1
    bdb = bdt[A_HEADS:A_HEADS + B_HEADS]
    gext = jnp.concatenate([jnp.full((B_HEADS, lo), NEG, F32), bdb], axis=1)
    vecs = gext[:, :CMP_STRIDE * nvec].reshape(B_HEADS, nvec, CMP_STRIDE).transpose(0, 2, 1)
    bcq = _toeplitz(vecs, na, ncmp)
    bcq = bcq.transpose(0, 2, 1, 3).reshape(B_HEADS, s, ncmp)
    bct = jnp.transpose(bcq, (0, 2, 1))
    nslc = s // SLC_LEN
    cstart = jnp.arange(ncmp) * CMP_STRIDE
    sstart = jnp.arange(nslc) * SLC_LEN
    covt = ((cstart[None, :] < sstart[:, None] + SLC_LEN)
            & (cstart[None, :] + CMP_LEN > sstart[:, None])).astype(F32)
    nblk = s // MOBA_BLK
    avg = ((jnp.arange(s)[None, :] // MOBA_BLK == jnp.arange(nblk)[:, None]).astype(F32)
           / MOBA_BLK).astype(BF16)
    return tzd, tzs, tzw, tzc, bct, covt, avg


def _pack_body(u_ref, v_ref, o_ref):
    _, te, d = u_ref.shape
    nlt = d // LANE
    ub = lax.bitcast_convert_type(u_ref[0].astype(BF16).astype(F32), jnp.uint32)
    vb = lax.bitcast_convert_type(v_ref[0].astype(BF16).astype(F32), jnp.uint32)
    word = lax.shift_right_logical(ub, jnp.uint32(16)) | vb
    for j in range(nlt):
        o_ref[pl.ds(j, te, stride=nlt), :] = word[:, j * LANE:(j + 1) * LANE]


def _pack_uv(u, v, layer, te=256):
    _, e, d = u.shape
    nlt = d // LANE
    spec = pl.BlockSpec((1, te, d), lambda i: (layer, i, 0))
    out = pl.pallas_call(
        _pack_body,
        grid=(e // te,),
        in_specs=[spec, spec],
        out_specs=pl.BlockSpec((te * nlt, LANE), lambda i: (i, 0)),
        out_shape=jax.ShapeDtypeStruct((e * nlt, LANE), jnp.uint32),
        compiler_params=_params("parallel"),
        name="pack_experts",
    )(u, v)
    return out.reshape(e, nlt, LANE)


GATE_COL0 = 3 * A_HEADS * LANE + B_HEADS * LANE + 6 * NSA_KV * LANE
GATE_COLS = 3 * B_HEADS


def _w_main_body(w_ref, o_ref):
    x = w_ref[0]
    o_ref[:, :GATE_COL0] = x[:, :GATE_COL0].astype(BF16)
    o_ref[:, GATE_COL0:] = x[:, GATE_COL0 + GATE_COLS:].astype(BF16)


def _reorder_w_in(w_in, layer, tk=256):
    _, d, cols = w_in.shape
    main = pl.pallas_call(
        _w_main_body,
        grid=(d // tk,),
        in_specs=[pl.BlockSpec((1, tk, cols), lambda i: (layer, i, 0))],
        out_specs=pl.BlockSpec((tk, MAIN_COLS), lambda i: (i, 0)),
        out_shape=jax.ShapeDtypeStruct((d, MAIN_COLS), BF16),
        compiler_params=_params("parallel"),
        name="reorder_w_in",
    )(w_in)
    per = GATE_COLS // NSA_KV
    wg = w_in[layer, :, GATE_COL0:GATE_COL0 + GATE_COLS]
    gate = jnp.concatenate([jnp.pad(wg[:, g * per:(g + 1) * per], ((0, 0), (0, LANE - per))) for g in range(NSA_KV)],
                           axis=1)
    return main, gate.astype(BF16)


def kernel(x, p, ln_mix, w_in, cmp_wk, cmp_wv, cmp_pos, out_norm, w_out, rel_bias, ln_ffn, peer_wq,
           peer_keys, peer_u, peer_v, ln_ple, ple_gate, ple_proj, ln_final):
    b, s, d = x.shape
    t = b * s
    depth = w_in.shape[0]
    assert s % BQ == 0 and BQ == BK == MOBA_BLK and d % LANE == 0
    tm = 1024 if t % 1024 == 0 else 512
    tzd, tzs, tzw, tzc, bct, covt, avg = _bias_tables(rel_bias, s)
    h = x.reshape(t, d)
    for i in range(depth):
        w_main, w_gate = _reorder_w_in(w_in, i)
        proj = _normmm(h, ln_mix[i], w_main, BF16, tm, 1024, "in_projection").reshape(b, s, MAIN_COLS)
        gates = _normmm(h, ln_mix[i], w_gate, F32, tm, w_gate.shape[1], "gate_projection")
        gates = gates.reshape(b, s, NSA_KV * LANE)
        oa = _dilated(proj, tzd)
        kcmp, vcmp = _compress(proj, cmp_wk[i], cmp_wv[i], cmp_pos[i])
        ob = _nsa(proj, gates, kcmp, vcmp, bct, covt, tzs, tzw)
        oc = _moba(proj, avg, tzc)
        h = _outproj(oa.reshape(t, -1), ob.reshape(t, -1), oc.reshape(t, -1), out_norm[i],
                     w_out[i].astype(BF16), h, tm, 1024)
        keys = peer_keys[i].reshape(PEER_HEADS * 2, PEER_NKEYS, -1)
        h = _peer_ffn(h, ln_ffn[i], peer_wq[i].astype(BF16), keys, _pack_uv(peer_u, peer_v, i), LANE)
        h = _ple(h, ln_ple[i], ple_gate[i].astype(BF16), p[i].reshape(t, -1), ple_proj[i].astype(BF16),
                 tm, 1024)
    return _rmsnorm(h, ln_final, tm).reshape(b, s, d)
```

```python
import functools
import math

import numpy as np
import jax
import jax.numpy as jnp
from jax import lax
from jax.experimental import pallas as pl
from jax.experimental.pallas import tpu as pltpu

F32 = jnp.float32
BF16 = jnp.bfloat16
I32 = jnp.int32

HEAD_DIM = 128
A_HEADS, B_HEADS, C_HEADS = 6, 6, 4
NSA_KV, NSA_GROUP = 2, 3
DIL_PATTERNS = ((128, 1), (512, 4), (2048, 16))
CMP_LEN, CMP_STRIDE = 32, 16
SLC_LEN, SLC_TOPN = 64, 8
NSA_WINDOW = 512
FORCE = 1e4
MOBA_BLK, MOBA_TOPK = 256, 3
REL_BUCKETS, REL_MAX_DIST = 32, 1024
PEER_HEADS, PEER_NKEYS, PEER_TOPK = 8, 128, 16
EPS = 1e-6

LANE = 128
BQ = 256
BK = 256
NEG = -1e30
HALF_NEG = -5e29
VMEM_LIMIT = 56 * 1024 * 1024

CB_QA, CB_KA, CB_VA = 0, 6, 12
CB_QB, CB_KC, CB_VC, CB_KS, CB_VS, CB_KW, CB_VW = 18, 24, 26, 28, 30, 32, 34
CB_QC, CB_KCC, CB_VCC = 36, 40, 44
MAIN_COLS = 48 * LANE


def _nt(a, b, precision=None):
    return lax.dot_general(a, b, (((1,), (1,)), ((), ())), preferred_element_type=F32,
                           precision=precision)


def _rel_bucket(dist):
    exact = REL_BUCKETS // 2
    d = jnp.maximum(dist, 0)
    logd = jnp.log(jnp.maximum(d, 1).astype(F32) / exact)
    large = exact + (logd / math.log(REL_MAX_DIST / exact) * (REL_BUCKETS - exact)).astype(I32)
    large = jnp.clip(large, exact, REL_BUCKETS - 1)
    return jnp.where(d < exact, d, large)


def _masked_softmax(s, axis):
    valid = s > HALF_NEG
    m = jnp.max(s, axis=axis, keepdims=True)
    e = jnp.where(valid, jnp.exp(s - m), 0.0)
    z = jnp.sum(e, axis=axis, keepdims=True)
    zs = jnp.where(z > 0, z, 1.0)
    return e * (1.0 / zs)


def _params(*sem):
    return pltpu.CompilerParams(dimension_semantics=sem, vmem_limit_bytes=VMEM_LIMIT)


def _normmm_body(x_ref, g_ref, w_ref, o_ref, xn_ref):
    @pl.when(pl.program_id(1) == 0)
    def _():
        x = x_ref[...]
        ms = jnp.mean(x * x, axis=-1, keepdims=True)
        xn_ref[...] = (x * lax.rsqrt(ms + EPS) * g_ref[...]).astype(BF16)

    o_ref[...] = jnp.dot(xn_ref[...], w_ref[...], preferred_element_type=F32).astype(o_ref.dtype)


def _normmm(x, g, w, out_dtype, tm, tn, name):
    t, d = x.shape
    n = w.shape[1]
    return pl.pallas_call(
        _normmm_body,
        grid=(t // tm, n // tn),
        in_specs=[pl.BlockSpec((tm, d), lambda i, j: (i, 0)),
                  pl.BlockSpec((1, d), lambda i, j: (0, 0)),
                  pl.BlockSpec((d, tn), lambda i, j: (0, j))],
        out_specs=pl.BlockSpec((tm, tn), lambda i, j: (i, j)),
        out_shape=jax.ShapeDtypeStruct((t, n), out_dtype),
        scratch_shapes=[pltpu.VMEM((tm, d), BF16)],
        compiler_params=_params("parallel", "arbitrary"),
        name=name,
    )(x, g.reshape(1, d), w)


def _flash(qs, k_ref, v_ref, lo, hi, score_fns, scale):
    def tile(kj):
        off = pl.multiple_of(kj * BK, BK)
        return k_ref[0, pl.ds(off, BK), :], v_ref[0, pl.ds(off, BK), :]

    def update(state, blocks):
        m, l, acc = state
        m_new = m
        for s, _ in blocks:
            m_new = jnp.maximum(m_new, jnp.max(s, axis=1, keepdims=True))
        alpha = jnp.exp(m - m_new)
        l = alpha * l
        acc = alpha * acc
        for s, v in blocks:
            p = jnp.exp(s - m_new)
            l = l + jnp.sum(p, axis=1, keepdims=True)
            acc = acc + jnp.dot(p.astype(BF16), v, preferred_element_type=F32)
        return m_new, l, acc

    def step(kjs, states):
        kv = [tile(kj) for kj in kjs]
        return tuple(
            update(states[i], [(score_fns[i](kj, _nt(q, k) * scale), v) for kj, (k, v) in zip(kjs, kv)])
            for i, q in enumerate(qs))

    states = tuple((jnp.full((BQ, 1), NEG, F32), jnp.zeros((BQ, 1), F32), jnp.zeros((BQ, HEAD_DIM), F32))
                   for _ in qs)
    start = lo
    for width in ((4, 2, 1) if len(qs) == 1 else (2, 1)):
        shift = width.bit_length() - 1
        count = lax.shift_right_logical(hi - start, shift)

        def group(p, st, start=start, width=width):
            return step([start + width * p + i for i in range(width)], st)

        states = lax.fori_loop(0, count, group, states)
        start = start + count * width
    return [acc / l for _, l, acc in states]


def _dilated_body(q_ref, k_ref, v_ref, tz_ref, o_ref):
    qi = pl.program_id(2)
    scale = HEAD_DIM ** -0.5

    def score(kj, s):
        return s + tz_ref[0, qi - kj]

    o_ref[0] = _flash([q_ref[0]], k_ref, v_ref, 0, qi + 1, [score], scale)[0]


def _dilated(proj, tzd):
    b, s, _ = proj.shape
    nd = s // BK
    return pl.pallas_call(
        _dilated_body,
        grid=(b, A_HEADS, s // BQ),
        in_specs=[pl.BlockSpec((1, BQ, LANE), lambda bi, h, qi: (bi, qi, CB_QA + h)),
                  pl.BlockSpec((1, s, LANE), lambda bi, h, qi: (bi, 0, CB_KA + h)),
                  pl.BlockSpec((1, s, LANE), lambda bi, h, qi: (bi, 0, CB_VA + h)),
                  pl.BlockSpec((1, nd, BQ, BK), lambda bi, h, qi: (h, 0, 0, 0))],
        out_specs=pl.BlockSpec((1, BQ, LANE), lambda bi, h, qi: (bi, qi, h)),
        out_shape=jax.ShapeDtypeStruct((b, s, A_HEADS * LANE), F32),
        compiler_params=_params("parallel", "parallel", "arbitrary"),
        name="dilated_attention",
    )(proj, proj, proj, tzd)


def _compress_body(kc_ref, vc_ref, wk_ref, wv_ref, pos_ref, ko_ref, vo_ref, xk_ref, xv_ref):
    s = kc_ref.shape[1]
    ncmp = ko_ref.shape[2]
    xk_ref[pl.ds(0, s), :] = kc_ref[0].astype(F32)
    xv_ref[pl.ds(0, s), :] = vc_ref[0].astype(F32)
    xk_ref[pl.ds(s, CMP_LEN), :] = jnp.zeros((CMP_LEN, HEAD_DIM), F32)
    xv_ref[pl.ds(s, CMP_LEN), :] = jnp.zeros((CMP_LEN, HEAD_DIM), F32)
    acck = jnp.zeros((ncmp, HEAD_DIM), F32)
    accv = jnp.zeros((ncmp, HEAD_DIM), F32)
    for l in range(CMP_LEN):
        pos = pos_ref[pl.ds(l, 1), :]
        rk = (xk_ref[pl.ds(l, ncmp, stride=CMP_STRIDE), :] + pos).astype(BF16)
        rv = (xv_ref[pl.ds(l, ncmp, stride=CMP_STRIDE), :] + pos).astype(BF16)
        acck = acck + jnp.dot(rk, wk_ref[l], preferred_element_type=F32)
        accv = accv + jnp.dot(rv, wv_ref[l], preferred_element_type=F32)
    ko_ref[0, 0] = acck.astype(BF16)
    vo_ref[0, 0] = accv.astype(BF16)


def _compress(proj, cmp_wk, cmp_wv, cmp_pos):
    b, s, _ = proj.shape
    ncmp = s // CMP_STRIDE
    out = jax.ShapeDtypeStruct((b, NSA_KV, ncmp, HEAD_DIM), BF16)
    return pl.pallas_call(
        _compress_body,
        grid=(b, NSA_KV),
        in_specs=[pl.BlockSpec((1, s, LANE), lambda bi, g: (bi, 0, CB_KC + g)),
                  pl.BlockSpec((1, s, LANE), lambda bi, g: (bi, 0, CB_VC + g)),
                  pl.BlockSpec((CMP_LEN, HEAD_DIM, HEAD_DIM), lambda bi, g: (0, 0, 0)),
                  pl.BlockSpec((CMP_LEN, HEAD_DIM, HEAD_DIM), lambda bi, g: (0, 0, 0)),
                  pl.BlockSpec((CMP_LEN, HEAD_DIM), lambda bi, g: (0, 0))],
        out_specs=[pl.BlockSpec((1, 1, ncmp, HEAD_DIM), lambda bi, g: (bi, g, 0, 0)),
                   pl.BlockSpec((1, 1, ncmp, HEAD_DIM), lambda bi, g: (bi, g, 0, 0))],
        out_shape=[out, out],
        scratch_shapes=[pltpu.VMEM((s + CMP_LEN, HEAD_DIM), F32),
                        pltpu.VMEM((s + CMP_LEN, HEAD_DIM), F32)],
        compiler_params=_params("parallel", "parallel"),
        name="nsa_compress",
    )(proj, proj, cmp_wk.astype(BF16), cmp_wv.astype(BF16), cmp_pos)


def _nsa_body(q_ref, kcmp_ref, vcmp_ref, ks_ref, vs_ref, kw_ref, vw_ref, gt_ref, bct_ref,
              covt_ref, tzs_ref, tzw_ref, o_ref, mask_ref):
    qi = pl.program_id(2)
    scale = HEAD_DIM ** -0.5
    nslc = covt_ref.shape[0]
    nkt = mask_ref.shape[0]
    kcmp = kcmp_ref[0, 0]
    vcmp = vcmp_ref[0, 0]
    gates = jax.nn.sigmoid(gt_ref[0])

    psum = jnp.zeros((kcmp.shape[0], BQ), F32)
    for r in range(NSA_GROUP):
        q = q_ref[0, :, r * LANE:(r + 1) * LANE]
        pt = _masked_softmax(_nt(kcmp, q) * scale + bct_ref[r], axis=0)
        o_cmp = lax.dot_general(pt.astype(BF16), vcmp, (((0,), (0,)), ((), ())),
                                preferred_element_type=F32)
        o_ref[0, :, r * LANE:(r + 1) * LANE] = gates[:, 3 * r:3 * r + 1] * o_cmp
        psum = psum + pt
    imp = jnp.dot(covt_ref[...], psum, preferred_element_type=F32,
                  precision=lax.Precision.HIGHEST)
    jb = lax.broadcasted_iota(I32, (nslc, BQ), 0)
    qpos = qi * BQ + lax.broadcasted_iota(I32, (nslc, BQ), 1)
    qblk = lax.shift_right_logical(qpos, int(math.log2(SLC_LEN)))
    forced = (jb == 0) | (jb == qblk) | (jb == qblk - 1)
    imp = jnp.where(forced, FORCE, jnp.where(jb > qblk, -FORCE, imp))
    beaten = jnp.zeros((nslc, BQ), F32)
    for jp in range(nslc):
        row = imp[jp:jp + 1, :]
        wins = (row > imp) | ((row == imp) & (jp < jb))
        beaten = beaten + wins.astype(F32)
    selt = (beaten < SLC_TOPN).astype(BF16)
    eye = (lax.broadcasted_iota(I32, (BQ, BQ), 0) == lax.broadcasted_iota(I32, (BQ, BQ), 1)).astype(BF16)
    selq = _nt(eye, selt).astype(BF16)
    kpos = lax.broadcasted_iota(I32, (nslc, BK), 1)
    kblk_row = lax.broadcasted_iota(I32, (nslc, BK), 0)

    for t in range(nkt):
        expand = (lax.shift_right_logical(t * BK + kpos, int(math.log2(SLC_LEN))) == kblk_row).astype(BF16)
        mask_ref[t] = (jnp.dot(selq, expand, preferred_element_type=F32) - 1.0) * (-NEG)

    qs = [q_ref[0, :, r * LANE:(r + 1) * LANE] for r in range(NSA_GROUP)]
    sel_fns = [lambda kj, s, r=r: s + tzs_ref[r, qi - kj] + mask_ref[kj] for r in range(NSA_GROUP)]
    win_fns = [lambda kj, s, r=r: s + tzw_ref[r, qi - kj] for r in range(NSA_GROUP)]
    nwin = tzw_ref.shape[1]
    o_sel = _flash(qs, ks_ref, vs_ref, 0, qi + 1, sel_fns, scale)
    o_win = _flash(qs, kw_ref, vw_ref, jnp.maximum(qi - (nwin - 1), 0), qi + 1, win_fns, scale)
    for r in range(NSA_GROUP):
        o_ref[0, :, r * LANE:(r + 1) * LANE] += (gates[:, 3 * r + 1:3 * r + 2] * o_sel[r]
                                                 + gates[:, 3 * r + 2:3 * r + 3] * o_win[r])


def _nsa(proj, gates, kcmp, vcmp, bct, covt, tzs, tzw):
    b, s, _ = proj.shape
    nd = s // BK
    ncmp = kcmp.shape[2]
    nslc = covt.shape[0]
    nwin = tzw.shape[1]
    gw = NSA_GROUP * LANE
    kv = lambda cb: pl.BlockSpec((1, s, LANE), lambda bi, g, qi: (bi, 0, cb + g))
    return pl.pallas_call(
        _nsa_body,
        grid=(b, NSA_KV, s // BQ),
        in_specs=[pl.BlockSpec((1, BQ, gw), lambda bi, g, qi: (bi, qi, CB_QB // NSA_GROUP + g)),
                  pl.BlockSpec((1, 1, ncmp, HEAD_DIM), lambda bi, g, qi: (bi, g, 0, 0)),
                  pl.BlockSpec((1, 1, ncmp, HEAD_DIM), lambda bi, g, qi: (bi, g, 0, 0)),
                  kv(CB_KS), kv(CB_VS), kv(CB_KW), kv(CB_VW),
                  pl.BlockSpec((1, BQ, LANE), lambda bi, g, qi: (bi, qi, g)),
                  pl.BlockSpec((NSA_GROUP, ncmp, BQ), lambda bi, g, qi: (g, 0, qi)),
                  pl.BlockSpec((nslc, ncmp), lambda bi, g, qi: (0, 0)),
                  pl.BlockSpec((NSA_GROUP, nd, BQ, BK), lambda bi, g, qi: (g, 0, 0, 0)),
                  pl.BlockSpec((NSA_GROUP, nwin, BQ, BK), lambda bi, g, qi: (g, 0, 0, 0))],
        out_specs=pl.BlockSpec((1, BQ, gw), lambda bi, g, qi: (bi, qi, g)),
        out_shape=jax.ShapeDtypeStruct((b, s, B_HEADS * LANE), F32),
        scratch_shapes=[pltpu.VMEM((nd, BQ, BK), F32)],
        compiler_params=_params("parallel", "parallel", "arbitrary"),
        name="nsa_attention",
    )(proj, kcmp, vcmp, proj, proj, proj, proj, gates, bct, covt, tzs, tzw)


def _moba_body(q_ref, k_ref, v_ref, avg_ref, tz_ref, o_ref, mask_ref):
    qi = pl.program_id(2)
    scale = HEAD_DIM ** -0.5
    nblk = avg_ref.shape[0]
    q = q_ref[0]
    kmean = jnp.dot(avg_ref[...], k_ref[0], preferred_element_type=F32)
    gate = _nt(kmean, q.astype(F32), precision=lax.Precision.HIGHEST)
    nb = lax.broadcasted_iota(I32, (nblk, BQ), 0)
    past = nb < qi
    beaten = jnp.zeros((nblk, BQ), F32)
    for np_ in range(nblk):
        row = gate[np_:np_ + 1, :]
        wins = (np_ < qi) & ((row > gate) | ((row == gate) & (np_ < nb)))
        beaten = beaten + wins.astype(F32)
    selt = (past & (beaten < MOBA_TOPK)).astype(BF16)
    eye = (lax.broadcasted_iota(I32, (BQ, BQ), 0) == lax.broadcasted_iota(I32, (BQ, BQ), 1)).astype(BF16)
    selq = _nt(eye, selt)
    for n in range(nblk):
        mask_ref[n] = jnp.broadcast_to(selq[:, n:n + 1], (BQ, BK))

    def score(kj, s):
        keep = (mask_ref[kj] > 0.5) | (kj == qi)
        return jnp.where(keep, s + tz_ref[0, qi - kj], NEG)

    o_ref[0] = _flash([q], k_ref, v_ref, 0, qi + 1, [score], scale)[0]


def _moba(proj, avg, tzc):
    b, s, _ = proj.shape
    nd = s // BK
    nblk = avg.shape[0]
    return pl.pallas_call(
        _moba_body,
        grid=(b, C_HEADS, s // BQ),
        in_specs=[pl.BlockSpec((1, BQ, LANE), lambda bi, h, qi: (bi, qi, CB_QC + h)),
                  pl.BlockSpec((1, s, LANE), lambda bi, h, qi: (bi, 0, CB_KCC + h)),
                  pl.BlockSpec((1, s, LANE), lambda bi, h, qi: (bi, 0, CB_VCC + h)),
                  pl.BlockSpec((nblk, s), lambda bi, h, qi: (0, 0)),
                  pl.BlockSpec((1, nd, BQ, BK), lambda bi, h, qi: (h, 0, 0, 0))],
        out_specs=pl.BlockSpec((1, BQ, LANE), lambda bi, h, qi: (bi, qi, h)),
        out_shape=jax.ShapeDtypeStruct((b, s, C_HEADS * LANE), F32),
        scratch_shapes=[pltpu.VMEM((nblk, BQ, BK), F32)],
        compiler_params=_params("parallel", "parallel", "arbitrary"),
        name="moba_attention",
    )(proj, proj, proj, avg, tzc)


def _outproj_body(oa_ref, ob_ref, oc_ref, g_ref, w_ref, h_ref, o_ref, y_ref):
    @pl.when(pl.program_id(1) == 0)
    def _():
        c0 = 0
        for ref in (oa_ref, ob_ref, oc_ref):
            x = ref[...]
            wd = x.shape[1]
            ms = jnp.mean(x * x, axis=-1, keepdims=True)
            y_ref[:, c0:c0 + wd] = (x * lax.rsqrt(ms + EPS) * g_ref[:, c0:c0 + wd]).astype(BF16)
            c0 += wd

    o_ref[...] = h_ref[...] + jnp.dot(y_ref[...], w_ref[...], preferred_element_type=F32)


def _outproj(oa, ob, oc, g, w, h, tm, tn):
    t, d = h.shape
    row = lambda a: pl.BlockSpec((tm, a.shape[1]), lambda i, j: (i, 0))
    return pl.pallas_call(
        _outproj_body,
        grid=(t // tm, d // tn),
        in_specs=[row(oa), row(ob), row(oc),
                  pl.BlockSpec((1, d), lambda i, j: (0, 0)),
                  pl.BlockSpec((d, tn), lambda i, j: (0, j)),
                  pl.BlockSpec((tm, tn), lambda i, j: (i, j))],
        out_specs=pl.BlockSpec((tm, tn), lambda i, j: (i, j)),
        out_shape=jax.ShapeDtypeStruct((t, d), F32),
        scratch_shapes=[pltpu.VMEM((tm, d), BF16)],
        compiler_params=_params("parallel", "arbitrary"),
        name="out_projection",
    )(oa, ob, oc, g.reshape(1, d), w, h)


def _topk_rows(x, k, payload=None):
    n, tm = x.shape
    rows = lax.broadcasted_iota(I32, x.shape, 0)
    slot = lax.broadcasted_iota(I32, (k, tm), 0)
    vals = jnp.zeros((k, tm), F32)
    idxs = jnp.zeros((k, tm), I32)
    for it in range(k):
        mx = jnp.max(x, axis=0, keepdims=True)
        idx = jnp.min(jnp.where(x == mx, rows, n), axis=0, keepdims=True)
        hit = rows == idx
        if payload is not None:
            idx = jnp.sum(jnp.where(hit, payload, 0), axis=0, keepdims=True)
        vals = jnp.where(slot == it, mx, vals)
        idxs = jnp.where(slot == it, idx, idxs)
        x = jnp.where(hit, -jnp.inf, x)
    return vals, idxs


def _retrieve(xn_ref, wq_ref, keys_ref, e_ref, g_ref, h0, h1):
    dq = keys_ref.shape[2]

    def head(h, carry):
        col = pl.multiple_of(h * (2 * dq), 2 * dq)
        qh = jnp.dot(xn_ref[...], wq_ref[:, pl.ds(col, 2 * dq)], preferred_element_type=F32)
        tops = []
        for c in range(2):
            sc = _nt(keys_ref[2 * h + c], qh[:, c * dq:(c + 1) * dq])
            tops.append(_topk_rows(sc, PEER_TOPK))
        (s0, i0), (s1, i1) = tops
        half = PEER_TOPK // 2
        sub = lax.broadcasted_iota(I32, (half, s0.shape[1]), 0)
        pieces = [s0[0:1] + s1]
        pieces_i = [i0[0:1] * PEER_NKEYS + i1]
        for a in range(1, half):
            keep = sub < PEER_TOPK // (a + 1)
            pieces.append(jnp.where(keep, s0[a:a + 1] + s1[0:half], -jnp.inf))
            pieces_i.append(i0[a:a + 1] * PEER_NKEYS + i1[0:half])
        pieces.append(s0[half:] + s1[0:1])
        pieces_i.append(i0[half:] * PEER_NKEYS + i1[0:1])
        cand = jnp.concatenate(pieces, axis=0)
        cand_i = jnp.concatenate(pieces_i, axis=0)
        bs, be = _topk_rows(cand, PEER_TOPK, payload=cand_i)
        e = jnp.exp(bs - jnp.max(bs, axis=0, keepdims=True))
        row = pl.multiple_of(h * PEER_TOPK, PEER_TOPK)
        g_ref[pl.ds(row, PEER_TOPK), :] = e / jnp.sum(e, axis=0, keepdims=True)
        e_ref[pl.ds(row, PEER_TOPK), :] = be
        return carry

    lax.fori_loop(h0, h1, head, 0)


PEER_NBUF = 16
PEER_AHEAD = 14
PEER_PIECE = 2


def _gelu(x):
    return 0.5 * x * (1.0 + lax.erf(x * (2.0 ** -0.5)))


def _hi_lo(x):
    hi = x.astype(BF16)
    lo = (x - hi.astype(F32)).astype(BF16)
    return jnp.concatenate([hi, lo], axis=1)


def _peer_expert_body(h_ref, hn_ref, ln_ref, wq_ref, keys_ref, uvw_ref, o_ref, xn_ref, acc_ref, lhs_ref,
                      buf_ref, sem_ref, idv_ref, gk_ref, ids_ref, gat_ref, csem, xq_ref):
    tb, d = h_ref.shape
    hk = gk_ref.shape[0]
    nrg = hk // 8
    nlt = d // LANE
    assert nrg == nlt and hk % (2 * nlt) == 0
    step = pl.program_id(0)
    cur, nxt = 0, 1
    x = h_ref[...]
    ms = jnp.mean(x * x, axis=-1, keepdims=True)
    xn_ref[...] = x * lax.rsqrt(ms + EPS) * ln_ref[...]
    g_ref = gat_ref.at[cur]

    def normalised(x_ref):
        xx = x_ref[...]
        return (xx * lax.rsqrt(jnp.mean(xx * xx, axis=-1, keepdims=True) + EPS) * ln_ref[...]).astype(BF16)

    def retrieve(h0, h1):
        _retrieve(xq_ref, wq_ref, keys_ref, idv_ref, gk_ref, h0, h1)

    def publish(par):
        copy = pltpu.make_async_copy(idv_ref, ids_ref.at[par], csem)
        copy.start()
        copy.wait()

    def issue(par, row, slot, k0, k1):
        for k in range(k0, k1):
            e = ids_ref[par, k, row]
            pltpu.make_async_copy(uvw_ref.at[e], buf_ref.at[slot, :, k, :],
                                  sem_ref.at[slot]).start(priority=k % 2)

    def issue_ahead(t, slot, k0, k1):
        row = t + PEER_AHEAD
        if isinstance(row, int) and row >= tb:
            issue(nxt, row - tb, slot, k0, k1)
        else:
            issue(cur, row, slot, k0, k1)

    def wait_all(slot):
        pltpu.make_async_copy(buf_ref.at[slot], buf_ref.at[slot], sem_ref.at[slot]).wait()

    def x_tiles(t):
        xrow = xn_ref[pl.ds(t, 1), :]
        return [jnp.broadcast_to(xrow[:, j * LANE:(j + 1) * LANE], (8, LANE)) for j in range(nlt)]

    def dots(xb, slot, r):
        a = None
        for j in range(nlt):
            w = buf_ref[slot, j, r * 8:(r + 1) * 8, :]
            pr = lax.bitcast_convert_type(lax.shift_left(w, jnp.uint32(16)), F32) * xb[j]
            a = pr if a is None else a + pr
        return a

    def coefficients(t, slot):
        act = jnp.sum(acc_ref[slot % 2].T, axis=0, keepdims=True)
        coef = jnp.broadcast_to(_gelu(act) * g_ref[pl.ds(t, 1), :], (8, hk))
        chi = coef.astype(BF16).astype(F32)
        return jnp.concatenate([chi, coef - chi], axis=0).astype(BF16)

    def weighted(lhs, slot, j):
        w = buf_ref[slot, j]
        v = lax.bitcast_convert_type(w & jnp.uint32(0xFFFF0000), F32).astype(BF16)
        yj = jnp.dot(lhs, v, preferred_element_type=F32)
        return yj[0:1, :] + yj[8:9, :]

    def flush(done):
        ya, yb, tp = done
        y = jnp.concatenate([ya[j:j + 1, :] for j in range(8)] + [yb[j:j + 1, :] for j in range(8)], axis=1)
        o_ref[pl.ds(tp, 1), :] = h_ref[pl.ds(tp, 1), :] + y

    def turn(t, slot, stages, done):
        tgt = (slot + PEER_AHEAD) % PEER_NBUF
        far = (slot + 2) % PEER_NBUF
        per = hk // (2 * nlt)
        if stages >= 3:
            wait_all(far)
            xb = x_tiles(t + 2)
        lhs = lhs_ref[slot % 2]
        ys, parts = [], []
        for c in range(nlt):
            if stages >= 3:
                parts.append(dots(xb, far, c))
            issue_ahead(t, tgt, 2 * c * per, (2 * c + 1) * per)
            if 2 * c < nlt:
                ys.append(weighted(lhs, slot, 2 * c))
                ys.append(weighted(lhs, slot, 2 * c + 1))
            if 2 * c == nlt and stages >= 2:
                lhs_next = coefficients(t + 1, slot + 1)
            issue_ahead(t, tgt, (2 * c + 1) * per, (2 * c + 2) * per)
        if stages >= 3:
            acc_ref[slot % 2] = jnp.concatenate(parts, axis=0)
        if stages >= 2:
            lhs_ref[(slot + 1) % 2] = lhs_next
        flush(done)
        return (jnp.concatenate(ys[:8], axis=0), jnp.concatenate(ys[8:], axis=0), t)

    @pl.when(step == 0)
    def _():
        xq_ref[...] = normalised(h_ref)
        retrieve(0, PEER_HEADS)
        gat_ref[nxt] = gk_ref[...].T

    publish(cur)
    gat_ref[cur] = gat_ref[nxt]

    @pl.when(step == 0)
    def _():
        for t0 in range(PEER_AHEAD):
            issue(cur, t0, t0, 0, hk)

    xq_ref[...] = normalised(hn_ref)
    nring = (tb - 2) // PEER_NBUF
    assert PEER_PIECE * (1 + (nring - 1) // 2) >= PEER_HEADS
    retrieve(0, PEER_PIECE)

    for t0 in range(2):
        wait_all(t0)
        xb0 = x_tiles(t0)
        acc_ref[t0] = jnp.concatenate([dots(xb0, t0, r) for r in range(nrg)], axis=0)
    lhs_ref[0] = coefficients(0, 0)

    def ring(i, stacks):
        done = stacks + (jnp.maximum(i * PEER_NBUF - 1, 0),)
        for slot in range(PEER_NBUF):
            done = turn(i * PEER_NBUF + slot, slot, 3, done)

        @pl.when(lax.rem(i, 2) == 1)
        def _():
            first = jnp.minimum(lax.div(i + 1, 2) * PEER_PIECE, PEER_HEADS)
            retrieve(first, jnp.minimum(first + PEER_PIECE, PEER_HEADS))

        return done[:2]

    zero = jnp.zeros((8, LANE), F32)
    assert nring * PEER_NBUF - 1 + PEER_AHEAD < tb
    stacks = lax.fori_loop(0, nring, ring, (zero, zero))
    gat_ref[nxt] = gk_ref[...].T
    publish(nxt)
    done = stacks + (nring * PEER_NBUF - 1,)
    for t in range(nring * PEER_NBUF, tb - 2):
        done = turn(t, t % PEER_NBUF, 3, done)
    done = turn(tb - 2, (tb - 2) % PEER_NBUF, 2, done)
    done = turn(tb - 1, (tb - 1) % PEER_NBUF, 1, done)
    flush(done)

    @pl.when(step == pl.num_programs(0) - 1)
    def _():
        for t0 in range(PEER_AHEAD):
            wait_all((tb + t0) % PEER_NBUF)


def _peer_ffn(h, ln, wq, keys, slabs, tb):
    t, d = h.shape
    hk = PEER_HEADS * PEER_TOPK
    nblk = t // tb
    assert hk == LANE and tb == LANE and tb % PEER_NBUF == 0 and PEER_NBUF % 2 == 0 and d % LANE == 0
    assert PEER_AHEAD <= PEER_NBUF - 2
    assert slabs.shape[1:] == (d // LANE, LANE)
    assert wq.shape == (d, keys.shape[0] * keys.shape[2]) and keys.shape[2] == LANE
    return pl.pallas_call(
        _peer_expert_body,
        grid=(nblk,),
        in_specs=[pl.BlockSpec((tb, d), lambda i: (i, 0)),
                  pl.BlockSpec((tb, d), lambda i: (jnp.minimum(i + 1, nblk - 1), 0)),
                  pl.BlockSpec((1, d), lambda i: (0, 0)),
                  pl.BlockSpec(wq.shape, lambda i: (0, 0)),
                  pl.BlockSpec(keys.shape, lambda i: (0, 0, 0)),
                  pl.BlockSpec(memory_space=pl.ANY)],
        out_specs=pl.BlockSpec((tb, d), lambda i: (i, 0)),
        out_shape=jax.ShapeDtypeStruct((t, d), F32),
        scratch_shapes=[pltpu.VMEM((tb, d), F32),
                        pltpu.VMEM((2, hk, LANE), F32),
                        pltpu.VMEM((2, 16, hk), BF16),
                        pltpu.VMEM((PEER_NBUF, d // LANE, hk, LANE), jnp.uint32),
                        pltpu.SemaphoreType.DMA((PEER_NBUF,)),
                        pltpu.VMEM((hk, tb), I32),
                        pltpu.VMEM((hk, tb), F32),
                        pltpu.SMEM((2, hk, tb), I32),
                        pltpu.VMEM((2, tb, hk), F32),
                        pltpu.SemaphoreType.DMA(()),
                        pltpu.VMEM((tb, d), BF16)],
        compiler_params=_params("arbitrary"),
        name="peer_ffn",
    )(h, h, ln.reshape(1, d), wq, keys, slabs)


def _ple_body(x_ref, g_ref, wg_ref, p_ref, wp_ref, h_ref, o_ref, xn_ref):
    @pl.when(pl.program_id(1) == 0)
    def _():
        x = x_ref[...]
        ms = jnp.mean(x * x, axis=-1, keepdims=True)
        xn_ref[...] = (x * lax.rsqrt(ms + EPS) * g_ref[...]).astype(BF16)

    z = jnp.dot(xn_ref[...], wg_ref[...], preferred_element_type=F32)
    pp = jnp.dot(p_ref[...].astype(BF16), wp_ref[...], preferred_element_type=F32)
    o_ref[...] = h_ref[...] + jax.nn.sigmoid(z) * pp


def _ple(h, g, wg, p, wp, tm, tn):
    t, d = h.shape
    pd = p.shape[1]
    return pl.pallas_call(
        _ple_body,
        grid=(t // tm, d // tn),
        in_specs=[pl.BlockSpec((tm, d), lambda i, j: (i, 0)),
                  pl.BlockSpec((1, d), lambda i, j: (0, 0)),
                  pl.BlockSpec((d, tn), lambda i, j: (0, j)),
                  pl.BlockSpec((tm, pd), lambda i, j: (i, 0)),
                  pl.BlockSpec((pd, tn), lambda i, j: (0, j)),
                  pl.BlockSpec((tm, tn), lambda i, j: (i, j))],
        out_specs=pl.BlockSpec((tm, tn), lambda i, j: (i, j)),
        out_shape=jax.ShapeDtypeStruct((t, d), F32),
        scratch_shapes=[pltpu.VMEM((tm, d), BF16)],
        compiler_params=_params("parallel", "arbitrary"),
        name="ple_gate",
    )(h, g.reshape(1, d), wg, p, wp, h)


def _rmsnorm_body(x_ref, g_ref, o_ref):
    x = x_ref[...]
    ms = jnp.mean(x * x, axis=-1, keepdims=True)
    o_ref[...] = x * lax.rsqrt(ms + EPS) * g_ref[...]


def _rmsnorm(x, g, tm):
    t, d = x.shape
    return pl.pallas_call(
        _rmsnorm_body,
        grid=(t // tm,),
        in_specs=[pl.BlockSpec((tm, d), lambda i: (i, 0)), pl.BlockSpec((1, d), lambda i: (0, 0))],
        out_specs=pl.BlockSpec((tm, d), lambda i: (i, 0)),
        out_shape=jax.ShapeDtypeStruct((t, d), F32),
        compiler_params=_params("parallel"),
        name="final_rmsnorm",
    )(x, g.reshape(1, d))


def _toeplitz(vec, nq, nk):
    assert nq == nk
    lead = vec.shape[:-1]
    g = int(np.prod(lead))
    w = jnp.concatenate([jnp.zeros(lead + (1,), vec.dtype), vec[..., ::-1]], axis=-1).reshape(g, 1, 2 * nk)

    def body(w_ref, o_ref):
        rows = jnp.broadcast_to(w_ref[0], (nq, 2 * nk))
        o_ref[0] = pltpu.roll(rows, 0, 1, stride=1, stride_axis=0)[:, nk:]

    out = pl.pallas_call(
        body,
        grid=(g,),
        in_specs=[pl.BlockSpec((1, 1, 2 * nk), lambda i: (i, 0, 0))],
        out_specs=pl.BlockSpec((1, nq, nk), lambda i: (i, 0, 0)),
        out_shape=jax.ShapeDtypeStruct((g, nq, nk), vec.dtype),
        compiler_params=_params("parallel"),
        name="toeplitz_tiles",
    )(w)
    return out.reshape(lead + (nq, nk))


def _bias_tables(rel_bias, s):
    nd = s // BK
    heads = rel_bias.shape[1]
    bdt = rel_bias[_rel_bucket(jnp.arange(s))].astype(F32).T
    dd = (jnp.arange(nd)[:, None, None] * BK + jnp.arange(BQ)[None, :, None]
          - jnp.arange(BK)[None, None, :])
    ext = jnp.pad(bdt, ((0, 0), (BK - 1, 0)))
    segs = jnp.stack([ext[:, dl * BK:dl * BK + BQ + BK - 1] for dl in range(nd)], axis=1)
    tz = _toeplitz(segs, BQ, BK)
    causal = dd >= 0
    mult = sum(((dd % dil == 0) & (dd // dil <= window // dil)).astype(F32) for window, dil in DIL_PATTERNS)
    ok = causal & (mult > 0)
    tzd = jnp.where(ok, tz[:A_HEADS] + jnp.log(jnp.where(ok, mult, 1.0)), NEG)
    tzb = tz[A_HEADS:A_HEADS + B_HEADS]
    tzs = jnp.where(causal, tzb, NEG)
    nwin = -(-(NSA_WINDOW - 1) // BK) + 1
    tzw = jnp.where(causal & (dd <= NSA_WINDOW - 1), tzb, NEG)[:, :nwin]
    tzc = jnp.where(causal, tz[A_HEADS + B_HEADS:], NEG)
    ncmp = s // CMP_STRIDE
    na = s // CMP_STRIDE
    nvec = na + ncmp - 1
    lo = CMP_STRIDE * (ncmp - 1) + CMP_LEN - 1
    bdb = bdt[A_HEADS:A_HEADS + B_HEADS]
    gext = jnp.concatenate([jnp.full((B_HEADS, lo), NEG, F32), bdb], axis=1)
    vecs = gext[:, :CMP_STRIDE * nvec].reshape(B_HEADS, nvec, CMP_STRIDE).transpose(0, 2, 1)
    bcq = _toeplitz(vecs, na, ncmp)
    bcq = bcq.transpose(0, 2, 1, 3).reshape(B_HEADS, s, ncmp)
    bct = jnp.transpose(bcq, (0, 2, 1))
    nslc = s // SLC_LEN
    cstart = jnp.arange(ncmp) * CMP_STRIDE
    sstart = jnp.arange(nslc) * SLC_LEN
    covt = ((cstart[None, :] < sstart[:, None] + SLC_LEN)
            & (cstart[None, :] + CMP_LEN > sstart[:, None])).astype(F32)
    nblk = s // MOBA_BLK
    avg = ((jnp.arange(s)[None, :] // MOBA_BLK == jnp.arange(nblk)[:, None]).astype(F32)
           / MOBA_BLK).astype(BF16)
    return tzd, tzs, tzw, tzc, bct, covt, avg


def _pack_body(u_ref, v_ref, o_ref):
    _, te, d = u_ref.shape
    nlt = d // LANE
    ub = lax.bitcast_convert_type(u_ref[0].astype(BF16).astype(F32), jnp.uint32)
    vb = lax.bitcast_convert_type(v_ref[0].astype(BF16).astype(F32), jnp.uint32)
    word = lax.shift_right_logical(ub, jnp.uint32(16)) | vb
    for j in range(nlt):
        o_ref[pl.ds(j, te, stride=nlt), :] = word[:, j * LANE:(j + 1) * LANE]


def _pack_uv(u, v, layer, te=256):
    _, e, d = u.shape
    nlt = d // LANE
    spec = pl.BlockSpec((1, te, d), lambda i: (layer, i, 0))
    out = pl.pallas_call(
        _pack_body,
        grid=(e // te,),
        in_specs=[spec, spec],
        out_specs=pl.BlockSpec((te * nlt, LANE), lambda i: (i, 0)),
        out_shape=jax.ShapeDtypeStruct((e * nlt, LANE), jnp.uint32),
        compiler_params=_params("parallel"),
        name="pack_experts",
    )(u, v)
    return out.reshape(e, nlt, LANE)


GATE_COL0 = 3 * A_HEADS * LANE + B_HEADS * LANE + 6 * NSA_KV * LANE
GATE_COLS = 3 * B_HEADS


def _w_main_body(w_ref, o_ref):
    x = w_ref[0]
    o_ref[:, :GATE_COL0] = x[:, :GATE_COL0].astype(BF16)
    o_ref[:, GATE_COL0:] = x[:, GATE_COL0 + GATE_COLS:].astype(BF16)


def _reorder_w_in(w_in, layer, tk=256):
    _, d, cols = w_in.shape
    main = pl.pallas_call(
        _w_main_body,
        grid=(d // tk,),
        in_specs=[pl.BlockSpec((1, tk, cols), lambda i: (layer, i, 0))],
        out_specs=pl.BlockSpec((tk, MAIN_COLS), lambda i: (i, 0)),
        out_shape=jax.ShapeDtypeStruct((d, MAIN_COLS), BF16),
        compiler_params=_params("parallel"),
        name="reorder_w_in",
    )(w_in)
    per = GATE_COLS // NSA_KV
    wg = w_in[layer, :, GATE_COL0:GATE_COL0 + GATE_COLS]
    gate = jnp.concatenate([jnp.pad(wg[:, g * per:(g + 1) * per], ((0, 0), (0, LANE - per))) for g in range(NSA_KV)],
                           axis=1)
    return main, gate.astype(BF16)


def kernel(x, p, ln_mix, w_in, cmp_wk, cmp_wv, cmp_pos, out_norm, w_out, rel_bias, ln_ffn, peer_wq,
           peer_keys, peer_u, peer_v, ln_ple, ple_gate, ple_proj, ln_final):
    b, s, d = x.shape
    t = b * s
    depth = w_in.shape[0]
    assert s % BQ == 0 and BQ == BK == MOBA_BLK and d % LANE == 0
    tm = 1024 if t % 1024 == 0 else 512
    tzd, tzs, tzw, tzc, bct, covt, avg = _bias_tables(rel_bias, s)
    h = x.reshape(t, d)
    for i in range(depth):
        w_main, w_gate = _reorder_w_in(w_in, i)
        proj = _normmm(h, ln_mix[i], w_main, BF16, tm, 1024, "in_projection").reshape(b, s, MAIN_COLS)
        gates = _normmm(h, ln_mix[i], w_gate, F32, tm, w_gate.shape[1], "gate_projection")
        gates = gates.reshape(b, s, NSA_KV * LANE)
        oa = _dilated(proj, tzd)
        kcmp, vcmp = _compress(proj, cmp_wk[i], cmp_wv[i], cmp_pos[i])
        ob = _nsa(proj, gates, kcmp, vcmp, bct, covt, tzs, tzw)
        oc = _moba(proj, avg, tzc)
        h = _outproj(oa.reshape(t, -1), ob.reshape(t, -1), oc.reshape(t, -1), out_norm[i],
                     w_out[i].astype(BF16), h, tm, 1024)
        keys = peer_keys[i].reshape(PEER_HEADS * 2, PEER_NKEYS, -1)
        h = _peer_ffn(h, ln_ffn[i], peer_wq[i].astype(BF16), keys, _pack_uv(peer_u, peer_v, i), LANE)
        h = _ple(h, ln_ple[i], ple_gate[i].astype(BF16), p[i].reshape(t, -1), ple_proj[i].astype(BF16),
                 tm, 1024)
    return _rmsnorm(h, ln_final, tm).reshape(b, s, d)
```

```python
import math

import numpy as np
import jax
import jax.numpy as jnp
from jax import lax
from jax.experimental import pallas as pl
from jax.experimental.pallas import tpu as pltpu

F32 = jnp.float32
BF16 = jnp.bfloat16
I32 = jnp.int32

HEAD_DIM = 128
A_HEADS, B_HEADS, C_HEADS = 6, 6, 4
NSA_KV, NSA_GROUP = 2, 3
DIL_PATTERNS = ((128, 1), (512, 4), (2048, 16))
CMP_LEN, CMP_STRIDE = 32, 16
SLC_LEN, SLC_TOPN = 64, 8
NSA_WINDOW = 512
FORCE = 1e4
MOBA_BLK, MOBA_TOPK = 256, 3
REL_BUCKETS, REL_MAX_DIST = 32, 1024
PEER_HEADS, PEER_NKEYS, PEER_TOPK = 8, 128, 16
EPS = 1e-6

LANE = 128
BQ = 256
BK = 256
NEG = -1e30
HALF_NEG = -5e29
VMEM_LIMIT = 56 * 1024 * 1024

CB_QA, CB_KA, CB_VA = 0, 6, 12
CB_QB, CB_KC, CB_VC, CB_KS, CB_VS, CB_KW, CB_VW = 18, 24, 26, 28, 30, 32, 34
CB_QC, CB_KCC, CB_VCC = 36, 40, 44
MAIN_COLS = 48 * LANE


def _nt(a, b, precision=None):
    return lax.dot_general(a, b, (((1,), (1,)), ((), ())), preferred_element_type=F32,
                           precision=precision)


def _rel_bucket(dist):
    exact = REL_BUCKETS // 2
    d = jnp.maximum(dist, 0)
    logd = jnp.log(jnp.maximum(d, 1).astype(F32) / exact)
    large = exact + (logd / math.log(REL_MAX_DIST / exact) * (REL_BUCKETS - exact)).astype(I32)
    large = jnp.clip(large, exact, REL_BUCKETS - 1)
    return jnp.where(d < exact, d, large)


def _masked_softmax(s, axis):
    valid = s > HALF_NEG
    m = jnp.max(s, axis=axis, keepdims=True)
    e = jnp.where(valid, jnp.exp(s - m), 0.0)
    z = jnp.sum(e, axis=axis, keepdims=True)
    zs = jnp.where(z > 0, z, 1.0)
    return e * (1.0 / zs)


def _params(*sem):
    return pltpu.CompilerParams(dimension_semantics=sem, vmem_limit_bytes=VMEM_LIMIT)


def _normmm_body(x_ref, g_ref, w_ref, o_ref, xn_ref):
    @pl.when(pl.program_id(1) == 0)
    def _():
        x = x_ref[...]
        ms = jnp.mean(x * x, axis=-1, keepdims=True)
        xn_ref[...] = (x * lax.rsqrt(ms + EPS) * g_ref[...]).astype(BF16)

    o_ref[...] = jnp.dot(xn_ref[...], w_ref[...], preferred_element_type=F32).astype(o_ref.dtype)


def _normmm(x, g, w, out_dtype, tm, tn, name):
    t, d = x.shape
    n = w.shape[1]
    return pl.pallas_call(
        _normmm_body,
        grid=(t // tm, n // tn),
        in_specs=[pl.BlockSpec((tm, d), lambda i, j: (i, 0)),
                  pl.BlockSpec((1, d), lambda i, j: (0, 0)),
                  pl.BlockSpec((d, tn), lambda i, j: (0, j))],
        out_specs=pl.BlockSpec((tm, tn), lambda i, j: (i, j)),
        out_shape=jax.ShapeDtypeStruct((t, n), out_dtype),
        scratch_shapes=[pltpu.VMEM((tm, d), BF16)],
        compiler_params=_params("parallel", "arbitrary"),
        name=name,
    )(x, g.reshape(1, d), w)


def _flash(qs, k_ref, v_ref, lo, hi, score_fns, scale):
    def tile(kj):
        off = pl.multiple_of(kj * BK, BK)
        return k_ref[0, pl.ds(off, BK), :], v_ref[0, pl.ds(off, BK), :]

    def update(state, blocks):
        m, l, acc = state
        m_new = m
        for s, _ in blocks:
            m_new = jnp.maximum(m_new, jnp.max(s, axis=1, keepdims=True))
        alpha = jnp.exp(m - m_new)
        l = alpha * l
        acc = alpha * acc
        for s, v in blocks:
            p = jnp.exp(s - m_new)
            l = l + jnp.sum(p, axis=1, keepdims=True)
            acc = acc + jnp.dot(p.astype(BF16), v, preferred_element_type=F32)
        return m_new, l, acc

    def step(kjs, states):
        kv = [tile(kj) for kj in kjs]
        return tuple(
            update(states[i], [(score_fns[i](kj, _nt(q, k) * scale), v) for kj, (k, v) in zip(kjs, kv)])
            for i, q in enumerate(qs))

    states = tuple((jnp.full((BQ, 1), NEG, F32), jnp.zeros((BQ, 1), F32), jnp.zeros((BQ, HEAD_DIM), F32))
                   for _ in qs)
    start = lo
    for width in ((4, 2, 1) if len(qs) == 1 else (2, 1)):
        shift = width.bit_length() - 1
        count = lax.shift_right_logical(hi - start, shift)

        def group(p, st, start=start, width=width):
            return step([start + width * p + i for i in range(width)], st)

        states = lax.fori_loop(0, count, group, states)
        start = start + count * width
    return [acc / l for _, l, acc in states]


def _dilated_body(q_ref, k_ref, v_ref, tz_ref, o_ref):
    qi = pl.program_id(2)
    scale = HEAD_DIM ** -0.5

    def score(kj, s):
        return s + tz_ref[0, qi - kj]

    o_ref[0] = _flash([q_ref[0]], k_ref, v_ref, 0, qi + 1, [score], scale)[0]


def _dilated(proj, tzd):
    b, s, _ = proj.shape
    nd = s // BK
    return pl.pallas_call(
        _dilated_body,
        grid=(b, A_HEADS, s // BQ),
        in_specs=[pl.BlockSpec((1, BQ, LANE), lambda bi, h, qi: (bi, qi, CB_QA + h)),
                  pl.BlockSpec((1, s, LANE), lambda bi, h, qi: (bi, 0, CB_KA + h)),
                  pl.BlockSpec((1, s, LANE), lambda bi, h, qi: (bi, 0, CB_VA + h)),
                  pl.BlockSpec((1, nd, BQ, BK), lambda bi, h, qi: (h, 0, 0, 0))],
        out_specs=pl.BlockSpec((1, BQ, LANE), lambda bi, h, qi: (bi, qi, h)),
        out_shape=jax.ShapeDtypeStruct((b, s, A_HEADS * LANE), F32),
        compiler_params=_params("parallel", "parallel", "arbitrary"),
        name="dilated_attention",
    )(proj, proj, proj, tzd)


def _compress_body(kc_ref, vc_ref, wk_ref, wv_ref, pos_ref, ko_ref, vo_ref, xk_ref, xv_ref):
    s = kc_ref.shape[1]
    ncmp = ko_ref.shape[2]
    xk_ref[pl.ds(0, s), :] = kc_ref[0].astype(F32)
    xv_ref[pl.ds(0, s), :] = vc_ref[0].astype(F32)
    xk_ref[pl.ds(s, CMP_LEN), :] = jnp.zeros((CMP_LEN, HEAD_DIM), F32)
    xv_ref[pl.ds(s, CMP_LEN), :] = jnp.zeros((CMP_LEN, HEAD_DIM), F32)
    acck = jnp.zeros((ncmp, HEAD_DIM), F32)
    accv = jnp.zeros((ncmp, HEAD_DIM), F32)
    for l in range(CMP_LEN):
        pos = pos_ref[pl.ds(l, 1), :]
        rk = (xk_ref[pl.ds(l, ncmp, stride=CMP_STRIDE), :] + pos).astype(BF16)
        rv = (xv_ref[pl.ds(l, ncmp, stride=CMP_STRIDE), :] + pos).astype(BF16)
        acck = acck + jnp.dot(rk, wk_ref[l], preferred_element_type=F32)
        accv = accv + jnp.dot(rv, wv_ref[l], preferred_element_type=F32)
    ko_ref[0, 0] = acck.astype(BF16)
    vo_ref[0, 0] = accv.astype(BF16)


def _compress(proj, cmp_wk, cmp_wv, cmp_pos):
    b, s, _ = proj.shape
    ncmp = s // CMP_STRIDE
    out = jax.ShapeDtypeStruct((b, NSA_KV, ncmp, HEAD_DIM), BF16)
    return pl.pallas_call(
        _compress_body,
        grid=(b, NSA_KV),
        in_specs=[pl.BlockSpec((1, s, LANE), lambda bi, g: (bi, 0, CB_KC + g)),
                  pl.BlockSpec((1, s, LANE), lambda bi, g: (bi, 0, CB_VC + g)),
                  pl.BlockSpec((CMP_LEN, HEAD_DIM, HEAD_DIM), lambda bi, g: (0, 0, 0)),
                  pl.BlockSpec((CMP_LEN, HEAD_DIM, HEAD_DIM), lambda bi, g: (0, 0, 0)),
                  pl.BlockSpec((CMP_LEN, HEAD_DIM), lambda bi, g: (0, 0))],
        out_specs=[pl.BlockSpec((1, 1, ncmp, HEAD_DIM), lambda bi, g: (bi, g, 0, 0)),
                   pl.BlockSpec((1, 1, ncmp, HEAD_DIM), lambda bi, g: (bi, g, 0, 0))],
        out_shape=[out, out],
        scratch_shapes=[pltpu.VMEM((s + CMP_LEN, HEAD_DIM), F32),
                        pltpu.VMEM((s + CMP_LEN, HEAD_DIM), F32)],
        compiler_params=_params("parallel", "parallel"),
        name="nsa_compress",
    )(proj, proj, cmp_wk.astype(BF16), cmp_wv.astype(BF16), cmp_pos)


def _nsa_body(q_ref, kcmp_ref, vcmp_ref, ks_ref, vs_ref, kw_ref, vw_ref, gt_ref, bct_ref,
              covt_ref, tzs_ref, tzw_ref, o_ref, mask_ref):
    qi = pl.program_id(2)
    scale = HEAD_DIM ** -0.5
    nslc = covt_ref.shape[0]
    nkt = mask_ref.shape[0]
    kcmp = kcmp_ref[0, 0]
    vcmp = vcmp_ref[0, 0]
    gates = jax.nn.sigmoid(gt_ref[0])

    psum = jnp.zeros((kcmp.shape[0], BQ), F32)
    for r in range(NSA_GROUP):
        q = q_ref[0, :, r * LANE:(r + 1) * LANE]
        pt = _masked_softmax(_nt(kcmp, q) * scale + bct_ref[r], axis=0)
        o_cmp = lax.dot_general(pt.astype(BF16), vcmp, (((0,), (0,)), ((), ())),
                                preferred_element_type=F32)
        o_ref[0, :, r * LANE:(r + 1) * LANE] = gates[:, 3 * r:3 * r + 1] * o_cmp
        psum = psum + pt
    imp = jnp.dot(covt_ref[...], psum, preferred_element_type=F32,
                  precision=lax.Precision.HIGHEST)
    jb = lax.broadcasted_iota(I32, (nslc, BQ), 0)
    qpos = qi * BQ + lax.broadcasted_iota(I32, (nslc, BQ), 1)
    qblk = lax.shift_right_logical(qpos, int(math.log2(SLC_LEN)))
    forced = (jb == 0) | (jb == qblk) | (jb == qblk - 1)
    imp = jnp.where(forced, FORCE, jnp.where(jb > qblk, -FORCE, imp))
    beaten = jnp.zeros((nslc, BQ), F32)
    for jp in range(nslc):
        row = imp[jp:jp + 1, :]
        wins = (row > imp) | ((row == imp) & (jp < jb))
        beaten = beaten + wins.astype(F32)
    selt = (beaten < SLC_TOPN).astype(BF16)
    eye = (lax.broadcasted_iota(I32, (BQ, BQ), 0) == lax.broadcasted_iota(I32, (BQ, BQ), 1)).astype(BF16)
    selq = _nt(eye, selt).astype(BF16)
    kpos = lax.broadcasted_iota(I32, (nslc, BK), 1)
    kblk_row = lax.broadcasted_iota(I32, (nslc, BK), 0)

    for t in range(nkt):
        expand = (lax.shift_right_logical(t * BK + kpos, int(math.log2(SLC_LEN))) == kblk_row).astype(BF16)
        mask_ref[t] = (jnp.dot(selq, expand, preferred_element_type=F32) - 1.0) * (-NEG)

    qs = [q_ref[0, :, r * LANE:(r + 1) * LANE] for r in range(NSA_GROUP)]
    sel_fns = [lambda kj, s, r=r: s + tzs_ref[r, qi - kj] + mask_ref[kj] for r in range(NSA_GROUP)]
    win_fns = [lambda kj, s, r=r: s + tzw_ref[r, qi - kj] for r in range(NSA_GROUP)]
    nwin = tzw_ref.shape[1]
    o_sel = _flash(qs, ks_ref, vs_ref, 0, qi + 1, sel_fns, scale)
    o_win = _flash(qs, kw_ref, vw_ref, jnp.maximum(qi - (nwin - 1), 0), qi + 1, win_fns, scale)
    for r in range(NSA_GROUP):
        o_ref[0, :, r * LANE:(r + 1) * LANE] += (gates[:, 3 * r + 1:3 * r + 2] * o_sel[r]
                                                 + gates[:, 3 * r + 2:3 * r + 3] * o_win[r])


def _nsa(proj, gates, kcmp, vcmp, bct, covt, tzs, tzw):
    b, s, _ = proj.shape
    nd = s // BK
    ncmp = kcmp.shape[2]
    nslc = covt.shape[0]
    nwin = tzw.shape[1]
    gw = NSA_GROUP * LANE
    kv = lambda cb: pl.BlockSpec((1, s, LANE), lambda bi, g, qi: (bi, 0, cb + g))
    return pl.pallas_call(
        _nsa_body,
        grid=(b, NSA_KV, s // BQ),
        in_specs=[pl.BlockSpec((1, BQ, gw), lambda bi, g, qi: (bi, qi, CB_QB // NSA_GROUP + g)),
                  pl.BlockSpec((1, 1, ncmp, HEAD_DIM), lambda bi, g, qi: (bi, g, 0, 0)),
                  pl.BlockSpec((1, 1, ncmp, HEAD_DIM), lambda bi, g, qi: (bi, g, 0, 0)),
                  kv(CB_KS), kv(CB_VS), kv(CB_KW), kv(CB_VW),
                  pl.BlockSpec((1, BQ, LANE), lambda bi, g, qi: (bi, qi, g)),
                  pl.BlockSpec((NSA_GROUP, ncmp, BQ), lambda bi, g, qi: (g, 0, qi)),
                  pl.BlockSpec((nslc, ncmp), lambda bi, g, qi: (0, 0)),
                  pl.BlockSpec((NSA_GROUP, nd, BQ, BK), lambda bi, g, qi: (g, 0, 0, 0)),
                  pl.BlockSpec((NSA_GROUP, nwin, BQ, BK), lambda bi, g, qi: (g, 0, 0, 0))],
        out_specs=pl.BlockSpec((1, BQ, gw), lambda bi, g, qi: (bi, qi, g)),
        out_shape=jax.ShapeDtypeStruct((b, s, B_HEADS * LANE), F32),
        scratch_shapes=[pltpu.VMEM((nd, BQ, BK), F32)],
        compiler_params=_params("parallel", "parallel", "arbitrary"),
        name="nsa_attention",
    )(proj, kcmp, vcmp, proj, proj, proj, proj, gates, bct, covt, tzs, tzw)


def _moba_body(q_ref, k_ref, v_ref, avg_ref, tz_ref, o_ref, mask_ref):
    qi = pl.program_id(2)
    scale = HEAD_DIM ** -0.5
    nblk = avg_ref.shape[0]
    q = q_ref[0]
    kmean = jnp.dot(avg_ref[...], k_ref[0], preferred_element_type=F32)
    gate = _nt(kmean, q.astype(F32), precision=lax.Precision.HIGHEST)
    nb = lax.broadcasted_iota(I32, (nblk, BQ), 0)
    past = nb < qi
    beaten = jnp.zeros((nblk, BQ), F32)
    for np_ in range(nblk):
        row = gate[np_:np_ + 1, :]
        wins = (np_ < qi) & ((row > gate) | ((row == gate) & (np_ < nb)))
        beaten = beaten + wins.astype(F32)
    selt = (past & (beaten < MOBA_TOPK)).astype(BF16)
    eye = (lax.broadcasted_iota(I32, (BQ, BQ), 0) == lax.broadcasted_iota(I32, (BQ, BQ), 1)).astype(BF16)
    selq = _nt(eye, selt)
    for n in range(nblk):
        mask_ref[n] = jnp.broadcast_to(selq[:, n:n + 1], (BQ, BK))

    def score(kj, s):
        keep = (mask_ref[kj] > 0.5) | (kj == qi)
        return jnp.where(keep, s + tz_ref[0, qi - kj], NEG)

    o_ref[0] = _flash([q], k_ref, v_ref, 0, qi + 1, [score], scale)[0]


def _moba(proj, avg, tzc):
    b, s, _ = proj.shape
    nd = s // BK
    nblk = avg.shape[0]
    return pl.pallas_call(
        _moba_body,
        grid=(b, C_HEADS, s // BQ),
        in_specs=[pl.BlockSpec((1, BQ, LANE), lambda bi, h, qi: (bi, qi, CB_QC + h)),
                  pl.BlockSpec((1, s, LANE), lambda bi, h, qi: (bi, 0, CB_KCC + h)),
                  pl.BlockSpec((1, s, LANE), lambda bi, h, qi: (bi, 0, CB_VCC + h)),
                  pl.BlockSpec((nblk, s), lambda bi, h, qi: (0, 0)),
                  pl.BlockSpec((1, nd, BQ, BK), lambda bi, h, qi: (h, 0, 0, 0))],
        out_specs=pl.BlockSpec((1, BQ, LANE), lambda bi, h, qi: (bi, qi, h)),
        out_shape=jax.ShapeDtypeStruct((b, s, C_HEADS * LANE), F32),
        scratch_shapes=[pltpu.VMEM((nblk, BQ, BK), F32)],
        compiler_params=_params("parallel", "parallel", "arbitrary"),
        name="moba_attention",
    )(proj, proj, proj, avg, tzc)


def _outproj_body(oa_ref, ob_ref, oc_ref, g_ref, w_ref, h_ref, o_ref, y_ref):
    @pl.when(pl.program_id(1) == 0)
    def _():
        c0 = 0
        for ref in (oa_ref, ob_ref, oc_ref):
            x = ref[...]
            wd = x.shape[1]
            ms = jnp.mean(x * x, axis=-1, keepdims=True)
            y_ref[:, c0:c0 + wd] = (x * lax.rsqrt(ms + EPS) * g_ref[:, c0:c0 + wd]).astype(BF16)
            c0 += wd

    o_ref[...] = h_ref[...] + jnp.dot(y_ref[...], w_ref[...], preferred_element_type=F32)


def _outproj(oa, ob, oc, g, w, h, tm, tn):
    t, d = h.shape
    row = lambda a: pl.BlockSpec((tm, a.shape[1]), lambda i, j: (i, 0))
    return pl.pallas_call(
        _outproj_body,
        grid=(t // tm, d // tn),
        in_specs=[row(oa), row(ob), row(oc),
                  pl.BlockSpec((1, d), lambda i, j: (0, 0)),
                  pl.BlockSpec((d, tn), lambda i, j: (0, j)),
                  pl.BlockSpec((tm, tn), lambda i, j: (i, j))],
        out_specs=pl.BlockSpec((tm, tn), lambda i, j: (i, j)),
        out_shape=jax.ShapeDtypeStruct((t, d), F32),
        scratch_shapes=[pltpu.VMEM((tm, d), BF16)],
        compiler_params=_params("parallel", "arbitrary"),
        name="out_projection",
    )(oa, ob, oc, g.reshape(1, d), w, h)


def _topk_rows(x, k, payload=None):
    n, tm = x.shape
    rows = lax.broadcasted_iota(I32, x.shape, 0)
    slot = lax.broadcasted_iota(I32, (k, tm), 0)
    vals = jnp.zeros((k, tm), F32)
    idxs = jnp.zeros((k, tm), I32)
    for it in range(k):
        mx = jnp.max(x, axis=0, keepdims=True)
        idx = jnp.min(jnp.where(x == mx, rows, n), axis=0, keepdims=True)
        hit = rows == idx
        if payload is not None:
            idx = jnp.sum(jnp.where(hit, payload, 0), axis=0, keepdims=True)
        vals = jnp.where(slot == it, mx, vals)
        idxs = jnp.where(slot == it, idx, idxs)
        x = jnp.where(hit, -jnp.inf, x)
    return vals, idxs


def _peer_topk_body(h_ref, ln_ref, wq_ref, keys_ref, e_ref, g_ref):
    x = h_ref[...]
    ms = jnp.mean(x * x, axis=-1, keepdims=True)
    xn = (x * lax.rsqrt(ms + EPS) * ln_ref[...]).astype(BF16)
    dq = keys_ref.shape[2]
    for h in range(PEER_HEADS):
        qh = jnp.dot(xn, wq_ref[:, 2 * h * dq:2 * (h + 1) * dq], preferred_element_type=F32)
        tops = []
        for c in range(2):
            sc = _nt(keys_ref[2 * h + c], qh[:, c * dq:(c + 1) * dq])
            tops.append(_topk_rows(sc, PEER_TOPK))
        (s0, i0), (s1, i1) = tops
        half = PEER_TOPK // 2
        sub = lax.broadcasted_iota(I32, (half, s0.shape[1]), 0)
        pieces = [s0[0:1] + s1]
        pieces_i = [i0[0:1] * PEER_NKEYS + i1]
        for a in range(1, half):
            keep = sub < PEER_TOPK // (a + 1)
            pieces.append(jnp.where(keep, s0[a:a + 1] + s1[0:half], -jnp.inf))
            pieces_i.append(i0[a:a + 1] * PEER_NKEYS + i1[0:half])
        pieces.append(s0[half:] + s1[0:1])
        pieces_i.append(i0[half:] * PEER_NKEYS + i1[0:1])
        cand = jnp.concatenate(pieces, axis=0)
        cand_i = jnp.concatenate(pieces_i, axis=0)
        bs, be = _topk_rows(cand, PEER_TOPK, payload=cand_i)
        e = jnp.exp(bs - jnp.max(bs, axis=0, keepdims=True))
        g_ref[h * PEER_TOPK:(h + 1) * PEER_TOPK, :] = e / jnp.sum(e, axis=0, keepdims=True)
        e_ref[h * PEER_TOPK:(h + 1) * PEER_TOPK, :] = be


def _peer_topk(h, ln, wq, keys, tm):
    t, d = h.shape
    hk = PEER_HEADS * PEER_TOPK
    assert wq.shape == (d, keys.shape[0] * keys.shape[2]) and keys.shape[2] == LANE
    return pl.pallas_call(
        _peer_topk_body,
        grid=(t // tm,),
        in_specs=[pl.BlockSpec((tm, d), lambda i: (i, 0)),
                  pl.BlockSpec((1, d), lambda i: (0, 0)),
                  pl.BlockSpec(wq.shape, lambda i: (0, 0)),
                  pl.BlockSpec(keys.shape, lambda i: (0, 0, 0))],
        out_specs=[pl.BlockSpec((hk, tm), lambda i: (0, i)),
                   pl.BlockSpec((hk, tm), lambda i: (0, i))],
        out_shape=[jax.ShapeDtypeStruct((hk, t), I32), jax.ShapeDtypeStruct((hk, t), F32)],
        compiler_params=_params("parallel"),
        name="peer_topk",
    )(h, ln.reshape(1, d), wq, keys)


PEER_NBUF = 16
PEER_AHEAD = 14


def _gelu(x):
    return 0.5 * x * (1.0 + lax.erf(x * (2.0 ** -0.5)))


def _peer_expert_body(ids0_ref, idsn_ref, h_ref, ln_ref, g_ref, uvw_ref, o_ref, xn_ref, acc_ref, lhs_ref,
                      buf_ref, sem_ref):
    tb, d = h_ref.shape
    hk = idsn_ref.shape[1]
    nrg = hk // 8
    nlt = d // LANE
    assert nrg == nlt and hk % (2 * nlt) == 0
    step = pl.program_id(0)
    x = h_ref[...]
    ms = jnp.mean(x * x, axis=-1, keepdims=True)
    xn_ref[...] = x * lax.rsqrt(ms + EPS) * ln_ref[...]

    def issue(ids_ref, row, slot, k0, k1):
        for k in range(k0, k1):
            e = ids_ref[row, k]
            pltpu.make_async_copy(uvw_ref.at[e], buf_ref.at[slot, :, k, :],
                                  sem_ref.at[slot]).start(priority=k % 2)

    def wait_all(slot):
        pltpu.make_async_copy(buf_ref.at[slot], buf_ref.at[slot], sem_ref.at[slot]).wait()

    def x_tiles(t):
        xrow = xn_ref[pl.ds(t, 1), :]
        return [jnp.broadcast_to(xrow[:, j * LANE:(j + 1) * LANE], (8, LANE)) for j in range(nlt)]

    def dots(xb, slot, r):
        a = None
        for j in range(nlt):
            w = buf_ref[slot, j, r * 8:(r + 1) * 8, :]
            pr = lax.bitcast_convert_type(lax.shift_left(w, jnp.uint32(16)), F32) * xb[j]
            a = pr if a is None else a + pr
        return a

    def coefficients(t, slot):
        act = jnp.sum(acc_ref[slot % 2].T, axis=0, keepdims=True)
        coef = jnp.broadcast_to(_gelu(act) * g_ref[pl.ds(t, 1), :], (8, hk))
        chi = coef.astype(BF16).astype(F32)
        return jnp.concatenate([chi, coef - chi], axis=0).astype(BF16)

    def weighted(lhs, slot, j):
        w = buf_ref[slot, j]
        v = lax.bitcast_convert_type(w & jnp.uint32(0xFFFF0000), F32).astype(BF16)
        yj = jnp.dot(lhs, v, preferred_element_type=F32)
        return yj[0:1, :] + yj[8:9, :]

    def flush(done):
        ya, yb, tp = done
        y = jnp.concatenate([ya[j:j + 1, :] for j in range(8)] + [yb[j:j + 1, :] for j in range(8)], axis=1)
        o_ref[pl.ds(tp, 1), :] = h_ref[pl.ds(tp, 1), :] + y

    def turn(t, slot, stages, done):
        tgt = (slot + PEER_AHEAD) % PEER_NBUF
        far = (slot + 2) % PEER_NBUF
        per = hk // (2 * nlt)
        if stages >= 3:
            wait_all(far)
            xb = x_tiles(t + 2)
        lhs = lhs_ref[slot % 2]
        ys, parts = [], []
        for c in range(nlt):
            if stages >= 3:
                parts.append(dots(xb, far, c))
            issue(idsn_ref, t, tgt, 2 * c * per, (2 * c + 1) * per)
            if 2 * c < nlt:
                ys.append(weighted(lhs, slot, 2 * c))
                ys.append(weighted(lhs, slot, 2 * c + 1))
            if 2 * c == nlt and stages >= 2:
                lhs_next = coefficients(t + 1, slot + 1)
            issue(idsn_ref, t, tgt, (2 * c + 1) * per, (2 * c + 2) * per)
        if stages >= 3:
            acc_ref[slot % 2] = jnp.concatenate(parts, axis=0)
        if stages >= 2:
            lhs_ref[(slot + 1) % 2] = lhs_next
        flush(done)
        return (jnp.concatenate(ys[:8], axis=0), jnp.concatenate(ys[8:], axis=0), t)

    @pl.when(step == 0)
    def _():
        for t0 in range(PEER_AHEAD):
            issue(ids0_ref, t0, t0, 0, hk)

    for t0 in range(2):
        wait_all(t0)
        xb0 = x_tiles(t0)
        acc_ref[t0] = jnp.concatenate([dots(xb0, t0, r) for r in range(nrg)], axis=0)
    lhs_ref[0] = coefficients(0, 0)

    def ring(i, stacks):
        done = stacks + (jnp.maximum(i * PEER_NBUF - 1, 0),)
        for slot in range(PEER_NBUF):
            done = turn(i * PEER_NBUF + slot, slot, 3, done)
        return done[:2]

    zero = jnp.zeros((8, LANE), F32)
    nring = (tb - 2) // PEER_NBUF
    done = lax.fori_loop(0, nring, ring, (zero, zero)) + (nring * PEER_NBUF - 1,)
    for t in range(nring * PEER_NBUF, tb - 2):
        done = turn(t, t % PEER_NBUF, 3, done)
    done = turn(tb - 2, (tb - 2) % PEER_NBUF, 2, done)
    done = turn(tb - 1, (tb - 1) % PEER_NBUF, 1, done)
    flush(done)

    @pl.when(step == pl.num_programs(0) - 1)
    def _():
        for t0 in range(PEER_AHEAD):
            wait_all((tb + t0) % PEER_NBUF)


def _peer_experts(ids, gates, h, ln, slabs, tb):
    t, d = h.shape
    hk = ids.shape[1]
    assert hk == LANE and tb % PEER_NBUF == 0 and PEER_NBUF % 2 == 0 and d % LANE == 0
    assert PEER_AHEAD <= PEER_NBUF - 2
    ids_next = jnp.concatenate([ids[PEER_AHEAD:], ids[:PEER_AHEAD]], axis=0)
    assert slabs.shape[1:] == (d // LANE, LANE)
    return pl.pallas_call(
        _peer_expert_body,
        grid=(t // tb,),
        in_specs=[pl.BlockSpec((PEER_NBUF, hk), lambda i: (0, 0), memory_space=pltpu.SMEM),
                  pl.BlockSpec((tb, hk), lambda i: (i, 0), memory_space=pltpu.SMEM),
                  pl.BlockSpec((tb, d), lambda i: (i, 0)),
                  pl.BlockSpec((1, d), lambda i: (0, 0)),
                  pl.BlockSpec((tb, hk), lambda i: (i, 0)),
                  pl.BlockSpec(memory_space=pl.ANY)],
        out_specs=pl.BlockSpec((tb, d), lambda i: (i, 0)),
        out_shape=jax.ShapeDtypeStruct((t, d), F32),
        scratch_shapes=[pltpu.VMEM((tb, d), F32),
                        pltpu.VMEM((2, hk, LANE), F32),
                        pltpu.VMEM((2, 16, hk), BF16),
                        pltpu.VMEM((PEER_NBUF, d // LANE, hk, LANE), jnp.uint32),
                        pltpu.SemaphoreType.DMA((PEER_NBUF,))],
        compiler_params=_params("arbitrary"),
        name="peer_experts",
    )(ids[:PEER_NBUF], ids_next, h, ln.reshape(1, d), gates, slabs)


def _ple_body(x_ref, g_ref, wg_ref, p_ref, wp_ref, h_ref, o_ref, xn_ref):
    @pl.when(pl.program_id(1) == 0)
    def _():
        x = x_ref[...]
        ms = jnp.mean(x * x, axis=-1, keepdims=True)
        xn_ref[...] = (x * lax.rsqrt(ms + EPS) * g_ref[...]).astype(BF16)

    z = jnp.dot(xn_ref[...], wg_ref[...], preferred_element_type=F32)
    pp = jnp.dot(p_ref[...].astype(BF16), wp_ref[...], preferred_element_type=F32)
    o_ref[...] = h_ref[...] + jax.nn.sigmoid(z) * pp


def _ple(h, g, wg, p, wp, tm, tn):
    t, d = h.shape
    pd = p.shape[1]
    return pl.pallas_call(
        _ple_body,
        grid=(t // tm, d // tn),
        in_specs=[pl.BlockSpec((tm, d), lambda i, j: (i, 0)),
                  pl.BlockSpec((1, d), lambda i, j: (0, 0)),
                  pl.BlockSpec((d, tn), lambda i, j: (0, j)),
                  pl.BlockSpec((tm, pd), lambda i, j: (i, 0)),
                  pl.BlockSpec((pd, tn), lambda i, j: (0, j)),
                  pl.BlockSpec((tm, tn), lambda i, j: (i, j))],
        out_specs=pl.BlockSpec((tm, tn), lambda i, j: (i, j)),
        out_shape=jax.ShapeDtypeStruct((t, d), F32),
        scratch_shapes=[pltpu.VMEM((tm, d), BF16)],
        compiler_params=_params("parallel", "arbitrary"),
        name="ple_gate",
    )(h, g.reshape(1, d), wg, p, wp, h)


def _rmsnorm_body(x_ref, g_ref, o_ref):
    x = x_ref[...]
    ms = jnp.mean(x * x, axis=-1, keepdims=True)
    o_ref[...] = x * lax.rsqrt(ms + EPS) * g_ref[...]


def _rmsnorm(x, g, tm):
    t, d = x.shape
    return pl.pallas_call(
        _rmsnorm_body,
        grid=(t // tm,),
        in_specs=[pl.BlockSpec((tm, d), lambda i: (i, 0)), pl.BlockSpec((1, d), lambda i: (0, 0))],
        out_specs=pl.BlockSpec((tm, d), lambda i: (i, 0)),
        out_shape=jax.ShapeDtypeStruct((t, d), F32),
        compiler_params=_params("parallel"),
        name="final_rmsnorm",
    )(x, g.reshape(1, d))


def _toeplitz(vec, nq, nk):
    assert nq == nk
    lead = vec.shape[:-1]
    g = int(np.prod(lead))
    w = jnp.concatenate([jnp.zeros(lead + (1,), vec.dtype), vec[..., ::-1]], axis=-1).reshape(g, 1, 2 * nk)

    def body(w_ref, o_ref):
        rows = jnp.broadcast_to(w_ref[0], (nq, 2 * nk))
        o_ref[0] = pltpu.roll(rows, 0, 1, stride=1, stride_axis=0)[:, nk:]

    out = pl.pallas_call(
        body,
        grid=(g,),
        in_specs=[pl.BlockSpec((1, 1, 2 * nk), lambda i: (i, 0, 0))],
        out_specs=pl.BlockSpec((1, nq, nk), lambda i: (i, 0, 0)),
        out_shape=jax.ShapeDtypeStruct((g, nq, nk), vec.dtype),
        compiler_params=_params("parallel"),
        name="toeplitz_tiles",
    )(w)
    return out.reshape(lead + (nq, nk))


def _bias_tables(rel_bias, s):
    nd = s // BK
    bdt = rel_bias[_rel_bucket(jnp.arange(s))].astype(F32).T
    dd = (jnp.arange(nd)[:, None, None] * BK + jnp.arange(BQ)[None, :, None]
          - jnp.arange(BK)[None, None, :])
    ext = jnp.pad(bdt, ((0, 0), (BK - 1, 0)))
    segs = jnp.stack([ext[:, dl * BK:dl * BK + BQ + BK - 1] for dl in range(nd)], axis=1)
    tz = _toeplitz(segs, BQ, BK)
    causal = dd >= 0
    mult = sum(((dd % dil == 0) & (dd // dil <= window // dil)).astype(F32) for window, dil in DIL_PATTERNS)
    ok = causal & (mult > 0)
    tzd = jnp.where(ok, tz[:A_HEADS] + jnp.log(jnp.where(ok, mult, 1.0)), NEG)
    tzb = tz[A_HEADS:A_HEADS + B_HEADS]
    tzs = jnp.where(causal, tzb, NEG)
    nwin = -(-(NSA_WINDOW - 1) // BK) + 1
    tzw = jnp.where(causal & (dd <= NSA_WINDOW - 1), tzb, NEG)[:, :nwin]
    tzc = jnp.where(causal, tz[A_HEADS + B_HEADS:], NEG)
    ncmp = s // CMP_STRIDE
    na = s // CMP_STRIDE
    nvec = na + ncmp - 1
    lo = CMP_STRIDE * (ncmp - 1) + CMP_LEN - 1
    bdb = bdt[A_HEADS:A_HEADS + B_HEADS]
    gext = jnp.concatenate([jnp.full((B_HEADS, lo), NEG, F32), bdb], axis=1)
    vecs = gext[:, :CMP_STRIDE * nvec].reshape(B_HEADS, nvec, CMP_STRIDE).transpose(0, 2, 1)
    bcq = _toeplitz(vecs, na, ncmp)
    bcq = bcq.transpose(0, 2, 1, 3).reshape(B_HEADS, s, ncmp)
    bct = jnp.transpose(bcq, (0, 2, 1))
    nslc = s // SLC_LEN
    cstart = jnp.arange(ncmp) * CMP_STRIDE
    sstart = jnp.arange(nslc) * SLC_LEN
    covt = ((cstart[None, :] < sstart[:, None] + SLC_LEN)
            & (cstart[None, :] + CMP_LEN > sstart[:, None])).astype(F32)
    nblk = s // MOBA_BLK
    avg = ((jnp.arange(s)[None, :] // MOBA_BLK == jnp.arange(nblk)[:, None]).astype(F32)
           / MOBA_BLK).astype(BF16)
    return tzd, tzs, tzw, tzc, bct, covt, avg


def _pack_body(u_ref, v_ref, o_ref):
    _, te, d = u_ref.shape
    nlt = d // LANE
    ub = lax.bitcast_convert_type(u_ref[0].astype(BF16).astype(F32), jnp.uint32)
    vb = lax.bitcast_convert_type(v_ref[0].astype(BF16).astype(F32), jnp.uint32)
    word = lax.shift_right_logical(ub, jnp.uint32(16)) | vb
    for j in range(nlt):
        o_ref[pl.ds(j, te, stride=nlt), :] = word[:, j * LANE:(j + 1) * LANE]


def _pack_uv(u, v, layer, te=256):
    _, e, d = u.shape
    nlt = d // LANE
    spec = pl.BlockSpec((1, te, d), lambda i: (layer, i, 0))
    out = pl.pallas_call(
        _pack_body,
        grid=(e // te,),
        in_specs=[spec, spec],
        out_specs=pl.BlockSpec((te * nlt, LANE), lambda i: (i, 0)),
        out_shape=jax.ShapeDtypeStruct((e * nlt, LANE), jnp.uint32),
        compiler_params=_params("parallel"),
        name="pack_experts",
    )(u, v)
    return out.reshape(e, nlt, LANE)


GATE_COL0 = 3 * A_HEADS * LANE + B_HEADS * LANE + 6 * NSA_KV * LANE
GATE_COLS = 3 * B_HEADS


def _w_main_body(w_ref, o_ref):
    x = w_ref[0]
    o_ref[:, :GATE_COL0] = x[:, :GATE_COL0].astype(BF16)
    o_ref[:, GATE_COL0:] = x[:, GATE_COL0 + GATE_COLS:].astype(BF16)


def _reorder_w_in(w_in, layer, tk=256):
    _, d, cols = w_in.shape
    main = pl.pallas_call(
        _w_main_body,
        grid=(d // tk,),
        in_specs=[pl.BlockSpec((1, tk, cols), lambda i: (layer, i, 0))],
        out_specs=pl.BlockSpec((tk, MAIN_COLS), lambda i: (i, 0)),
        out_shape=jax.ShapeDtypeStruct((d, MAIN_COLS), BF16),
        compiler_params=_params("parallel"),
        name="reorder_w_in",
    )(w_in)
    per = GATE_COLS // NSA_KV
    wg = w_in[layer, :, GATE_COL0:GATE_COL0 + GATE_COLS]
    gate = jnp.concatenate([jnp.pad(wg[:, g * per:(g + 1) * per], ((0, 0), (0, LANE - per))) for g in range(NSA_KV)],
                           axis=1)
    return main, gate.astype(BF16)


def kernel(x, p, ln_mix, w_in, cmp_wk, cmp_wv, cmp_pos, out_norm, w_out, rel_bias, ln_ffn, peer_wq,
           peer_keys, peer_u, peer_v, ln_ple, ple_gate, ple_proj, ln_final):
    b, s, d = x.shape
    t = b * s
    depth = w_in.shape[0]
    assert s % BQ == 0 and BQ == BK == MOBA_BLK and d % LANE == 0
    tm = 1024 if t % 1024 == 0 else 512
    tzd, tzs, tzw, tzc, bct, covt, avg = _bias_tables(rel_bias, s)
    h = x.reshape(t, d)
    for i in range(depth):
        w_main, w_gate = _reorder_w_in(w_in, i)
        proj = _normmm(h, ln_mix[i], w_main, BF16, tm, 1024, "in_projection").reshape(b, s, MAIN_COLS)
        gates = _normmm(h, ln_mix[i], w_gate, F32, tm, w_gate.shape[1], "gate_projection")
        gates = gates.reshape(b, s, NSA_KV * LANE)
        oa = _dilated(proj, tzd)
        kcmp, vcmp = _compress(proj, cmp_wk[i], cmp_wv[i], cmp_pos[i])
        ob = _nsa(proj, gates, kcmp, vcmp, bct, covt, tzs, tzw)
        oc = _moba(proj, avg, tzc)
        h = _outproj(oa.reshape(t, -1), ob.reshape(t, -1), oc.reshape(t, -1), out_norm[i],
                     w_out[i].astype(BF16), h, tm, 1024)
        keys = peer_keys[i].reshape(PEER_HEADS * 2, PEER_NKEYS, -1)
        e_t, g_t = _peer_topk(h, ln_ffn[i], peer_wq[i].astype(BF16), keys, 256)
        h = _peer_experts(e_t.T, g_t.T, h, ln_ffn[i], _pack_uv(peer_u, peer_v, i), LANE)
        h = _ple(h, ln_ple[i], ple_gate[i].astype(BF16), p[i].reshape(t, -1), ple_proj[i].astype(BF16),
                 tm, 1024)
    return _rmsnorm(h, ln_final, tm).reshape(b, s, d)
```

```python
import math

import numpy as np
import jax
import jax.numpy as jnp
from jax import lax
from jax.experimental import pallas as pl
from jax.experimental.pallas import tpu as pltpu

F32 = jnp.float32
BF16 = jnp.bfloat16
I32 = jnp.int32

HEAD_DIM = 128
A_HEADS, B_HEADS, C_HEADS = 6, 6, 4
NSA_KV, NSA_GROUP = 2, 3
DIL_PATTERNS = ((128, 1), (512, 4), (2048, 16))
CMP_LEN, CMP_STRIDE = 32, 16
SLC_LEN, SLC_TOPN = 64, 8
NSA_WINDOW = 512
FORCE = 1e4
MOBA_BLK, MOBA_TOPK = 256, 3
REL_BUCKETS, REL_MAX_DIST = 32, 1024
PEER_HEADS, PEER_NKEYS, PEER_TOPK = 8, 128, 16
EPS = 1e-6

LANE = 128
BQ = 256
BK = 256
NEG = -1e30
HALF_NEG = -5e29
VMEM_LIMIT = 56 * 1024 * 1024

CB_QA, CB_KA, CB_VA = 0, 6, 12
CB_QB, CB_KC, CB_VC, CB_KS, CB_VS, CB_KW, CB_VW = 18, 24, 26, 28, 30, 32, 34
CB_QC, CB_KCC, CB_VCC = 36, 40, 44
MAIN_COLS = 48 * LANE


def _nt(a, b, precision=None):
    return lax.dot_general(a, b, (((1,), (1,)), ((), ())), preferred_element_type=F32,
                           precision=precision)


def _rel_bucket(dist):
    exact = REL_BUCKETS // 2
    d = jnp.maximum(dist, 0)
    logd = jnp.log(jnp.maximum(d, 1).astype(F32) / exact)
    large = exact + (logd / math.log(REL_MAX_DIST / exact) * (REL_BUCKETS - exact)).astype(I32)
    large = jnp.clip(large, exact, REL_BUCKETS - 1)
    return jnp.where(d < exact, d, large)


def _masked_softmax(s, axis):
    valid = s > HALF_NEG
    m = jnp.max(s, axis=axis, keepdims=True)
    e = jnp.where(valid, jnp.exp(s - m), 0.0)
    z = jnp.sum(e, axis=axis, keepdims=True)
    zs = jnp.where(z > 0, z, 1.0)
    return e * (1.0 / zs)


def _params(*sem):
    return pltpu.CompilerParams(dimension_semantics=sem, vmem_limit_bytes=VMEM_LIMIT)


def _normmm_body(x_ref, g_ref, w_ref, o_ref, xn_ref):
    @pl.when(pl.program_id(1) == 0)
    def _():
        x = x_ref[...]
        ms = jnp.mean(x * x, axis=-1, keepdims=True)
        xn_ref[...] = (x * lax.rsqrt(ms + EPS) * g_ref[...]).astype(BF16)

    o_ref[...] = jnp.dot(xn_ref[...], w_ref[...], preferred_element_type=F32).astype(o_ref.dtype)


def _normmm(x, g, w, out_dtype, tm, tn, name):
    t, d = x.shape
    n = w.shape[1]
    return pl.pallas_call(
        _normmm_body,
        grid=(t // tm, n // tn),
        in_specs=[pl.BlockSpec((tm, d), lambda i, j: (i, 0)),
                  pl.BlockSpec((1, d), lambda i, j: (0, 0)),
                  pl.BlockSpec((d, tn), lambda i, j: (0, j))],
        out_specs=pl.BlockSpec((tm, tn), lambda i, j: (i, j)),
        out_shape=jax.ShapeDtypeStruct((t, n), out_dtype),
        scratch_shapes=[pltpu.VMEM((tm, d), BF16)],
        compiler_params=_params("parallel", "arbitrary"),
        name=name,
    )(x, g.reshape(1, d), w)


def _flash(qs, k_ref, v_ref, lo, hi, score_fns, scale):
    def tile(kj):
        off = pl.multiple_of(kj * BK, BK)
        return k_ref[0, pl.ds(off, BK), :], v_ref[0, pl.ds(off, BK), :]

    def update(state, blocks):
        m, l, acc = state
        m_new = m
        for s, _ in blocks:
            m_new = jnp.maximum(m_new, jnp.max(s, axis=1, keepdims=True))
        alpha = jnp.exp(m - m_new)
        l = alpha * l
        acc = alpha * acc
        for s, v in blocks:
            p = jnp.exp(s - m_new)
            l = l + jnp.sum(p, axis=1, keepdims=True)
            acc = acc + jnp.dot(p.astype(BF16), v, preferred_element_type=F32)
        return m_new, l, acc

    def step(kjs, states):
        kv = [tile(kj) for kj in kjs]
        return tuple(
            update(states[i], [(score_fns[i](kj, _nt(q, k) * scale), v) for kj, (k, v) in zip(kjs, kv)])
            for i, q in enumerate(qs))

    states = tuple((jnp.full((BQ, 1), NEG, F32), jnp.zeros((BQ, 1), F32), jnp.zeros((BQ, HEAD_DIM), F32))
                   for _ in qs)
    start = lo
    for width in ((4, 2, 1) if len(qs) == 1 else (2, 1)):
        shift = width.bit_length() - 1
        count = lax.shift_right_logical(hi - start, shift)

        def group(p, st, start=start, width=width):
            return step([start + width * p + i for i in range(width)], st)

        states = lax.fori_loop(0, count, group, states)
        start = start + count * width
    return [acc / l for _, l, acc in states]


def _dilated_body(q_ref, k_ref, v_ref, tz_ref, o_ref):
    qi = pl.program_id(2)
    scale = HEAD_DIM ** -0.5

    def score(kj, s):
        return s + tz_ref[0, qi - kj]

    o_ref[0] = _flash([q_ref[0]], k_ref, v_ref, 0, qi + 1, [score], scale)[0]


def _dilated(proj, tzd):
    b, s, _ = proj.shape
    nd = s // BK
    return pl.pallas_call(
        _dilated_body,
        grid=(b, A_HEADS, s // BQ),
        in_specs=[pl.BlockSpec((1, BQ, LANE), lambda bi, h, qi: (bi, qi, CB_QA + h)),
                  pl.BlockSpec((1, s, LANE), lambda bi, h, qi: (bi, 0, CB_KA + h)),
                  pl.BlockSpec((1, s, LANE), lambda bi, h, qi: (bi, 0, CB_VA + h)),
                  pl.BlockSpec((1, nd, BQ, BK), lambda bi, h, qi: (h, 0, 0, 0))],
        out_specs=pl.BlockSpec((1, BQ, LANE), lambda bi, h, qi: (bi, qi, h)),
        out_shape=jax.ShapeDtypeStruct((b, s, A_HEADS * LANE), F32),
        compiler_params=_params("parallel", "parallel", "arbitrary"),
        name="dilated_attention",
    )(proj, proj, proj, tzd)


def _compress_body(kc_ref, vc_ref, wk_ref, wv_ref, pos_ref, ko_ref, vo_ref, xk_ref, xv_ref):
    s = kc_ref.shape[1]
    ncmp = ko_ref.shape[2]
    xk_ref[pl.ds(0, s), :] = kc_ref[0].astype(F32)
    xv_ref[pl.ds(0, s), :] = vc_ref[0].astype(F32)
    xk_ref[pl.ds(s, CMP_LEN), :] = jnp.zeros((CMP_LEN, HEAD_DIM), F32)
    xv_ref[pl.ds(s, CMP_LEN), :] = jnp.zeros((CMP_LEN, HEAD_DIM), F32)
    acck = jnp.zeros((ncmp, HEAD_DIM), F32)
    accv = jnp.zeros((ncmp, HEAD_DIM), F32)
    for l in range(CMP_LEN):
        pos = pos_ref[pl.ds(l, 1), :]
        rk = (xk_ref[pl.ds(l, ncmp, stride=CMP_STRIDE), :] + pos).astype(BF16)
        rv = (xv_ref[pl.ds(l, ncmp, stride=CMP_STRIDE), :] + pos).astype(BF16)
        acck = acck + jnp.dot(rk, wk_ref[l], preferred_element_type=F32)
        accv = accv + jnp.dot(rv, wv_ref[l], preferred_element_type=F32)
    ko_ref[0, 0] = acck.astype(BF16)
    vo_ref[0, 0] = accv.astype(BF16)


def _compress(proj, cmp_wk, cmp_wv, cmp_pos):
    b, s, _ = proj.shape
    ncmp = s // CMP_STRIDE
    out = jax.ShapeDtypeStruct((b, NSA_KV, ncmp, HEAD_DIM), BF16)
    return pl.pallas_call(
        _compress_body,
        grid=(b, NSA_KV),
        in_specs=[pl.BlockSpec((1, s, LANE), lambda bi, g: (bi, 0, CB_KC + g)),
                  pl.BlockSpec((1, s, LANE), lambda bi, g: (bi, 0, CB_VC + g)),
                  pl.BlockSpec((CMP_LEN, HEAD_DIM, HEAD_DIM), lambda bi, g: (0, 0, 0)),
                  pl.BlockSpec((CMP_LEN, HEAD_DIM, HEAD_DIM), lambda bi, g: (0, 0, 0)),
                  pl.BlockSpec((CMP_LEN, HEAD_DIM), lambda bi, g: (0, 0))],
        out_specs=[pl.BlockSpec((1, 1, ncmp, HEAD_DIM), lambda bi, g: (bi, g, 0, 0)),
                   pl.BlockSpec((1, 1, ncmp, HEAD_DIM), lambda bi, g: (bi, g, 0, 0))],
        out_shape=[out, out],
        scratch_shapes=[pltpu.VMEM((s + CMP_LEN, HEAD_DIM), F32),
                        pltpu.VMEM((s + CMP_LEN, HEAD_DIM), F32)],
        compiler_params=_params("parallel", "parallel"),
        name="nsa_compress",
    )(proj, proj, cmp_wk.astype(BF16), cmp_wv.astype(BF16), cmp_pos)


def _nsa_body(q_ref, kcmp_ref, vcmp_ref, ks_ref, vs_ref, kw_ref, vw_ref, gt_ref, bct_ref,
              covt_ref, tzs_ref, tzw_ref, o_ref, mask_ref):
    qi = pl.program_id(2)
    scale = HEAD_DIM ** -0.5
    nslc = covt_ref.shape[0]
    nkt = mask_ref.shape[0]
    kcmp = kcmp_ref[0, 0]
    vcmp = vcmp_ref[0, 0]
    gates = jax.nn.sigmoid(gt_ref[0])

    psum = jnp.zeros((kcmp.shape[0], BQ), F32)
    for r in range(NSA_GROUP):
        q = q_ref[0, :, r * LANE:(r + 1) * LANE]
        pt = _masked_softmax(_nt(kcmp, q) * scale + bct_ref[r], axis=0)
        o_cmp = lax.dot_general(pt.astype(BF16), vcmp, (((0,), (0,)), ((), ())),
                                preferred_element_type=F32)
        o_ref[0, :, r * LANE:(r + 1) * LANE] = gates[:, 3 * r:3 * r + 1] * o_cmp
        psum = psum + pt
    imp = jnp.dot(covt_ref[...], psum, preferred_element_type=F32,
                  precision=lax.Precision.HIGHEST)
    jb = lax.broadcasted_iota(I32, (nslc, BQ), 0)
    qpos = qi * BQ + lax.broadcasted_iota(I32, (nslc, BQ), 1)
    qblk = lax.shift_right_logical(qpos, int(math.log2(SLC_LEN)))
    forced = (jb == 0) | (jb == qblk) | (jb == qblk - 1)
    imp = jnp.where(forced, FORCE, jnp.where(jb > qblk, -FORCE, imp))
    beaten = jnp.zeros((nslc, BQ), F32)
    for jp in range(nslc):
        row = imp[jp:jp + 1, :]
        wins = (row > imp) | ((row == imp) & (jp < jb))
        beaten = beaten + wins.astype(F32)
    selt = (beaten < SLC_TOPN).astype(BF16)
    eye = (lax.broadcasted_iota(I32, (BQ, BQ), 0) == lax.broadcasted_iota(I32, (BQ, BQ), 1)).astype(BF16)
    selq = _nt(eye, selt).astype(BF16)
    kpos = lax.broadcasted_iota(I32, (nslc, BK), 1)
    kblk_row = lax.broadcasted_iota(I32, (nslc, BK), 0)

    for t in range(nkt):
        expand = (lax.shift_right_logical(t * BK + kpos, int(math.log2(SLC_LEN))) == kblk_row).astype(BF16)
        mask_ref[t] = (jnp.dot(selq, expand, preferred_element_type=F32) - 1.0) * (-NEG)

    qs = [q_ref[0, :, r * LANE:(r + 1) * LANE] for r in range(NSA_GROUP)]
    sel_fns = [lambda kj, s, r=r: s + tzs_ref[r, qi - kj] + mask_ref[kj] for r in range(NSA_GROUP)]
    win_fns = [lambda kj, s, r=r: s + tzw_ref[r, qi - kj] for r in range(NSA_GROUP)]
    nwin = tzw_ref.shape[1]
    o_sel = _flash(qs, ks_ref, vs_ref, 0, qi + 1, sel_fns, scale)
    o_win = _flash(qs, kw_ref, vw_ref, jnp.maximum(qi - (nwin - 1), 0), qi + 1, win_fns, scale)
    for r in range(NSA_GROUP):
        o_ref[0, :, r * LANE:(r + 1) * LANE] += (gates[:, 3 * r + 1:3 * r + 2] * o_sel[r]
                                                 + gates[:, 3 * r + 2:3 * r + 3] * o_win[r])


def _nsa(proj, gates, kcmp, vcmp, bct, covt, tzs, tzw):
    b, s, _ = proj.shape
    nd = s // BK
    ncmp = kcmp.shape[2]
    nslc = covt.shape[0]
    nwin = tzw.shape[1]
    gw = NSA_GROUP * LANE
    kv = lambda cb: pl.BlockSpec((1, s, LANE), lambda bi, g, qi: (bi, 0, cb + g))
    return pl.pallas_call(
        _nsa_body,
        grid=(b, NSA_KV, s // BQ),
        in_specs=[pl.BlockSpec((1, BQ, gw), lambda bi, g, qi: (bi, qi, CB_QB // NSA_GROUP + g)),
                  pl.BlockSpec((1, 1, ncmp, HEAD_DIM), lambda bi, g, qi: (bi, g, 0, 0)),
                  pl.BlockSpec((1, 1, ncmp, HEAD_DIM), lambda bi, g, qi: (bi, g, 0, 0)),
                  kv(CB_KS), kv(CB_VS), kv(CB_KW), kv(CB_VW),
                  pl.BlockSpec((1, BQ, LANE), lambda bi, g, qi: (bi, qi, g)),
                  pl.BlockSpec((NSA_GROUP, ncmp, BQ), lambda bi, g, qi: (g, 0, qi)),
                  pl.BlockSpec((nslc, ncmp), lambda bi, g, qi: (0, 0)),
                  pl.BlockSpec((NSA_GROUP, nd, BQ, BK), lambda bi, g, qi: (g, 0, 0, 0)),
                  pl.BlockSpec((NSA_GROUP, nwin, BQ, BK), lambda bi, g, qi: (g, 0, 0, 0))],
        out_specs=pl.BlockSpec((1, BQ, gw), lambda bi, g, qi: (bi, qi, g)),
        out_shape=jax.ShapeDtypeStruct((b, s, B_HEADS * LANE), F32),
        scratch_shapes=[pltpu.VMEM((nd, BQ, BK), F32)],
        compiler_params=_params("parallel", "parallel", "arbitrary"),
        name="nsa_attention",
    )(proj, kcmp, vcmp, proj, proj, proj, proj, gates, bct, covt, tzs, tzw)


def _moba_body(q_ref, k_ref, v_ref, avg_ref, tz_ref, o_ref, mask_ref):
    qi = pl.program_id(2)
    scale = HEAD_DIM ** -0.5
    nblk = avg_ref.shape[0]
    q = q_ref[0]
    kmean = jnp.dot(avg_ref[...], k_ref[0], preferred_element_type=F32)
    gate = _nt(kmean, q.astype(F32), precision=lax.Precision.HIGHEST)
    nb = lax.broadcasted_iota(I32, (nblk, BQ), 0)
    past = nb < qi
    beaten = jnp.zeros((nblk, BQ), F32)
    for np_ in range(nblk):
        row = gate[np_:np_ + 1, :]
        wins = (np_ < qi) & ((row > gate) | ((row == gate) & (np_ < nb)))
        beaten = beaten + wins.astype(F32)
    selt = (past & (beaten < MOBA_TOPK)).astype(BF16)
    eye = (lax.broadcasted_iota(I32, (BQ, BQ), 0) == lax.broadcasted_iota(I32, (BQ, BQ), 1)).astype(BF16)
    selq = _nt(eye, selt)
    for n in range(nblk):
        mask_ref[n] = jnp.broadcast_to(selq[:, n:n + 1], (BQ, BK))

    def score(kj, s):
        keep = (mask_ref[kj] > 0.5) | (kj == qi)
        return jnp.where(keep, s + tz_ref[0, qi - kj], NEG)

    o_ref[0] = _flash([q], k_ref, v_ref, 0, qi + 1, [score], scale)[0]


def _moba(proj, avg, tzc):
    b, s, _ = proj.shape
    nd = s // BK
    nblk = avg.shape[0]
    return pl.pallas_call(
        _moba_body,
        grid=(b, C_HEADS, s // BQ),
        in_specs=[pl.BlockSpec((1, BQ, LANE), lambda bi, h, qi: (bi, qi, CB_QC + h)),
                  pl.BlockSpec((1, s, LANE), lambda bi, h, qi: (bi, 0, CB_KCC + h)),
                  pl.BlockSpec((1, s, LANE), lambda bi, h, qi: (bi, 0, CB_VCC + h)),
                  pl.BlockSpec((nblk, s), lambda bi, h, qi: (0, 0)),
                  pl.BlockSpec((1, nd, BQ, BK), lambda bi, h, qi: (h, 0, 0, 0))],
        out_specs=pl.BlockSpec((1, BQ, LANE), lambda bi, h, qi: (bi, qi, h)),
        out_shape=jax.ShapeDtypeStruct((b, s, C_HEADS * LANE), F32),
        scratch_shapes=[pltpu.VMEM((nblk, BQ, BK), F32)],
        compiler_params=_params("parallel", "parallel", "arbitrary"),
        name="moba_attention",
    )(proj, proj, proj, avg, tzc)


def _outproj_body(oa_ref, ob_ref, oc_ref, g_ref, w_ref, h_ref, o_ref, y_ref):
    @pl.when(pl.program_id(1) == 0)
    def _():
        c0 = 0
        for ref in (oa_ref, ob_ref, oc_ref):
            x = ref[...]
            wd = x.shape[1]
            ms = jnp.mean(x * x, axis=-1, keepdims=True)
            y_ref[:, c0:c0 + wd] = (x * lax.rsqrt(ms + EPS) * g_ref[:, c0:c0 + wd]).astype(BF16)
            c0 += wd

    o_ref[...] = h_ref[...] + jnp.dot(y_ref[...], w_ref[...], preferred_element_type=F32)


def _outproj(oa, ob, oc, g, w, h, tm, tn):
    t, d = h.shape
    row = lambda a: pl.BlockSpec((tm, a.shape[1]), lambda i, j: (i, 0))
    return pl.pallas_call(
        _outproj_body,
        grid=(t // tm, d // tn),
        in_specs=[row(oa), row(ob), row(oc),
                  pl.BlockSpec((1, d), lambda i, j: (0, 0)),
                  pl.BlockSpec((d, tn), lambda i, j: (0, j)),
                  pl.BlockSpec((tm, tn), lambda i, j: (i, j))],
        out_specs=pl.BlockSpec((tm, tn), lambda i, j: (i, j)),
        out_shape=jax.ShapeDtypeStruct((t, d), F32),
        scratch_shapes=[pltpu.VMEM((tm, d), BF16)],
        compiler_params=_params("parallel", "arbitrary"),
        name="out_projection",
    )(oa, ob, oc, g.reshape(1, d), w, h)


def _topk_rows(x, k, payload=None):
    n, tm = x.shape
    rows = lax.broadcasted_iota(I32, x.shape, 0)
    slot = lax.broadcasted_iota(I32, (k, tm), 0)
    vals = jnp.zeros((k, tm), F32)
    idxs = jnp.zeros((k, tm), I32)
    for it in range(k):
        mx = jnp.max(x, axis=0, keepdims=True)
        idx = jnp.min(jnp.where(x == mx, rows, n), axis=0, keepdims=True)
        hit = rows == idx
        if payload is not None:
            idx = jnp.sum(jnp.where(hit, payload, 0), axis=0, keepdims=True)
        vals = jnp.where(slot == it, mx, vals)
        idxs = jnp.where(slot == it, idx, idxs)
        x = jnp.where(hit, -jnp.inf, x)
    return vals, idxs


def _oddeven_merge_sort(n):
    pairs = []
    p = 1
    while p < n:
        k = p
        while k >= 1:
            for j in range(k % p, n - k, 2 * k):
                for i in range(min(k, n - j - k)):
                    if (i + j) // (2 * p) == (i + j + k) // (2 * p):
                        pairs.append((i + j, i + j + k))
            k //= 2
        p *= 2
    return pairs


def _topk_rows_sorted(x, k):
    n, tm = x.shape
    assert n == 8 * k
    rows = lax.broadcasted_iota(I32, (n, tm), 0)
    slot = lax.broadcasted_iota(I32, (k, tm), 0)
    out_v, out_i = [], []
    for c in range(tm // LANE):
        lanes = slice(c * LANE, (c + 1) * LANE)
        v = [x[8 * r:8 * r + 8, lanes] for r in range(k)]
        ix = [rows[8 * r:8 * r + 8, lanes] for r in range(k)]
        for i, j in _oddeven_merge_sort(k):
            va, vb, ia, ib = v[i], v[j], ix[i], ix[j]
            keep = (va > vb) | ((va == vb) & (ia < ib))
            v[i], v[j] = jnp.maximum(va, vb), jnp.minimum(va, vb)
            ix[i], ix[j] = jnp.where(keep, ia, ib), jnp.where(keep, ib, ia)
        vals = jnp.zeros((k, LANE), F32)
        idxs = jnp.zeros((k, LANE), I32)
        for t in range(k):
            mx = jnp.max(v[0], axis=0, keepdims=True)
            best = jnp.min(jnp.where(v[0] == mx, ix[0], n), axis=0, keepdims=True)
            win = ix[0] == best
            vals = jnp.where(slot[:, lanes] == t, mx, vals)
            idxs = jnp.where(slot[:, lanes] == t, best, idxs)
            for i in range(k - 1 - t):
                v[i] = jnp.where(win, v[i + 1], v[i])
                ix[i] = jnp.where(win, ix[i + 1], ix[i])
        out_v.append(vals)
        out_i.append(idxs)
    return jnp.concatenate(out_v, axis=1), jnp.concatenate(out_i, axis=1)


def _peer_topk_body(h_ref, ln_ref, wq_ref, keys_ref, e_ref, g_ref):
    x = h_ref[...]
    ms = jnp.mean(x * x, axis=-1, keepdims=True)
    xn = (x * lax.rsqrt(ms + EPS) * ln_ref[...]).astype(BF16)
    dq = keys_ref.shape[2]
    for h in range(PEER_HEADS):
        qh = jnp.dot(xn, wq_ref[:, 2 * h * dq:2 * (h + 1) * dq], preferred_element_type=F32)
        tops = []
        for c in range(2):
            sc = _nt(keys_ref[2 * h + c], qh[:, c * dq:(c + 1) * dq])
            tops.append(_topk_rows_sorted(sc, PEER_TOPK))
        (s0, i0), (s1, i1) = tops
        half = PEER_TOPK // 2
        sub = lax.broadcasted_iota(I32, (half, s0.shape[1]), 0)
        pieces = [s0[0:1] + s1]
        pieces_i = [i0[0:1] * PEER_NKEYS + i1]
        for a in range(1, half):
            keep = sub < PEER_TOPK // (a + 1)
            pieces.append(jnp.where(keep, s0[a:a + 1] + s1[0:half], -jnp.inf))
            pieces_i.append(i0[a:a + 1] * PEER_NKEYS + i1[0:half])
        pieces.append(s0[half:] + s1[0:1])
        pieces_i.append(i0[half:] * PEER_NKEYS + i1[0:1])
        cand = jnp.concatenate(pieces, axis=0)
        cand_i = jnp.concatenate(pieces_i, axis=0)
        bs, be = _topk_rows(cand, PEER_TOPK, payload=cand_i)
        e = jnp.exp(bs - jnp.max(bs, axis=0, keepdims=True))
        g_ref[h * PEER_TOPK:(h + 1) * PEER_TOPK, :] = e / jnp.sum(e, axis=0, keepdims=True)
        e_ref[h * PEER_TOPK:(h + 1) * PEER_TOPK, :] = be


def _peer_topk(h, ln, wq, keys, tm):
    t, d = h.shape
    hk = PEER_HEADS * PEER_TOPK
    assert wq.shape == (d, keys.shape[0] * keys.shape[2]) and keys.shape[2] == LANE
    return pl.pallas_call(
        _peer_topk_body,
        grid=(t // tm,),
        in_specs=[pl.BlockSpec((tm, d), lambda i: (i, 0)),
                  pl.BlockSpec((1, d), lambda i: (0, 0)),
                  pl.BlockSpec(wq.shape, lambda i: (0, 0)),
                  pl.BlockSpec(keys.shape, lambda i: (0, 0, 0))],
        out_specs=[pl.BlockSpec((hk, tm), lambda i: (0, i)),
                   pl.BlockSpec((hk, tm), lambda i: (0, i))],
        out_shape=[jax.ShapeDtypeStruct((hk, t), I32), jax.ShapeDtypeStruct((hk, t), F32)],
        compiler_params=_params("parallel"),
        name="peer_topk",
    )(h, ln.reshape(1, d), wq, keys)


PEER_NBUF = 16
PEER_AHEAD = 14


def _gelu(x):
    return 0.5 * x * (1.0 + lax.erf(x * (2.0 ** -0.5)))


def _peer_expert_body(ids0_ref, idsn_ref, h_ref, ln_ref, g_ref, uvw_ref, o_ref, xn_ref, acc_ref, lhs_ref,
                      buf_ref, sem_ref):
    tb, d = h_ref.shape
    hk = idsn_ref.shape[1]
    nrg = hk // 8
    nlt = d // LANE
    assert nrg == nlt and hk % (2 * nlt) == 0
    step = pl.program_id(0)
    x = h_ref[...]
    ms = jnp.mean(x * x, axis=-1, keepdims=True)
    xn_ref[...] = x * lax.rsqrt(ms + EPS) * ln_ref[...]

    def issue(ids_ref, row, slot, k0, k1):
        for k in range(k0, k1):
            e = ids_ref[row, k]
            pltpu.make_async_copy(uvw_ref.at[e], buf_ref.at[slot, :, k, :],
                                  sem_ref.at[slot]).start(priority=k % 2)

    def wait_all(slot):
        pltpu.make_async_copy(buf_ref.at[slot], buf_ref.at[slot], sem_ref.at[slot]).wait()

    def x_tiles(t):
        xrow = xn_ref[pl.ds(t, 1), :]
        return [jnp.broadcast_to(xrow[:, j * LANE:(j + 1) * LANE], (8, LANE)) for j in range(nlt)]

    def dots(xb, slot, r):
        a = None
        for j in range(nlt):
            w = buf_ref[slot, j, r * 8:(r + 1) * 8, :]
            pr = lax.bitcast_convert_type(lax.shift_left(w, jnp.uint32(16)), F32) * xb[j]
            a = pr if a is None else a + pr
        return a

    def coefficients(t, slot):
        act = jnp.sum(acc_ref[slot % 2].T, axis=0, keepdims=True)
        coef = jnp.broadcast_to(_gelu(act) * g_ref[pl.ds(t, 1), :], (8, hk))
        chi = coef.astype(BF16).astype(F32)
        return jnp.concatenate([chi, coef - chi], axis=0).astype(BF16)

    def weighted(lhs, slot, j):
        w = buf_ref[slot, j]
        v = lax.bitcast_convert_type(w & jnp.uint32(0xFFFF0000), F32).astype(BF16)
        yj = jnp.dot(lhs, v, preferred_element_type=F32)
        return yj[0:1, :] + yj[8:9, :]

    def flush(done):
        ya, yb, tp = done
        y = jnp.concatenate([ya[j:j + 1, :] for j in range(8)] + [yb[j:j + 1, :] for j in range(8)], axis=1)
        o_ref[pl.ds(tp, 1), :] = h_ref[pl.ds(tp, 1), :] + y

    def turn(t, slot, stages, done):
        tgt = (slot + PEER_AHEAD) % PEER_NBUF
        far = (slot + 2) % PEER_NBUF
        per = hk // (2 * nlt)
        if stages >= 3:
            wait_all(far)
            xb = x_tiles(t + 2)
        lhs = lhs_ref[slot % 2]
        ys, parts = [], []
        for c in range(nlt):
            if stages >= 3:
                parts.append(dots(xb, far, c))
            issue(idsn_ref, t, tgt, 2 * c * per, (2 * c + 1) * per)
            if 2 * c < nlt:
                ys.append(weighted(lhs, slot, 2 * c))
                ys.append(weighted(lhs, slot, 2 * c + 1))
            if 2 * c == nlt and stages >= 2:
                lhs_next = coefficients(t + 1, slot + 1)
            issue(idsn_ref, t, tgt, (2 * c + 1) * per, (2 * c + 2) * per)
        if stages >= 3:
            acc_ref[slot % 2] = jnp.concatenate(parts, axis=0)
        if stages >= 2:
            lhs_ref[(slot + 1) % 2] = lhs_next
        flush(done)
        return (jnp.concatenate(ys[:8], axis=0), jnp.concatenate(ys[8:], axis=0), t)

    @pl.when(step == 0)
    def _():
        for t0 in range(PEER_AHEAD):
            issue(ids0_ref, t0, t0, 0, hk)

    for t0 in range(2):
        wait_all(t0)
        xb0 = x_tiles(t0)
        acc_ref[t0] = jnp.concatenate([dots(xb0, t0, r) for r in range(nrg)], axis=0)
    lhs_ref[0] = coefficients(0, 0)

    def ring(i, stacks):
        done = stacks + (jnp.maximum(i * PEER_NBUF - 1, 0),)
        for slot in range(PEER_NBUF):
            done = turn(i * PEER_NBUF + slot, slot, 3, done)
        return done[:2]

    zero = jnp.zeros((8, LANE), F32)
    nring = (tb - 2) // PEER_NBUF
    done = lax.fori_loop(0, nring, ring, (zero, zero)) + (nring * PEER_NBUF - 1,)
    for t in range(nring * PEER_NBUF, tb - 2):
        done = turn(t, t % PEER_NBUF, 3, done)
    done = turn(tb - 2, (tb - 2) % PEER_NBUF, 2, done)
    done = turn(tb - 1, (tb - 1) % PEER_NBUF, 1, done)
    flush(done)

    @pl.when(step == pl.num_programs(0) - 1)
    def _():
        for t0 in range(PEER_AHEAD):
            wait_all((tb + t0) % PEER_NBUF)


def _peer_experts(ids, gates, h, ln, slabs, tb):
    t, d = h.shape
    hk = ids.shape[1]
    assert hk == LANE and tb % PEER_NBUF == 0 and PEER_NBUF % 2 == 0 and d % LANE == 0
    assert PEER_AHEAD <= PEER_NBUF - 2
    ids_next = jnp.concatenate([ids[PEER_AHEAD:], ids[:PEER_AHEAD]], axis=0)
    assert slabs.shape[1:] == (d // LANE, LANE)
    return pl.pallas_call(
        _peer_expert_body,
        grid=(t // tb,),
        in_specs=[pl.BlockSpec((PEER_NBUF, hk), lambda i: (0, 0), memory_space=pltpu.SMEM),
                  pl.BlockSpec((tb, hk), lambda i: (i, 0), memory_space=pltpu.SMEM),
                  pl.BlockSpec((tb, d), lambda i: (i, 0)),
                  pl.BlockSpec((1, d), lambda i: (0, 0)),
                  pl.BlockSpec((tb, hk), lambda i: (i, 0)),
                  pl.BlockSpec(memory_space=pl.ANY)],
        out_specs=pl.BlockSpec((tb, d), lambda i: (i, 0)),
        out_shape=jax.ShapeDtypeStruct((t, d), F32),
        scratch_shapes=[pltpu.VMEM((tb, d), F32),
                        pltpu.VMEM((2, hk, LANE), F32),
                        pltpu.VMEM((2, 16, hk), BF16),
                        pltpu.VMEM((PEER_NBUF, d // LANE, hk, LANE), jnp.uint32),
                        pltpu.SemaphoreType.DMA((PEER_NBUF,))],
        compiler_params=_params("arbitrary"),
        name="peer_experts",
    )(ids[:PEER_NBUF], ids_next, h, ln.reshape(1, d), gates, slabs)


def _ple_body(x_ref, g_ref, wg_ref, p_ref, wp_ref, h_ref, o_ref, xn_ref):
    @pl.when(pl.program_id(1) == 0)
    def _():
        x = x_ref[...]
        ms = jnp.mean(x * x, axis=-1, keepdims=True)
        xn_ref[...] = (x * lax.rsqrt(ms + EPS) * g_ref[...]).astype(BF16)

    z = jnp.dot(xn_ref[...], wg_ref[...], preferred_element_type=F32)
    pp = jnp.dot(p_ref[...].astype(BF16), wp_ref[...], preferred_element_type=F32)
    o_ref[...] = h_ref[...] + jax.nn.sigmoid(z) * pp


def _ple(h, g, wg, p, wp, tm, tn):
    t, d = h.shape
    pd = p.shape[1]
    return pl.pallas_call(
        _ple_body,
        grid=(t // tm, d // tn),
        in_specs=[pl.BlockSpec((tm, d), lambda i, j: (i, 0)),
                  pl.BlockSpec((1, d), lambda i, j: (0, 0)),
                  pl.BlockSpec((d, tn), lambda i, j: (0, j)),
                  pl.BlockSpec((tm, pd), lambda i, j: (i, 0)),
                  pl.BlockSpec((pd, tn), lambda i, j: (0, j)),
                  pl.BlockSpec((tm, tn), lambda i, j: (i, j))],
        out_specs=pl.BlockSpec((tm, tn), lambda i, j: (i, j)),
        out_shape=jax.ShapeDtypeStruct((t, d), F32),
        scratch_shapes=[pltpu.VMEM((tm, d), BF16)],
        compiler_params=_params("parallel", "arbitrary"),
        name="ple_gate",
    )(h, g.reshape(1, d), wg, p, wp, h)


def _rmsnorm_body(x_ref, g_ref, o_ref):
    x = x_ref[...]
    ms = jnp.mean(x * x, axis=-1, keepdims=True)
    o_ref[...] = x * lax.rsqrt(ms + EPS) * g_ref[...]


def _rmsnorm(x, g, tm):
    t, d = x.shape
    return pl.pallas_call(
        _rmsnorm_body,
        grid=(t // tm,),
        in_specs=[pl.BlockSpec((tm, d), lambda i: (i, 0)), pl.BlockSpec((1, d), lambda i: (0, 0))],
        out_specs=pl.BlockSpec((tm, d), lambda i: (i, 0)),
        out_shape=jax.ShapeDtypeStruct((t, d), F32),
        compiler_params=_params("parallel"),
        name="final_rmsnorm",
    )(x, g.reshape(1, d))


def _toeplitz(vec, nq, nk):
    assert nq == nk
    lead = vec.shape[:-1]
    g = int(np.prod(lead))
    w = jnp.concatenate([jnp.zeros(lead + (1,), vec.dtype), vec[..., ::-1]], axis=-1).reshape(g, 1, 2 * nk)

    def body(w_ref, o_ref):
        rows = jnp.broadcast_to(w_ref[0], (nq, 2 * nk))
        o_ref[0] = pltpu.roll(rows, 0, 1, stride=1, stride_axis=0)[:, nk:]

    out = pl.pallas_call(
        body,
        grid=(g,),
        in_specs=[pl.BlockSpec((1, 1, 2 * nk), lambda i: (i, 0, 0))],
        out_specs=pl.BlockSpec((1, nq, nk), lambda i: (i, 0, 0)),
        out_shape=jax.ShapeDtypeStruct((g, nq, nk), vec.dtype),
        compiler_params=_params("parallel"),
        name="toeplitz_tiles",
    )(w)
    return out.reshape(lead + (nq, nk))


def _bias_tables(rel_bias, s):
    nd = s // BK
    bdt = rel_bias[_rel_bucket(jnp.arange(s))].astype(F32).T
    dd = (jnp.arange(nd)[:, None, None] * BK + jnp.arange(BQ)[None, :, None]
          - jnp.arange(BK)[None, None, :])
    ext = jnp.pad(bdt, ((0, 0), (BK - 1, 0)))
    segs = jnp.stack([ext[:, dl * BK:dl * BK + BQ + BK - 1] for dl in range(nd)], axis=1)
    tz = _toeplitz(segs, BQ, BK)
    causal = dd >= 0
    mult = sum(((dd % dil == 0) & (dd // dil <= window // dil)).astype(F32) for window, dil in DIL_PATTERNS)
    ok = causal & (mult > 0)
    tzd = jnp.where(ok, tz[:A_HEADS] + jnp.log(jnp.where(ok, mult, 1.0)), NEG)
    tzb = tz[A_HEADS:A_HEADS + B_HEADS]
    tzs = jnp.where(causal, tzb, NEG)
    nwin = -(-(NSA_WINDOW - 1) // BK) + 1
    tzw = jnp.where(causal & (dd <= NSA_WINDOW - 1), tzb, NEG)[:, :nwin]
    tzc = jnp.where(causal, tz[A_HEADS + B_HEADS:], NEG)
    ncmp = s // CMP_STRIDE
    na = s // CMP_STRIDE
    nvec = na + ncmp - 1
    lo = CMP_STRIDE * (ncmp - 1) + CMP_LEN - 1
    bdb = bdt[A_HEADS:A_HEADS + B_HEADS]
    gext = jnp.concatenate([jnp.full((B_HEADS, lo), NEG, F32), bdb], axis=1)
    vecs = gext[:, :CMP_STRIDE * nvec].reshape(B_HEADS, nvec, CMP_STRIDE).transpose(0, 2, 1)
    bcq = _toeplitz(vecs, na, ncmp)
    bcq = bcq.transpose(0, 2, 1, 3).reshape(B_HEADS, s, ncmp)
    bct = jnp.transpose(bcq, (0, 2, 1))
    nslc = s // SLC_LEN
    cstart = jnp.arange(ncmp) * CMP_STRIDE
    sstart = jnp.arange(nslc) * SLC_LEN
    covt = ((cstart[None, :] < sstart[:, None] + SLC_LEN)
            & (cstart[None, :] + CMP_LEN > sstart[:, None])).astype(F32)
    nblk = s // MOBA_BLK
    avg = ((jnp.arange(s)[None, :] // MOBA_BLK == jnp.arange(nblk)[:, None]).astype(F32)
           / MOBA_BLK).astype(BF16)
    return tzd, tzs, tzw, tzc, bct, covt, avg


def _pack_body(u_ref, v_ref, o_ref):
    _, te, d = u_ref.shape
    nlt = d // LANE
    ub = lax.bitcast_convert_type(u_ref[0].astype(BF16).astype(F32), jnp.uint32)
    vb = lax.bitcast_convert_type(v_ref[0].astype(BF16).astype(F32), jnp.uint32)
    word = lax.shift_right_logical(ub, jnp.uint32(16)) | vb
    for j in range(nlt):
        o_ref[pl.ds(j, te, stride=nlt), :] = word[:, j * LANE:(j + 1) * LANE]


def _pack_uv(u, v, layer, te=256):
    _, e, d = u.shape
    nlt = d // LANE
    spec = pl.BlockSpec((1, te, d), lambda i: (layer, i, 0))
    out = pl.pallas_call(
        _pack_body,
        grid=(e // te,),
        in_specs=[spec, spec],
        out_specs=pl.BlockSpec((te * nlt, LANE), lambda i: (i, 0)),
        out_shape=jax.ShapeDtypeStruct((e * nlt, LANE), jnp.uint32),
        compiler_params=_params("parallel"),
        name="pack_experts",
    )(u, v)
    return out.reshape(e, nlt, LANE)


GATE_COL0 = 3 * A_HEADS * LANE + B_HEADS * LANE + 6 * NSA_KV * LANE
GATE_COLS = 3 * B_HEADS


def _w_main_body(w_ref, o_ref):
    x = w_ref[0]
    o_ref[:, :GATE_COL0] = x[:, :GATE_COL0].astype(BF16)
    o_ref[:, GATE_COL0:] = x[:, GATE_COL0 + GATE_COLS:].astype(BF16)


def _reorder_w_in(w_in, layer, tk=256):
    _, d, cols = w_in.shape
    main = pl.pallas_call(
        _w_main_body,
        grid=(d // tk,),
        in_specs=[pl.BlockSpec((1, tk, cols), lambda i: (layer, i, 0))],
        out_specs=pl.BlockSpec((tk, MAIN_COLS), lambda i: (i, 0)),
        out_shape=jax.ShapeDtypeStruct((d, MAIN_COLS), BF16),
        compiler_params=_params("parallel"),
        name="reorder_w_in",
    )(w_in)
    per = GATE_COLS // NSA_KV
    wg = w_in[layer, :, GATE_COL0:GATE_COL0 + GATE_COLS]
    gate = jnp.concatenate([jnp.pad(wg[:, g * per:(g + 1) * per], ((0, 0), (0, LANE - per))) for g in range(NSA_KV)],
                           axis=1)
    return main, gate.astype(BF16)


def kernel(x, p, ln_mix, w_in, cmp_wk, cmp_wv, cmp_pos, out_norm, w_out, rel_bias, ln_ffn, peer_wq,
           peer_keys, peer_u, peer_v, ln_ple, ple_gate, ple_proj, ln_final):
    b, s, d = x.shape
    t = b * s
    depth = w_in.shape[0]
    assert s % BQ == 0 and BQ == BK == MOBA_BLK and d % LANE == 0
    tm = 1024 if t % 1024 == 0 else 512
    tzd, tzs, tzw, tzc, bct, covt, avg = _bias_tables(rel_bias, s)
    h = x.reshape(t, d)
    for i in range(depth):
        w_main, w_gate = _reorder_w_in(w_in, i)
        proj = _normmm(h, ln_mix[i], w_main, BF16, tm, 1024, "in_projection").reshape(b, s, MAIN_COLS)
        gates = _normmm(h, ln_mix[i], w_gate, F32, tm, w_gate.shape[1], "gate_projection")
        gates = gates.reshape(b, s, NSA_KV * LANE)
        oa = _dilated(proj, tzd)
        kcmp, vcmp = _compress(proj, cmp_wk[i], cmp_wv[i], cmp_pos[i])
        ob = _nsa(proj, gates, kcmp, vcmp, bct, covt, tzs, tzw)
        oc = _moba(proj, avg, tzc)
        h = _outproj(oa.reshape(t, -1), ob.reshape(t, -1), oc.reshape(t, -1), out_norm[i],
                     w_out[i].astype(BF16), h, tm, 1024)
        keys = peer_keys[i].reshape(PEER_HEADS * 2, PEER_NKEYS, -1)
        e_t, g_t = _peer_topk(h, ln_ffn[i], peer_wq[i].astype(BF16), keys, LANE)
        h = _peer_experts(e_t.T, g_t.T, h, ln_ffn[i], _pack_uv(peer_u, peer_v, i), LANE)
        h = _ple(h, ln_ple[i], ple_gate[i].astype(BF16), p[i].reshape(t, -1), ple_proj[i].astype(BF16),
                 tm, 1024)
    return _rmsnorm(h, ln_final, tm).reshape(b, s, d)
```

```python
import math

import numpy as np
import jax
import jax.numpy as jnp
from jax import lax
from jax.experimental import pallas as pl
from jax.experimental.pallas import tpu as pltpu

F32 = jnp.float32
BF16 = jnp.bfloat16
I32 = jnp.int32

HEAD_DIM = 128
A_HEADS, B_HEADS, C_HEADS = 6, 6, 4
NSA_KV, NSA_GROUP = 2, 3
DIL_PATTERNS = ((128, 1), (512, 4), (2048, 16))
CMP_LEN, CMP_STRIDE = 32, 16
SLC_LEN, SLC_TOPN = 64, 8
NSA_WINDOW = 512
FORCE = 1e4
MOBA_BLK, MOBA_TOPK = 256, 3
REL_BUCKETS, REL_MAX_DIST = 32, 1024
PEER_HEADS, PEER_NKEYS, PEER_TOPK = 8, 128, 16
EPS = 1e-6

LANE = 128
BQ = 256
BK = 256
NEG = -1e30
HALF_NEG = -5e29
VMEM_LIMIT = 56 * 1024 * 1024

CB_QA, CB_KA, CB_VA = 0, 6, 12
CB_QB, CB_KC, CB_VC, CB_KS, CB_VS, CB_KW, CB_VW = 18, 24, 26, 28, 30, 32, 34
CB_QC, CB_KCC, CB_VCC = 36, 40, 44
MAIN_COLS = 48 * LANE


def _nt(a, b, precision=None):
    return lax.dot_general(a, b, (((1,), (1,)), ((), ())), preferred_element_type=F32,
                           precision=precision)


def _rel_bucket(dist):
    exact = REL_BUCKETS // 2
    d = jnp.maximum(dist, 0)
    logd = jnp.log(jnp.maximum(d, 1).astype(F32) / exact)
    large = exact + (logd / math.log(REL_MAX_DIST / exact) * (REL_BUCKETS - exact)).astype(I32)
    large = jnp.clip(large, exact, REL_BUCKETS - 1)
    return jnp.where(d < exact, d, large)


def _masked_softmax(s, axis):
    valid = s > HALF_NEG
    m = jnp.max(s, axis=axis, keepdims=True)
    e = jnp.where(valid, jnp.exp(s - m), 0.0)
    z = jnp.sum(e, axis=axis, keepdims=True)
    zs = jnp.where(z > 0, z, 1.0)
    return e * (1.0 / zs)


def _params(*sem):
    return pltpu.CompilerParams(dimension_semantics=sem, vmem_limit_bytes=VMEM_LIMIT)


def _normmm_body(x_ref, g_ref, w_ref, o_ref, xn_ref):
    @pl.when(pl.program_id(1) == 0)
    def _():
        x = x_ref[...]
        ms = jnp.mean(x * x, axis=-1, keepdims=True)
        xn_ref[...] = (x * lax.rsqrt(ms + EPS) * g_ref[...]).astype(BF16)

    o_ref[...] = jnp.dot(xn_ref[...], w_ref[...], preferred_element_type=F32).astype(o_ref.dtype)


def _normmm(x, g, w, out_dtype, tm, tn, name):
    t, d = x.shape
    n = w.shape[1]
    return pl.pallas_call(
        _normmm_body,
        grid=(t // tm, n // tn),
        in_specs=[pl.BlockSpec((tm, d), lambda i, j: (i, 0)),
                  pl.BlockSpec((1, d), lambda i, j: (0, 0)),
                  pl.BlockSpec((d, tn), lambda i, j: (0, j))],
        out_specs=pl.BlockSpec((tm, tn), lambda i, j: (i, j)),
        out_shape=jax.ShapeDtypeStruct((t, n), out_dtype),
        scratch_shapes=[pltpu.VMEM((tm, d), BF16)],
        compiler_params=_params("parallel", "arbitrary"),
        name=name,
    )(x, g.reshape(1, d), w)


def _flash(qs, k_ref, v_ref, lo, hi, score_fns, scale):
    def tile(kj):
        off = pl.multiple_of(kj * BK, BK)
        return k_ref[0, pl.ds(off, BK), :], v_ref[0, pl.ds(off, BK), :]

    def update(state, blocks):
        m, l, acc = state
        m_new = m
        for s, _ in blocks:
            m_new = jnp.maximum(m_new, jnp.max(s, axis=1, keepdims=True))
        alpha = jnp.exp(m - m_new)
        l = alpha * l
        acc = alpha * acc
        for s, v in blocks:
            p = jnp.exp(s - m_new)
            l = l + jnp.sum(p, axis=1, keepdims=True)
            acc = acc + jnp.dot(p.astype(BF16), v, preferred_element_type=F32)
        return m_new, l, acc

    def step(kjs, states):
        kv = [tile(kj) for kj in kjs]
        return tuple(
            update(states[i], [(score_fns[i](kj, _nt(q, k) * scale), v) for kj, (k, v) in zip(kjs, kv)])
            for i, q in enumerate(qs))

    states = tuple((jnp.full((BQ, 1), NEG, F32), jnp.zeros((BQ, 1), F32), jnp.zeros((BQ, HEAD_DIM), F32))
                   for _ in qs)
    start = lo
    for width in ((4, 2, 1) if len(qs) == 1 else (2, 1)):
        shift = width.bit_length() - 1
        count = lax.shift_right_logical(hi - start, shift)

        def group(p, st, start=start, width=width):
            return step([start + width * p + i for i in range(width)], st)

        states = lax.fori_loop(0, count, group, states)
        start = start + count * width
    return [acc / l for _, l, acc in states]


def _dilated_body(q_ref, k_ref, v_ref, tz_ref, o_ref):
    qi = pl.program_id(2)
    scale = HEAD_DIM ** -0.5

    def score(kj, s):
        return s + tz_ref[0, qi - kj]

    o_ref[0] = _flash([q_ref[0]], k_ref, v_ref, 0, qi + 1, [score], scale)[0]


def _dilated(proj, tzd):
    b, s, _ = proj.shape
    nd = s // BK
    return pl.pallas_call(
        _dilated_body,
        grid=(b, A_HEADS, s // BQ),
        in_specs=[pl.BlockSpec((1, BQ, LANE), lambda bi, h, qi: (bi, qi, CB_QA + h)),
                  pl.BlockSpec((1, s, LANE), lambda bi, h, qi: (bi, 0, CB_KA + h)),
                  pl.BlockSpec((1, s, LANE), lambda bi, h, qi: (bi, 0, CB_VA + h)),
                  pl.BlockSpec((1, nd, BQ, BK), lambda bi, h, qi: (h, 0, 0, 0))],
        out_specs=pl.BlockSpec((1, BQ, LANE), lambda bi, h, qi: (bi, qi, h)),
        out_shape=jax.ShapeDtypeStruct((b, s, A_HEADS * LANE), F32),
        compiler_params=_params("parallel", "parallel", "arbitrary"),
        name="dilated_attention",
    )(proj, proj, proj, tzd)


def _compress_body(kc_ref, vc_ref, wk_ref, wv_ref, pos_ref, ko_ref, vo_ref, xk_ref, xv_ref):
    s = kc_ref.shape[1]
    ncmp = ko_ref.shape[2]
    xk_ref[pl.ds(0, s), :] = kc_ref[0].astype(F32)
    xv_ref[pl.ds(0, s), :] = vc_ref[0].astype(F32)
    xk_ref[pl.ds(s, CMP_LEN), :] = jnp.zeros((CMP_LEN, HEAD_DIM), F32)
    xv_ref[pl.ds(s, CMP_LEN), :] = jnp.zeros((CMP_LEN, HEAD_DIM), F32)
    acck = jnp.zeros((ncmp, HEAD_DIM), F32)
    accv = jnp.zeros((ncmp, HEAD_DIM), F32)
    for l in range(CMP_LEN):
        pos = pos_ref[pl.ds(l, 1), :]
        rk = (xk_ref[pl.ds(l, ncmp, stride=CMP_STRIDE), :] + pos).astype(BF16)
        rv = (xv_ref[pl.ds(l, ncmp, stride=CMP_STRIDE), :] + pos).astype(BF16)
        acck = acck + jnp.dot(rk, wk_ref[l], preferred_element_type=F32)
        accv = accv + jnp.dot(rv, wv_ref[l], preferred_element_type=F32)
    ko_ref[0, 0] = acck.astype(BF16)
    vo_ref[0, 0] = accv.astype(BF16)


def _compress(proj, cmp_wk, cmp_wv, cmp_pos):
    b, s, _ = proj.shape
    ncmp = s // CMP_STRIDE
    out = jax.ShapeDtypeStruct((b, NSA_KV, ncmp, HEAD_DIM), BF16)
    return pl.pallas_call(
        _compress_body,
        grid=(b, NSA_KV),
        in_specs=[pl.BlockSpec((1, s, LANE), lambda bi, g: (bi, 0, CB_KC + g)),
                  pl.BlockSpec((1, s, LANE), lambda bi, g: (bi, 0, CB_VC + g)),
                  pl.BlockSpec((CMP_LEN, HEAD_DIM, HEAD_DIM), lambda bi, g: (0, 0, 0)),
                  pl.BlockSpec((CMP_LEN, HEAD_DIM, HEAD_DIM), lambda bi, g: (0, 0, 0)),
                  pl.BlockSpec((CMP_LEN, HEAD_DIM), lambda bi, g: (0, 0))],
        out_specs=[pl.BlockSpec((1, 1, ncmp, HEAD_DIM), lambda bi, g: (bi, g, 0, 0)),
                   pl.BlockSpec((1, 1, ncmp, HEAD_DIM), lambda bi, g: (bi, g, 0, 0))],
        out_shape=[out, out],
        scratch_shapes=[pltpu.VMEM((s + CMP_LEN, HEAD_DIM), F32),
                        pltpu.VMEM((s + CMP_LEN, HEAD_DIM), F32)],
        compiler_params=_params("parallel", "parallel"),
        name="nsa_compress",
    )(proj, proj, cmp_wk.astype(BF16), cmp_wv.astype(BF16), cmp_pos)


def _nsa_body(q_ref, kcmp_ref, vcmp_ref, ks_ref, vs_ref, kw_ref, vw_ref, gt_ref, bct_ref,
              covt_ref, tzs_ref, tzw_ref, o_ref, mask_ref):
    qi = pl.program_id(2)
    scale = HEAD_DIM ** -0.5
    nslc = covt_ref.shape[0]
    nkt = mask_ref.shape[0]
    kcmp = kcmp_ref[0, 0]
    vcmp = vcmp_ref[0, 0]
    gates = jax.nn.sigmoid(gt_ref[0])

    psum = jnp.zeros((kcmp.shape[0], BQ), F32)
    for r in range(NSA_GROUP):
        q = q_ref[0, :, r * LANE:(r + 1) * LANE]
        pt = _masked_softmax(_nt(kcmp, q) * scale + bct_ref[r], axis=0)
        o_cmp = lax.dot_general(pt.astype(BF16), vcmp, (((0,), (0,)), ((), ())),
                                preferred_element_type=F32)
        o_ref[0, :, r * LANE:(r + 1) * LANE] = gates[:, 3 * r:3 * r + 1] * o_cmp
        psum = psum + pt
    imp = jnp.dot(covt_ref[...], psum, preferred_element_type=F32,
                  precision=lax.Precision.HIGHEST)
    jb = lax.broadcasted_iota(I32, (nslc, BQ), 0)
    qpos = qi * BQ + lax.broadcasted_iota(I32, (nslc, BQ), 1)
    qblk = lax.shift_right_logical(qpos, int(math.log2(SLC_LEN)))
    forced = (jb == 0) | (jb == qblk) | (jb == qblk - 1)
    imp = jnp.where(forced, FORCE, jnp.where(jb > qblk, -FORCE, imp))
    beaten = jnp.zeros((nslc, BQ), F32)
    for jp in range(nslc):
        row = imp[jp:jp + 1, :]
        wins = (row > imp) | ((row == imp) & (jp < jb))
        beaten = beaten + wins.astype(F32)
    selt = (beaten < SLC_TOPN).astype(BF16)
    eye = (lax.broadcasted_iota(I32, (BQ, BQ), 0) == lax.broadcasted_iota(I32, (BQ, BQ), 1)).astype(BF16)
    selq = _nt(eye, selt).astype(BF16)
    kpos = lax.broadcasted_iota(I32, (nslc, BK), 1)
    kblk_row = lax.broadcasted_iota(I32, (nslc, BK), 0)

    for t in range(nkt):
        expand = (lax.shift_right_logical(t * BK + kpos, int(math.log2(SLC_LEN))) == kblk_row).astype(BF16)
        mask_ref[t] = (jnp.dot(selq, expand, preferred_element_type=F32) - 1.0) * (-NEG)

    qs = [q_ref[0, :, r * LANE:(r + 1) * LANE] for r in range(NSA_GROUP)]
    sel_fns = [lambda kj, s, r=r: s + tzs_ref[r, qi - kj] + mask_ref[kj] for r in range(NSA_GROUP)]
    win_fns = [lambda kj, s, r=r: s + tzw_ref[r, qi - kj] for r in range(NSA_GROUP)]
    nwin = tzw_ref.shape[1]
    o_sel = _flash(qs, ks_ref, vs_ref, 0, qi + 1, sel_fns, scale)
    o_win = _flash(qs, kw_ref, vw_ref, jnp.maximum(qi - (nwin - 1), 0), qi + 1, win_fns, scale)
    for r in range(NSA_GROUP):
        o_ref[0, :, r * LANE:(r + 1) * LANE] += (gates[:, 3 * r + 1:3 * r + 2] * o_sel[r]
                                                 + gates[:, 3 * r + 2:3 * r + 3] * o_win[r])


def _nsa(proj, gates, kcmp, vcmp, bct, covt, tzs, tzw):
    b, s, _ = proj.shape
    nd = s // BK
    ncmp = kcmp.shape[2]
    nslc = covt.shape[0]
    nwin = tzw.shape[1]
    gw = NSA_GROUP * LANE
    kv = lambda cb: pl.BlockSpec((1, s, LANE), lambda bi, g, qi: (bi, 0, cb + g))
    return pl.pallas_call(
        _nsa_body,
        grid=(b, NSA_KV, s // BQ),
        in_specs=[pl.BlockSpec((1, BQ, gw), lambda bi, g, qi: (bi, qi, CB_QB // NSA_GROUP + g)),
                  pl.BlockSpec((1, 1, ncmp, HEAD_DIM), lambda bi, g, qi: (bi, g, 0, 0)),
                  pl.BlockSpec((1, 1, ncmp, HEAD_DIM), lambda bi, g, qi: (bi, g, 0, 0)),
                  kv(CB_KS), kv(CB_VS), kv(CB_KW), kv(CB_VW),
                  pl.BlockSpec((1, BQ, LANE), lambda bi, g, qi: (bi, qi, g)),
                  pl.BlockSpec((NSA_GROUP, ncmp, BQ), lambda bi, g, qi: (g, 0, qi)),
                  pl.BlockSpec((nslc, ncmp), lambda bi, g, qi: (0, 0)),
                  pl.BlockSpec((NSA_GROUP, nd, BQ, BK), lambda bi, g, qi: (g, 0, 0, 0)),
                  pl.BlockSpec((NSA_GROUP, nwin, BQ, BK), lambda bi, g, qi: (g, 0, 0, 0))],
        out_specs=pl.BlockSpec((1, BQ, gw), lambda bi, g, qi: (bi, qi, g)),
        out_shape=jax.ShapeDtypeStruct((b, s, B_HEADS * LANE), F32),
        scratch_shapes=[pltpu.VMEM((nd, BQ, BK), F32)],
        compiler_params=_params("parallel", "parallel", "arbitrary"),
        name="nsa_attention",
    )(proj, kcmp, vcmp, proj, proj, proj, proj, gates, bct, covt, tzs, tzw)


def _moba_body(q_ref, k_ref, v_ref, avg_ref, tz_ref, o_ref, mask_ref):
    qi = pl.program_id(2)
    scale = HEAD_DIM ** -0.5
    nblk = avg_ref.shape[0]
    q = q_ref[0]
    kmean = jnp.dot(avg_ref[...], k_ref[0], preferred_element_type=F32)
    gate = _nt(kmean, q.astype(F32), precision=lax.Precision.HIGHEST)
    nb = lax.broadcasted_iota(I32, (nblk, BQ), 0)
    past = nb < qi
    beaten = jnp.zeros((nblk, BQ), F32)
    for np_ in range(nblk):
        row = gate[np_:np_ + 1, :]
        wins = (np_ < qi) & ((row > gate) | ((row == gate) & (np_ < nb)))
        beaten = beaten + wins.astype(F32)
    selt = (past & (beaten < MOBA_TOPK)).astype(BF16)
    eye = (lax.broadcasted_iota(I32, (BQ, BQ), 0) == lax.broadcasted_iota(I32, (BQ, BQ), 1)).astype(BF16)
    selq = _nt(eye, selt)
    for n in range(nblk):
        mask_ref[n] = jnp.broadcast_to(selq[:, n:n + 1], (BQ, BK))

    def score(kj, s):
        keep = (mask_ref[kj] > 0.5) | (kj == qi)
        return jnp.where(keep, s + tz_ref[0, qi - kj], NEG)

    o_ref[0] = _flash([q], k_ref, v_ref, 0, qi + 1, [score], scale)[0]


def _moba(proj, avg, tzc):
    b, s, _ = proj.shape
    nd = s // BK
    nblk = avg.shape[0]
    return pl.pallas_call(
        _moba_body,
        grid=(b, C_HEADS, s // BQ),
        in_specs=[pl.BlockSpec((1, BQ, LANE), lambda bi, h, qi: (bi, qi, CB_QC + h)),
                  pl.BlockSpec((1, s, LANE), lambda bi, h, qi: (bi, 0, CB_KCC + h)),
                  pl.BlockSpec((1, s, LANE), lambda bi, h, qi: (bi, 0, CB_VCC + h)),
                  pl.BlockSpec((nblk, s), lambda bi, h, qi: (0, 0)),
                  pl.BlockSpec((1, nd, BQ, BK), lambda bi, h, qi: (h, 0, 0, 0))],
        out_specs=pl.BlockSpec((1, BQ, LANE), lambda bi, h, qi: (bi, qi, h)),
        out_shape=jax.ShapeDtypeStruct((b, s, C_HEADS * LANE), F32),
        scratch_shapes=[pltpu.VMEM((nblk, BQ, BK), F32)],
        compiler_params=_params("parallel", "parallel", "arbitrary"),
        name="moba_attention",
    )(proj, proj, proj, avg, tzc)


def _outproj_body(oa_ref, ob_ref, oc_ref, g_ref, w_ref, h_ref, o_ref, y_ref):
    @pl.when(pl.program_id(1) == 0)
    def _():
        c0 = 0
        for ref in (oa_ref, ob_ref, oc_ref):
            x = ref[...]
            wd = x.shape[1]
            ms = jnp.mean(x * x, axis=-1, keepdims=True)
            y_ref[:, c0:c0 + wd] = (x * lax.rsqrt(ms + EPS) * g_ref[:, c0:c0 + wd]).astype(BF16)
            c0 += wd

    o_ref[...] = h_ref[...] + jnp.dot(y_ref[...], w_ref[...], preferred_element_type=F32)


def _outproj(oa, ob, oc, g, w, h, tm, tn):
    t, d = h.shape
    row = lambda a: pl.BlockSpec((tm, a.shape[1]), lambda i, j: (i, 0))
    return pl.pallas_call(
        _outproj_body,
        grid=(t // tm, d // tn),
        in_specs=[row(oa), row(ob), row(oc),
                  pl.BlockSpec((1, d), lambda i, j: (0, 0)),
                  pl.BlockSpec((d, tn), lambda i, j: (0, j)),
                  pl.BlockSpec((tm, tn), lambda i, j: (i, j))],
        out_specs=pl.BlockSpec((tm, tn), lambda i, j: (i, j)),
        out_shape=jax.ShapeDtypeStruct((t, d), F32),
        scratch_shapes=[pltpu.VMEM((tm, d), BF16)],
        compiler_params=_params("parallel", "arbitrary"),
        name="out_projection",
    )(oa, ob, oc, g.reshape(1, d), w, h)


def _topk_rows(x, k, payload=None):
    n, tm = x.shape
    rows = lax.broadcasted_iota(I32, x.shape, 0)
    slot = lax.broadcasted_iota(I32, (k, tm), 0)
    vals = jnp.zeros((k, tm), F32)
    idxs = jnp.zeros((k, tm), I32)
    for it in range(k):
        mx = jnp.max(x, axis=0, keepdims=True)
        idx = jnp.min(jnp.where(x == mx, rows, n), axis=0, keepdims=True)
        hit = rows == idx
        if payload is not None:
            idx = jnp.sum(jnp.where(hit, payload, 0), axis=0, keepdims=True)
        vals = jnp.where(slot == it, mx, vals)
        idxs = jnp.where(slot == it, idx, idxs)
        x = jnp.where(hit, -jnp.inf, x)
    return vals, idxs


def _oddeven_merge_sort(n):
    pairs = []
    p = 1
    while p < n:
        k = p
        while k >= 1:
            for j in range(k % p, n - k, 2 * k):
                for i in range(min(k, n - j - k)):
                    if (i + j) // (2 * p) == (i + j + k) // (2 * p):
                        pairs.append((i + j, i + j + k))
            k //= 2
        p *= 2
    return pairs


def _topk_rows_sorted(x, k):
    n, tm = x.shape
    assert n == 8 * k
    rows = lax.broadcasted_iota(I32, (n, tm), 0)
    slot = lax.broadcasted_iota(I32, (k, tm), 0)
    out_v, out_i = [], []
    for c in range(tm // LANE):
        lanes = slice(c * LANE, (c + 1) * LANE)
        v = [x[8 * r:8 * r + 8, lanes] for r in range(k)]
        ix = [rows[8 * r:8 * r + 8, lanes] for r in range(k)]
        for i, j in _oddeven_merge_sort(k):
            va, vb, ia, ib = v[i], v[j], ix[i], ix[j]
            keep = (va > vb) | ((va == vb) & (ia < ib))
            v[i], v[j] = jnp.maximum(va, vb), jnp.minimum(va, vb)
            ix[i], ix[j] = jnp.where(keep, ia, ib), jnp.where(keep, ib, ia)
        vals = jnp.zeros((k, LANE), F32)
        idxs = jnp.zeros((k, LANE), I32)
        for t in range(k):
            mx = jnp.max(v[0], axis=0, keepdims=True)
            best = jnp.min(jnp.where(v[0] == mx, ix[0], n), axis=0, keepdims=True)
            win = ix[0] == best
            vals = jnp.where(slot[:, lanes] == t, mx, vals)
            idxs = jnp.where(slot[:, lanes] == t, best, idxs)
            for i in range(k - 1 - t):
                v[i] = jnp.where(win, v[i + 1], v[i])
                ix[i] = jnp.where(win, ix[i + 1], ix[i])
        out_v.append(vals)
        out_i.append(idxs)
    return jnp.concatenate(out_v, axis=1), jnp.concatenate(out_i, axis=1)


def _best_pairs(s0, i0, s1, i1):
    k, tm = s0.shape
    half = k // 2
    assert half == 8
    sub = lax.broadcasted_iota(I32, (half, tm), 0)
    slot = lax.broadcasted_iota(I32, (k, tm), 0)
    lv, le = [], []
    for b in range(k):
        deep = jnp.where(sub < min(k // (b + 1), half), s0[0:half] + s1[b:b + 1], -jnp.inf)
        lv.append(deep)
        le.append(i0[0:half] * PEER_NKEYS + i1[b:b + 1])
    tail_v = s0[half:] + s1[0:1]
    tail_e = i0[half:] * PEER_NKEYS + i1[0:1]
    vals = jnp.zeros((k, tm), F32)
    ids = jnp.zeros((k, tm), I32)
    for t in range(k):
        mx = jnp.max(jnp.maximum(lv[0], tail_v), axis=0, keepdims=True)
        first = jnp.minimum(jnp.where(lv[0] == mx, sub, k), jnp.where(tail_v == mx, sub + half, k))
        best = jnp.min(first, axis=0, keepdims=True)
        win = sub == best
        win_tail = sub + half == best
        expert = jnp.sum(jnp.where(win, le[0], 0) + jnp.where(win_tail, tail_e, 0), axis=0, keepdims=True)
        vals = jnp.where(slot == t, mx, vals)
        ids = jnp.where(slot == t, expert, ids)
        for i in range(k - 1 - t):
            lv[i] = jnp.where(win, lv[i + 1], lv[i])
            le[i] = jnp.where(win, le[i + 1], le[i])
        tail_v = jnp.where(win_tail, -jnp.inf, tail_v)
    return vals, ids


def _peer_topk_body(h_ref, ln_ref, wq_ref, keys_ref, e_ref, g_ref):
    x = h_ref[...]
    ms = jnp.mean(x * x, axis=-1, keepdims=True)
    xn = (x * lax.rsqrt(ms + EPS) * ln_ref[...]).astype(BF16)
    dq = keys_ref.shape[2]
    for h in range(PEER_HEADS):
        qh = jnp.dot(xn, wq_ref[:, 2 * h * dq:2 * (h + 1) * dq], preferred_element_type=F32)
        tops = []
        for c in range(2):
            sc = _nt(keys_ref[2 * h + c], qh[:, c * dq:(c + 1) * dq])
            tops.append(_topk_rows_sorted(sc, PEER_TOPK))
        (s0, i0), (s1, i1) = tops
        bs, be = _best_pairs(s0, i0, s1, i1)
        e = jnp.exp(bs - jnp.max(bs, axis=0, keepdims=True))
        g_ref[h * PEER_TOPK:(h + 1) * PEER_TOPK, :] = e / jnp.sum(e, axis=0, keepdims=True)
        e_ref[h * PEER_TOPK:(h + 1) * PEER_TOPK, :] = be


def _peer_topk(h, ln, wq, keys, tm):
    t, d = h.shape
    hk = PEER_HEADS * PEER_TOPK
    assert wq.shape == (d, keys.shape[0] * keys.shape[2]) and keys.shape[2] == LANE
    return pl.pallas_call(
        _peer_topk_body,
        grid=(t // tm,),
        in_specs=[pl.BlockSpec((tm, d), lambda i: (i, 0)),
                  pl.BlockSpec((1, d), lambda i: (0, 0)),
                  pl.BlockSpec(wq.shape, lambda i: (0, 0)),
                  pl.BlockSpec(keys.shape, lambda i: (0, 0, 0))],
        out_specs=[pl.BlockSpec((hk, tm), lambda i: (0, i)),
                   pl.BlockSpec((hk, tm), lambda i: (0, i))],
        out_shape=[jax.ShapeDtypeStruct((hk, t), I32), jax.ShapeDtypeStruct((hk, t), F32)],
        compiler_params=_params("parallel"),
        name="peer_topk",
    )(h, ln.reshape(1, d), wq, keys)


PEER_NBUF = 16
PEER_AHEAD = 14


def _gelu(x):
    return 0.5 * x * (1.0 + lax.erf(x * (2.0 ** -0.5)))


def _peer_expert_body(ids0_ref, idsn_ref, h_ref, ln_ref, g_ref, uvw_ref, o_ref, xn_ref, acc_ref, lhs_ref,
                      buf_ref, sem_ref):
    tb, d = h_ref.shape
    hk = idsn_ref.shape[1]
    nrg = hk // 8
    nlt = d // LANE
    assert nrg == nlt and hk % (2 * nlt) == 0
    step = pl.program_id(0)
    x = h_ref[...]
    ms = jnp.mean(x * x, axis=-1, keepdims=True)
    xn_ref[...] = x * lax.rsqrt(ms + EPS) * ln_ref[...]

    def issue(ids_ref, row, slot, k0, k1):
        for k in range(k0, k1):
            e = ids_ref[row, k]
            pltpu.make_async_copy(uvw_ref.at[e], buf_ref.at[slot, :, k, :],
                                  sem_ref.at[slot]).start(priority=k % 2)

    def wait_all(slot):
        pltpu.make_async_copy(buf_ref.at[slot], buf_ref.at[slot], sem_ref.at[slot]).wait()

    def x_tiles(t):
        xrow = xn_ref[pl.ds(t, 1), :]
        return [jnp.broadcast_to(xrow[:, j * LANE:(j + 1) * LANE], (8, LANE)) for j in range(nlt)]

    def dots(xb, slot, r):
        a = None
        for j in range(nlt):
            w = buf_ref[slot, j, r * 8:(r + 1) * 8, :]
            pr = lax.bitcast_convert_type(lax.shift_left(w, jnp.uint32(16)), F32) * xb[j]
            a = pr if a is None else a + pr
        return a

    def coefficients(t, slot):
        act = jnp.sum(acc_ref[slot % 2].T, axis=0, keepdims=True)
        coef = jnp.broadcast_to(_gelu(act) * g_ref[pl.ds(t, 1), :], (8, hk))
        chi = coef.astype(BF16).astype(F32)
        return jnp.concatenate([chi, coef - chi], axis=0).astype(BF16)

    def weighted(lhs, slot, j):
        w = buf_ref[slot, j]
        v = lax.bitcast_convert_type(w & jnp.uint32(0xFFFF0000), F32).astype(BF16)
        yj = jnp.dot(lhs, v, preferred_element_type=F32)
        return yj[0:1, :] + yj[8:9, :]

    def flush(done):
        ya, yb, tp = done
        y = jnp.concatenate([ya[j:j + 1, :] for j in range(8)] + [yb[j:j + 1, :] for j in range(8)], axis=1)
        o_ref[pl.ds(tp, 1), :] = h_ref[pl.ds(tp, 1), :] + y

    def turn(t, slot, stages, done):
        tgt = (slot + PEER_AHEAD) % PEER_NBUF
        far = (slot + 2) % PEER_NBUF
        per = hk // (2 * nlt)
        if stages >= 3:
            wait_all(far)
            xb = x_tiles(t + 2)
        lhs = lhs_ref[slot % 2]
        ys, parts = [], []
        for c in range(nlt):
            if stages >= 3:
                parts.append(dots(xb, far, c))
            issue(idsn_ref, t, tgt, 2 * c * per, (2 * c + 1) * per)
            if 2 * c < nlt:
                ys.append(weighted(lhs, slot, 2 * c))
                ys.append(weighted(lhs, slot, 2 * c + 1))
            if 2 * c == nlt and stages >= 2:
                lhs_next = coefficients(t + 1, slot + 1)
            issue(idsn_ref, t, tgt, (2 * c + 1) * per, (2 * c + 2) * per)
        if stages >= 3:
            acc_ref[slot % 2] = jnp.concatenate(parts, axis=0)
        if stages >= 2:
            lhs_ref[(slot + 1) % 2] = lhs_next
        flush(done)
        return (jnp.concatenate(ys[:8], axis=0), jnp.concatenate(ys[8:], axis=0), t)

    @pl.when(step == 0)
    def _():
        for t0 in range(PEER_AHEAD):
            issue(ids0_ref, t0, t0, 0, hk)

    for t0 in range(2):
        wait_all(t0)
        xb0 = x_tiles(t0)
        acc_ref[t0] = jnp.concatenate([dots(xb0, t0, r) for r in range(nrg)], axis=0)
    lhs_ref[0] = coefficients(0, 0)

    def ring(i, stacks):
        done = stacks + (jnp.maximum(i * PEER_NBUF - 1, 0),)
        for slot in range(PEER_NBUF):
            done = turn(i * PEER_NBUF + slot, slot, 3, done)
        return done[:2]

    zero = jnp.zeros((8, LANE), F32)
    nring = (tb - 2) // PEER_NBUF
    done = lax.fori_loop(0, nring, ring, (zero, zero)) + (nring * PEER_NBUF - 1,)
    for t in range(nring * PEER_NBUF, tb - 2):
        done = turn(t, t % PEER_NBUF, 3, done)
    done = turn(tb - 2, (tb - 2) % PEER_NBUF, 2, done)
    done = turn(tb - 1, (tb - 1) % PEER_NBUF, 1, done)
    flush(done)

    @pl.when(step == pl.num_programs(0) - 1)
    def _():
        for t0 in range(PEER_AHEAD):
            wait_all((tb + t0) % PEER_NBUF)


def _peer_experts(ids, gates, h, ln, slabs, tb):
    t, d = h.shape
    hk = ids.shape[1]
    assert hk == LANE and tb % PEER_NBUF == 0 and PEER_NBUF % 2 == 0 and d % LANE == 0
    assert PEER_AHEAD <= PEER_NBUF - 2
    ids_next = jnp.concatenate([ids[PEER_AHEAD:], ids[:PEER_AHEAD]], axis=0)
    assert slabs.shape[1:] == (d // LANE, LANE)
    return pl.pallas_call(
        _peer_expert_body,
        grid=(t // tb,),
        in_specs=[pl.BlockSpec((PEER_NBUF, hk), lambda i: (0, 0), memory_space=pltpu.SMEM),
                  pl.BlockSpec((tb, hk), lambda i: (i, 0), memory_space=pltpu.SMEM),
                  pl.BlockSpec((tb, d), lambda i: (i, 0)),
                  pl.BlockSpec((1, d), lambda i: (0, 0)),
                  pl.BlockSpec((tb, hk), lambda i: (i, 0)),
                  pl.BlockSpec(memory_space=pl.ANY)],
        out_specs=pl.BlockSpec((tb, d), lambda i: (i, 0)),
        out_shape=jax.ShapeDtypeStruct((t, d), F32),
        scratch_shapes=[pltpu.VMEM((tb, d), F32),
                        pltpu.VMEM((2, hk, LANE), F32),
                        pltpu.VMEM((2, 16, hk), BF16),
                        pltpu.VMEM((PEER_NBUF, d // LANE, hk, LANE), jnp.uint32),
                        pltpu.SemaphoreType.DMA((PEER_NBUF,))],
        compiler_params=_params("arbitrary"),
        name="peer_experts",
    )(ids[:PEER_NBUF], ids_next, h, ln.reshape(1, d), gates, slabs)


def _ple_body(x_ref, g_ref, wg_ref, p_ref, wp_ref, h_ref, o_ref, xn_ref):
    @pl.when(pl.program_id(1) == 0)
    def _():
        x = x_ref[...]
        ms = jnp.mean(x * x, axis=-1, keepdims=True)
        xn_ref[...] = (x * lax.rsqrt(ms + EPS) * g_ref[...]).astype(BF16)

    z = jnp.dot(xn_ref[...], wg_ref[...], preferred_element_type=F32)
    pp = jnp.dot(p_ref[...].astype(BF16), wp_ref[...], preferred_element_type=F32)
    o_ref[...] = h_ref[...] + jax.nn.sigmoid(z) * pp


def _ple(h, g, wg, p, wp, tm, tn):
    t, d = h.shape
    pd = p.shape[1]
    return pl.pallas_call(
        _ple_body,
        grid=(t // tm, d // tn),
        in_specs=[pl.BlockSpec((tm, d), lambda i, j: (i, 0)),
                  pl.BlockSpec((1, d), lambda i, j: (0, 0)),
                  pl.BlockSpec((d, tn), lambda i, j: (0, j)),
                  pl.BlockSpec((tm, pd), lambda i, j: (i, 0)),
                  pl.BlockSpec((pd, tn), lambda i, j: (0, j)),
                  pl.BlockSpec((tm, tn), lambda i, j: (i, j))],
        out_specs=pl.BlockSpec((tm, tn), lambda i, j: (i, j)),
        out_shape=jax.ShapeDtypeStruct((t, d), F32),
        scratch_shapes=[pltpu.VMEM((tm, d), BF16)],
        compiler_params=_params("parallel", "arbitrary"),
        name="ple_gate",
    )(h, g.reshape(1, d), wg, p, wp, h)


def _rmsnorm_body(x_ref, g_ref, o_ref):
    x = x_ref[...]
    ms = jnp.mean(x * x, axis=-1, keepdims=True)
    o_ref[...] = x * lax.rsqrt(ms + EPS) * g_ref[...]


def _rmsnorm(x, g, tm):
    t, d = x.shape
    return pl.pallas_call(
        _rmsnorm_body,
        grid=(t // tm,),
        in_specs=[pl.BlockSpec((tm, d), lambda i: (i, 0)), pl.BlockSpec((1, d), lambda i: (0, 0))],
        out_specs=pl.BlockSpec((tm, d), lambda i: (i, 0)),
        out_shape=jax.ShapeDtypeStruct((t, d), F32),
        compiler_params=_params("parallel"),
        name="final_rmsnorm",
    )(x, g.reshape(1, d))


def _toeplitz(vec, nq, nk):
    assert nq == nk
    lead = vec.shape[:-1]
    g = int(np.prod(lead))
    w = jnp.concatenate([jnp.zeros(lead + (1,), vec.dtype), vec[..., ::-1]], axis=-1).reshape(g, 1, 2 * nk)

    def body(w_ref, o_ref):
        rows = jnp.broadcast_to(w_ref[0], (nq, 2 * nk))
        o_ref[0] = pltpu.roll(rows, 0, 1, stride=1, stride_axis=0)[:, nk:]

    out = pl.pallas_call(
        body,
        grid=(g,),
        in_specs=[pl.BlockSpec((1, 1, 2 * nk), lambda i: (i, 0, 0))],
        out_specs=pl.BlockSpec((1, nq, nk), lambda i: (i, 0, 0)),
        out_shape=jax.ShapeDtypeStruct((g, nq, nk), vec.dtype),
        compiler_params=_params("parallel"),
        name="toeplitz_tiles",
    )(w)
    return out.reshape(lead + (nq, nk))


def _bias_tables(rel_bias, s):
    nd = s // BK
    bdt = rel_bias[_rel_bucket(jnp.arange(s))].astype(F32).T
    dd = (jnp.arange(nd)[:, None, None] * BK + jnp.arange(BQ)[None, :, None]
          - jnp.arange(BK)[None, None, :])
    ext = jnp.pad(bdt, ((0, 0), (BK - 1, 0)))
    segs = jnp.stack([ext[:, dl * BK:dl * BK + BQ + BK - 1] for dl in range(nd)], axis=1)
    tz = _toeplitz(segs, BQ, BK)
    causal = dd >= 0
    mult = sum(((dd % dil == 0) & (dd // dil <= window // dil)).astype(F32) for window, dil in DIL_PATTERNS)
    ok = causal & (mult > 0)
    tzd = jnp.where(ok, tz[:A_HEADS] + jnp.log(jnp.where(ok, mult, 1.0)), NEG)
    tzb = tz[A_HEADS:A_HEADS + B_HEADS]
    tzs = jnp.where(causal, tzb, NEG)
    nwin = -(-(NSA_WINDOW - 1) // BK) + 1
    tzw = jnp.where(causal & (dd <= NSA_WINDOW - 1), tzb, NEG)[:, :nwin]
    tzc = jnp.where(causal, tz[A_HEADS + B_HEADS:], NEG)
    ncmp = s // CMP_STRIDE
    na = s // CMP_STRIDE
    nvec = na + ncmp - 1
    lo = CMP_STRIDE * (ncmp - 1) + CMP_LEN - 1
    bdb = bdt[A_HEADS:A_HEADS + B_HEADS]
    gext = jnp.concatenate([jnp.full((B_HEADS, lo), NEG, F32), bdb], axis=1)
    vecs = gext[:, :CMP_STRIDE * nvec].reshape(B_HEADS, nvec, CMP_STRIDE).transpose(0, 2, 1)
    bcq = _toeplitz(vecs, na, ncmp)
    bcq = bcq.transpose(0, 2, 1, 3).reshape(B_HEADS, s, ncmp)
    bct = jnp.transpose(bcq, (0, 2, 1))
    nslc = s // SLC_LEN
    cstart = jnp.arange(ncmp) * CMP_STRIDE
    sstart = jnp.arange(nslc) * SLC_LEN
    covt = ((cstart[None, :] < sstart[:, None] + SLC_LEN)
            & (cstart[None, :] + CMP_LEN > sstart[:, None])).astype(F32)
    nblk = s // MOBA_BLK
    avg = ((jnp.arange(s)[None, :] // MOBA_BLK == jnp.arange(nblk)[:, None]).astype(F32)
           / MOBA_BLK).astype(BF16)
    return tzd, tzs, tzw, tzc, bct, covt, avg


def _pack_body(u_ref, v_ref, o_ref):
    _, te, d = u_ref.shape
    nlt = d // LANE
    ub = lax.bitcast_convert_type(u_ref[0].astype(BF16).astype(F32), jnp.uint32)
    vb = lax.bitcast_convert_type(v_ref[0].astype(BF16).astype(F32), jnp.uint32)
    word = lax.shift_right_logical(ub, jnp.uint32(16)) | vb
    for j in range(nlt):
        o_ref[pl.ds(j, te, stride=nlt), :] = word[:, j * LANE:(j + 1) * LANE]


def _pack_uv(u, v, layer, te=256):
    _, e, d = u.shape
    nlt = d // LANE
    spec = pl.BlockSpec((1, te, d), lambda i: (layer, i, 0))
    out = pl.pallas_call(
        _pack_body,
        grid=(e // te,),
        in_specs=[spec, spec],
        out_specs=pl.BlockSpec((te * nlt, LANE), lambda i: (i, 0)),
        out_shape=jax.ShapeDtypeStruct((e * nlt, LANE), jnp.uint32),
        compiler_params=_params("parallel"),
        name="pack_experts",
    )(u, v)
    return out.reshape(e, nlt, LANE)


GATE_COL0 = 3 * A_HEADS * LANE + B_HEADS * LANE + 6 * NSA_KV * LANE
GATE_COLS = 3 * B_HEADS


def _w_main_body(w_ref, o_ref):
    x = w_ref[0]
    o_ref[:, :GATE_COL0] = x[:, :GATE_COL0].astype(BF16)
    o_ref[:, GATE_COL0:] = x[:, GATE_COL0 + GATE_COLS:].astype(BF16)


def _reorder_w_in(w_in, layer, tk=256):
    _, d, cols = w_in.shape
    main = pl.pallas_call(
        _w_main_body,
        grid=(d // tk,),
        in_specs=[pl.BlockSpec((1, tk, cols), lambda i: (layer, i, 0))],
        out_specs=pl.BlockSpec((tk, MAIN_COLS), lambda i: (i, 0)),
        out_shape=jax.ShapeDtypeStruct((d, MAIN_COLS), BF16),
        compiler_params=_params("parallel"),
        name="reorder_w_in",
    )(w_in)
    per = GATE_COLS // NSA_KV
    wg = w_in[layer, :, GATE_COL0:GATE_COL0 + GATE_COLS]
    gate = jnp.concatenate([jnp.pad(wg[:, g * per:(g + 1) * per], ((0, 0), (0, LANE - per))) for g in range(NSA_KV)],
                           axis=1)
    return main, gate.astype(BF16)


def kernel(x, p, ln_mix, w_in, cmp_wk, cmp_wv, cmp_pos, out_norm, w_out, rel_bias, ln_ffn, peer_wq,
           peer_keys, peer_u, peer_v, ln_ple, ple_gate, ple_proj, ln_final):
    b, s, d = x.shape
    t = b * s
    depth = w_in.shape[0]
    assert s % BQ == 0 and BQ == BK == MOBA_BLK and d % LANE == 0
    tm = 1024 if t % 1024 == 0 else 512
    tzd, tzs, tzw, tzc, bct, covt, avg = _bias_tables(rel_bias, s)
    h = x.reshape(t, d)
    for i in range(depth):
        w_main, w_gate = _reorder_w_in(w_in, i)
        proj = _normmm(h, ln_mix[i], w_main, BF16, tm, 1024, "in_projection").reshape(b, s, MAIN_COLS)
        gates = _normmm(h, ln_mix[i], w_gate, F32, tm, w_gate.shape[1], "gate_projection")
        gates = gates.reshape(b, s, NSA_KV * LANE)
        oa = _dilated(proj, tzd)
        kcmp, vcmp = _compress(proj, cmp_wk[i], cmp_wv[i], cmp_pos[i])
        ob = _nsa(proj, gates, kcmp, vcmp, bct, covt, tzs, tzw)
        oc = _moba(proj, avg, tzc)
        h = _outproj(oa.reshape(t, -1), ob.reshape(t, -1), oc.reshape(t, -1), out_norm[i],
                     w_out[i].astype(BF16), h, tm, 1024)
        keys = peer_keys[i].reshape(PEER_HEADS * 2, PEER_NKEYS, -1)
        e_t, g_t = _peer_topk(h, ln_ffn[i], peer_wq[i].astype(BF16), keys, LANE)
        h = _peer_experts(e_t.T, g_t.T, h, ln_ffn[i], _pack_uv(peer_u, peer_v, i), LANE)
        h = _ple(h, ln_ple[i], ple_gate[i].astype(BF16), p[i].reshape(t, -1), ple_proj[i].astype(BF16),
                 tm, 1024)
    return _rmsnorm(h, ln_final, tm).reshape(b, s, d)
```

```python
import math

import numpy as np
import jax
import jax.numpy as jnp
from jax import lax
from jax.experimental import pallas as pl
from jax.experimental.pallas import tpu as pltpu

F32 = jnp.float32
BF16 = jnp.bfloat16
I32 = jnp.int32

HEAD_DIM = 128
A_HEADS, B_HEADS, C_HEADS = 6, 6, 4
NSA_KV, NSA_GROUP = 2, 3
DIL_PATTERNS = ((128, 1), (512, 4), (2048, 16))
CMP_LEN, CMP_STRIDE = 32, 16
SLC_LEN, SLC_TOPN = 64, 8
NSA_WINDOW = 512
FORCE = 1e4
MOBA_BLK, MOBA_TOPK = 256, 3
REL_BUCKETS, REL_MAX_DIST = 32, 1024
PEER_HEADS, PEER_NKEYS, PEER_TOPK = 8, 128, 16
EPS = 1e-6

LANE = 128
BQ = 256
BK = 256
NEG = -1e30
HALF_NEG = -5e29
VMEM_LIMIT = 56 * 1024 * 1024

CB_QA, CB_KA, CB_VA = 0, 6, 12
CB_QB, CB_KC, CB_VC, CB_KS, CB_VS, CB_KW, CB_VW = 18, 24, 26, 28, 30, 32, 34
CB_QC, CB_KCC, CB_VCC = 36, 40, 44
MAIN_COLS = 48 * LANE


def _nt(a, b, precision=None):
    return lax.dot_general(a, b, (((1,), (1,)), ((), ())), preferred_element_type=F32,
                           precision=precision)


def _rel_bucket(dist):
    exact = REL_BUCKETS // 2
    d = jnp.maximum(dist, 0)
    logd = jnp.log(jnp.maximum(d, 1).astype(F32) / exact)
    large = exact + (logd / math.log(REL_MAX_DIST / exact) * (REL_BUCKETS - exact)).astype(I32)
    large = jnp.clip(large, exact, REL_BUCKETS - 1)
    return jnp.where(d < exact, d, large)


def _masked_softmax(s, axis):
    valid = s > HALF_NEG
    m = jnp.max(s, axis=axis, keepdims=True)
    e = jnp.where(valid, jnp.exp(s - m), 0.0)
    z = jnp.sum(e, axis=axis, keepdims=True)
    zs = jnp.where(z > 0, z, 1.0)
    return e * (1.0 / zs)


def _params(*sem):
    return pltpu.CompilerParams(dimension_semantics=sem, vmem_limit_bytes=VMEM_LIMIT)


def _normmm_body(x_ref, g_ref, w_ref, o_ref, xn_ref):
    @pl.when(pl.program_id(1) == 0)
    def _():
        x = x_ref[...]
        ms = jnp.mean(x * x, axis=-1, keepdims=True)
        xn_ref[...] = (x * lax.rsqrt(ms + EPS) * g_ref[...]).astype(BF16)

    o_ref[...] = jnp.dot(xn_ref[...], w_ref[...], preferred_element_type=F32).astype(o_ref.dtype)


def _normmm(x, g, w, out_dtype, tm, tn, name):
    t, d = x.shape
    n = w.shape[1]
    return pl.pallas_call(
        _normmm_body,
        grid=(t // tm, n // tn),
        in_specs=[pl.BlockSpec((tm, d), lambda i, j: (i, 0)),
                  pl.BlockSpec((1, d), lambda i, j: (0, 0)),
                  pl.BlockSpec((d, tn), lambda i, j: (0, j))],
        out_specs=pl.BlockSpec((tm, tn), lambda i, j: (i, j)),
        out_shape=jax.ShapeDtypeStruct((t, n), out_dtype),
        scratch_shapes=[pltpu.VMEM((tm, d), BF16)],
        compiler_params=_params("parallel", "arbitrary"),
        name=name,
    )(x, g.reshape(1, d), w)


def _flash(qs, k_ref, v_ref, lo, hi, score_fns, scale):
    def tile(kj):
        off = pl.multiple_of(kj * BK, BK)
        return k_ref[0, pl.ds(off, BK), :], v_ref[0, pl.ds(off, BK), :]

    def update(state, blocks):
        m, l, acc = state
        m_new = m
        for s, _ in blocks:
            m_new = jnp.maximum(m_new, jnp.max(s, axis=1, keepdims=True))
        alpha = jnp.exp(m - m_new)
        l = alpha * l
        acc = alpha * acc
        for s, v in blocks:
            p = jnp.exp(s - m_new)
            l = l + jnp.sum(p, axis=1, keepdims=True)
            acc = acc + jnp.dot(p.astype(BF16), v, preferred_element_type=F32)
        return m_new, l, acc

    def step(kjs, states):
        kv = [tile(kj) for kj in kjs]
        return tuple(
            update(states[i], [(score_fns[i](kj, _nt(q, k) * scale), v) for kj, (k, v) in zip(kjs, kv)])
            for i, q in enumerate(qs))

    states = tuple((jnp.full((BQ, 1), NEG, F32), jnp.zeros((BQ, 1), F32), jnp.zeros((BQ, HEAD_DIM), F32))
                   for _ in qs)
    def run(states, start, width, count):
        def group(p, st):
            return step([start + width * p + i for i in range(width)], st)

        return lax.fori_loop(0, count, group, states)

    n = hi - lo
    if len(qs) == 1:
        full = lax.shift_right_logical(n, 2)
        states = run(states, lo, 4, full)
        rest = n - 4 * full
        for width in (3, 2, 1):
            states = run(states, lo + 4 * full, width, jnp.where(rest == width, 1, 0))
    else:
        pairs = lax.shift_right_logical(n, 1)
        states = run(states, lo, 2, pairs)
        states = run(states, lo + 2 * pairs, 1, n - 2 * pairs)
    return [acc / l for _, l, acc in states]


def _dilated_body(q_ref, k_ref, v_ref, tz_ref, o_ref):
    qi = pl.program_id(2)
    scale = HEAD_DIM ** -0.5

    def score(kj, s):
        return s + tz_ref[0, qi - kj]

    o_ref[0] = _flash([q_ref[0]], k_ref, v_ref, 0, qi + 1, [score], scale)[0]


def _dilated(proj, tzd):
    b, s, _ = proj.shape
    nd = s // BK
    return pl.pallas_call(
        _dilated_body,
        grid=(b, A_HEADS, s // BQ),
        in_specs=[pl.BlockSpec((1, BQ, LANE), lambda bi, h, qi: (bi, qi, CB_QA + h)),
                  pl.BlockSpec((1, s, LANE), lambda bi, h, qi: (bi, 0, CB_KA + h)),
                  pl.BlockSpec((1, s, LANE), lambda bi, h, qi: (bi, 0, CB_VA + h)),
                  pl.BlockSpec((1, nd, BQ, BK), lambda bi, h, qi: (h, 0, 0, 0))],
        out_specs=pl.BlockSpec((1, BQ, LANE), lambda bi, h, qi: (bi, qi, h)),
        out_shape=jax.ShapeDtypeStruct((b, s, A_HEADS * LANE), F32),
        compiler_params=_params("parallel", "parallel", "arbitrary"),
        name="dilated_attention",
    )(proj, proj, proj, tzd)


def _compress_body(kc_ref, vc_ref, wk_ref, wv_ref, pos_ref, ko_ref, vo_ref, xk_ref, xv_ref):
    s = kc_ref.shape[1]
    ncmp = ko_ref.shape[2]
    xk_ref[pl.ds(0, s), :] = kc_ref[0].astype(F32)
    xv_ref[pl.ds(0, s), :] = vc_ref[0].astype(F32)
    xk_ref[pl.ds(s, CMP_LEN), :] = jnp.zeros((CMP_LEN, HEAD_DIM), F32)
    xv_ref[pl.ds(s, CMP_LEN), :] = jnp.zeros((CMP_LEN, HEAD_DIM), F32)
    acck = jnp.zeros((ncmp, HEAD_DIM), F32)
    accv = jnp.zeros((ncmp, HEAD_DIM), F32)
    for l in range(CMP_LEN):
        pos = pos_ref[pl.ds(l, 1), :]
        rk = (xk_ref[pl.ds(l, ncmp, stride=CMP_STRIDE), :] + pos).astype(BF16)
        rv = (xv_ref[pl.ds(l, ncmp, stride=CMP_STRIDE), :] + pos).astype(BF16)
        acck = acck + jnp.dot(rk, wk_ref[l], preferred_element_type=F32)
        accv = accv + jnp.dot(rv, wv_ref[l], preferred_element_type=F32)
    ko_ref[0, 0] = acck.astype(BF16)
    vo_ref[0, 0] = accv.astype(BF16)


def _compress(proj, cmp_wk, cmp_wv, cmp_pos):
    b, s, _ = proj.shape
    ncmp = s // CMP_STRIDE
    out = jax.ShapeDtypeStruct((b, NSA_KV, ncmp, HEAD_DIM), BF16)
    return pl.pallas_call(
        _compress_body,
        grid=(b, NSA_KV),
        in_specs=[pl.BlockSpec((1, s, LANE), lambda bi, g: (bi, 0, CB_KC + g)),
                  pl.BlockSpec((1, s, LANE), lambda bi, g: (bi, 0, CB_VC + g)),
                  pl.BlockSpec((CMP_LEN, HEAD_DIM, HEAD_DIM), lambda bi, g: (0, 0, 0)),
                  pl.BlockSpec((CMP_LEN, HEAD_DIM, HEAD_DIM), lambda bi, g: (0, 0, 0)),
                  pl.BlockSpec((CMP_LEN, HEAD_DIM), lambda bi, g: (0, 0))],
        out_specs=[pl.BlockSpec((1, 1, ncmp, HEAD_DIM), lambda bi, g: (bi, g, 0, 0)),
                   pl.BlockSpec((1, 1, ncmp, HEAD_DIM), lambda bi, g: (bi, g, 0, 0))],
        out_shape=[out, out],
        scratch_shapes=[pltpu.VMEM((s + CMP_LEN, HEAD_DIM), F32),
                        pltpu.VMEM((s + CMP_LEN, HEAD_DIM), F32)],
        compiler_params=_params("parallel", "parallel"),
        name="nsa_compress",
    )(proj, proj, cmp_wk.astype(BF16), cmp_wv.astype(BF16), cmp_pos)


def _nsa_body(q_ref, kcmp_ref, vcmp_ref, ks_ref, vs_ref, kw_ref, vw_ref, gt_ref, bct_ref,
              covt_ref, tzs_ref, tzw_ref, o_ref, mask_ref):
    qi = pl.program_id(2)
    scale = HEAD_DIM ** -0.5
    nslc = covt_ref.shape[0]
    nkt = mask_ref.shape[0]
    kcmp = kcmp_ref[0, 0]
    vcmp = vcmp_ref[0, 0]
    gates = jax.nn.sigmoid(gt_ref[0])

    psum = jnp.zeros((kcmp.shape[0], BQ), F32)
    for r in range(NSA_GROUP):
        q = q_ref[0, :, r * LANE:(r + 1) * LANE]
        pt = _masked_softmax(_nt(kcmp, q) * scale + bct_ref[r], axis=0)
        o_cmp = lax.dot_general(pt.astype(BF16), vcmp, (((0,), (0,)), ((), ())),
                                preferred_element_type=F32)
        o_ref[0, :, r * LANE:(r + 1) * LANE] = gates[:, 3 * r:3 * r + 1] * o_cmp
        psum = psum + pt
    imp = jnp.dot(covt_ref[...], psum, preferred_element_type=F32,
                  precision=lax.Precision.HIGHEST)
    jb = lax.broadcasted_iota(I32, (nslc, BQ), 0)
    qpos = qi * BQ + lax.broadcasted_iota(I32, (nslc, BQ), 1)
    qblk = lax.shift_right_logical(qpos, int(math.log2(SLC_LEN)))
    forced = (jb == 0) | (jb == qblk) | (jb == qblk - 1)
    imp = jnp.where(forced, FORCE, jnp.where(jb > qblk, -FORCE, imp))
    beaten = jnp.zeros((nslc, BQ), F32)
    for jp in range(nslc):
        row = imp[jp:jp + 1, :]
        wins = (row > imp) | ((row == imp) & (jp < jb))
        beaten = beaten + wins.astype(F32)
    selt = (beaten < SLC_TOPN).astype(BF16)
    eye = (lax.broadcasted_iota(I32, (BQ, BQ), 0) == lax.broadcasted_iota(I32, (BQ, BQ), 1)).astype(BF16)
    selq = _nt(eye, selt).astype(BF16)
    kpos = lax.broadcasted_iota(I32, (nslc, BK), 1)
    kblk_row = lax.broadcasted_iota(I32, (nslc, BK), 0)

    for t in range(nkt):
        expand = (lax.shift_right_logical(t * BK + kpos, int(math.log2(SLC_LEN))) == kblk_row).astype(BF16)
        mask_ref[t] = (jnp.dot(selq, expand, preferred_element_type=F32) - 1.0) * (-NEG)

    qs = [q_ref[0, :, r * LANE:(r + 1) * LANE] for r in range(NSA_GROUP)]
    sel_fns = [lambda kj, s, r=r: s + tzs_ref[r, qi - kj] + mask_ref[kj] for r in range(NSA_GROUP)]
    win_fns = [lambda kj, s, r=r: s + tzw_ref[r, qi - kj] for r in range(NSA_GROUP)]
    nwin = tzw_ref.shape[1]
    o_sel = _flash(qs, ks_ref, vs_ref, 0, qi + 1, sel_fns, scale)
    o_win = _flash(qs, kw_ref, vw_ref, jnp.maximum(qi - (nwin - 1), 0), qi + 1, win_fns, scale)
    for r in range(NSA_GROUP):
        o_ref[0, :, r * LANE:(r + 1) * LANE] += (gates[:, 3 * r + 1:3 * r + 2] * o_sel[r]
                                                 + gates[:, 3 * r + 2:3 * r + 3] * o_win[r])


def _nsa(proj, gates, kcmp, vcmp, bct, covt, tzs, tzw):
    b, s, _ = proj.shape
    nd = s // BK
    ncmp = kcmp.shape[2]
    nslc = covt.shape[0]
    nwin = tzw.shape[1]
    gw = NSA_GROUP * LANE
    kv = lambda cb: pl.BlockSpec((1, s, LANE), lambda bi, g, qi: (bi, 0, cb + g))
    return pl.pallas_call(
        _nsa_body,
        grid=(b, NSA_KV, s // BQ),
        in_specs=[pl.BlockSpec((1, BQ, gw), lambda bi, g, qi: (bi, qi, CB_QB // NSA_GROUP + g)),
                  pl.BlockSpec((1, 1, ncmp, HEAD_DIM), lambda bi, g, qi: (bi, g, 0, 0)),
                  pl.BlockSpec((1, 1, ncmp, HEAD_DIM), lambda bi, g, qi: (bi, g, 0, 0)),
                  kv(CB_KS), kv(CB_VS), kv(CB_KW), kv(CB_VW),
                  pl.BlockSpec((1, BQ, LANE), lambda bi, g, qi: (bi, qi, g)),
                  pl.BlockSpec((NSA_GROUP, ncmp, BQ), lambda bi, g, qi: (g, 0, qi)),
                  pl.BlockSpec((nslc, ncmp), lambda bi, g, qi: (0, 0)),
                  pl.BlockSpec((NSA_GROUP, nd, BQ, BK), lambda bi, g, qi: (g, 0, 0, 0)),
                  pl.BlockSpec((NSA_GROUP, nwin, BQ, BK), lambda bi, g, qi: (g, 0, 0, 0))],
        out_specs=pl.BlockSpec((1, BQ, gw), lambda bi, g, qi: (bi, qi, g)),
        out_shape=jax.ShapeDtypeStruct((b, s, B_HEADS * LANE), F32),
        scratch_shapes=[pltpu.VMEM((nd, BQ, BK), F32)],
        compiler_params=_params("parallel", "parallel", "arbitrary"),
        name="nsa_attention",
    )(proj, kcmp, vcmp, proj, proj, proj, proj, gates, bct, covt, tzs, tzw)


def _moba_body(q_ref, k_ref, v_ref, avg_ref, tz_ref, o_ref, mask_ref):
    qi = pl.program_id(2)
    scale = HEAD_DIM ** -0.5
    nblk = avg_ref.shape[0]
    q = q_ref[0]
    kmean = jnp.dot(avg_ref[...], k_ref[0], preferred_element_type=F32)
    gate = _nt(kmean, q.astype(F32), precision=lax.Precision.HIGHEST)
    nb = lax.broadcasted_iota(I32, (nblk, BQ), 0)
    past = nb < qi
    beaten = jnp.zeros((nblk, BQ), F32)
    for np_ in range(nblk):
        row = gate[np_:np_ + 1, :]
        wins = (np_ < qi) & ((row > gate) | ((row == gate) & (np_ < nb)))
        beaten = beaten + wins.astype(F32)
    selt = (past & (beaten < MOBA_TOPK)).astype(BF16)
    eye = (lax.broadcasted_iota(I32, (BQ, BQ), 0) == lax.broadcasted_iota(I32, (BQ, BQ), 1)).astype(BF16)
    selq = _nt(eye, selt)
    for n in range(nblk):
        mask_ref[n] = jnp.broadcast_to(selq[:, n:n + 1], (BQ, BK))

    def score(kj, s):
        keep = (mask_ref[kj] > 0.5) | (kj == qi)
        return jnp.where(keep, s + tz_ref[0, qi - kj], NEG)

    o_ref[0] = _flash([q], k_ref, v_ref, 0, qi + 1, [score], scale)[0]


def _moba(proj, avg, tzc):
    b, s, _ = proj.shape
    nd = s // BK
    nblk = avg.shape[0]
    return pl.pallas_call(
        _moba_body,
        grid=(b, C_HEADS, s // BQ),
        in_specs=[pl.BlockSpec((1, BQ, LANE), lambda bi, h, qi: (bi, qi, CB_QC + h)),
                  pl.BlockSpec((1, s, LANE), lambda bi, h, qi: (bi, 0, CB_KCC + h)),
                  pl.BlockSpec((1, s, LANE), lambda bi, h, qi: (bi, 0, CB_VCC + h)),
                  pl.BlockSpec((nblk, s), lambda bi, h, qi: (0, 0)),
                  pl.BlockSpec((1, nd, BQ, BK), lambda bi, h, qi: (h, 0, 0, 0))],
        out_specs=pl.BlockSpec((1, BQ, LANE), lambda bi, h, qi: (bi, qi, h)),
        out_shape=jax.ShapeDtypeStruct((b, s, C_HEADS * LANE), F32),
        scratch_shapes=[pltpu.VMEM((nblk, BQ, BK), F32)],
        compiler_params=_params("parallel", "parallel", "arbitrary"),
        name="moba_attention",
    )(proj, proj, proj, avg, tzc)


def _outproj_body(oa_ref, ob_ref, oc_ref, g_ref, w_ref, h_ref, o_ref, y_ref):
    @pl.when(pl.program_id(1) == 0)
    def _():
        c0 = 0
        for ref in (oa_ref, ob_ref, oc_ref):
            x = ref[...]
            wd = x.shape[1]
            ms = jnp.mean(x * x, axis=-1, keepdims=True)
            y_ref[:, c0:c0 + wd] = (x * lax.rsqrt(ms + EPS) * g_ref[:, c0:c0 + wd]).astype(BF16)
            c0 += wd

    o_ref[...] = h_ref[...] + jnp.dot(y_ref[...], w_ref[...], preferred_element_type=F32)


def _outproj(oa, ob, oc, g, w, h, tm, tn):
    t, d = h.shape
    row = lambda a: pl.BlockSpec((tm, a.shape[1]), lambda i, j: (i, 0))
    return pl.pallas_call(
        _outproj_body,
        grid=(t // tm, d // tn),
        in_specs=[row(oa), row(ob), row(oc),
                  pl.BlockSpec((1, d), lambda i, j: (0, 0)),
                  pl.BlockSpec((d, tn), lambda i, j: (0, j)),
                  pl.BlockSpec((tm, tn), lambda i, j: (i, j))],
        out_specs=pl.BlockSpec((tm, tn), lambda i, j: (i, j)),
        out_shape=jax.ShapeDtypeStruct((t, d), F32),
        scratch_shapes=[pltpu.VMEM((tm, d), BF16)],
        compiler_params=_params("parallel", "arbitrary"),
        name="out_projection",
    )(oa, ob, oc, g.reshape(1, d), w, h)


def _topk_rows(x, k, payload=None):
    n, tm = x.shape
    rows = lax.broadcasted_iota(I32, x.shape, 0)
    slot = lax.broadcasted_iota(I32, (k, tm), 0)
    vals = jnp.zeros((k, tm), F32)
    idxs = jnp.zeros((k, tm), I32)
    for it in range(k):
        mx = jnp.max(x, axis=0, keepdims=True)
        idx = jnp.min(jnp.where(x == mx, rows, n), axis=0, keepdims=True)
        hit = rows == idx
        if payload is not None:
            idx = jnp.sum(jnp.where(hit, payload, 0), axis=0, keepdims=True)
        vals = jnp.where(slot == it, mx, vals)
        idxs = jnp.where(slot == it, idx, idxs)
        x = jnp.where(hit, -jnp.inf, x)
    return vals, idxs


def _oddeven_merge_sort(n):
    pairs = []
    p = 1
    while p < n:
        k = p
        while k >= 1:
            for j in range(k % p, n - k, 2 * k):
                for i in range(min(k, n - j - k)):
                    if (i + j) // (2 * p) == (i + j + k) // (2 * p):
                        pairs.append((i + j, i + j + k))
            k //= 2
        p *= 2
    return pairs


def _topk_rows_sorted(x, k):
    n, tm = x.shape
    assert n == 8 * k
    rows = lax.broadcasted_iota(I32, (n, tm), 0)
    slot = lax.broadcasted_iota(I32, (k, tm), 0)
    out_v, out_i = [], []
    for c in range(tm // LANE):
        lanes = slice(c * LANE, (c + 1) * LANE)
        v = [x[8 * r:8 * r + 8, lanes] for r in range(k)]
        ix = [rows[8 * r:8 * r + 8, lanes] for r in range(k)]
        for i, j in _oddeven_merge_sort(k):
            va, vb, ia, ib = v[i], v[j], ix[i], ix[j]
            keep = (va > vb) | ((va == vb) & (ia < ib))
            v[i], v[j] = jnp.maximum(va, vb), jnp.minimum(va, vb)
            ix[i], ix[j] = jnp.where(keep, ia, ib), jnp.where(keep, ib, ia)
        vals = jnp.zeros((k, LANE), F32)
        idxs = jnp.zeros((k, LANE), I32)
        for t in range(k):
            mx = jnp.max(v[0], axis=0, keepdims=True)
            best = jnp.min(jnp.where(v[0] == mx, ix[0], n), axis=0, keepdims=True)
            win = ix[0] == best
            vals = jnp.where(slot[:, lanes] == t, mx, vals)
            idxs = jnp.where(slot[:, lanes] == t, best, idxs)
            for i in range(k - 1 - t):
                v[i] = jnp.where(win, v[i + 1], v[i])
                ix[i] = jnp.where(win, ix[i + 1], ix[i])
        out_v.append(vals)
        out_i.append(idxs)
    return jnp.concatenate(out_v, axis=1), jnp.concatenate(out_i, axis=1)


def _best_pairs(s0, i0, s1, i1):
    k, tm = s0.shape
    half = k // 2
    assert half == 8
    sub = lax.broadcasted_iota(I32, (half, tm), 0)
    slot = lax.broadcasted_iota(I32, (k, tm), 0)
    lv, le = [], []
    for b in range(k):
        deep = jnp.where(sub < min(k // (b + 1), half), s0[0:half] + s1[b:b + 1], -jnp.inf)
        lv.append(deep)
        le.append(i0[0:half] * PEER_NKEYS + i1[b:b + 1])
    tail_v = s0[half:] + s1[0:1]
    tail_e = i0[half:] * PEER_NKEYS + i1[0:1]
    vals = jnp.zeros((k, tm), F32)
    ids = jnp.zeros((k, tm), I32)
    for t in range(k):
        mx = jnp.max(jnp.maximum(lv[0], tail_v), axis=0, keepdims=True)
        first = jnp.minimum(jnp.where(lv[0] == mx, sub, k), jnp.where(tail_v == mx, sub + half, k))
        best = jnp.min(first, axis=0, keepdims=True)
        win = sub == best
        win_tail = sub + half == best
        expert = jnp.sum(jnp.where(win, le[0], 0) + jnp.where(win_tail, tail_e, 0), axis=0, keepdims=True)
        vals = jnp.where(slot == t, mx, vals)
        ids = jnp.where(slot == t, expert, ids)
        for i in range(k - 1 - t):
            lv[i] = jnp.where(win, lv[i + 1], lv[i])
            le[i] = jnp.where(win, le[i + 1], le[i])
        tail_v = jnp.where(win_tail, -jnp.inf, tail_v)
    return vals, ids


def _peer_topk_body(h_ref, ln_ref, wq_ref, keys_ref, e_ref, g_ref):
    x = h_ref[...]
    ms = jnp.mean(x * x, axis=-1, keepdims=True)
    xn = (x * lax.rsqrt(ms + EPS) * ln_ref[...]).astype(BF16)
    dq = keys_ref.shape[2]
    for h in range(PEER_HEADS):
        qh = jnp.dot(xn, wq_ref[:, 2 * h * dq:2 * (h + 1) * dq], preferred_element_type=F32)
        for part in range(qh.shape[0] // LANE):
            toks = slice(part * LANE, (part + 1) * LANE)
            tops = []
            for c in range(2):
                sc = _nt(keys_ref[2 * h + c], qh[toks, c * dq:(c + 1) * dq])
                tops.append(_topk_rows_sorted(sc, PEER_TOPK))
            (s0, i0), (s1, i1) = tops
            bs, be = _best_pairs(s0, i0, s1, i1)
            e = jnp.exp(bs - jnp.max(bs, axis=0, keepdims=True))
            g_ref[h * PEER_TOPK:(h + 1) * PEER_TOPK, toks] = e / jnp.sum(e, axis=0, keepdims=True)
            e_ref[h * PEER_TOPK:(h + 1) * PEER_TOPK, toks] = be


def _peer_topk(h, ln, wq, keys, tm):
    t, d = h.shape
    hk = PEER_HEADS * PEER_TOPK
    assert wq.shape == (d, keys.shape[0] * keys.shape[2]) and keys.shape[2] == LANE
    return pl.pallas_call(
        _peer_topk_body,
        grid=(t // tm,),
        in_specs=[pl.BlockSpec((tm, d), lambda i: (i, 0)),
                  pl.BlockSpec((1, d), lambda i: (0, 0)),
                  pl.BlockSpec(wq.shape, lambda i: (0, 0)),
                  pl.BlockSpec(keys.shape, lambda i: (0, 0, 0))],
        out_specs=[pl.BlockSpec((hk, tm), lambda i: (0, i)),
                   pl.BlockSpec((hk, tm), lambda i: (0, i))],
        out_shape=[jax.ShapeDtypeStruct((hk, t), I32), jax.ShapeDtypeStruct((hk, t), F32)],
        compiler_params=_params("parallel"),
        name="peer_topk",
    )(h, ln.reshape(1, d), wq, keys)


PEER_NBUF = 16
PEER_AHEAD = 14


def _gelu(x):
    return 0.5 * x * (1.0 + lax.erf(x * (2.0 ** -0.5)))


def _peer_expert_body(ids0_ref, idsn_ref, h_ref, ln_ref, g_ref, uvw_ref, o_ref, xn_ref, acc_ref, lhs_ref,
                      buf_ref, sem_ref):
    tb, d = h_ref.shape
    hk = idsn_ref.shape[1]
    nrg = hk // 8
    nlt = d // LANE
    assert nrg == nlt and hk % (2 * nlt) == 0
    step = pl.program_id(0)
    x = h_ref[...]
    ms = jnp.mean(x * x, axis=-1, keepdims=True)
    xn_ref[...] = x * lax.rsqrt(ms + EPS) * ln_ref[...]

    def issue(ids_ref, row, slot, k0, k1):
        for k in range(k0, k1):
            e = ids_ref[row, k]
            pltpu.make_async_copy(uvw_ref.at[e], buf_ref.at[slot, :, k, :],
                                  sem_ref.at[slot]).start(priority=k % 2)

    def wait_all(slot):
        pltpu.make_async_copy(buf_ref.at[slot], buf_ref.at[slot], sem_ref.at[slot]).wait()

    def x_tiles(t):
        xrow = xn_ref[pl.ds(t, 1), :]
        return [jnp.broadcast_to(xrow[:, j * LANE:(j + 1) * LANE], (8, LANE)) for j in range(nlt)]

    def dots(xb, slot, r):
        a = None
        for j in range(nlt):
            w = buf_ref[slot, j, r * 8:(r + 1) * 8, :]
            pr = lax.bitcast_convert_type(lax.shift_left(w, jnp.uint32(16)), F32) * xb[j]
            a = pr if a is None else a + pr
        return a

    def coefficients(t, slot):
        act = jnp.sum(acc_ref[slot % 2].T, axis=0, keepdims=True)
        coef = jnp.broadcast_to(_gelu(act) * g_ref[pl.ds(t, 1), :], (8, hk))
        chi = coef.astype(BF16).astype(F32)
        return jnp.concatenate([chi, coef - chi], axis=0).astype(BF16)

    def weighted(lhs, slot, j):
        w = buf_ref[slot, j]
        v = lax.bitcast_convert_type(w & jnp.uint32(0xFFFF0000), F32).astype(BF16)
        yj = jnp.dot(lhs, v, preferred_element_type=F32)
        return yj[0:1, :] + yj[8:9, :]

    def flush(done):
        ya, yb, tp = done
        y = jnp.concatenate([ya[j:j + 1, :] for j in range(8)] + [yb[j:j + 1, :] for j in range(8)], axis=1)
        o_ref[pl.ds(tp, 1), :] = h_ref[pl.ds(tp, 1), :] + y

    def turn(t, slot, stages, done):
        tgt = (slot + PEER_AHEAD) % PEER_NBUF
        far = (slot + 2) % PEER_NBUF
        per = hk // (2 * nlt)
        if stages >= 3:
            wait_all(far)
            xb = x_tiles(t + 2)
        lhs = lhs_ref[slot % 2]
        ys, parts = [], []
        for c in range(nlt):
            if stages >= 3:
                parts.append(dots(xb, far, c))
            issue(idsn_ref, t, tgt, 2 * c * per, (2 * c + 1) * per)
            if 2 * c < nlt:
                ys.append(weighted(lhs, slot, 2 * c))
                ys.append(weighted(lhs, slot, 2 * c + 1))
            if 2 * c == nlt and stages >= 2:
                lhs_next = coefficients(t + 1, slot + 1)
            issue(idsn_ref, t, tgt, (2 * c + 1) * per, (2 * c + 2) * per)
        if stages >= 3:
            acc_ref[slot % 2] = jnp.concatenate(parts, axis=0)
        if stages >= 2:
            lhs_ref[(slot + 1) % 2] = lhs_next
        flush(done)
        return (jnp.concatenate(ys[:8], axis=0), jnp.concatenate(ys[8:], axis=0), t)

    @pl.when(step == 0)
    def _():
        for t0 in range(PEER_AHEAD):
            issue(ids0_ref, t0, t0, 0, hk)

    for t0 in range(2):
        wait_all(t0)
        xb0 = x_tiles(t0)
        acc_ref[t0] = jnp.concatenate([dots(xb0, t0, r) for r in range(nrg)], axis=0)
    lhs_ref[0] = coefficients(0, 0)

    def ring(i, stacks):
        done = stacks + (jnp.maximum(i * PEER_NBUF - 1, 0),)
        for slot in range(PEER_NBUF):
            done = turn(i * PEER_NBUF + slot, slot, 3, done)
        return done[:2]

    zero = jnp.zeros((8, LANE), F32)
    nring = (tb - 2) // PEER_NBUF
    done = lax.fori_loop(0, nring, ring, (zero, zero)) + (nring * PEER_NBUF - 1,)
    for t in range(nring * PEER_NBUF, tb - 2):
        done = turn(t, t % PEER_NBUF, 3, done)
    done = turn(tb - 2, (tb - 2) % PEER_NBUF, 2, done)
    done = turn(tb - 1, (tb - 1) % PEER_NBUF, 1, done)
    flush(done)

    @pl.when(step == pl.num_programs(0) - 1)
    def _():
        for t0 in range(PEER_AHEAD):
            wait_all((tb + t0) % PEER_NBUF)


def _peer_experts(ids, gates, h, ln, slabs, tb):
    t, d = h.shape
    hk = ids.shape[1]
    assert hk == LANE and tb % PEER_NBUF == 0 and PEER_NBUF % 2 == 0 and d % LANE == 0
    assert PEER_AHEAD <= PEER_NBUF - 2
    ids_next = jnp.concatenate([ids[PEER_AHEAD:], ids[:PEER_AHEAD]], axis=0)
    assert slabs.shape[1:] == (d // LANE, LANE)
    return pl.pallas_call(
        _peer_expert_body,
        grid=(t // tb,),
        in_specs=[pl.BlockSpec((PEER_NBUF, hk), lambda i: (0, 0), memory_space=pltpu.SMEM),
                  pl.BlockSpec((tb, hk), lambda i: (i, 0), memory_space=pltpu.SMEM),
                  pl.BlockSpec((tb, d), lambda i: (i, 0)),
                  pl.BlockSpec((1, d), lambda i: (0, 0)),
                  pl.BlockSpec((tb, hk), lambda i: (i, 0)),
                  pl.BlockSpec(memory_space=pl.ANY)],
        out_specs=pl.BlockSpec((tb, d), lambda i: (i, 0)),
        out_shape=jax.ShapeDtypeStruct((t, d), F32),
        scratch_shapes=[pltpu.VMEM((tb, d), F32),
                        pltpu.VMEM((2, hk, LANE), F32),
                        pltpu.VMEM((2, 16, hk), BF16),
                        pltpu.VMEM((PEER_NBUF, d // LANE, hk, LANE), jnp.uint32),
                        pltpu.SemaphoreType.DMA((PEER_NBUF,))],
        compiler_params=_params("arbitrary"),
        name="peer_experts",
    )(ids[:PEER_NBUF], ids_next, h, ln.reshape(1, d), gates, slabs)


def _ple_body(x_ref, g_ref, wg_ref, p_ref, wp_ref, h_ref, o_ref, xn_ref):
    @pl.when(pl.program_id(1) == 0)
    def _():
        x = x_ref[...]
        ms = jnp.mean(x * x, axis=-1, keepdims=True)
        xn_ref[...] = (x * lax.rsqrt(ms + EPS) * g_ref[...]).astype(BF16)

    z = jnp.dot(xn_ref[...], wg_ref[...], preferred_element_type=F32)
    pp = jnp.dot(p_ref[...].astype(BF16), wp_ref[...], preferred_element_type=F32)
    o_ref[...] = h_ref[...] + jax.nn.sigmoid(z) * pp


def _ple(h, g, wg, p, wp, tm, tn):
    t, d = h.shape
    pd = p.shape[1]
    return pl.pallas_call(
        _ple_body,
        grid=(t // tm, d // tn),
        in_specs=[pl.BlockSpec((tm, d), lambda i, j: (i, 0)),
                  pl.BlockSpec((1, d), lambda i, j: (0, 0)),
                  pl.BlockSpec((d, tn), lambda i, j: (0, j)),
                  pl.BlockSpec((tm, pd), lambda i, j: (i, 0)),
                  pl.BlockSpec((pd, tn), lambda i, j: (0, j)),
                  pl.BlockSpec((tm, tn), lambda i, j: (i, j))],
        out_specs=pl.BlockSpec((tm, tn), lambda i, j: (i, j)),
        out_shape=jax.ShapeDtypeStruct((t, d), F32),
        scratch_shapes=[pltpu.VMEM((tm, d), BF16)],
        compiler_params=_params("parallel", "arbitrary"),
        name="ple_gate",
    )(h, g.reshape(1, d), wg, p, wp, h)


def _rmsnorm_body(x_ref, g_ref, o_ref):
    x = x_ref[...]
    ms = jnp.mean(x * x, axis=-1, keepdims=True)
    o_ref[...] = x * lax.rsqrt(ms + EPS) * g_ref[...]


def _rmsnorm(x, g, tm):
    t, d = x.shape
    return pl.pallas_call(
        _rmsnorm_body,
        grid=(t // tm,),
        in_specs=[pl.BlockSpec((tm, d), lambda i: (i, 0)), pl.BlockSpec((1, d), lambda i: (0, 0))],
        out_specs=pl.BlockSpec((tm, d), lambda i: (i, 0)),
        out_shape=jax.ShapeDtypeStruct((t, d), F32),
        compiler_params=_params("parallel"),
        name="final_rmsnorm",
    )(x, g.reshape(1, d))


def _toeplitz(vec, nq, nk):
    assert nq == nk
    lead = vec.shape[:-1]
    g = int(np.prod(lead))
    w = jnp.concatenate([jnp.zeros(lead + (1,), vec.dtype), vec[..., ::-1]], axis=-1).reshape(g, 1, 2 * nk)

    def body(w_ref, o_ref):
        rows = jnp.broadcast_to(w_ref[0], (nq, 2 * nk))
        o_ref[0] = pltpu.roll(rows, 0, 1, stride=1, stride_axis=0)[:, nk:]

    out = pl.pallas_call(
        body,
        grid=(g,),
        in_specs=[pl.BlockSpec((1, 1, 2 * nk), lambda i: (i, 0, 0))],
        out_specs=pl.BlockSpec((1, nq, nk), lambda i: (i, 0, 0)),
        out_shape=jax.ShapeDtypeStruct((g, nq, nk), vec.dtype),
        compiler_params=_params("parallel"),
        name="toeplitz_tiles",
    )(w)
    return out.reshape(lead + (nq, nk))


def _bias_tables(rel_bias, s):
    nd = s // BK
    bdt = rel_bias[_rel_bucket(jnp.arange(s))].astype(F32).T
    dd = (jnp.arange(nd)[:, None, None] * BK + jnp.arange(BQ)[None, :, None]
          - jnp.arange(BK)[None, None, :])
    ext = jnp.pad(bdt, ((0, 0), (BK - 1, 0)))
    segs = jnp.stack([ext[:, dl * BK:dl * BK + BQ + BK - 1] for dl in range(nd)], axis=1)
    tz = _toeplitz(segs, BQ, BK)
    causal = dd >= 0
    mult = sum(((dd % dil == 0) & (dd // dil <= window // dil)).astype(F32) for window, dil in DIL_PATTERNS)
    ok = causal & (mult > 0)
    tzd = jnp.where(ok, tz[:A_HEADS] + jnp.log(jnp.where(ok, mult, 1.0)), NEG)
    tzb = tz[A_HEADS:A_HEADS + B_HEADS]
    tzs = jnp.where(causal, tzb, NEG)
    nwin = -(-(NSA_WINDOW - 1) // BK) + 1
    tzw = jnp.where(causal & (dd <= NSA_WINDOW - 1), tzb, NEG)[:, :nwin]
    tzc = jnp.where(causal, tz[A_HEADS + B_HEADS:], NEG)
    ncmp = s // CMP_STRIDE
    na = s // CMP_STRIDE
    nvec = na + ncmp - 1
    lo = CMP_STRIDE * (ncmp - 1) + CMP_LEN - 1
    bdb = bdt[A_HEADS:A_HEADS + B_HEADS]
    gext = jnp.concatenate([jnp.full((B_HEADS, lo), NEG, F32), bdb], axis=1)
    vecs = gext[:, :CMP_STRIDE * nvec].reshape(B_HEADS, nvec, CMP_STRIDE).transpose(0, 2, 1)
    bcq = _toeplitz(vecs, na, ncmp)
    bcq = bcq.transpose(0, 2, 1, 3).reshape(B_HEADS, s, ncmp)
    bct = jnp.transpose(bcq, (0, 2, 1))
    nslc = s // SLC_LEN
    cstart = jnp.arange(ncmp) * CMP_STRIDE
    sstart = jnp.arange(nslc) * SLC_LEN
    covt = ((cstart[None, :] < sstart[:, None] + SLC_LEN)
            & (cstart[None, :] + CMP_LEN > sstart[:, None])).astype(F32)
    nblk = s // MOBA_BLK
    avg = ((jnp.arange(s)[None, :] // MOBA_BLK == jnp.arange(nblk)[:, None]).astype(F32)
           / MOBA_BLK).astype(BF16)
    return tzd, tzs, tzw, tzc, bct, covt, avg


def _pack_body(u_ref, v_ref, o_ref):
    _, te, d = u_ref.shape
    nlt = d // LANE
    ub = lax.bitcast_convert_type(u_ref[0].astype(BF16).astype(F32), jnp.uint32)
    vb = lax.bitcast_convert_type(v_ref[0].astype(BF16).astype(F32), jnp.uint32)
    word = lax.shift_right_logical(ub, jnp.uint32(16)) | vb
    for j in range(nlt):
        o_ref[pl.ds(j, te, stride=nlt), :] = word[:, j * LANE:(j + 1) * LANE]


def _pack_uv(u, v, layer, te=256):
    _, e, d = u.shape
    nlt = d // LANE
    spec = pl.BlockSpec((1, te, d), lambda i: (layer, i, 0))
    out = pl.pallas_call(
        _pack_body,
        grid=(e // te,),
        in_specs=[spec, spec],
        out_specs=pl.BlockSpec((te * nlt, LANE), lambda i: (i, 0)),
        out_shape=jax.ShapeDtypeStruct((e * nlt, LANE), jnp.uint32),
        compiler_params=_params("parallel"),
        name="pack_experts",
    )(u, v)
    return out.reshape(e, nlt, LANE)


GATE_COL0 = 3 * A_HEADS * LANE + B_HEADS * LANE + 6 * NSA_KV * LANE
GATE_COLS = 3 * B_HEADS


def _w_main_body(w_ref, o_ref):
    x = w_ref[0]
    o_ref[:, :GATE_COL0] = x[:, :GATE_COL0].astype(BF16)
    o_ref[:, GATE_COL0:] = x[:, GATE_COL0 + GATE_COLS:].astype(BF16)


def _reorder_w_in(w_in, layer, tk=256):
    _, d, cols = w_in.shape
    main = pl.pallas_call(
        _w_main_body,
        grid=(d // tk,),
        in_specs=[pl.BlockSpec((1, tk, cols), lambda i: (layer, i, 0))],
        out_specs=pl.BlockSpec((tk, MAIN_COLS), lambda i: (i, 0)),
        out_shape=jax.ShapeDtypeStruct((d, MAIN_COLS), BF16),
        compiler_params=_params("parallel"),
        name="reorder_w_in",
    )(w_in)
    per = GATE_COLS // NSA_KV
    wg = w_in[layer, :, GATE_COL0:GATE_COL0 + GATE_COLS]
    gate = jnp.concatenate([jnp.pad(wg[:, g * per:(g + 1) * per], ((0, 0), (0, LANE - per))) for g in range(NSA_KV)],
                           axis=1)
    return main, gate.astype(BF16)


def kernel(x, p, ln_mix, w_in, cmp_wk, cmp_wv, cmp_pos, out_norm, w_out, rel_bias, ln_ffn, peer_wq,
           peer_keys, peer_u, peer_v, ln_ple, ple_gate, ple_proj, ln_final):
    b, s, d = x.shape
    t = b * s
    depth = w_in.shape[0]
    assert s % BQ == 0 and BQ == BK == MOBA_BLK and d % LANE == 0
    tm = 1024 if t % 1024 == 0 else 512
    tzd, tzs, tzw, tzc, bct, covt, avg = _bias_tables(rel_bias, s)
    h = x.reshape(t, d)
    for i in range(depth):
        w_main, w_gate = _reorder_w_in(w_in, i)
        proj = _normmm(h, ln_mix[i], w_main, BF16, tm, 1024, "in_projection").reshape(b, s, MAIN_COLS)
        gates = _normmm(h, ln_mix[i], w_gate, F32, tm, w_gate.shape[1], "gate_projection")
        gates = gates.reshape(b, s, NSA_KV * LANE)
        oa = _dilated(proj, tzd)
        kcmp, vcmp = _compress(proj, cmp_wk[i], cmp_wv[i], cmp_pos[i])
        ob = _nsa(proj, gates, kcmp, vcmp, bct, covt, tzs, tzw)
        oc = _moba(proj, avg, tzc)
        h = _outproj(oa.reshape(t, -1), ob.reshape(t, -1), oc.reshape(t, -1), out_norm[i],
                     w_out[i].astype(BF16), h, tm, 1024)
        keys = peer_keys[i].reshape(PEER_HEADS * 2, PEER_NKEYS, -1)
        e_t, g_t = _peer_topk(h, ln_ffn[i], peer_wq[i].astype(BF16), keys, 2 * LANE)
        h = _peer_experts(e_t.T, g_t.T, h, ln_ffn[i], _pack_uv(peer_u, peer_v, i), LANE)
        h = _ple(h, ln_ple[i], ple_gate[i].astype(BF16), p[i].reshape(t, -1), ple_proj[i].astype(BF16),
                 tm, 1024)
    return _rmsnorm(h, ln_final, tm).reshape(b, s, d)
```
